```python
import numpy as np
import jax
import jax.numpy as jnp
from jax import lax

D_MODEL = 1024
BATCH = 4
SEQ = 8192
DEPTH = 4

D_MIX = D_MODEL
BLK = 128
LN_EPS = 1e-5
NORM_EPS = 1e-6
DEEPNORM_ALPHA = (2 * DEPTH) ** 0.25
DEEPNORM_BETA = (8 * DEPTH) ** -0.25

RET_HEADS = 4
RET_DK = 32
RET_DV = 64
RET_W = RET_HEADS * RET_DV
RET_THETA = 10000.0
DIL_HEADS = 4
DIL_DH = 64
DIL_W = DIL_HEADS * DIL_DH
DIL_PATTERNS = ((128, 1), (512, 4), (2048, 16))
ROPE_THETA = 500000.0
ROPE_ROT_DIM = DIL_DH // 4
RWKV_HEADS = 4
RWKV_DH = 64
RWKV_W = RWKV_HEADS * RWKV_DH
DECAY_LORA = 64
AAA_LORA = 64
MV_LORA = 32
GATE_LORA = 128
RWKV_GN_EPS = 64e-5
MLA_HEADS = 4
MLA_NOPE = 64
MLA_ROPE = 32
MLA_DV = 64
MLA_W = MLA_HEADS * MLA_DV
Q_LORA = 256
KV_LORA = 128
MLA_THETA = 10000.0
N_EXPERTS = 32
TOP_K = 4
D_FF_EXPERT = D_MODEL
SWIGLU_LIMIT = 7.0
SWIGLU_ALPHA = 1.702
MOE_BLK = 256

RET_SPLITS = (RET_HEADS * RET_DK, RET_HEADS * RET_DK, RET_W, RET_W)
DIL_SPLITS = (DIL_W, DIL_W, DIL_W)
RWKV_SPLITS = (RWKV_W, RWKV_W, RWKV_W, DECAY_LORA, AAA_LORA, GATE_LORA)
MLA_SPLITS = (Q_LORA, KV_LORA, MLA_ROPE)
A_END = sum(RET_SPLITS)
B_END = A_END + sum(DIL_SPLITS)
C_END = B_END + sum(RWKV_SPLITS)
N_IN = C_END + sum(MLA_SPLITS)
RWKV_SHIFT_W = sum(RWKV_SPLITS)

kernel_name = "hybrid_parallel_heads_moe_trunk"

F32 = jnp.float32


def split_cols(p, sizes):
    idx = np.cumsum(sizes)[:-1].tolist()
    return jnp.split(p, idx, axis=-1)


def layer_norm(x, g, b, eps=LN_EPS):
    xf = x.astype(F32)
    mu = jnp.mean(xf, -1, keepdims=True)
    var = jnp.mean(jnp.square(xf - mu), -1, keepdims=True)
    return ((xf - mu) * lax.rsqrt(var + eps) * g.astype(F32) + b.astype(F32)).astype(x.dtype)


def rms_norm(x, g, eps=NORM_EPS):
    xf = x.astype(F32)
    return (xf * lax.rsqrt(jnp.mean(xf * xf, -1, keepdims=True) + eps) * g.astype(F32)).astype(x.dtype)


def rope_table(n_pos, rot_dim, theta):
    inv_freq = 1.0 / (theta ** (jnp.arange(0, rot_dim, 2, dtype=F32) / rot_dim))
    ang = jnp.arange(n_pos, dtype=F32)[:, None] * inv_freq[None, :]
    return jnp.cos(ang), jnp.sin(ang)


def apply_rope(x, cs):
    cos = cs[0][None, :, None, :]
    sin = cs[1][None, :, None, :]
    x1, x2 = jnp.split(x.astype(F32), 2, axis=-1)
    return jnp.concatenate([x1 * cos - x2 * sin, x1 * sin + x2 * cos], -1).astype(x.dtype)


def apply_partial_rope(x, cs, rot_dim):
    return jnp.concatenate([apply_rope(x[..., :rot_dim], cs), x[..., rot_dim:]], -1)


def retention_mixer(q, k, v, g, norm_g, cs):
    B_, S_, _ = q.shape
    n_ch = S_ // BLK
    q = apply_rope(q.reshape(B_, S_, RET_HEADS, RET_DK), cs)
    k = apply_rope(k.reshape(B_, S_, RET_HEADS, RET_DK), cs) * (RET_DK ** -0.5)
    v = v.reshape(B_, S_, RET_HEADS, RET_DV)
    log_gamma = jnp.log(1.0 - 2.0 ** (-5.0 - jnp.arange(RET_HEADS, dtype=F32)))
    idx = jnp.arange(BLK, dtype=F32)
    dist = idx[:, None] - idx[None, :]
    inner_decay = jnp.where(dist >= 0, jnp.exp(jnp.maximum(dist, 0.0)[None] * log_gamma[:, None, None]), 0.0)
    qc = q.reshape(B_, n_ch, BLK, RET_HEADS, RET_DK)
    kc = k.reshape(B_, n_ch, BLK, RET_HEADS, RET_DK)
    vc = v.reshape(B_, n_ch, BLK, RET_HEADS, RET_DV)
    scores = jnp.einsum('bnihd,bnjhd->bnhij', qc, kc) * inner_decay
    o_inner = jnp.einsum('bnhij,bnjhe->bnihe', scores, vc)
    zeta = jnp.exp((BLK - 1 - idx)[None, :] * log_gamma[:, None])
    kv_chunk = jnp.einsum('bnjhd,hj,bnjhe->nbhde', kc, zeta, vc)
    chunk_decay = jnp.exp(BLK * log_gamma)[None, :, None, None]

    def step(state, kv):
        return chunk_decay * state + kv, state

    _, states = lax.scan(step, jnp.zeros_like(kv_chunk[0]), kv_chunk)
    xi = jnp.exp((idx + 1.0)[None, :] * log_gamma[:, None])
    o_cross = jnp.einsum('bnihd,nbhde,hi->bnihe', qc, states, xi)
    o = (o_inner + o_cross).reshape(B_, S_, RET_HEADS, RET_DV)
    o = rms_norm(o, norm_g.reshape(RET_HEADS, RET_DV))
    return (jax.nn.silu(g.astype(F32)) * o.reshape(B_, S_, RET_W)).astype(q.dtype)


def banded_block_attention(q, k, v, n_back):
    G, L, H, dh = q.shape
    nb = L // BLK
    qb = q.reshape(G, nb, BLK, H, dh)

    def with_prev(t):
        t = t.reshape(G, nb, BLK, H, dh)
        prev = jnp.concatenate([jnp.zeros_like(t[:, :1]), t[:, :-1]], axis=1)
        return jnp.concatenate([prev, t], axis=2)

    kk = with_prev(k)
    vv = with_prev(v)
    s = jnp.einsum('gnihd,gnjhd->gnhij', qb, kk).astype(F32) * (dh ** -0.5)
    qpos = BLK + jnp.arange(BLK)
    kpos = jnp.arange(2 * BLK)
    dist = qpos[:, None] - kpos[None, :]
    band = (dist >= 0) & (dist <= n_back)
    first = band & (kpos[None, :] >= BLK)
    mask = jnp.where((jnp.arange(nb) == 0)[:, None, None], first[None], band[None])
    s = jnp.where(mask[None, :, None], s, -jnp.inf)
    lse = jax.nn.logsumexp(s, axis=-1)
    p = jnp.exp(s - lse[..., None])
    o = jnp.einsum('gnhij,gnjhd->gnihd', p.astype(v.dtype), vv)
    return o.reshape(G, L, H, dh), jnp.swapaxes(lse, 2, 3).reshape(G, L, H)


def dilated_mixer(q, k, v, norm_g, cs):
    B_, S_, _ = q.shape
    shp = (B_, S_, DIL_HEADS, DIL_DH)
    q = apply_partial_rope(q.reshape(shp), cs, ROPE_ROT_DIM)
    k = apply_partial_rope(k.reshape(shp), cs, ROPE_ROT_DIM)
    v = v.reshape(shp)
    outs, lses = [], []
    for window, dil in DIL_PATTERNS:
        span = dil * BLK
        s_pad = -(-S_ // span) * span
        m = s_pad // dil

        def to_residues(t):
            t = jnp.pad(t, ((0, 0), (0, s_pad - S_), (0, 0), (0, 0)))
            return t.reshape(B_, m, dil, DIL_HEADS, DIL_DH).transpose(0, 2, 1, 3, 4).reshape(B_ * dil, m, DIL_HEADS, DIL_DH)

        o, lse = banded_block_attention(to_residues(q), to_residues(k), to_residues(v), window // dil)
        o = o.reshape(B_, dil, m, DIL_HEADS, DIL_DH).transpose(0, 2, 1, 3, 4).reshape(B_, s_pad, DIL_HEADS, DIL_DH)[:, :S_]
        lse = lse.reshape(B_, dil, m, DIL_HEADS).transpose(0, 2, 1, 3).reshape(B_, s_pad, DIL_HEADS)[:, :S_]
        outs.append(o)
        lses.append(lse)
    wts = jax.nn.softmax(jnp.stack(lses), axis=0)
    o = jnp.einsum('pbsh,pbshd->bshd', wts, jnp.stack(outs).astype(F32))
    return rms_norm(o.reshape(B_, S_, DIL_W), norm_g).astype(q.dtype)


def token_shift(p, mu):
    prev = jnp.pad(p, ((0, 0), (1, 0), (0, 0)))[:, :-1]
    return p + (prev - p) * mu


def l2_normalize(x, eps=1e-12):
    xf = x.astype(F32)
    return xf / jnp.maximum(jnp.sqrt(jnp.sum(xf * xf, -1, keepdims=True)), eps)


def rwkv7_scan(r, w, k, v, a, b):
    B_, S_, H, N = r.shape

    def step(state, inp):
        r_t, w_t, k_t, v_t, a_t, b_t = inp
        sa = jnp.einsum('bhij,bhj->bhi', state, a_t)
        state = state * w_t[:, :, None, :] + sa[..., None] * b_t[:, :, None, :] + v_t[..., None] * k_t[:, :, None, :]
        return state, jnp.einsum('bhij,bhj->bhi', state, r_t)

    xs = (jnp.moveaxis(r, 1, 0), jnp.moveaxis(w, 1, 0), jnp.moveaxis(k, 1, 0),
          jnp.moveaxis(v, 1, 0), jnp.moveaxis(a, 1, 0), jnp.moveaxis(b, 1, 0))
    _, y = lax.scan(step, jnp.zeros((B_, H, N, N), F32), xs)
    return jnp.moveaxis(y, 0, 1)


def rwkv7_mixer(r, k, v, wd, ad, gd, w0, w_up, a0, a_up, g_up, k_k, k_a, r_k, ln_g, ln_b):
    B_, S_, _ = r.shape
    hs = (B_, S_, RWKV_HEADS, RWKV_DH)
    w = -jax.nn.softplus(-(w0 + jnp.tanh(wd) @ w_up)) - 0.5
    decay = jnp.exp(-jnp.exp(w.astype(F32)))
    a = jax.nn.sigmoid((a0 + ad @ a_up).astype(F32))
    g = (jax.nn.sigmoid(gd) @ g_up).astype(F32)
    kk = l2_normalize((k * k_k).reshape(hs))
    k = k.astype(F32) * (1.0 + (a - 1.0) * k_a.astype(F32))
    rf = r.astype(F32).reshape(hs)
    kf = k.reshape(hs)
    vf = v.astype(F32).reshape(hs)
    y = rwkv7_scan(rf, decay.reshape(hs), kf, vf, -kk, kk * a.reshape(hs))
    y = layer_norm(y, ln_g.reshape(RWKV_HEADS, RWKV_DH), ln_b.reshape(RWKV_HEADS, RWKV_DH), RWKV_GN_EPS)
    y = y + jnp.sum(rf * kf * r_k.astype(F32), -1, keepdims=True) * vf
    return (y.reshape(B_, S_, RWKV_W) * g).astype(r.dtype)


def causal_block_attention(q, k, v, scale):
    B_, S_, H, dq = q.shape
    nb = S_ // BLK
    qb = jnp.moveaxis(q.reshape(B_, nb, BLK, H, dq), 1, 0)
    kpos = jnp.arange(S_)

    def one_block(args):
        q_blk, blk = args
        s = jnp.einsum('bqhd,bkhd->bhqk', q_blk, k).astype(F32) * scale
        qpos = blk * BLK + jnp.arange(BLK)
        s = jnp.where(kpos[None, :] <= qpos[:, None], s, -jnp.inf)
        p = jax.nn.softmax(s, axis=-1)
        return jnp.einsum('bhqk,bkhd->bqhd', p.astype(v.dtype), v)

    o = lax.map(one_block, (qb, jnp.arange(nb)))
    return jnp.moveaxis(o, 0, 1).reshape(B_, S_, H, v.shape[-1])


def mla_mixer(c_q, c_kv, k_rope, q_norm_g, w_q_up, kv_norm_g, w_kv_up, out_norm_g, cs):
    B_, S_, _ = c_q.shape
    q = (rms_norm(c_q, q_norm_g) @ w_q_up).reshape(B_, S_, MLA_HEADS, MLA_NOPE + MLA_ROPE)
    kv = (rms_norm(c_kv, kv_norm_g) @ w_kv_up).reshape(B_, S_, MLA_HEADS, MLA_NOPE + MLA_DV)
    q = jnp.concatenate([q[..., :MLA_NOPE], apply_rope(q[..., MLA_NOPE:], cs)], -1)
    k_pe = apply_rope(k_rope[:, :, None, :], cs)
    k = jnp.concatenate([kv[..., :MLA_NOPE], jnp.broadcast_to(k_pe, (B_, S_, MLA_HEADS, MLA_ROPE))], -1)
    v = kv[..., MLA_NOPE:]
    o = causal_block_attention(q, k, v, (MLA_NOPE + MLA_ROPE) ** -0.5)
    return rms_norm(o.reshape(B_, S_, MLA_W), out_norm_g)


def moe_ffn(h, w_router, b_router, w_gu, b_gu, w_dn, b_dn):
    B_, S_, D = h.shape
    n_tok = B_ * S_
    t = h.reshape(n_tok, D)
    logits = (t @ w_router + b_router).astype(F32)
    top_val, top_idx = lax.top_k(logits, TOP_K)
    gate = jax.nn.softmax(top_val, axis=-1)
    n_assign = n_tok * TOP_K
    e_flat = top_idx.reshape(-1).astype(jnp.int32)
    tok_flat = jnp.repeat(jnp.arange(n_tok, dtype=jnp.int32), TOP_K)
    g_flat = gate.reshape(-1)
    order = jnp.argsort(e_flat)
    e_sorted = e_flat[order]
    counts = jnp.bincount(e_flat, length=N_EXPERTS).astype(jnp.int32)
    padded = (counts + MOE_BLK - 1) // MOE_BLK * MOE_BLK
    start = jnp.cumsum(counts) - counts
    pend = jnp.cumsum(padded)
    pstart = pend - padded
    dest = pstart[e_sorted] + (jnp.arange(n_assign, dtype=jnp.int32) - start[e_sorted])
    n_rows = (n_assign + N_EXPERTS * (MOE_BLK - 1) + MOE_BLK - 1) // MOE_BLK * MOE_BLK
    n_blocks = n_rows // MOE_BLK
    row_tok = jnp.full((n_rows,), n_tok, jnp.int32).at[dest].set(tok_flat[order])
    row_gate = jnp.zeros((n_rows,), F32).at[dest].set(g_flat[order])
    block_e = jnp.minimum(jnp.searchsorted(pend, jnp.arange(n_blocks, dtype=jnp.int32) * MOE_BLK, side='right'), N_EXPERTS - 1)
    t_pad = jnp.concatenate([t, jnp.zeros((1, D), t.dtype)], axis=0)

    def expert_block(args):
        toks, e = args
        xb = t_pad[toks]
        gu = xb @ w_gu[e] + b_gu[e]
        x_glu = jnp.minimum(gu[:, 0::2], SWIGLU_LIMIT)
        x_lin = jnp.clip(gu[:, 1::2], -SWIGLU_LIMIT, SWIGLU_LIMIT)
        act = x_glu * jax.nn.sigmoid(SWIGLU_ALPHA * x_glu) * (x_lin + 1.0)
        return act @ w_dn[e] + b_dn[e]

    yb = lax.map(expert_block, (row_tok.reshape(n_blocks, MOE_BLK), block_e))
    y = jax.ops.segment_sum(yb.reshape(n_rows, D).astype(F32) * row_gate[:, None], row_tok, num_segments=n_tok + 1)[:n_tok]
    return y.reshape(B_, S_, D).astype(h.dtype)


def setup_inputs(seed: int = 0) -> dict:
    key = jax.random.key(seed)
    keys = jax.random.split(key, 40)
    it = iter(range(40))

    def nk():
        return keys[next(it)]

    def nrm(shape, scale):
        return jax.random.normal(nk(), shape, jnp.float32) * scale

    def unif(shape, lo, hi):
        return jax.random.uniform(nk(), shape, jnp.float32, lo, hi)

    def gain(shape):
        return 1.0 + nrm(shape, 0.02)

    L = DEPTH
    Lv = DEPTH - 1
    return {
        "x": nrm((BATCH, SEQ, D_MODEL), 1.0),
        "w_in": nrm((L, D_MODEL, N_IN), D_MODEL ** -0.5),
        "w_out": nrm((L, D_MIX, D_MODEL), D_MIX ** -0.5 * DEEPNORM_BETA),
        "ret_norm_g": gain((L, RET_W)),
        "dil_norm_g": gain((L, DIL_W)),
        "rwkv_mu": unif((L, RWKV_SHIFT_W), 0.0, 1.0),
        "rwkv_w0": unif((L, RWKV_W), -6.0, -0.5),
        "rwkv_w_up": nrm((L, DECAY_LORA, RWKV_W), 0.1),
        "rwkv_a0": nrm((L, RWKV_W), 0.2),
        "rwkv_a_up": nrm((L, AAA_LORA, RWKV_W), 0.5 * AAA_LORA ** -0.5),
        "rwkv_g_up": nrm((L, GATE_LORA, RWKV_W), GATE_LORA ** -0.5),
        "rwkv_k_k": 0.85 + nrm((L, RWKV_W), 0.02),
        "rwkv_k_a": gain((L, RWKV_W)),
        "rwkv_r_k": nrm((L, RWKV_HEADS, RWKV_DH), 0.1),
        "rwkv_ln_g": gain((L, RWKV_W)),
        "rwkv_ln_b": nrm((L, RWKV_W), 0.02),
        "rwkv_vres_down": nrm((Lv, D_MODEL, MV_LORA), D_MODEL ** -0.5),
        "rwkv_vres_mu": unif((Lv, MV_LORA), 0.0, 1.0),
        "rwkv_v0": 0.5 + nrm((Lv, RWKV_W), 0.1),
        "rwkv_v_up": nrm((Lv, MV_LORA, RWKV_W), 0.5 * MV_LORA ** -0.5),
        "mla_q_norm_g": gain((L, Q_LORA)),
        "mla_w_q_up": nrm((L, Q_LORA, MLA_HEADS * (MLA_NOPE + MLA_ROPE)), Q_LORA ** -0.5),
        "mla_kv_norm_g": gain((L, KV_LORA)),
        "mla_w_kv_up": nrm((L, KV_LORA, MLA_HEADS * (MLA_NOPE + MLA_DV)), KV_LORA ** -0.5),
        "mla_out_norm_g": gain((L, MLA_W)),
        "ln1_g": gain((L, D_MODEL)),
        "ln1_b": nrm((L, D_MODEL), 0.02),
        "router_w": nrm((L, D_MODEL, N_EXPERTS), D_MODEL ** -0.5),
        "router_b": nrm((L, N_EXPERTS), 0.01),
        "exp_w_gu": nrm((L, N_EXPERTS, D_MODEL, 2 * D_FF_EXPERT), D_MODEL ** -0.5),
        "exp_b_gu": nrm((L, N_EXPERTS, 2 * D_FF_EXPERT), 0.02),
        "exp_w_dn": nrm((L, N_EXPERTS, D_FF_EXPERT, D_MODEL), D_FF_EXPERT ** -0.5 * DEEPNORM_BETA),
        "exp_b_dn": nrm((L, N_EXPERTS, D_MODEL), 0.02),
        "ln2_g": gain((L, D_MODEL)),
        "ln2_b": nrm((L, D_MODEL), 0.02),
    }


def reference(x, w_in, w_out, ret_norm_g, dil_norm_g, rwkv_mu, rwkv_w0, rwkv_w_up, rwkv_a0, rwkv_a_up,
              rwkv_g_up, rwkv_k_k, rwkv_k_a, rwkv_r_k, rwkv_ln_g, rwkv_ln_b, rwkv_vres_down, rwkv_vres_mu,
              rwkv_v0, rwkv_v_up, mla_q_norm_g, mla_w_q_up, mla_kv_norm_g, mla_w_kv_up, mla_out_norm_g,
              ln1_g, ln1_b, router_w, router_b, exp_w_gu, exp_b_gu, exp_w_dn, exp_b_dn, ln2_g, ln2_b):
    S_ = x.shape[1]
    ret_cs = rope_table(S_, RET_DK, RET_THETA)
    dil_cs = rope_table(S_, ROPE_ROT_DIM, ROPE_THETA)
    mla_cs = rope_table(S_, MLA_ROPE, MLA_THETA)
    v_first = None
    for l in range(DEPTH):
        if l == 0:
            p = x @ w_in[l]
        else:
            p = x @ jnp.concatenate([w_in[l], rwkv_vres_down[l - 1]], axis=1)

        a_q, a_k, a_v, a_g = split_cols(p[..., :A_END], RET_SPLITS)
        o_a = retention_mixer(a_q, a_k, a_v, a_g, ret_norm_g[l], ret_cs)

        b_q, b_k, b_v = split_cols(p[..., A_END:B_END], DIL_SPLITS)
        o_b = dilated_mixer(b_q, b_k, b_v, dil_norm_g[l], dil_cs)

        c_r, c_k, c_v, c_wd, c_ad, c_gd = split_cols(token_shift(p[..., B_END:C_END], rwkv_mu[l]), RWKV_SPLITS)
        if l == 0:
            v_first = c_v
        else:
            vd = token_shift(p[..., N_IN:], rwkv_vres_mu[l - 1])
            c_v = c_v + (v_first - c_v) * jax.nn.sigmoid(rwkv_v0[l - 1] + vd @ rwkv_v_up[l - 1])
        o_c = rwkv7_mixer(c_r, c_k, c_v, c_wd, c_ad, c_gd, rwkv_w0[l], rwkv_w_up[l], rwkv_a0[l], rwkv_a_up[l],
                          rwkv_g_up[l], rwkv_k_k[l], rwkv_k_a[l], rwkv_r_k[l], rwkv_ln_g[l], rwkv_ln_b[l])

        d_cq, d_ckv, d_kr = split_cols(p[..., C_END:N_IN], MLA_SPLITS)
        o_d = mla_mixer(d_cq, d_ckv, d_kr, mla_q_norm_g[l], mla_w_q_up[l], mla_kv_norm_g[l], mla_w_kv_up[l],
                        mla_out_norm_g[l], mla_cs)

        mix = jnp.concatenate([o_a, o_b, o_c, o_d], axis=-1).astype(x.dtype) @ w_out[l]
        x = layer_norm(DEEPNORM_ALPHA * x + mix, ln1_g[l], ln1_b[l])
        ffn = moe_ffn(x, router_w[l], router_b[l], exp_w_gu[l], exp_b_gu[l], exp_w_dn[l], exp_b_dn[l])
        x = layer_norm(DEEPNORM_ALPHA * x + ffn, ln2_g[l], ln2_b[l])
    return x
```

```python
import functools

import numpy as np
import jax
import jax.numpy as jnp
from jax import lax
from jax.experimental import pallas as pl
from jax.experimental.pallas import tpu as pltpu

F32 = jnp.float32

BLK = 128
LN_EPS = 1e-5
NORM_EPS = 1e-6

RET_HEADS, RET_DK, RET_DV = 4, 32, 64
RET_W = RET_HEADS * RET_DV
RET_THETA = 10000.0
DIL_HEADS, DIL_DH = 4, 64
DIL_W = DIL_HEADS * DIL_DH
DIL_PATTERNS = ((128, 1), (512, 4), (2048, 16))
ROPE_THETA = 500000.0
ROPE_ROT_DIM = DIL_DH // 4
RWKV_HEADS, RWKV_DH = 4, 64
RWKV_W = RWKV_HEADS * RWKV_DH
DECAY_LORA, AAA_LORA, MV_LORA, GATE_LORA = 64, 64, 32, 128
RWKV_GN_EPS = 64e-5
MLA_HEADS, MLA_NOPE, MLA_ROPE, MLA_DV = 4, 64, 32, 64
MLA_W = MLA_HEADS * MLA_DV
Q_LORA, KV_LORA = 256, 128
MLA_THETA = 10000.0
N_EXPERTS, TOP_K = 32, 4
SWIGLU_LIMIT, SWIGLU_ALPHA = 7.0, 1.702
MOE_BLK = 256

RET_SPLITS = (RET_HEADS * RET_DK, RET_HEADS * RET_DK, RET_W, RET_W)
DIL_SPLITS = (DIL_W, DIL_W, DIL_W)
RWKV_SPLITS = (RWKV_W, RWKV_W, RWKV_W, DECAY_LORA, AAA_LORA, GATE_LORA)
MLA_SPLITS = (Q_LORA, KV_LORA, MLA_ROPE)
A_END = sum(RET_SPLITS)
B_END = A_END + sum(DIL_SPLITS)
C_END = B_END + sum(RWKV_SPLITS)
N_IN = C_END + sum(MLA_SPLITS)

RWKV_CHUNK = 64

HI = lax.Precision.HIGHEST


def split_cols(p, sizes):
    idx = np.cumsum(sizes)[:-1].tolist()
    return jnp.split(p, idx, axis=-1)


def layer_norm(x, g, b, eps=LN_EPS):
    mu = jnp.mean(x, -1, keepdims=True)
    var = jnp.mean(jnp.square(x - mu), -1, keepdims=True)
    return (x - mu) * lax.rsqrt(var + eps) * g + b


def rms_norm(x, g, eps=NORM_EPS):
    return x * lax.rsqrt(jnp.mean(x * x, -1, keepdims=True) + eps) * g


def rope_table(n_pos, rot_dim, theta):
    inv_freq = 1.0 / (theta ** (jnp.arange(0, rot_dim, 2, dtype=F32) / rot_dim))
    ang = jnp.arange(n_pos, dtype=F32)[:, None] * inv_freq[None, :]
    return jnp.cos(ang), jnp.sin(ang)


def apply_rope(x, cs):
    cos = cs[0][None, :, None, :]
    sin = cs[1][None, :, None, :]
    x1, x2 = jnp.split(x, 2, axis=-1)
    return jnp.concatenate([x1 * cos - x2 * sin, x1 * sin + x2 * cos], -1)


def apply_partial_rope(x, cs, rot_dim):
    return jnp.concatenate([apply_rope(x[..., :rot_dim], cs), x[..., rot_dim:]], -1)


def retention_mixer(q, k, v, g, norm_g, cs):
    B_, S_, _ = q.shape
    n_ch = S_ // BLK
    q = apply_rope(q.reshape(B_, S_, RET_HEADS, RET_DK), cs)
    k = apply_rope(k.reshape(B_, S_, RET_HEADS, RET_DK), cs) * (RET_DK ** -0.5)
    v = v.reshape(B_, S_, RET_HEADS, RET_DV)
    log_gamma = jnp.log(1.0 - 2.0 ** (-5.0 - jnp.arange(RET_HEADS, dtype=F32)))
    idx = jnp.arange(BLK, dtype=F32)
    dist = idx[:, None] - idx[None, :]
    inner_decay = jnp.where(dist >= 0, jnp.exp(jnp.maximum(dist, 0.0)[None] * log_gamma[:, None, None]), 0.0)
    qc = q.reshape(B_, n_ch, BLK, RET_HEADS, RET_DK)
    kc = k.reshape(B_, n_ch, BLK, RET_HEADS, RET_DK)
    vc = v.reshape(B_, n_ch, BLK, RET_HEADS, RET_DV)
    scores = jnp.einsum('bnihd,bnjhd->bnhij', qc, kc) * inner_decay
    o_inner = jnp.einsum('bnhij,bnjhe->bnihe', scores, vc)
    zeta = jnp.exp((BLK - 1 - idx)[None, :] * log_gamma[:, None])
    kv_chunk = jnp.einsum('bnjhd,hj,bnjhe->nbhde', kc, zeta, vc)
    chunk_decay = jnp.exp(BLK * log_gamma)[None, :, None, None]

    def step(state, kv):
        return chunk_decay * state + kv, state

    _, states = lax.scan(step, jnp.zeros_like(kv_chunk[0]), kv_chunk)
    xi = jnp.exp((idx + 1.0)[None, :] * log_gamma[:, None])
    o_cross = jnp.einsum('bnihd,nbhde,hi->bnihe', qc, states, xi)
    o = (o_inner + o_cross).reshape(B_, S_, RET_HEADS, RET_DV)
    o = rms_norm(o, norm_g.reshape(RET_HEADS, RET_DV))
    return jax.nn.silu(g) * o.reshape(B_, S_, RET_W)


def banded_block_attention(q, k, v, n_back):
    G, L, H, dh = q.shape
    nb = L // BLK
    qb = q.reshape(G, nb, BLK, H, dh)

    def with_prev(t):
        t = t.reshape(G, nb, BLK, H, dh)
        prev = jnp.concatenate([jnp.zeros_like(t[:, :1]), t[:, :-1]], axis=1)
        return jnp.concatenate([prev, t], axis=2)

    kk = with_prev(k)
    vv = with_prev(v)
    s = jnp.einsum('gnihd,gnjhd->gnhij', qb, kk) * (dh ** -0.5)
    qpos = BLK + jnp.arange(BLK)
    kpos = jnp.arange(2 * BLK)
    dist = qpos[:, None] - kpos[None, :]
    band = (dist >= 0) & (dist <= n_back)
    first = band & (kpos[None, :] >= BLK)
    mask = jnp.where((jnp.arange(nb) == 0)[:, None, None], first[None], band[None])
    s = jnp.where(mask[None, :, None], s, -jnp.inf)
    lse = jax.nn.logsumexp(s, axis=-1)
    p = jnp.exp(s - lse[..., None])
    o = jnp.einsum('gnhij,gnjhd->gnihd', p, vv)
    return o.reshape(G, L, H, dh), jnp.swapaxes(lse, 2, 3).reshape(G, L, H)


def dilated_mixer(q, k, v, norm_g, cs):
    B_, S_, _ = q.shape
    shp = (B_, S_, DIL_HEADS, DIL_DH)
    q = apply_partial_rope(q.reshape(shp), cs, ROPE_ROT_DIM)
    k = apply_partial_rope(k.reshape(shp), cs, ROPE_ROT_DIM)
    v = v.reshape(shp)
    outs, lses = [], []
    for window, dil in DIL_PATTERNS:
        span = dil * BLK
        s_pad = -(-S_ // span) * span
        m = s_pad // dil

        def to_residues(t):
            t = jnp.pad(t, ((0, 0), (0, s_pad - S_), (0, 0), (0, 0)))
            return t.reshape(B_, m, dil, DIL_HEADS, DIL_DH).transpose(0, 2, 1, 3, 4).reshape(B_ * dil, m, DIL_HEADS, DIL_DH)

        o, lse = banded_block_attention(to_residues(q), to_residues(k), to_residues(v), window // dil)
        o = o.reshape(B_, dil, m, DIL_HEADS, DIL_DH).transpose(0, 2, 1, 3, 4).reshape(B_, s_pad, DIL_HEADS, DIL_DH)[:, :S_]
        lse = lse.reshape(B_, dil, m, DIL_HEADS).transpose(0, 2, 1, 3).reshape(B_, s_pad, DIL_HEADS)[:, :S_]
        outs.append(o)
        lses.append(lse)
    wts = jax.nn.softmax(jnp.stack(lses), axis=0)
    o = jnp.einsum('pbsh,pbshd->bshd', wts, jnp.stack(outs))
    return rms_norm(o.reshape(B_, S_, DIL_W), norm_g)


def token_shift(p, mu):
    prev = jnp.pad(p, ((0, 0), (1, 0), (0, 0)))[:, :-1]
    return p + (prev - p) * mu


def l2_normalize(x, eps=1e-12):
    return x / jnp.maximum(jnp.sqrt(jnp.sum(x * x, -1, keepdims=True)), eps)


def _dot(a, b):
    return jnp.dot(a, b, precision=HI, preferred_element_type=F32)


def _dot_nt(a, b):
    return lax.dot_general(a, b, (((1,), (1,)), ((), ())), precision=HI, preferred_element_type=F32)


def _dot_tn(a, b):
    return lax.dot_general(a, b, (((0,), (0,)), ((), ())), precision=HI, preferred_element_type=F32)


def _rwkv7_chunk_kernel(r_ref, lw_ref, k_ref, v_ref, a_ref, b_ref, y_ref, state_ref, *, heads, dh):
    c_len = r_ref.shape[1]

    @pl.when(pl.program_id(1) == 0)
    def _():
        state_ref[...] = jnp.zeros_like(state_ref)

    row = lax.broadcasted_iota(jnp.int32, (c_len, c_len), 0)
    col = lax.broadcasted_iota(jnp.int32, (c_len, c_len), 1)
    strict = col < row
    incl = col <= row

    logw = lw_ref[0]
    cum = _dot(incl.astype(F32), logw)
    total = cum[c_len - 1:c_len, :]
    e_in = jnp.exp(cum)
    e_ex = jnp.exp(cum - logw)
    e_neg = jnp.exp(-cum)
    e_rem = jnp.exp(total - cum)
    w_total = jnp.exp(total)

    r_hat = r_ref[0] * e_in
    a_hat = a_ref[0] * e_ex
    b_til = b_ref[0] * e_neg
    k_til = k_ref[0] * e_neg
    b_rem = b_ref[0] * e_rem
    k_rem = k_ref[0] * e_rem
    v_all = v_ref[0]

    n_double = int(np.log2(c_len))
    assert 2 ** n_double == c_len
    ys = []
    for h in range(heads):
        sl = slice(h * dh, (h + 1) * dh)
        ah, rh, bt, kt, vh = a_hat[:, sl], r_hat[:, sl], b_til[:, sl], k_til[:, sl], v_all[:, sl]
        l_ab = jnp.where(strict, _dot_nt(ah, bt), 0.0)
        l_ak = jnp.where(strict, _dot_nt(ah, kt), 0.0)
        m_rb = jnp.where(incl, _dot_nt(rh, bt), 0.0)
        m_rk = jnp.where(incl, _dot_nt(rh, kt), 0.0)
        z = jnp.concatenate([ah, _dot(l_ak, vh)], axis=1)
        lp = l_ab
        for m in range(n_double):
            z = z + _dot(lp, z)
            if m + 1 < n_double:
                lp = _dot(lp, lp)
        p_mat, q_mat = z[:, :dh], z[:, dh:]
        s0 = state_ref[h]
        u = _dot_nt(p_mat, s0) + q_mat
        y = _dot_nt(rh, s0) + _dot(m_rb, u) + _dot(m_rk, vh)
        state_ref[h] = s0 * w_total[:, sl] + _dot_tn(u, b_rem[:, sl]) + _dot_tn(vh, k_rem[:, sl])
        ys.append(y)
    y_ref[0] = jnp.concatenate(ys, axis=1)


def rwkv7_scan(r, logw, k, v, a, b, heads=RWKV_HEADS, dh=RWKV_DH, chunk=RWKV_CHUNK):
    B_, S_, W_ = r.shape
    assert W_ == heads * dh and S_ % chunk == 0
    spec = pl.BlockSpec((1, chunk, W_), lambda bi, ci: (bi, ci, 0))
    return pl.pallas_call(
        functools.partial(_rwkv7_chunk_kernel, heads=heads, dh=dh),
        grid=(B_, S_ // chunk),
        in_specs=[spec] * 6,
        out_specs=spec,
        out_shape=jax.ShapeDtypeStruct((B_, S_, W_), F32),
        scratch_shapes=[pltpu.VMEM((heads, dh, dh), F32)],
        compiler_params=pltpu.CompilerParams(dimension_semantics=("arbitrary", "arbitrary")),
        name="rwkv7_chunk_scan",
    )(r, logw, k, v, a, b)


def rwkv7_mixer(r, k, v, wd, ad, gd, w0, w_up, a0, a_up, g_up, k_k, k_a, r_k, ln_g, ln_b):
    B_, S_, _ = r.shape
    hs = (B_, S_, RWKV_HEADS, RWKV_DH)
    w = -jax.nn.softplus(-(w0 + jnp.tanh(wd) @ w_up)) - 0.5
    logw = -jnp.exp(w)
    a = jax.nn.sigmoid(a0 + ad @ a_up)
    g = jax.nn.sigmoid(gd) @ g_up
    kk = l2_normalize((k * k_k).reshape(hs)).reshape(B_, S_, RWKV_W)
    k = k * (1.0 + (a - 1.0) * k_a)
    y = rwkv7_scan(r, logw, k, v, -kk, kk * a)
    y = layer_norm(y.reshape(hs), ln_g.reshape(RWKV_HEADS, RWKV_DH), ln_b.reshape(RWKV_HEADS, RWKV_DH), RWKV_GN_EPS)
    rf, kf, vf = r.reshape(hs), k.reshape(hs), v.reshape(hs)
    y = y + jnp.sum(rf * kf * r_k, -1, keepdims=True) * vf
    return y.reshape(B_, S_, RWKV_W) * g


def causal_block_attention(q, k, v, scale):
    B_, S_, H, dq = q.shape
    nb = S_ // BLK
    qb = jnp.moveaxis(q.reshape(B_, nb, BLK, H, dq), 1, 0)
    kpos = jnp.arange(S_)

    def one_block(args):
        q_blk, blk = args
        s = jnp.einsum('bqhd,bkhd->bhqk', q_blk, k) * scale
        qpos = blk * BLK + jnp.arange(BLK)
        s = jnp.where(kpos[None, :] <= qpos[:, None], s, -jnp.inf)
        p = jax.nn.softmax(s, axis=-1)
        return jnp.einsum('bhqk,bkhd->bqhd', p, v)

    o = lax.map(one_block, (qb, jnp.arange(nb)))
    return jnp.moveaxis(o, 0, 1).reshape(B_, S_, H, v.shape[-1])


def mla_mixer(c_q, c_kv, k_rope, q_norm_g, w_q_up, kv_norm_g, w_kv_up, out_norm_g, cs):
    B_, S_, _ = c_q.shape
    q = (rms_norm(c_q, q_norm_g) @ w_q_up).reshape(B_, S_, MLA_HEADS, MLA_NOPE + MLA_ROPE)
    kv = (rms_norm(c_kv, kv_norm_g) @ w_kv_up).reshape(B_, S_, MLA_HEADS, MLA_NOPE + MLA_DV)
    q = jnp.concatenate([q[..., :MLA_NOPE], apply_rope(q[..., MLA_NOPE:], cs)], -1)
    k_pe = apply_rope(k_rope[:, :, None, :], cs)
    k = jnp.concatenate([kv[..., :MLA_NOPE], jnp.broadcast_to(k_pe, (B_, S_, MLA_HEADS, MLA_ROPE))], -1)
    v = kv[..., MLA_NOPE:]
    o = causal_block_attention(q, k, v, (MLA_NOPE + MLA_ROPE) ** -0.5)
    return rms_norm(o.reshape(B_, S_, MLA_W), out_norm_g)


def moe_ffn(h, w_router, b_router, w_gu, b_gu, w_dn, b_dn):
    B_, S_, D = h.shape
    n_tok = B_ * S_
    t = h.reshape(n_tok, D)
    logits = t @ w_router + b_router
    top_val, top_idx = lax.top_k(logits, TOP_K)
    gate = jax.nn.softmax(top_val, axis=-1)
    n_assign = n_tok * TOP_K
    e_flat = top_idx.reshape(-1).astype(jnp.int32)
    tok_flat = jnp.repeat(jnp.arange(n_tok, dtype=jnp.int32), TOP_K)
    g_flat = gate.reshape(-1)
    order = jnp.argsort(e_flat)
    e_sorted = e_flat[order]
    counts = jnp.bincount(e_flat, length=N_EXPERTS).astype(jnp.int32)
    padded = (counts + MOE_BLK - 1) // MOE_BLK * MOE_BLK
    start = jnp.cumsum(counts) - counts
    pend = jnp.cumsum(padded)
    pstart = pend - padded
    dest = pstart[e_sorted] + (jnp.arange(n_assign, dtype=jnp.int32) - start[e_sorted])
    n_rows = (n_assign + N_EXPERTS * (MOE_BLK - 1) + MOE_BLK - 1) // MOE_BLK * MOE_BLK
    n_blocks = n_rows // MOE_BLK
    row_tok = jnp.full((n_rows,), n_tok, jnp.int32).at[dest].set(tok_flat[order])
    row_gate = jnp.zeros((n_rows,), F32).at[dest].set(g_flat[order])
    block_e = jnp.minimum(jnp.searchsorted(pend, jnp.arange(n_blocks, dtype=jnp.int32) * MOE_BLK, side='right'), N_EXPERTS - 1)
    t_pad = jnp.concatenate([t, jnp.zeros((1, D), t.dtype)], axis=0)

    def expert_block(args):
        toks, e = args
        xb = t_pad[toks]
        gu = xb @ w_gu[e] + b_gu[e]
        x_glu = jnp.minimum(gu[:, 0::2], SWIGLU_LIMIT)
        x_lin = jnp.clip(gu[:, 1::2], -SWIGLU_LIMIT, SWIGLU_LIMIT)
        act = x_glu * jax.nn.sigmoid(SWIGLU_ALPHA * x_glu) * (x_lin + 1.0)
        return act @ w_dn[e] + b_dn[e]

    yb = lax.map(expert_block, (row_tok.reshape(n_blocks, MOE_BLK), block_e))
    y = jax.ops.segment_sum(yb.reshape(n_rows, D) * row_gate[:, None], row_tok, num_segments=n_tok + 1)[:n_tok]
    return y.reshape(B_, S_, D)


def kernel(x, w_in, w_out, ret_norm_g, dil_norm_g, rwkv_mu, rwkv_w0, rwkv_w_up, rwkv_a0, rwkv_a_up, rwkv_g_up, rwkv_k_k, rwkv_k_a, rwkv_r_k, rwkv_ln_g, rwkv_ln_b, rwkv_vres_down, rwkv_vres_mu, rwkv_v0, rwkv_v_up, mla_q_norm_g, mla_w_q_up, mla_kv_norm_g, mla_w_kv_up, mla_out_norm_g, ln1_g, ln1_b, router_w, router_b, exp_w_gu, exp_b_gu, exp_w_dn, exp_b_dn, ln2_g, ln2_b):
    depth = w_in.shape[0]
    S_ = x.shape[1]
    alpha = (2 * depth) ** 0.25
    ret_cs = rope_table(S_, RET_DK, RET_THETA)
    dil_cs = rope_table(S_, ROPE_ROT_DIM, ROPE_THETA)
    mla_cs = rope_table(S_, MLA_ROPE, MLA_THETA)
    v_first = None
    for l in range(depth):
        if l == 0:
            p = x @ w_in[l]
        else:
            p = x @ jnp.concatenate([w_in[l], rwkv_vres_down[l - 1]], axis=1)

        a_q, a_k, a_v, a_g = split_cols(p[..., :A_END], RET_SPLITS)
        o_a = retention_mixer(a_q, a_k, a_v, a_g, ret_norm_g[l], ret_cs)

        b_q, b_k, b_v = split_cols(p[..., A_END:B_END], DIL_SPLITS)
        o_b = dilated_mixer(b_q, b_k, b_v, dil_norm_g[l], dil_cs)

        c_r, c_k, c_v, c_wd, c_ad, c_gd = split_cols(token_shift(p[..., B_END:C_END], rwkv_mu[l]), RWKV_SPLITS)
        if l == 0:
            v_first = c_v
        else:
            vd = token_shift(p[..., N_IN:], rwkv_vres_mu[l - 1])
            c_v = c_v + (v_first - c_v) * jax.nn.sigmoid(rwkv_v0[l - 1] + vd @ rwkv_v_up[l - 1])
        o_c = rwkv7_mixer(c_r, c_k, c_v, c_wd, c_ad, c_gd, rwkv_w0[l], rwkv_w_up[l], rwkv_a0[l], rwkv_a_up[l],
                          rwkv_g_up[l], rwkv_k_k[l], rwkv_k_a[l], rwkv_r_k[l], rwkv_ln_g[l], rwkv_ln_b[l])

        d_cq, d_ckv, d_kr = split_cols(p[..., C_END:N_IN], MLA_SPLITS)
        o_d = mla_mixer(d_cq, d_ckv, d_kr, mla_q_norm_g[l], mla_w_q_up[l], mla_kv_norm_g[l], mla_w_kv_up[l],
                        mla_out_norm_g[l], mla_cs)

        mix = jnp.concatenate([o_a, o_b, o_c, o_d], axis=-1) @ w_out[l]
        x = layer_norm(alpha * x + mix, ln1_g[l], ln1_b[l])
        ffn = moe_ffn(x, router_w[l], router_b[l], exp_w_gu[l], exp_b_gu[l], exp_w_dn[l], exp_b_dn[l])
        x = layer_norm(alpha * x + ffn, ln2_g[l], ln2_b[l])
    return x
```

```python
import functools

import numpy as np
import jax
import jax.numpy as jnp
from jax import lax
from jax.experimental import pallas as pl
from jax.experimental.pallas import tpu as pltpu

F32 = jnp.float32

BLK = 128
LN_EPS = 1e-5
NORM_EPS = 1e-6

RET_HEADS, RET_DK, RET_DV = 4, 32, 64
RET_W = RET_HEADS * RET_DV
RET_THETA = 10000.0
DIL_HEADS, DIL_DH = 4, 64
DIL_W = DIL_HEADS * DIL_DH
DIL_PATTERNS = ((128, 1), (512, 4), (2048, 16))
ROPE_THETA = 500000.0
ROPE_ROT_DIM = DIL_DH // 4
RWKV_HEADS, RWKV_DH = 4, 64
RWKV_W = RWKV_HEADS * RWKV_DH
DECAY_LORA, AAA_LORA, MV_LORA, GATE_LORA = 64, 64, 32, 128
RWKV_GN_EPS = 64e-5
MLA_HEADS, MLA_NOPE, MLA_ROPE, MLA_DV = 4, 64, 32, 64
MLA_W = MLA_HEADS * MLA_DV
Q_LORA, KV_LORA = 256, 128
MLA_THETA = 10000.0
N_EXPERTS, TOP_K = 32, 4
SWIGLU_LIMIT, SWIGLU_ALPHA = 7.0, 1.702
MOE_BLK = 256

RET_SPLITS = (RET_HEADS * RET_DK, RET_HEADS * RET_DK, RET_W, RET_W)
DIL_SPLITS = (DIL_W, DIL_W, DIL_W)
RWKV_SPLITS = (RWKV_W, RWKV_W, RWKV_W, DECAY_LORA, AAA_LORA, GATE_LORA)
MLA_SPLITS = (Q_LORA, KV_LORA, MLA_ROPE)
A_END = sum(RET_SPLITS)
B_END = A_END + sum(DIL_SPLITS)
C_END = B_END + sum(RWKV_SPLITS)
N_IN = C_END + sum(MLA_SPLITS)

RWKV_CHUNK = 64

HI = lax.Precision.HIGHEST


def split_cols(p, sizes):
    idx = np.cumsum(sizes)[:-1].tolist()
    return jnp.split(p, idx, axis=-1)


def layer_norm(x, g, b, eps=LN_EPS):
    mu = jnp.mean(x, -1, keepdims=True)
    var = jnp.mean(jnp.square(x - mu), -1, keepdims=True)
    return (x - mu) * lax.rsqrt(var + eps) * g + b


def rms_norm(x, g, eps=NORM_EPS):
    return x * lax.rsqrt(jnp.mean(x * x, -1, keepdims=True) + eps) * g


def rope_table(n_pos, rot_dim, theta):
    inv_freq = 1.0 / (theta ** (jnp.arange(0, rot_dim, 2, dtype=F32) / rot_dim))
    ang = jnp.arange(n_pos, dtype=F32)[:, None] * inv_freq[None, :]
    return jnp.cos(ang), jnp.sin(ang)


def apply_rope(x, cs):
    cos = cs[0][None, :, None, :]
    sin = cs[1][None, :, None, :]
    x1, x2 = jnp.split(x, 2, axis=-1)
    return jnp.concatenate([x1 * cos - x2 * sin, x1 * sin + x2 * cos], -1)


def apply_partial_rope(x, cs, rot_dim):
    return jnp.concatenate([apply_rope(x[..., :rot_dim], cs), x[..., rot_dim:]], -1)


def retention_mixer(q, k, v, g, norm_g, cs):
    B_, S_, _ = q.shape
    n_ch = S_ // BLK
    q = apply_rope(q.reshape(B_, S_, RET_HEADS, RET_DK), cs)
    k = apply_rope(k.reshape(B_, S_, RET_HEADS, RET_DK), cs) * (RET_DK ** -0.5)
    v = v.reshape(B_, S_, RET_HEADS, RET_DV)
    log_gamma = jnp.log(1.0 - 2.0 ** (-5.0 - jnp.arange(RET_HEADS, dtype=F32)))
    idx = jnp.arange(BLK, dtype=F32)
    dist = idx[:, None] - idx[None, :]
    inner_decay = jnp.where(dist >= 0, jnp.exp(jnp.maximum(dist, 0.0)[None] * log_gamma[:, None, None]), 0.0)
    qc = q.reshape(B_, n_ch, BLK, RET_HEADS, RET_DK)
    kc = k.reshape(B_, n_ch, BLK, RET_HEADS, RET_DK)
    vc = v.reshape(B_, n_ch, BLK, RET_HEADS, RET_DV)
    scores = jnp.einsum('bnihd,bnjhd->bnhij', qc, kc) * inner_decay
    o_inner = jnp.einsum('bnhij,bnjhe->bnihe', scores, vc)
    zeta = jnp.exp((BLK - 1 - idx)[None, :] * log_gamma[:, None])
    kv_chunk = jnp.einsum('bnjhd,hj,bnjhe->nbhde', kc, zeta, vc)
    chunk_decay = jnp.exp(BLK * log_gamma)[None, :, None, None]

    def step(state, kv):
        return chunk_decay * state + kv, state

    _, states = lax.scan(step, jnp.zeros_like(kv_chunk[0]), kv_chunk)
    xi = jnp.exp((idx + 1.0)[None, :] * log_gamma[:, None])
    o_cross = jnp.einsum('bnihd,nbhde,hi->bnihe', qc, states, xi)
    o = (o_inner + o_cross).reshape(B_, S_, RET_HEADS, RET_DV)
    o = rms_norm(o, norm_g.reshape(RET_HEADS, RET_DV))
    return jax.nn.silu(g) * o.reshape(B_, S_, RET_W)


def banded_block_attention(q, k, v, n_back):
    G, L, H, dh = q.shape
    nb = L // BLK
    qb = q.reshape(G, nb, BLK, H, dh)

    def with_prev(t):
        t = t.reshape(G, nb, BLK, H, dh)
        prev = jnp.concatenate([jnp.zeros_like(t[:, :1]), t[:, :-1]], axis=1)
        return jnp.concatenate([prev, t], axis=2)

    kk = with_prev(k)
    vv = with_prev(v)
    s = jnp.einsum('gnihd,gnjhd->gnhij', qb, kk) * (dh ** -0.5)
    qpos = BLK + jnp.arange(BLK)
    kpos = jnp.arange(2 * BLK)
    dist = qpos[:, None] - kpos[None, :]
    band = (dist >= 0) & (dist <= n_back)
    first = band & (kpos[None, :] >= BLK)
    mask = jnp.where((jnp.arange(nb) == 0)[:, None, None], first[None], band[None])
    s = jnp.where(mask[None, :, None], s, -jnp.inf)
    lse = jax.nn.logsumexp(s, axis=-1)
    p = jnp.exp(s - lse[..., None])
    o = jnp.einsum('gnhij,gnjhd->gnihd', p, vv)
    return o.reshape(G, L, H, dh), jnp.swapaxes(lse, 2, 3).reshape(G, L, H)


def dilated_mixer(q, k, v, norm_g, cs):
    B_, S_, _ = q.shape
    shp = (B_, S_, DIL_HEADS, DIL_DH)
    q = apply_partial_rope(q.reshape(shp), cs, ROPE_ROT_DIM)
    k = apply_partial_rope(k.reshape(shp), cs, ROPE_ROT_DIM)
    v = v.reshape(shp)
    outs, lses = [], []
    for window, dil in DIL_PATTERNS:
        span = dil * BLK
        s_pad = -(-S_ // span) * span
        m = s_pad // dil

        def to_residues(t):
            t = jnp.pad(t, ((0, 0), (0, s_pad - S_), (0, 0), (0, 0)))
            return t.reshape(B_, m, dil, DIL_HEADS, DIL_DH).transpose(0, 2, 1, 3, 4).reshape(B_ * dil, m, DIL_HEADS, DIL_DH)

        o, lse = banded_block_attention(to_residues(q), to_residues(k), to_residues(v), window // dil)
        o = o.reshape(B_, dil, m, DIL_HEADS, DIL_DH).transpose(0, 2, 1, 3, 4).reshape(B_, s_pad, DIL_HEADS, DIL_DH)[:, :S_]
        lse = lse.reshape(B_, dil, m, DIL_HEADS).transpose(0, 2, 1, 3).reshape(B_, s_pad, DIL_HEADS)[:, :S_]
        outs.append(o)
        lses.append(lse)
    wts = jax.nn.softmax(jnp.stack(lses), axis=0)
    o = jnp.einsum('pbsh,pbshd->bshd', wts, jnp.stack(outs))
    return rms_norm(o.reshape(B_, S_, DIL_W), norm_g)


def token_shift(p, mu):
    prev = jnp.pad(p, ((0, 0), (1, 0), (0, 0)))[:, :-1]
    return p + (prev - p) * mu


def l2_normalize(x, eps=1e-12):
    return x / jnp.maximum(jnp.sqrt(jnp.sum(x * x, -1, keepdims=True)), eps)


def _dot(a, b):
    return jnp.dot(a, b, precision=HI, preferred_element_type=F32)


def _dot_nt(a, b):
    return lax.dot_general(a, b, (((1,), (1,)), ((), ())), precision=HI, preferred_element_type=F32)


def _dot_tn(a, b):
    return lax.dot_general(a, b, (((0,), (0,)), ((), ())), precision=HI, preferred_element_type=F32)


def _rwkv7_chunk_kernel(r_ref, lw_ref, k_ref, v_ref, a_ref, b_ref, y_ref, state_ref, *, heads, dh):
    c_len = r_ref.shape[1]

    @pl.when(pl.program_id(1) == 0)
    def _():
        state_ref[...] = jnp.zeros_like(state_ref)

    row = lax.broadcasted_iota(jnp.int32, (c_len, c_len), 0)
    col = lax.broadcasted_iota(jnp.int32, (c_len, c_len), 1)
    strict = col < row
    incl = col <= row

    logw = lw_ref[0]
    cum = _dot(incl.astype(F32), logw)
    total = cum[c_len - 1:c_len, :]
    e_in = jnp.exp(cum)
    e_ex = jnp.exp(cum - logw)
    e_neg = jnp.exp(-cum)
    e_rem = jnp.exp(total - cum)
    w_total = jnp.exp(total)

    r_hat = r_ref[0] * e_in
    a_hat = a_ref[0] * e_ex
    b_til = b_ref[0] * e_neg
    k_til = k_ref[0] * e_neg
    b_rem = b_ref[0] * e_rem
    k_rem = k_ref[0] * e_rem
    v_all = v_ref[0]

    n_double = int(np.log2(c_len))
    assert 2 ** n_double == c_len
    ys = []
    for h in range(heads):
        sl = slice(h * dh, (h + 1) * dh)
        ah, rh, bt, kt, vh = a_hat[:, sl], r_hat[:, sl], b_til[:, sl], k_til[:, sl], v_all[:, sl]
        l_ab = jnp.where(strict, _dot_nt(ah, bt), 0.0)
        l_ak = jnp.where(strict, _dot_nt(ah, kt), 0.0)
        m_rb = jnp.where(incl, _dot_nt(rh, bt), 0.0)
        m_rk = jnp.where(incl, _dot_nt(rh, kt), 0.0)
        z = jnp.concatenate([ah, _dot(l_ak, vh)], axis=1)
        lp = l_ab
        for m in range(n_double):
            z = z + _dot(lp, z)
            if m + 1 < n_double:
                lp = _dot(lp, lp)
        p_mat, q_mat = z[:, :dh], z[:, dh:]
        s0 = state_ref[h]
        u = _dot_nt(p_mat, s0) + q_mat
        y = _dot_nt(rh, s0) + _dot(m_rb, u) + _dot(m_rk, vh)
        state_ref[h] = s0 * w_total[:, sl] + _dot_tn(u, b_rem[:, sl]) + _dot_tn(vh, k_rem[:, sl])
        ys.append(y)
    y_ref[0] = jnp.concatenate(ys, axis=1)


def rwkv7_scan(r, logw, k, v, a, b, heads=RWKV_HEADS, dh=RWKV_DH, chunk=RWKV_CHUNK):
    B_, S_, W_ = r.shape
    assert W_ == heads * dh and S_ % chunk == 0
    spec = pl.BlockSpec((1, chunk, W_), lambda bi, ci: (bi, ci, 0))
    return pl.pallas_call(
        functools.partial(_rwkv7_chunk_kernel, heads=heads, dh=dh),
        grid=(B_, S_ // chunk),
        in_specs=[spec] * 6,
        out_specs=spec,
        out_shape=jax.ShapeDtypeStruct((B_, S_, W_), F32),
        scratch_shapes=[pltpu.VMEM((heads, dh, dh), F32)],
        compiler_params=pltpu.CompilerParams(dimension_semantics=("arbitrary", "arbitrary")),
        name="rwkv7_chunk_scan",
    )(r, logw, k, v, a, b)


def rwkv7_mixer(r, k, v, wd, ad, gd, w0, w_up, a0, a_up, g_up, k_k, k_a, r_k, ln_g, ln_b):
    B_, S_, _ = r.shape
    hs = (B_, S_, RWKV_HEADS, RWKV_DH)
    w = -jax.nn.softplus(-(w0 + jnp.tanh(wd) @ w_up)) - 0.5
    logw = -jnp.exp(w)
    a = jax.nn.sigmoid(a0 + ad @ a_up)
    g = jax.nn.sigmoid(gd) @ g_up
    kk = l2_normalize((k * k_k).reshape(hs)).reshape(B_, S_, RWKV_W)
    k = k * (1.0 + (a - 1.0) * k_a)
    y = rwkv7_scan(r, logw, k, v, -kk, kk * a)
    y = layer_norm(y.reshape(hs), ln_g.reshape(RWKV_HEADS, RWKV_DH), ln_b.reshape(RWKV_HEADS, RWKV_DH), RWKV_GN_EPS)
    rf, kf, vf = r.reshape(hs), k.reshape(hs), v.reshape(hs)
    y = y + jnp.sum(rf * kf * r_k, -1, keepdims=True) * vf
    return y.reshape(B_, S_, RWKV_W) * g


def _causal_flash_kernel(q_ref, k_ref, v_ref, o_ref, *, tile):
    qi = pl.program_id(1)
    q = q_ref[0]
    dv = v_ref.shape[2]

    def kv_tile(j):
        start = pl.multiple_of(j * tile, tile)
        return k_ref[0, pl.ds(start, tile), :], v_ref[0, pl.ds(start, tile), :]

    def update(carry, s, v_t):
        m, l, acc = carry
        m_new = jnp.maximum(m, jnp.max(s, axis=-1, keepdims=True))
        corr = jnp.exp(m - m_new)
        p = jnp.exp(s - m_new)
        l_new = corr * l + jnp.sum(p, axis=-1, keepdims=True)
        acc_new = corr * acc + jnp.dot(p.astype(v_t.dtype), v_t, preferred_element_type=F32)
        return m_new, l_new, acc_new

    def body(j, carry):
        k_t, v_t = kv_tile(j)
        s = lax.dot_general(q, k_t, (((1,), (1,)), ((), ())), preferred_element_type=F32)
        return update(carry, s, v_t)

    init = (jnp.full((tile, 1), -jnp.inf, F32), jnp.zeros((tile, 1), F32), jnp.zeros((tile, dv), F32))
    carry = lax.fori_loop(0, qi, body, init)
    k_t, v_t = kv_tile(qi)
    s = lax.dot_general(q, k_t, (((1,), (1,)), ((), ())), preferred_element_type=F32)
    row = lax.broadcasted_iota(jnp.int32, (tile, tile), 0)
    col = lax.broadcasted_iota(jnp.int32, (tile, tile), 1)
    s = jnp.where(col <= row, s, -jnp.inf)
    m, l, acc = update(carry, s, v_t)
    o_ref[0] = acc / l


ATTN_TILE = 512


def causal_attention(q, k, v, scale, tile=ATTN_TILE):
    B_, S_, H, dq = q.shape
    dv = v.shape[-1]
    tile = min(tile, S_)
    assert S_ % tile == 0

    def heads_major(t):
        return jnp.moveaxis(t, 2, 1).reshape(B_ * H, S_, t.shape[-1]).astype(jnp.bfloat16)

    qh, kh, vh = heads_major(q * scale), heads_major(k), heads_major(v)
    o = pl.pallas_call(
        functools.partial(_causal_flash_kernel, tile=tile),
        grid=(B_ * H, S_ // tile),
        in_specs=[pl.BlockSpec((1, tile, dq), lambda g, i: (g, i, 0)),
                  pl.BlockSpec((1, S_, dq), lambda g, i: (g, 0, 0)),
                  pl.BlockSpec((1, S_, dv), lambda g, i: (g, 0, 0))],
        out_specs=pl.BlockSpec((1, tile, dv), lambda g, i: (g, i, 0)),
        out_shape=jax.ShapeDtypeStruct((B_ * H, S_, dv), F32),
        compiler_params=pltpu.CompilerParams(dimension_semantics=("arbitrary", "arbitrary")),
        name="mla_causal_flash",
    )(qh, kh, vh)
    return jnp.moveaxis(o.reshape(B_, H, S_, dv), 1, 2)


def mla_mixer(c_q, c_kv, k_rope, q_norm_g, w_q_up, kv_norm_g, w_kv_up, out_norm_g, cs):
    B_, S_, _ = c_q.shape
    q = (rms_norm(c_q, q_norm_g) @ w_q_up).reshape(B_, S_, MLA_HEADS, MLA_NOPE + MLA_ROPE)
    kv = (rms_norm(c_kv, kv_norm_g) @ w_kv_up).reshape(B_, S_, MLA_HEADS, MLA_NOPE + MLA_DV)
    q = jnp.concatenate([q[..., :MLA_NOPE], apply_rope(q[..., MLA_NOPE:], cs)], -1)
    k_pe = apply_rope(k_rope[:, :, None, :], cs)
    k = jnp.concatenate([kv[..., :MLA_NOPE], jnp.broadcast_to(k_pe, (B_, S_, MLA_HEADS, MLA_ROPE))], -1)
    v = kv[..., MLA_NOPE:]
    o = causal_attention(q, k, v, (MLA_NOPE + MLA_ROPE) ** -0.5)
    return rms_norm(o.reshape(B_, S_, MLA_W), out_norm_g)


def _moe_expert_kernel(tile_e_ref, n_used_ref, x_ref, wg_ref, wl_ref, bg_ref, bl_ref, wd_ref, bd_ref, o_ref):
    del tile_e_ref
    i = pl.program_id(0)

    @pl.when(i < n_used_ref[0])
    def _():
        x = x_ref[...]
        glu = jnp.dot(x, wg_ref[0], preferred_element_type=F32) + bg_ref[0]
        lin = jnp.dot(x, wl_ref[0], preferred_element_type=F32) + bl_ref[0]
        glu = jnp.minimum(glu, SWIGLU_LIMIT)
        lin = jnp.clip(lin, -SWIGLU_LIMIT, SWIGLU_LIMIT)
        act = glu * jax.nn.sigmoid(SWIGLU_ALPHA * glu) * (lin + 1.0)
        o_ref[...] = jnp.dot(act.astype(wd_ref.dtype), wd_ref[0], preferred_element_type=F32) + bd_ref[0]

    @pl.when(i >= n_used_ref[0])
    def _():
        o_ref[...] = jnp.zeros_like(o_ref)


MOE_TILE = 512
MOE_VMEM_LIMIT = 48 * 1024 * 1024


def moe_experts(xs, tile_e, n_used, w_glu, w_lin, b_glu, b_lin, w_dn, b_dn, tile):
    n_rows, D = xs.shape
    E, _, F = w_glu.shape
    w_spec = lambda shape: pl.BlockSpec((1,) + shape, lambda i, te, nu: (te[i], 0, 0))
    return pl.pallas_call(
        _moe_expert_kernel,
        grid_spec=pltpu.PrefetchScalarGridSpec(
            num_scalar_prefetch=2,
            grid=(n_rows // tile,),
            in_specs=[pl.BlockSpec((tile, D), lambda i, te, nu: (i, 0)),
                      w_spec((D, F)), w_spec((D, F)), w_spec((1, F)), w_spec((1, F)),
                      w_spec((F, D)), w_spec((1, D))],
            out_specs=pl.BlockSpec((tile, D), lambda i, te, nu: (i, 0)),
        ),
        out_shape=jax.ShapeDtypeStruct((n_rows, D), F32),
        compiler_params=pltpu.CompilerParams(dimension_semantics=("arbitrary",), vmem_limit_bytes=MOE_VMEM_LIMIT),
        name="moe_expert_ffn",
    )(tile_e, n_used, xs, w_glu, w_lin, b_glu.reshape(E, 1, F), b_lin.reshape(E, 1, F), w_dn, b_dn.reshape(E, 1, D))


def moe_ffn(h, w_router, b_router, w_gu, b_gu, w_dn, b_dn, tile=MOE_TILE):
    B_, S_, D = h.shape
    n_exp = w_router.shape[1]
    n_tok = B_ * S_
    t = h.reshape(n_tok, D)
    logits = t @ w_router + b_router
    top_val, top_idx = lax.top_k(logits, TOP_K)
    gate = jax.nn.softmax(top_val, axis=-1)
    n_assign = n_tok * TOP_K
    e_flat = top_idx.reshape(-1).astype(jnp.int32)
    e_sorted, order = lax.sort((e_flat, jnp.arange(n_assign, dtype=jnp.int32)), num_keys=1)
    counts = jnp.sum((e_flat[:, None] == jnp.arange(n_exp, dtype=jnp.int32)[None, :]).astype(jnp.int32), axis=0)
    padded = (counts + tile - 1) // tile * tile
    start = jnp.cumsum(counts) - counts
    pend = jnp.cumsum(padded)
    pstart = pend - padded
    n_tiles = (n_assign + n_exp * (tile - 1) + tile - 1) // tile
    n_rows = n_tiles * tile
    tile_e = jnp.minimum(jnp.searchsorted(pend, jnp.arange(n_tiles, dtype=jnp.int32) * tile, side='right'),
                         n_exp - 1).astype(jnp.int32)
    n_used = (pend[-1] // tile).astype(jnp.int32).reshape(1)
    row = jnp.arange(n_rows, dtype=jnp.int32)
    row_e = jnp.repeat(tile_e, tile)
    within = row - pstart[row_e]
    valid = within < counts[row_e]
    row_assign = order[jnp.clip(start[row_e] + within, 0, n_assign - 1)]
    row_tok = jnp.where(valid, row_assign // TOP_K, 0)
    dest = pstart[e_sorted] + (jnp.arange(n_assign, dtype=jnp.int32) - start[e_sorted])
    _, pos = lax.sort((order, dest), num_keys=1)
    xs = t.astype(jnp.bfloat16)[row_tok]
    bf = jnp.bfloat16
    yb = moe_experts(xs, tile_e, n_used, w_gu[:, :, 0::2].astype(bf), w_gu[:, :, 1::2].astype(bf),
                     b_gu[:, 0::2], b_gu[:, 1::2], w_dn.astype(bf), b_dn, tile)
    y = jnp.sum(yb[pos].reshape(n_tok, TOP_K, D) * gate[:, :, None], axis=1)
    return y.reshape(B_, S_, D)


def kernel(x, w_in, w_out, ret_norm_g, dil_norm_g, rwkv_mu, rwkv_w0, rwkv_w_up, rwkv_a0, rwkv_a_up, rwkv_g_up, rwkv_k_k, rwkv_k_a, rwkv_r_k, rwkv_ln_g, rwkv_ln_b, rwkv_vres_down, rwkv_vres_mu, rwkv_v0, rwkv_v_up, mla_q_norm_g, mla_w_q_up, mla_kv_norm_g, mla_w_kv_up, mla_out_norm_g, ln1_g, ln1_b, router_w, router_b, exp_w_gu, exp_b_gu, exp_w_dn, exp_b_dn, ln2_g, ln2_b):
    depth = w_in.shape[0]
    S_ = x.shape[1]
    alpha = (2 * depth) ** 0.25
    ret_cs = rope_table(S_, RET_DK, RET_THETA)
    dil_cs = rope_table(S_, ROPE_ROT_DIM, ROPE_THETA)
    mla_cs = rope_table(S_, MLA_ROPE, MLA_THETA)
    v_first = None
    for l in range(depth):
        if l == 0:
            p = x @ w_in[l]
        else:
            p = x @ jnp.concatenate([w_in[l], rwkv_vres_down[l - 1]], axis=1)

        a_q, a_k, a_v, a_g = split_cols(p[..., :A_END], RET_SPLITS)
        o_a = retention_mixer(a_q, a_k, a_v, a_g, ret_norm_g[l], ret_cs)

        b_q, b_k, b_v = split_cols(p[..., A_END:B_END], DIL_SPLITS)
        o_b = dilated_mixer(b_q, b_k, b_v, dil_norm_g[l], dil_cs)

        c_r, c_k, c_v, c_wd, c_ad, c_gd = split_cols(token_shift(p[..., B_END:C_END], rwkv_mu[l]), RWKV_SPLITS)
        if l == 0:
            v_first = c_v
        else:
            vd = token_shift(p[..., N_IN:], rwkv_vres_mu[l - 1])
            c_v = c_v + (v_first - c_v) * jax.nn.sigmoid(rwkv_v0[l - 1] + vd @ rwkv_v_up[l - 1])
        o_c = rwkv7_mixer(c_r, c_k, c_v, c_wd, c_ad, c_gd, rwkv_w0[l], rwkv_w_up[l], rwkv_a0[l], rwkv_a_up[l],
                          rwkv_g_up[l], rwkv_k_k[l], rwkv_k_a[l], rwkv_r_k[l], rwkv_ln_g[l], rwkv_ln_b[l])

        d_cq, d_ckv, d_kr = split_cols(p[..., C_END:N_IN], MLA_SPLITS)
        o_d = mla_mixer(d_cq, d_ckv, d_kr, mla_q_norm_g[l], mla_w_q_up[l], mla_kv_norm_g[l], mla_w_kv_up[l],
                        mla_out_norm_g[l], mla_cs)

        mix = jnp.concatenate([o_a, o_b, o_c, o_d], axis=-1) @ w_out[l]
        x = layer_norm(alpha * x + mix, ln1_g[l], ln1_b[l])
        ffn = moe_ffn(x, router_w[l], router_b[l], exp_w_gu[l], exp_b_gu[l], exp_w_dn[l], exp_b_dn[l])
        x = layer_norm(alpha * x + ffn, ln2_g[l], ln2_b[l])
    return x
```

```python
import functools

import numpy as np
import jax
import jax.numpy as jnp
from jax import lax
from jax.experimental import pallas as pl
from jax.experimental.pallas import tpu as pltpu

F32 = jnp.float32

BLK = 128
LN_EPS = 1e-5
NORM_EPS = 1e-6

RET_HEADS, RET_DK, RET_DV = 4, 32, 64
RET_W = RET_HEADS * RET_DV
RET_THETA = 10000.0
DIL_HEADS, DIL_DH = 4, 64
DIL_W = DIL_HEADS * DIL_DH
DIL_PATTERNS = ((128, 1), (512, 4), (2048, 16))
ROPE_THETA = 500000.0
ROPE_ROT_DIM = DIL_DH // 4
RWKV_HEADS, RWKV_DH = 4, 64
RWKV_W = RWKV_HEADS * RWKV_DH
DECAY_LORA, AAA_LORA, MV_LORA, GATE_LORA = 64, 64, 32, 128
RWKV_GN_EPS = 64e-5
MLA_HEADS, MLA_NOPE, MLA_ROPE, MLA_DV = 4, 64, 32, 64
MLA_W = MLA_HEADS * MLA_DV
Q_LORA, KV_LORA = 256, 128
MLA_THETA = 10000.0
N_EXPERTS, TOP_K = 32, 4
SWIGLU_LIMIT, SWIGLU_ALPHA = 7.0, 1.702
MOE_BLK = 256

RET_SPLITS = (RET_HEADS * RET_DK, RET_HEADS * RET_DK, RET_W, RET_W)
DIL_SPLITS = (DIL_W, DIL_W, DIL_W)
RWKV_SPLITS = (RWKV_W, RWKV_W, RWKV_W, DECAY_LORA, AAA_LORA, GATE_LORA)
MLA_SPLITS = (Q_LORA, KV_LORA, MLA_ROPE)
A_END = sum(RET_SPLITS)
B_END = A_END + sum(DIL_SPLITS)
C_END = B_END + sum(RWKV_SPLITS)
N_IN = C_END + sum(MLA_SPLITS)

RWKV_CHUNK = 64


def split_cols(p, sizes):
    idx = np.cumsum(sizes)[:-1].tolist()
    return jnp.split(p, idx, axis=-1)


def layer_norm(x, g, b, eps=LN_EPS):
    mu = jnp.mean(x, -1, keepdims=True)
    var = jnp.mean(jnp.square(x - mu), -1, keepdims=True)
    return (x - mu) * lax.rsqrt(var + eps) * g + b


def rms_norm(x, g, eps=NORM_EPS):
    return x * lax.rsqrt(jnp.mean(x * x, -1, keepdims=True) + eps) * g


def rope_table(n_pos, rot_dim, theta):
    inv_freq = 1.0 / (theta ** (jnp.arange(0, rot_dim, 2, dtype=F32) / rot_dim))
    ang = jnp.arange(n_pos, dtype=F32)[:, None] * inv_freq[None, :]
    return jnp.cos(ang), jnp.sin(ang)


def apply_rope(x, cs):
    cos = cs[0][None, :, None, :]
    sin = cs[1][None, :, None, :]
    x1, x2 = jnp.split(x, 2, axis=-1)
    return jnp.concatenate([x1 * cos - x2 * sin, x1 * sin + x2 * cos], -1)


def apply_partial_rope(x, cs, rot_dim):
    return jnp.concatenate([apply_rope(x[..., :rot_dim], cs), x[..., rot_dim:]], -1)


def retention_mixer(q, k, v, g, norm_g, cs):
    B_, S_, _ = q.shape
    n_ch = S_ // BLK
    q = apply_rope(q.reshape(B_, S_, RET_HEADS, RET_DK), cs)
    k = apply_rope(k.reshape(B_, S_, RET_HEADS, RET_DK), cs) * (RET_DK ** -0.5)
    v = v.reshape(B_, S_, RET_HEADS, RET_DV)
    log_gamma = jnp.log(1.0 - 2.0 ** (-5.0 - jnp.arange(RET_HEADS, dtype=F32)))
    idx = jnp.arange(BLK, dtype=F32)
    dist = idx[:, None] - idx[None, :]
    inner_decay = jnp.where(dist >= 0, jnp.exp(jnp.maximum(dist, 0.0)[None] * log_gamma[:, None, None]), 0.0)
    qc = q.reshape(B_, n_ch, BLK, RET_HEADS, RET_DK)
    kc = k.reshape(B_, n_ch, BLK, RET_HEADS, RET_DK)
    vc = v.reshape(B_, n_ch, BLK, RET_HEADS, RET_DV)
    scores = jnp.einsum('bnihd,bnjhd->bnhij', qc, kc) * inner_decay
    o_inner = jnp.einsum('bnhij,bnjhe->bnihe', scores, vc)
    zeta = jnp.exp((BLK - 1 - idx)[None, :] * log_gamma[:, None])
    kv_chunk = jnp.einsum('bnjhd,hj,bnjhe->nbhde', kc, zeta, vc)
    chunk_decay = jnp.exp(BLK * log_gamma)[None, :, None, None]

    def step(state, kv):
        return chunk_decay * state + kv, state

    _, states = lax.scan(step, jnp.zeros_like(kv_chunk[0]), kv_chunk)
    xi = jnp.exp((idx + 1.0)[None, :] * log_gamma[:, None])
    o_cross = jnp.einsum('bnihd,nbhde,hi->bnihe', qc, states, xi)
    o = (o_inner + o_cross).reshape(B_, S_, RET_HEADS, RET_DV)
    o = rms_norm(o, norm_g.reshape(RET_HEADS, RET_DV))
    return jax.nn.silu(g) * o.reshape(B_, S_, RET_W)


def banded_block_attention(q, k, v, n_back):
    G, L, H, dh = q.shape
    nb = L // BLK
    qb = q.reshape(G, nb, BLK, H, dh)

    def with_prev(t):
        t = t.reshape(G, nb, BLK, H, dh)
        prev = jnp.concatenate([jnp.zeros_like(t[:, :1]), t[:, :-1]], axis=1)
        return jnp.concatenate([prev, t], axis=2)

    kk = with_prev(k)
    vv = with_prev(v)
    s = jnp.einsum('gnihd,gnjhd->gnhij', qb, kk) * (dh ** -0.5)
    qpos = BLK + jnp.arange(BLK)
    kpos = jnp.arange(2 * BLK)
    dist = qpos[:, None] - kpos[None, :]
    band = (dist >= 0) & (dist <= n_back)
    first = band & (kpos[None, :] >= BLK)
    mask = jnp.where((jnp.arange(nb) == 0)[:, None, None], first[None], band[None])
    s = jnp.where(mask[None, :, None], s, -jnp.inf)
    lse = jax.nn.logsumexp(s, axis=-1)
    p = jnp.exp(s - lse[..., None])
    o = jnp.einsum('gnhij,gnjhd->gnihd', p, vv)
    return o.reshape(G, L, H, dh), jnp.swapaxes(lse, 2, 3).reshape(G, L, H)


def dilated_mixer(q, k, v, norm_g, cs):
    B_, S_, _ = q.shape
    shp = (B_, S_, DIL_HEADS, DIL_DH)
    q = apply_partial_rope(q.reshape(shp), cs, ROPE_ROT_DIM)
    k = apply_partial_rope(k.reshape(shp), cs, ROPE_ROT_DIM)
    v = v.reshape(shp)
    outs, lses = [], []
    for window, dil in DIL_PATTERNS:
        span = dil * BLK
        s_pad = -(-S_ // span) * span
        m = s_pad // dil

        def to_residues(t):
            t = jnp.pad(t, ((0, 0), (0, s_pad - S_), (0, 0), (0, 0)))
            return t.reshape(B_, m, dil, DIL_HEADS, DIL_DH).transpose(0, 2, 1, 3, 4).reshape(B_ * dil, m, DIL_HEADS, DIL_DH)

        o, lse = banded_block_attention(to_residues(q), to_residues(k), to_residues(v), window // dil)
        o = o.reshape(B_, dil, m, DIL_HEADS, DIL_DH).transpose(0, 2, 1, 3, 4).reshape(B_, s_pad, DIL_HEADS, DIL_DH)[:, :S_]
        lse = lse.reshape(B_, dil, m, DIL_HEADS).transpose(0, 2, 1, 3).reshape(B_, s_pad, DIL_HEADS)[:, :S_]
        outs.append(o)
        lses.append(lse)
    wts = jax.nn.softmax(jnp.stack(lses), axis=0)
    o = jnp.einsum('pbsh,pbshd->bshd', wts, jnp.stack(outs))
    return rms_norm(o.reshape(B_, S_, DIL_W), norm_g)


def token_shift(p, mu):
    prev = jnp.pad(p, ((0, 0), (1, 0), (0, 0)))[:, :-1]
    return p + (prev - p) * mu


def l2_normalize(x, eps=1e-12):
    return x / jnp.maximum(jnp.sqrt(jnp.sum(x * x, -1, keepdims=True)), eps)


def _bdot(a, b, dims):
    return lax.dot_general(a.astype(jnp.bfloat16), b.astype(jnp.bfloat16), ((dims[0], dims[1]), ((), ())),
                           preferred_element_type=F32)


_NN = ((1,), (0,))
_NT = ((1,), (1,))
_TN = ((0,), (0,))


def _split3(x):
    h1 = x.astype(jnp.bfloat16)
    r1 = x - h1.astype(F32)
    h2 = r1.astype(jnp.bfloat16)
    h3 = (r1 - h2.astype(F32)).astype(jnp.bfloat16)
    return h1, h2, h3


def _rwkv7_chunk_kernel(r_ref, lw_ref, k_ref, v_ref, a_ref, b_ref, y_ref, state_ref, *, heads, dh, chunk):
    rows = r_ref.shape[1]
    n_sub = rows // chunk
    n_double = int(np.log2(chunk))
    assert 2 ** n_double == chunk and n_sub * chunk == rows

    @pl.when(pl.program_id(1) == 0)
    def _():
        state_ref[...] = jnp.zeros_like(state_ref)

    ri = lax.broadcasted_iota(jnp.int32, (rows, rows), 0)
    ci = lax.broadcasted_iota(jnp.int32, (rows, rows), 1)
    tri = ((ci <= ri) & (ci >= (ri // chunk) * chunk)).astype(jnp.bfloat16)
    logw = lw_ref[0]
    cum = sum(jnp.dot(tri, piece, preferred_element_type=F32) for piece in _split3(logw))

    row = lax.broadcasted_iota(jnp.int32, (chunk, 2 * chunk), 0)
    col = lax.broadcasted_iota(jnp.int32, (chunk, 2 * chunk), 1) % chunk
    strict = col < row
    incl = col <= row
    zeros_cv = jnp.zeros((chunk, dh), F32)

    pairs = [(s, h) for s in range(n_sub) for h in range(heads)]
    ah, rh, vh, bk_rem, w_total = {}, {}, {}, {}, {}
    l_abk, m_rbk = {}, {}
    for s in range(n_sub):
        rs = slice(s * chunk, (s + 1) * chunk)
        cum_s, logw_s = cum[rs], logw[rs]
        total = cum_s[chunk - 1:chunk, :]
        e_in = jnp.exp(cum_s)
        e_neg = jnp.exp(-cum_s)
        e_rem = jnp.exp(total - cum_s)
        w_tot = jnp.exp(total)
        r_hat = r_ref[0, rs, :] * e_in
        a_hat = a_ref[0, rs, :] * jnp.exp(cum_s - logw_s)
        b_all, k_all, v_all = b_ref[0, rs, :], k_ref[0, rs, :], v_ref[0, rs, :]
        b_til, k_til = b_all * e_neg, k_all * e_neg
        b_rem, k_rem = b_all * e_rem, k_all * e_rem
        for h in range(heads):
            sl = slice(h * dh, (h + 1) * dh)
            ah[s, h], rh[s, h], vh[s, h] = a_hat[:, sl], r_hat[:, sl], v_all[:, sl]
            w_total[s, h] = w_tot[:, sl]
            bk_rem[s, h] = jnp.concatenate([b_rem[:, sl], k_rem[:, sl]], axis=0)
            bk_til = jnp.concatenate([b_til[:, sl], k_til[:, sl]], axis=0)
            l_abk[s, h] = jnp.where(strict, _bdot(ah[s, h], bk_til, _NT), 0.0)
            m_rbk[s, h] = jnp.where(incl, _bdot(rh[s, h], bk_til, _NT), 0.0)
    z = {p: jnp.concatenate([ah[p], _bdot(l_abk[p], jnp.concatenate([zeros_cv, vh[p]], axis=0), _NN)], axis=1)
         for p in pairs}
    lp = {p: l_abk[p][:, :chunk] for p in pairs}
    for m in range(n_double):
        z = {p: z[p] + _bdot(lp[p], z[p], _NN) for p in pairs}
        if m + 1 < n_double:
            lp = {p: _bdot(lp[p], lp[p], _NN) for p in pairs}
    state = [state_ref[h] for h in range(heads)]
    for s in range(n_sub):
        pr_s = [_bdot(jnp.concatenate([z[s, h][:, :dh], rh[s, h]], axis=0), state[h], _NT) for h in range(heads)]
        uv = [jnp.concatenate([pr_s[h][:chunk] + z[s, h][:, dh:], vh[s, h]], axis=0) for h in range(heads)]
        ys = [pr_s[h][chunk:] + _bdot(m_rbk[s, h], uv[h], _NN) for h in range(heads)]
        state = [state[h] * w_total[s, h] + _bdot(uv[h], bk_rem[s, h], _TN) for h in range(heads)]
        y_ref[0, s * chunk:(s + 1) * chunk, :] = jnp.concatenate(ys, axis=1)
    for h in range(heads):
        state_ref[h] = state[h]


RWKV_BLOCK = 256


def rwkv7_scan(r, logw, k, v, a, b, heads=RWKV_HEADS, dh=RWKV_DH, chunk=RWKV_CHUNK, block=RWKV_BLOCK):
    B_, S_, W_ = r.shape
    block = min(block, S_)
    assert W_ == heads * dh and S_ % block == 0 and block % chunk == 0
    spec = pl.BlockSpec((1, block, W_), lambda bi, ci: (bi, ci, 0))
    return pl.pallas_call(
        functools.partial(_rwkv7_chunk_kernel, heads=heads, dh=dh, chunk=chunk),
        grid=(B_, S_ // block),
        in_specs=[spec] * 6,
        out_specs=spec,
        out_shape=jax.ShapeDtypeStruct((B_, S_, W_), F32),
        scratch_shapes=[pltpu.VMEM((heads, dh, dh), F32)],
        compiler_params=pltpu.CompilerParams(dimension_semantics=("arbitrary", "arbitrary")),
        name="rwkv7_chunk_scan",
    )(r, logw, k, v, a, b)


def rwkv7_mixer(r, k, v, wd, ad, gd, w0, w_up, a0, a_up, g_up, k_k, k_a, r_k, ln_g, ln_b):
    B_, S_, _ = r.shape
    hs = (B_, S_, RWKV_HEADS, RWKV_DH)
    w = -jax.nn.softplus(-(w0 + jnp.tanh(wd) @ w_up)) - 0.5
    logw = -jnp.exp(w)
    a = jax.nn.sigmoid(a0 + ad @ a_up)
    g = jax.nn.sigmoid(gd) @ g_up
    kk = l2_normalize((k * k_k).reshape(hs)).reshape(B_, S_, RWKV_W)
    k = k * (1.0 + (a - 1.0) * k_a)
    y = rwkv7_scan(r, logw, k, v, -kk, kk * a)
    y = layer_norm(y.reshape(hs), ln_g.reshape(RWKV_HEADS, RWKV_DH), ln_b.reshape(RWKV_HEADS, RWKV_DH), RWKV_GN_EPS)
    rf, kf, vf = r.reshape(hs), k.reshape(hs), v.reshape(hs)
    y = y + jnp.sum(rf * kf * r_k, -1, keepdims=True) * vf
    return y.reshape(B_, S_, RWKV_W) * g


def _causal_flash_kernel(q_ref, k_ref, v_ref, o_ref, *, tile):
    qi = pl.program_id(1)
    q = q_ref[0]
    dv = v_ref.shape[2]

    def kv_tile(j):
        start = pl.multiple_of(j * tile, tile)
        return k_ref[0, pl.ds(start, tile), :], v_ref[0, pl.ds(start, tile), :]

    def update(carry, s, v_t):
        m, l, acc = carry
        m_new = jnp.maximum(m, jnp.max(s, axis=-1, keepdims=True))
        corr = jnp.exp(m - m_new)
        p = jnp.exp(s - m_new)
        l_new = corr * l + jnp.sum(p, axis=-1, keepdims=True)
        acc_new = corr * acc + jnp.dot(p.astype(v_t.dtype), v_t, preferred_element_type=F32)
        return m_new, l_new, acc_new

    def body(j, carry):
        k_t, v_t = kv_tile(j)
        s = lax.dot_general(q, k_t, (((1,), (1,)), ((), ())), preferred_element_type=F32)
        return update(carry, s, v_t)

    init = (jnp.full((tile, 1), -jnp.inf, F32), jnp.zeros((tile, 1), F32), jnp.zeros((tile, dv), F32))
    carry = lax.fori_loop(0, qi, body, init)
    k_t, v_t = kv_tile(qi)
    s = lax.dot_general(q, k_t, (((1,), (1,)), ((), ())), preferred_element_type=F32)
    row = lax.broadcasted_iota(jnp.int32, (tile, tile), 0)
    col = lax.broadcasted_iota(jnp.int32, (tile, tile), 1)
    s = jnp.where(col <= row, s, -jnp.inf)
    m, l, acc = update(carry, s, v_t)
    o_ref[0] = acc / l


ATTN_TILE = 512


def causal_attention(q, k, v, scale, tile=ATTN_TILE):
    B_, S_, H, dq = q.shape
    dv = v.shape[-1]
    tile = min(tile, S_)
    assert S_ % tile == 0

    def heads_major(t):
        return jnp.moveaxis(t, 2, 1).reshape(B_ * H, S_, t.shape[-1]).astype(jnp.bfloat16)

    qh, kh, vh = heads_major(q * scale), heads_major(k), heads_major(v)
    o = pl.pallas_call(
        functools.partial(_causal_flash_kernel, tile=tile),
        grid=(B_ * H, S_ // tile),
        in_specs=[pl.BlockSpec((1, tile, dq), lambda g, i: (g, i, 0)),
                  pl.BlockSpec((1, S_, dq), lambda g, i: (g, 0, 0)),
                  pl.BlockSpec((1, S_, dv), lambda g, i: (g, 0, 0))],
        out_specs=pl.BlockSpec((1, tile, dv), lambda g, i: (g, i, 0)),
        out_shape=jax.ShapeDtypeStruct((B_ * H, S_, dv), F32),
        compiler_params=pltpu.CompilerParams(dimension_semantics=("arbitrary", "arbitrary")),
        name="mla_causal_flash",
    )(qh, kh, vh)
    return jnp.moveaxis(o.reshape(B_, H, S_, dv), 1, 2)


def mla_mixer(c_q, c_kv, k_rope, q_norm_g, w_q_up, kv_norm_g, w_kv_up, out_norm_g, cs):
    B_, S_, _ = c_q.shape
    q = (rms_norm(c_q, q_norm_g) @ w_q_up).reshape(B_, S_, MLA_HEADS, MLA_NOPE + MLA_ROPE)
    kv = (rms_norm(c_kv, kv_norm_g) @ w_kv_up).reshape(B_, S_, MLA_HEADS, MLA_NOPE + MLA_DV)
    q = jnp.concatenate([q[..., :MLA_NOPE], apply_rope(q[..., MLA_NOPE:], cs)], -1)
    k_pe = apply_rope(k_rope[:, :, None, :], cs)
    k = jnp.concatenate([kv[..., :MLA_NOPE], jnp.broadcast_to(k_pe, (B_, S_, MLA_HEADS, MLA_ROPE))], -1)
    v = kv[..., MLA_NOPE:]
    o = causal_attention(q, k, v, (MLA_NOPE + MLA_ROPE) ** -0.5)
    return rms_norm(o.reshape(B_, S_, MLA_W), out_norm_g)


def _moe_expert_kernel(tile_e_ref, n_used_ref, x_ref, wg_ref, wl_ref, bg_ref, bl_ref, wd_ref, bd_ref, o_ref):
    del tile_e_ref
    i = pl.program_id(0)

    @pl.when(i < n_used_ref[0])
    def _():
        x = x_ref[...]
        glu = jnp.dot(x, wg_ref[0], preferred_element_type=F32) + bg_ref[0]
        lin = jnp.dot(x, wl_ref[0], preferred_element_type=F32) + bl_ref[0]
        glu = jnp.minimum(glu, SWIGLU_LIMIT)
        lin = jnp.clip(lin, -SWIGLU_LIMIT, SWIGLU_LIMIT)
        act = glu * jax.nn.sigmoid(SWIGLU_ALPHA * glu) * (lin + 1.0)
        o_ref[...] = jnp.dot(act.astype(wd_ref.dtype), wd_ref[0], preferred_element_type=F32) + bd_ref[0]

    @pl.when(i >= n_used_ref[0])
    def _():
        o_ref[...] = jnp.zeros_like(o_ref)


LANES = 128
DEINT_ROWS = 256


def _deinterleave_kernel(w_ref, even_ref, odd_ref):
    g = 2 * LANES
    i = lax.broadcasted_iota(jnp.int32, (g, g), 0)
    j = lax.broadcasted_iota(jnp.int32, (g, g), 1)
    perm = (i == jnp.where(j < LANES, 2 * j, 2 * (j - LANES) + 1)).astype(jnp.bfloat16)
    for c in range(w_ref.shape[2] // g):
        blk = w_ref[0, :, c * g:(c + 1) * g].astype(jnp.bfloat16)
        sorted_cols = jnp.dot(blk, perm, preferred_element_type=F32)
        even_ref[0, :, c * LANES:(c + 1) * LANES] = sorted_cols[:, :LANES].astype(even_ref.dtype)
        odd_ref[0, :, c * LANES:(c + 1) * LANES] = sorted_cols[:, LANES:].astype(odd_ref.dtype)


def deinterleave_to_bf16(w):
    E, D, F2 = w.shape
    rows = min(DEINT_ROWS, D)
    assert D % rows == 0 and F2 % (2 * LANES) == 0
    out = jax.ShapeDtypeStruct((E, D, F2 // 2), jnp.bfloat16)
    return pl.pallas_call(
        _deinterleave_kernel,
        grid=(E, D // rows),
        in_specs=[pl.BlockSpec((1, rows, F2), lambda e, r: (e, r, 0))],
        out_specs=[pl.BlockSpec((1, rows, F2 // 2), lambda e, r: (e, r, 0))] * 2,
        out_shape=[out, out],
        compiler_params=pltpu.CompilerParams(dimension_semantics=("arbitrary", "arbitrary")),
        name="deinterleave_cast",
    )(w)


MOE_TILE = 512
MOE_VMEM_LIMIT = 48 * 1024 * 1024


def moe_experts(xs, tile_e, n_used, w_glu, w_lin, b_glu, b_lin, w_dn, b_dn, tile):
    n_rows, D = xs.shape
    E, _, F = w_glu.shape
    w_spec = lambda shape: pl.BlockSpec((1,) + shape, lambda i, te, nu: (te[i], 0, 0))
    return pl.pallas_call(
        _moe_expert_kernel,
        grid_spec=pltpu.PrefetchScalarGridSpec(
            num_scalar_prefetch=2,
            grid=(n_rows // tile,),
            in_specs=[pl.BlockSpec((tile, D), lambda i, te, nu: (i, 0)),
                      w_spec((D, F)), w_spec((D, F)), w_spec((1, F)), w_spec((1, F)),
                      w_spec((F, D)), w_spec((1, D))],
            out_specs=pl.BlockSpec((tile, D), lambda i, te, nu: (i, 0)),
        ),
        out_shape=jax.ShapeDtypeStruct((n_rows, D), F32),
        compiler_params=pltpu.CompilerParams(dimension_semantics=("arbitrary",), vmem_limit_bytes=MOE_VMEM_LIMIT),
        name="moe_expert_ffn",
    )(tile_e, n_used, xs, w_glu, w_lin, b_glu.reshape(E, 1, F), b_lin.reshape(E, 1, F), w_dn, b_dn.reshape(E, 1, D))


def moe_ffn(h, w_router, b_router, w_gu, b_gu, w_dn, b_dn, tile=MOE_TILE):
    B_, S_, D = h.shape
    n_exp = w_router.shape[1]
    n_tok = B_ * S_
    t = h.reshape(n_tok, D)
    logits = t @ w_router + b_router
    top_val, top_idx = lax.top_k(logits, TOP_K)
    gate = jax.nn.softmax(top_val, axis=-1)
    n_assign = n_tok * TOP_K
    e_flat = top_idx.reshape(-1).astype(jnp.int32)
    e_sorted, order = lax.sort((e_flat, jnp.arange(n_assign, dtype=jnp.int32)), num_keys=1)
    counts = jnp.sum((e_flat[:, None] == jnp.arange(n_exp, dtype=jnp.int32)[None, :]).astype(jnp.int32), axis=0)
    padded = (counts + tile - 1) // tile * tile
    start = jnp.cumsum(counts) - counts
    pend = jnp.cumsum(padded)
    pstart = pend - padded
    n_tiles = (n_assign + n_exp * (tile - 1) + tile - 1) // tile
    n_rows = n_tiles * tile
    tile_e = jnp.minimum(jnp.searchsorted(pend, jnp.arange(n_tiles, dtype=jnp.int32) * tile, side='right'),
                         n_exp - 1).astype(jnp.int32)
    n_used = (pend[-1] // tile).astype(jnp.int32).reshape(1)
    row = jnp.arange(n_rows, dtype=jnp.int32)
    row_e = jnp.repeat(tile_e, tile)
    within = row - pstart[row_e]
    valid = within < counts[row_e]
    row_assign = order[jnp.clip(start[row_e] + within, 0, n_assign - 1)]
    row_tok = jnp.where(valid, row_assign // TOP_K, 0)
    dest = pstart[e_sorted] + (jnp.arange(n_assign, dtype=jnp.int32) - start[e_sorted])
    _, pos = lax.sort((order, dest), num_keys=1)
    xs = t.astype(jnp.bfloat16)[row_tok]
    bf = jnp.bfloat16
    w_glu, w_lin = deinterleave_to_bf16(w_gu)
    yb = moe_experts(xs, tile_e, n_used, w_glu, w_lin, b_gu[:, 0::2], b_gu[:, 1::2], w_dn.astype(bf), b_dn, tile)
    y = jnp.sum(yb[pos].reshape(n_tok, TOP_K, D) * gate[:, :, None], axis=1)
    return y.reshape(B_, S_, D)


def kernel(x, w_in, w_out, ret_norm_g, dil_norm_g, rwkv_mu, rwkv_w0, rwkv_w_up, rwkv_a0, rwkv_a_up, rwkv_g_up, rwkv_k_k, rwkv_k_a, rwkv_r_k, rwkv_ln_g, rwkv_ln_b, rwkv_vres_down, rwkv_vres_mu, rwkv_v0, rwkv_v_up, mla_q_norm_g, mla_w_q_up, mla_kv_norm_g, mla_w_kv_up, mla_out_norm_g, ln1_g, ln1_b, router_w, router_b, exp_w_gu, exp_b_gu, exp_w_dn, exp_b_dn, ln2_g, ln2_b):
    depth = w_in.shape[0]
    S_ = x.shape[1]
    alpha = (2 * depth) ** 0.25
    ret_cs = rope_table(S_, RET_DK, RET_THETA)
    dil_cs = rope_table(S_, ROPE_ROT_DIM, ROPE_THETA)
    mla_cs = rope_table(S_, MLA_ROPE, MLA_THETA)
    v_first = None
    for l in range(depth):
        if l == 0:
            p = x @ w_in[l]
        else:
            p = x @ jnp.concatenate([w_in[l], rwkv_vres_down[l - 1]], axis=1)

        a_q, a_k, a_v, a_g = split_cols(p[..., :A_END], RET_SPLITS)
        o_a = retention_mixer(a_q, a_k, a_v, a_g, ret_norm_g[l], ret_cs)

        b_q, b_k, b_v = split_cols(p[..., A_END:B_END], DIL_SPLITS)
        o_b = dilated_mixer(b_q, b_k, b_v, dil_norm_g[l], dil_cs)

        c_r, c_k, c_v, c_wd, c_ad, c_gd = split_cols(token_shift(p[..., B_END:C_END], rwkv_mu[l]), RWKV_SPLITS)
        if l == 0:
            v_first = c_v
        else:
            vd = token_shift(p[..., N_IN:], rwkv_vres_mu[l - 1])
            c_v = c_v + (v_first - c_v) * jax.nn.sigmoid(rwkv_v0[l - 1] + vd @ rwkv_v_up[l - 1])
        o_c = rwkv7_mixer(c_r, c_k, c_v, c_wd, c_ad, c_gd, rwkv_w0[l], rwkv_w_up[l], rwkv_a0[l], rwkv_a_up[l],
                          rwkv_g_up[l], rwkv_k_k[l], rwkv_k_a[l], rwkv_r_k[l], rwkv_ln_g[l], rwkv_ln_b[l])

        d_cq, d_ckv, d_kr = split_cols(p[..., C_END:N_IN], MLA_SPLITS)
        o_d = mla_mixer(d_cq, d_ckv, d_kr, mla_q_norm_g[l], mla_w_q_up[l], mla_kv_norm_g[l], mla_w_kv_up[l],
                        mla_out_norm_g[l], mla_cs)

        mix = jnp.concatenate([o_a, o_b, o_c, o_d], axis=-1) @ w_out[l]
        x = layer_norm(alpha * x + mix, ln1_g[l], ln1_b[l])
        ffn = moe_ffn(x, router_w[l], router_b[l], exp_w_gu[l], exp_b_gu[l], exp_w_dn[l], exp_b_dn[l])
        x = layer_norm(alpha * x + ffn, ln2_g[l], ln2_b[l])
    return x
```

```python
import functools

import numpy as np
import jax
import jax.numpy as jnp
from jax import lax
from jax.experimental import pallas as pl
from jax.experimental.pallas import tpu as pltpu

F32 = jnp.float32
BF16 = jnp.bfloat16

LANES = 128
BLK = 128
LN_EPS = 1e-5
NORM_EPS = 1e-6

RET_HEADS, RET_DK, RET_DV = 4, 32, 64
RET_QK = RET_HEADS * RET_DK
RET_W = RET_HEADS * RET_DV
RET_THETA = 10000.0
DIL_HEADS, DIL_DH = 4, 64
DIL_W = DIL_HEADS * DIL_DH
DIL_PATTERNS = ((128, 1), (512, 4), (2048, 16))
ROPE_THETA = 500000.0
ROPE_ROT_DIM = DIL_DH // 4
RWKV_HEADS, RWKV_DH = 4, 64
RWKV_W = RWKV_HEADS * RWKV_DH
DECAY_LORA, AAA_LORA, MV_LORA, GATE_LORA = 64, 64, 32, 128
RWKV_GN_EPS = 64e-5
MLA_HEADS, MLA_NOPE, MLA_ROPE, MLA_DV = 4, 64, 32, 64
MLA_W = MLA_HEADS * MLA_DV
Q_LORA, KV_LORA = 256, 128
MLA_THETA = 10000.0
TOP_K = 4
SWIGLU_LIMIT, SWIGLU_ALPHA = 7.0, 1.702

RET_SPLITS = (RET_QK, RET_QK, RET_W, RET_W)
DIL_SPLITS = (DIL_W, DIL_W, DIL_W)
RWKV_SPLITS = (RWKV_W, RWKV_W, RWKV_W, DECAY_LORA, AAA_LORA, GATE_LORA)
MLA_SPLITS = (Q_LORA, KV_LORA, MLA_ROPE)
A_END = sum(RET_SPLITS)
B_END = A_END + sum(DIL_SPLITS)
C_END = B_END + sum(RWKV_SPLITS)
N_IN = C_END + sum(MLA_SPLITS)

RWKV_CHUNK = 64
RWKV_BLOCK = 256

MLA_HEAD_PAD = 128
MLA_QK_W = MLA_HEADS * MLA_HEAD_PAD
ATTN_TILE = 512

IN_RET_W = 4 * RET_QK + 2 * RET_W
IN_DIL_W = 5 * DIL_W
IN_RWKV_W = sum(RWKV_SPLITS) + LANES
IN_MLA_W = Q_LORA + KV_LORA + LANES
IN_COLS = IN_RET_W + IN_DIL_W + IN_RWKV_W + IN_MLA_W
IN_TILE = 256
OUT_TILE = 512
ROUTER_PAD = 128
DIL_PAIR_W = 2 * DIL_DH
DEINT_ROWS = 256
MOE_TILE = 512
VMEM_LIMIT = 48 * 1024 * 1024
DIL_VMEM_LIMIT = 56 * 1024 * 1024


def split_cols(p, sizes):
    idx = np.cumsum(sizes)[:-1].tolist()
    return jnp.split(p, idx, axis=-1)


def rope_table(n_pos, rot_dim, theta):
    inv_freq = 1.0 / (theta ** (jnp.arange(0, rot_dim, 2, dtype=F32) / rot_dim))
    ang = jnp.arange(n_pos, dtype=F32)[:, None] * inv_freq[None, :]
    return jnp.cos(ang), jnp.sin(ang)


def _bdot(a, b, dims):
    return lax.dot_general(a.astype(BF16), b.astype(BF16), ((dims[0], dims[1]), ((), ())), preferred_element_type=F32)


_NN = ((1,), (0,))
_NT = ((1,), (1,))
_TN = ((0,), (0,))


def _split3(x):
    h1 = x.astype(BF16)
    r1 = x - h1.astype(F32)
    h2 = r1.astype(BF16)
    h3 = (r1 - h2.astype(F32)).astype(BF16)
    return h1, h2, h3


def _head_sum(x, dh):
    w = x.shape[1]
    i = lax.broadcasted_iota(jnp.int32, (w, w), 0) // dh
    j = lax.broadcasted_iota(jnp.int32, (w, w), 1) // dh
    ones = (i == j).astype(BF16)
    return sum(jnp.dot(p, ones, preferred_element_type=F32) for p in _split3(x))


def _layer_norm_rows(h, ln_ref):
    mu = jnp.mean(h, -1, keepdims=True)
    hc = h - mu
    var = jnp.mean(hc * hc, -1, keepdims=True)
    return hc * lax.rsqrt(var + LN_EPS) * ln_ref[0:1, :] + ln_ref[1:2, :]


def _rot_half_cols(w, heads, dh, rot):
    d_in = w.shape[0]
    w = w.reshape(d_in, heads, dh)
    half = rot // 2
    sw = jnp.concatenate([-w[..., half:rot], w[..., :half], jnp.zeros((d_in, heads, dh - rot), w.dtype)], axis=-1)
    return sw.reshape(d_in, heads * dh)


def _rope_lanes(cs, heads, dh, rot, lead=0):
    cos, sin = cs
    S_ = cos.shape[0]
    c = jnp.concatenate([jnp.ones((S_, lead), F32), cos, cos, jnp.ones((S_, dh - lead - rot), F32)], axis=1)
    s = jnp.concatenate([jnp.zeros((S_, lead), F32), sin, sin, jnp.zeros((S_, dh - lead - rot), F32)], axis=1)
    return jnp.tile(c, (1, heads)), jnp.tile(s, (1, heads))


def rope_tables(S_):
    ret = _rope_lanes(rope_table(S_, RET_DK, RET_THETA), RET_HEADS, RET_DK, RET_DK)
    dil = _rope_lanes(rope_table(S_, ROPE_ROT_DIM, ROPE_THETA), DIL_HEADS, DIL_DH, ROPE_ROT_DIM)
    cos, sin = rope_table(S_, MLA_ROPE, MLA_THETA)
    mq = _rope_lanes((cos, sin), MLA_HEADS, MLA_HEAD_PAD, MLA_ROPE, lead=MLA_NOPE)
    mk = jnp.concatenate([cos, cos, sin, sin, jnp.zeros((S_, LANES - 2 * MLA_ROPE), F32)], axis=1)
    return ret + dil + mq + (mk,)


def prep_in_weights(w_in_l, vres_down_l):
    D = w_in_l.shape[0]
    a_q, a_k, a_v, a_g = split_cols(w_in_l[:, :A_END], RET_SPLITS)
    b_q, b_k, b_v = split_cols(w_in_l[:, A_END:B_END], DIL_SPLITS)
    d_cq, d_ckv, d_kr = split_cols(w_in_l[:, C_END:N_IN], MLA_SPLITS)
    vres = jnp.zeros((D, LANES), F32)
    if vres_down_l is not None:
        vres = vres.at[:, :MV_LORA].set(vres_down_l)
    cols = [a_q, _rot_half_cols(a_q, RET_HEADS, RET_DK, RET_DK), a_k, _rot_half_cols(a_k, RET_HEADS, RET_DK, RET_DK), a_v, a_g,
            b_q, _rot_half_cols(b_q, DIL_HEADS, DIL_DH, ROPE_ROT_DIM), b_k, _rot_half_cols(b_k, DIL_HEADS, DIL_DH, ROPE_ROT_DIM), b_v,
            w_in_l[:, B_END:C_END], vres,
            d_cq, d_ckv, d_kr, _rot_half_cols(d_kr, 1, MLA_ROPE, MLA_ROPE), jnp.zeros((D, LANES - 2 * MLA_ROPE), F32)]
    w = jnp.concatenate(cols, axis=1)
    assert w.shape[1] == IN_COLS
    return w.astype(BF16)


def prep_rwkv_params(mu, w0, w_up, a0, a_up, g_up, k_k, k_a, vres_mu, v0, v_up):
    W = RWKV_W
    mu_ext = jnp.zeros((1, IN_RWKV_W), F32).at[0, :mu.shape[0]].set(mu)
    v_up_ext = jnp.zeros((LANES, W), F32)
    v0_ext = jnp.zeros((1, W), F32)
    if vres_mu is not None:
        mu_ext = mu_ext.at[0, mu.shape[0]:mu.shape[0] + MV_LORA].set(vres_mu)
        v_up_ext = v_up_ext.at[:MV_LORA].set(v_up)
        v0_ext = v0.reshape(1, W)
    wa_up = jnp.zeros((LANES, 2 * W), F32).at[:DECAY_LORA, :W].set(w_up).at[DECAY_LORA:, W:].set(a_up)
    return [mu_ext, wa_up.astype(BF16), jnp.concatenate([w0, a0]).reshape(1, 2 * W), g_up.astype(BF16),
            jnp.stack([k_k, k_a]), v_up_ext.astype(BF16), v0_ext]


def prep_mla_params(q_norm_g, w_q_up, kv_norm_g, w_kv_up):
    ql = w_q_up.shape[0]
    wq = w_q_up.reshape(ql, MLA_HEADS, MLA_NOPE + MLA_ROPE)
    pad = jnp.zeros((ql, MLA_HEADS, MLA_HEAD_PAD - MLA_NOPE - MLA_ROPE), F32)
    q_main = jnp.concatenate([wq, pad], axis=-1).reshape(ql, MLA_QK_W)
    rope = wq[..., MLA_NOPE:]
    half = MLA_ROPE // 2
    q_rot = jnp.concatenate([jnp.zeros((ql, MLA_HEADS, MLA_NOPE), F32), -rope[..., half:], rope[..., :half], pad], axis=-1)
    w_q = jnp.concatenate([q_main, q_rot.reshape(ql, MLA_QK_W)], axis=1).astype(BF16)
    kl = w_kv_up.shape[0]
    wkv = w_kv_up.reshape(kl, MLA_HEADS, MLA_NOPE + MLA_DV)
    k_main = jnp.concatenate([wkv[..., :MLA_NOPE], jnp.zeros((kl, MLA_HEADS, MLA_HEAD_PAD - MLA_NOPE), F32)], axis=-1)
    w_kv = jnp.concatenate([k_main.reshape(kl, MLA_QK_W), wkv[..., MLA_NOPE:].reshape(kl, MLA_W)], axis=1).astype(BF16)
    return [q_norm_g.reshape(1, Q_LORA), w_q, kv_norm_g.reshape(1, KV_LORA), w_kv]


def prep_out_weights(w_out_l, mla_out_norm_g):
    W = RET_W
    w_abc = w_out_l[:3 * W].astype(BF16)
    w_d = w_out_l[3 * W:].reshape(MLA_HEADS, MLA_DV, -1)
    g_d = mla_out_norm_g.reshape(MLA_HEADS, MLA_DV)
    zw = jnp.zeros_like(w_d[0])
    zg = jnp.zeros_like(g_d[0])
    rows, gains = [], []
    for h in range(MLA_HEADS):
        rows += [w_d[h], zw] if h % 2 == 0 else [zw, w_d[h]]
        gains += [g_d[h], zg] if h % 2 == 0 else [zg, g_d[h]]
    return w_abc, jnp.concatenate(rows, axis=0).astype(BF16), jnp.concatenate(gains).reshape(1, MLA_QK_W)


def retention_tables():
    log_gamma = jnp.log(1.0 - 2.0 ** (-5.0 - jnp.arange(RET_HEADS, dtype=F32)))
    idx = jnp.arange(BLK, dtype=F32)
    dist = idx[:, None] - idx[None, :]
    inner = jnp.where(dist >= 0, jnp.exp(jnp.maximum(dist, 0.0)[None] * log_gamma[:, None, None]), 0.0)
    zeta = jnp.exp((BLK - 1 - idx)[None, :] * log_gamma[:, None])
    xi = jnp.exp((idx + 1.0)[None, :] * log_gamma[:, None])
    zeta_k = jnp.repeat(zeta.T, RET_DK, axis=1)
    xi_v = jnp.repeat(xi.T, RET_DV, axis=1)
    row_h = jnp.arange(RET_QK) // RET_DK
    col_h = jnp.arange(RET_W) // RET_DV
    same = row_h[:, None] == col_h[None, :]
    state_decay = jnp.where(same, jnp.exp(BLK * log_gamma)[row_h][:, None], 0.0)
    return inner, zeta_k, xi_v, state_decay, same.astype(F32)


def _in_kernel(x_ref, w_ref, cr_ref, sr_ref, cd_ref, sd_ref, cq_ref, sq_ref, ck_ref,
               mu_ref, wa_up_ref, w0a0_ref, g_up_ref, kk_ka_ref, v_up_ref, v0_ref, vfirst_ref,
               qn_ref, wq_ref, kvn_ref, wkv_ref,
               ret_q, ret_k, ret_v, ret_g, dil_q, dil_k, dil_v,
               rw_r, rw_lw, rw_k, rw_v, rw_a, rw_b, rw_g, mla_q, mla_k, mla_v,
               carry_ref, *, has_vres):
    tm = x_ref.shape[1]

    @pl.when(pl.program_id(1) == 0)
    def _():
        carry_ref[...] = jnp.zeros_like(carry_ref)

    xb = x_ref[0].astype(BF16)
    o1, o2, o3 = IN_RET_W, IN_RET_W + IN_DIL_W, IN_RET_W + IN_DIL_W + IN_RWKV_W
    pa = jnp.dot(xb, w_ref[:, :o1], preferred_element_type=F32)
    pb = jnp.dot(xb, w_ref[:, o1:o2], preferred_element_type=F32)
    pc = jnp.dot(xb, w_ref[:, o2:o3], preferred_element_type=F32)
    pd = jnp.dot(xb, w_ref[:, o3:], preferred_element_type=F32)

    qk = RET_QK
    cr, sr = cr_ref[...], sr_ref[...]
    ret_q[0] = (pa[:, 0:qk] * cr + pa[:, qk:2 * qk] * sr).astype(ret_q.dtype)
    ret_k[0] = ((pa[:, 2 * qk:3 * qk] * cr + pa[:, 3 * qk:4 * qk] * sr) * (RET_DK ** -0.5)).astype(ret_k.dtype)
    ret_v[0] = pa[:, 4 * qk:4 * qk + RET_W].astype(ret_v.dtype)
    ret_g[0] = pa[:, 4 * qk + RET_W:]

    cd, sd = cd_ref[...], sd_ref[...]
    dil_q[0] = pb[:, 0:DIL_W] * cd + pb[:, DIL_W:2 * DIL_W] * sd
    dil_k[0] = pb[:, 2 * DIL_W:3 * DIL_W] * cd + pb[:, 3 * DIL_W:4 * DIL_W] * sd
    dil_v[0] = pb[:, 4 * DIL_W:]

    row = lax.broadcasted_iota(jnp.int32, pc.shape, 0)
    prev = jnp.where(row == 0, carry_ref[...], pltpu.roll(pc, 1, 0))
    carry_ref[...] = pc[tm - 1:tm, :]
    ps = pc + (prev - pc) * mu_ref[...]
    W = RWKV_W
    r, k, v = ps[:, 0:W], ps[:, W:2 * W], ps[:, 2 * W:3 * W]
    wd_ad = ps[:, 3 * W:3 * W + LANES]
    lane = lax.broadcasted_iota(jnp.int32, wd_ad.shape, 1)
    lora_in = jnp.where(lane < DECAY_LORA, jnp.tanh(wd_ad), wd_ad)
    lora = jnp.dot(lora_in.astype(BF16), wa_up_ref[...], preferred_element_type=F32) + w0a0_ref[...]
    w_raw = -jax.nn.softplus(-lora[:, :W]) - 0.5
    a_sig = jax.nn.sigmoid(lora[:, W:])
    gd = ps[:, 3 * W + LANES:3 * W + 2 * LANES]
    rw_g[0] = jnp.dot(jax.nn.sigmoid(gd).astype(BF16), g_up_ref[...], preferred_element_type=F32)
    kk = k * kk_ka_ref[0:1, :]
    kk = kk / jnp.maximum(jnp.sqrt(_head_sum(kk * kk, RWKV_DH)), 1e-12)
    if has_vres:
        vd = ps[:, 3 * W + 2 * LANES:]
        mix = jax.nn.sigmoid(jnp.dot(vd.astype(BF16), v_up_ref[...], preferred_element_type=F32) + v0_ref[...])
        v = v + (vfirst_ref[0] - v) * mix
    rw_r[0] = r
    rw_lw[0] = -jnp.exp(w_raw)
    rw_k[0] = k * (1.0 + (a_sig - 1.0) * kk_ka_ref[1:2, :])
    rw_v[0] = v
    rw_a[0] = -kk
    rw_b[0] = kk * a_sig

    c_q = pd[:, :Q_LORA]
    c_q = c_q * lax.rsqrt(jnp.mean(c_q * c_q, -1, keepdims=True) + NORM_EPS) * qn_ref[...]
    q2 = jnp.dot(c_q.astype(BF16), wq_ref[...], preferred_element_type=F32)
    scale = (MLA_NOPE + MLA_ROPE) ** -0.5
    mla_q[0] = ((q2[:, :MLA_QK_W] * cq_ref[...] + q2[:, MLA_QK_W:] * sq_ref[...]) * scale).astype(mla_q.dtype)
    c_kv = pd[:, Q_LORA:Q_LORA + KV_LORA]
    c_kv = c_kv * lax.rsqrt(jnp.mean(c_kv * c_kv, -1, keepdims=True) + NORM_EPS) * kvn_ref[...]
    kv = jnp.dot(c_kv.astype(BF16), wkv_ref[...], preferred_element_type=F32)
    kr = pd[:, Q_LORA + KV_LORA:] * ck_ref[...]
    i = lax.broadcasted_iota(jnp.int32, (LANES, MLA_QK_W), 0)
    j = lax.broadcasted_iota(jnp.int32, (LANES, MLA_QK_W), 1) % MLA_HEAD_PAD
    place = ((i < 2 * MLA_ROPE) & (j == MLA_NOPE + i % MLA_ROPE)).astype(BF16)
    k_pe = sum(jnp.dot(p, place, preferred_element_type=F32) for p in _split3(kr)[:2])
    mla_k[0] = (kv[:, :MLA_QK_W] + k_pe).astype(mla_k.dtype)
    mla_v[0] = kv[:, MLA_QK_W:].astype(mla_v.dtype)


def fused_in_proj(x, w_cat, tables, rw_params, mla_params, v_first, tile=IN_TILE):
    B_, S_, D = x.shape
    tile = min(tile, S_)
    assert S_ % tile == 0
    has_vres = v_first is not None
    tok = lambda w: pl.BlockSpec((1, tile, w), lambda b, i: (b, i, 0))
    pos = lambda a: pl.BlockSpec((tile, a.shape[1]), lambda b, i: (i, 0))
    full = lambda a: pl.BlockSpec(a.shape, lambda b, i: (0,) * a.ndim)
    if has_vres:
        vf_spec = tok(RWKV_W)
    else:
        v_first = jnp.zeros((1, tile, RWKV_W), F32)
        vf_spec = pl.BlockSpec((1, tile, RWKV_W), lambda b, i: (0, 0, 0))
    f32o = lambda w: jax.ShapeDtypeStruct((B_, S_, w), F32)
    bfo = lambda w: jax.ShapeDtypeStruct((B_, S_, w), BF16)
    out_shape = ([bfo(RET_QK), bfo(RET_QK), bfo(RET_W), f32o(RET_W)] + [f32o(DIL_W)] * 3 + [f32o(RWKV_W)] * 7
                 + [bfo(MLA_QK_W), bfo(MLA_QK_W), bfo(MLA_W)])
    return pl.pallas_call(
        functools.partial(_in_kernel, has_vres=has_vres),
        grid=(B_, S_ // tile),
        in_specs=[tok(D), full(w_cat)] + [pos(t) for t in tables] + [full(p) for p in rw_params] + [vf_spec]
                 + [full(p) for p in mla_params],
        out_specs=[tok(s.shape[-1]) for s in out_shape],
        out_shape=out_shape,
        scratch_shapes=[pltpu.VMEM((1, IN_RWKV_W), F32)],
        compiler_params=pltpu.CompilerParams(dimension_semantics=("arbitrary", "arbitrary"), vmem_limit_bytes=VMEM_LIMIT),
        name="fused_in_proj",
    )(x, w_cat, *tables, *rw_params, v_first, *mla_params)


def _retention_kernel(q_ref, k_ref, v_ref, g_ref, ng_ref, inner_ref, zeta_ref, xi_ref, sdec_ref, smask_ref, o_ref, state_ref):
    @pl.when(pl.program_id(1) == 0)
    def _():
        state_ref[...] = jnp.zeros_like(state_ref)

    q, k, v = q_ref[0], k_ref[0], v_ref[0]
    q_head = lax.broadcasted_iota(jnp.int32, q.shape, 1) // RET_DK
    v_head = lax.broadcasted_iota(jnp.int32, v.shape, 1) // RET_DV
    zero_q = jnp.zeros_like(q)
    scores = [_bdot(jnp.where(q_head == h, q, zero_q), k, _NT) * inner_ref[h] for h in range(RET_HEADS)]
    o_heads = [_bdot(s, v, _NN) for s in scores]
    o = _bdot(q, state_ref[...], _NN) * xi_ref[...]
    for h in range(RET_HEADS):
        o = o + jnp.where(v_head == h, o_heads[h], 0.0)
    state_ref[...] = state_ref[...] * sdec_ref[...] + _bdot(k.astype(F32) * zeta_ref[...], v, _TN) * smask_ref[...]
    o = o * lax.rsqrt(_head_sum(o * o, RET_DV) * (1.0 / RET_DV) + NORM_EPS) * ng_ref[...]
    g = g_ref[0]
    o_ref[0] = g * jax.nn.sigmoid(g) * o


def retention_fused(q, k, v, g, norm_g, tables):
    B_, S_, _ = q.shape
    assert S_ % BLK == 0
    tok = lambda w: pl.BlockSpec((1, BLK, w), lambda b, c: (b, c, 0))
    full = lambda a: pl.BlockSpec(a.shape, lambda b, c: (0,) * a.ndim)
    ng = norm_g.reshape(1, RET_W)
    return pl.pallas_call(
        _retention_kernel,
        grid=(B_, S_ // BLK),
        in_specs=[tok(RET_QK), tok(RET_QK), tok(RET_W), tok(RET_W), full(ng)] + [full(t) for t in tables],
        out_specs=tok(RET_W),
        out_shape=jax.ShapeDtypeStruct((B_, S_, RET_W), F32),
        scratch_shapes=[pltpu.VMEM((RET_QK, RET_W), F32)],
        compiler_params=pltpu.CompilerParams(dimension_semantics=("arbitrary", "arbitrary")),
        name="retention_chunk",
    )(q, k, v, g, ng, *tables)


def _dilated_kernel(q_ref, k_ref, v_ref, o_ref, m_ref, l_ref, *, dils):
    S_ = q_ref.shape[1]
    lane_head = lax.broadcasted_iota(jnp.int32, (BLK, DIL_PAIR_W), 1) // DIL_DH
    i_idx = lax.broadcasted_iota(jnp.int32, (BLK, 2 * BLK), 0)
    j_idx = lax.broadcasted_iota(jnp.int32, (BLK, 2 * BLK), 1)
    scale = DIL_DH ** -0.5

    def rows(start, n, d):
        return pl.ds(start, n) if d == 1 else pl.ds(start, n, stride=d)

    for pi, d in enumerate(dils):
        def tile(t, carry, d=d, first=(pi == 0)):
            nb, r = t // d, t % d
            kb = jnp.maximum(nb - 1, 0)
            q_rows = rows(nb * (BLK * d) + r, BLK, d)
            k_rows = rows(kb * (BLK * d) + r, 2 * BLK, d)
            q = q_ref[0, q_rows, :]
            kk = k_ref[0, k_rows, :].astype(BF16)
            vv = v_ref[0, k_rows, :].astype(BF16)
            delta = (nb - kb) * BLK + i_idx - j_idx
            valid = (delta >= 0) & (delta <= BLK)
            zero_q = jnp.zeros_like(q)
            s = [jnp.where(valid, _bdot(jnp.where(lane_head == h, q, zero_q), kk, _NT) * scale, -jnp.inf) for h in range(2)]
            m = [jnp.max(x, axis=-1, keepdims=True) for x in s]
            p = [jnp.exp(s[h] - m[h]) for h in range(2)]
            l = [jnp.sum(x, axis=-1, keepdims=True) for x in p]
            pv = [jnp.dot(p[h].astype(BF16), vv, preferred_element_type=F32) for h in range(2)]
            m_t = jnp.where(lane_head == 0, m[0], m[1])
            l_t = jnp.where(lane_head == 0, l[0], l[1])
            pv_t = jnp.where(lane_head == 0, pv[0], pv[1])
            if first:
                m_ref[q_rows, :] = m_t
                l_ref[q_rows, :] = l_t
                o_ref[0, q_rows, :] = pv_t
            else:
                m_old = m_ref[q_rows, :]
                m_new = jnp.maximum(m_old, m_t)
                c_old = jnp.exp(m_old - m_new)
                c_t = jnp.exp(m_t - m_new)
                m_ref[q_rows, :] = m_new
                l_ref[q_rows, :] = l_ref[q_rows, :] * c_old + l_t * c_t
                o_ref[0, q_rows, :] = o_ref[0, q_rows, :] * c_old + pv_t * c_t
            return carry

        lax.fori_loop(0, S_ // BLK, tile, 0)

    def finish(t, carry):
        r0 = pl.multiple_of(t * BLK, BLK)
        o_ref[0, pl.ds(r0, BLK), :] = o_ref[0, pl.ds(r0, BLK), :] / l_ref[pl.ds(r0, BLK), :]
        return carry

    lax.fori_loop(0, S_ // BLK, finish, 0)


def dilated_fused(q, k, v):
    B_, S_, W_ = q.shape
    dils = tuple(d for _, d in DIL_PATTERNS)
    assert all(w // d == BLK for w, d in DIL_PATTERNS) and S_ % (2 * BLK * max(dils)) == 0
    spec = pl.BlockSpec((1, S_, DIL_PAIR_W), lambda b, hp: (b, 0, hp))
    return pl.pallas_call(
        functools.partial(_dilated_kernel, dils=dils),
        grid=(B_, W_ // DIL_PAIR_W),
        in_specs=[spec] * 3,
        out_specs=spec,
        out_shape=jax.ShapeDtypeStruct((B_, S_, W_), F32),
        scratch_shapes=[pltpu.VMEM((S_, DIL_PAIR_W), F32)] * 2,
        compiler_params=pltpu.CompilerParams(dimension_semantics=("arbitrary", "arbitrary"), vmem_limit_bytes=DIL_VMEM_LIMIT),
        name="dilated_attention",
    )(q, k, v)


def _rwkv7_mixer_kernel(r_ref, lw_ref, k_ref, v_ref, a_ref, b_ref, g_ref, gn_ref, o_ref, state_ref, *, heads, dh, chunk):
    rows = r_ref.shape[1]
    n_sub = rows // chunk
    n_double = int(np.log2(chunk))
    assert 2 ** n_double == chunk and n_sub * chunk == rows

    @pl.when(pl.program_id(1) == 0)
    def _():
        state_ref[...] = jnp.zeros_like(state_ref)

    ri = lax.broadcasted_iota(jnp.int32, (rows, rows), 0)
    ci = lax.broadcasted_iota(jnp.int32, (rows, rows), 1)
    tri = ((ci <= ri) & (ci >= (ri // chunk) * chunk)).astype(BF16)
    logw = lw_ref[0]
    cum = sum(jnp.dot(tri, piece, preferred_element_type=F32) for piece in _split3(logw))

    row = lax.broadcasted_iota(jnp.int32, (chunk, 2 * chunk), 0)
    col = lax.broadcasted_iota(jnp.int32, (chunk, 2 * chunk), 1) % chunk
    strict = col < row
    incl = col <= row
    zeros_cv = jnp.zeros((chunk, dh), F32)

    pairs = [(s, h) for s in range(n_sub) for h in range(heads)]
    ah, rh, vh, bk_rem, w_total = {}, {}, {}, {}, {}
    l_abk, m_rbk = {}, {}
    for s in range(n_sub):
        rs = slice(s * chunk, (s + 1) * chunk)
        cum_s, logw_s = cum[rs], logw[rs]
        total = cum_s[chunk - 1:chunk, :]
        e_in = jnp.exp(cum_s)
        e_neg = jnp.exp(-cum_s)
        e_rem = jnp.exp(total - cum_s)
        w_tot = jnp.exp(total)
        r_hat = r_ref[0, rs, :] * e_in
        a_hat = a_ref[0, rs, :] * jnp.exp(cum_s - logw_s)
        b_all, k_all, v_all = b_ref[0, rs, :], k_ref[0, rs, :], v_ref[0, rs, :]
        b_til, k_til = b_all * e_neg, k_all * e_neg
        b_rem, k_rem = b_all * e_rem, k_all * e_rem
        for h in range(heads):
            sl = slice(h * dh, (h + 1) * dh)
            ah[s, h], rh[s, h], vh[s, h] = a_hat[:, sl], r_hat[:, sl], v_all[:, sl]
            w_total[s, h] = w_tot[:, sl]
            bk_rem[s, h] = jnp.concatenate([b_rem[:, sl], k_rem[:, sl]], axis=0)
            bk_til = jnp.concatenate([b_til[:, sl], k_til[:, sl]], axis=0)
            l_abk[s, h] = jnp.where(strict, _bdot(ah[s, h], bk_til, _NT), 0.0)
            m_rbk[s, h] = jnp.where(incl, _bdot(rh[s, h], bk_til, _NT), 0.0)
    z = {p: jnp.concatenate([ah[p], _bdot(l_abk[p], jnp.concatenate([zeros_cv, vh[p]], axis=0), _NN)], axis=1)
         for p in pairs}
    lp = {p: l_abk[p][:, :chunk] for p in pairs}
    for m in range(n_double):
        z = {p: z[p] + _bdot(lp[p], z[p], _NN) for p in pairs}
        if m + 1 < n_double:
            lp = {p: _bdot(lp[p], lp[p], _NN) for p in pairs}
    state = [state_ref[h] for h in range(heads)]
    y_parts = []
    for s in range(n_sub):
        pr_s = [_bdot(jnp.concatenate([z[s, h][:, :dh], rh[s, h]], axis=0), state[h], _NT) for h in range(heads)]
        uv = [jnp.concatenate([pr_s[h][:chunk] + z[s, h][:, dh:], vh[s, h]], axis=0) for h in range(heads)]
        ys = [pr_s[h][chunk:] + _bdot(m_rbk[s, h], uv[h], _NN) for h in range(heads)]
        state = [state[h] * w_total[s, h] + _bdot(uv[h], bk_rem[s, h], _TN) for h in range(heads)]
        y_parts.append(jnp.concatenate(ys, axis=1))
    for h in range(heads):
        state_ref[h] = state[h]
    y = jnp.concatenate(y_parts, axis=0)
    mean = _head_sum(y, dh) * (1.0 / dh)
    yc = y - mean
    var = _head_sum(yc * yc, dh) * (1.0 / dh)
    yn = yc * lax.rsqrt(var + RWKV_GN_EPS) * gn_ref[0:1, :] + gn_ref[1:2, :]
    bonus = _head_sum(r_ref[0] * k_ref[0] * gn_ref[2:3, :], dh) * v_ref[0]
    o_ref[0] = (yn + bonus) * g_ref[0]


def rwkv7_mixer_fused(r, logw, k, v, a, b, g, gn, heads=RWKV_HEADS, dh=RWKV_DH, chunk=RWKV_CHUNK, block=RWKV_BLOCK):
    B_, S_, W_ = r.shape
    block = min(block, S_)
    assert W_ == heads * dh and S_ % block == 0 and block % chunk == 0
    spec = pl.BlockSpec((1, block, W_), lambda bi, ci: (bi, ci, 0))
    return pl.pallas_call(
        functools.partial(_rwkv7_mixer_kernel, heads=heads, dh=dh, chunk=chunk),
        grid=(B_, S_ // block),
        in_specs=[spec] * 7 + [pl.BlockSpec(gn.shape, lambda bi, ci: (0, 0))],
        out_specs=spec,
        out_shape=jax.ShapeDtypeStruct((B_, S_, W_), F32),
        scratch_shapes=[pltpu.VMEM((heads, dh, dh), F32)],
        compiler_params=pltpu.CompilerParams(dimension_semantics=("arbitrary", "arbitrary")),
        name="rwkv7_mixer",
    )(r, logw, k, v, a, b, g, gn)


def _mla_flash_kernel(q_ref, k_ref, v_ref, o_ref, *, tile):
    h = pl.program_id(1)
    qi = pl.program_id(2)
    q = q_ref[0]
    dv = v_ref.shape[2]

    def kv_tile(j):
        start = pl.multiple_of(j * tile, tile)
        return k_ref[0, pl.ds(start, tile), :], v_ref[0, pl.ds(start, tile), :]

    def update(carry, s, v_t):
        m, l, acc = carry
        m_new = jnp.maximum(m, jnp.max(s, axis=-1, keepdims=True))
        corr = jnp.exp(m - m_new)
        p = jnp.exp(s - m_new)
        l_new = corr * l + jnp.sum(p, axis=-1, keepdims=True)
        acc_new = corr * acc + jnp.dot(p.astype(v_t.dtype), v_t, preferred_element_type=F32)
        return m_new, l_new, acc_new

    def body(j, carry):
        k_t, v_t = kv_tile(j)
        s = lax.dot_general(q, k_t, (_NT, ((), ())), preferred_element_type=F32)
        return update(carry, s, v_t)

    init = (jnp.full((tile, 1), -jnp.inf, F32), jnp.zeros((tile, 1), F32), jnp.zeros((tile, dv), F32))
    carry = lax.fori_loop(0, qi, body, init)
    k_t, v_t = kv_tile(qi)
    s = lax.dot_general(q, k_t, (_NT, ((), ())), preferred_element_type=F32)
    row = lax.broadcasted_iota(jnp.int32, (tile, tile), 0)
    col = lax.broadcasted_iota(jnp.int32, (tile, tile), 1)
    s = jnp.where(col <= row, s, -jnp.inf)
    m, l, acc = update(carry, s, v_t)
    lane_head = lax.broadcasted_iota(jnp.int32, (tile, dv), 1) // MLA_DV
    o_ref[0] = jnp.where(lane_head == h % 2, acc / l, 0.0)


def mla_attention(q, k, v, tile=ATTN_TILE):
    B_, S_, _ = q.shape
    tile = min(tile, S_)
    assert S_ % tile == 0 and MLA_HEAD_PAD == 2 * MLA_DV
    return pl.pallas_call(
        functools.partial(_mla_flash_kernel, tile=tile),
        grid=(B_, MLA_HEADS, S_ // tile),
        in_specs=[pl.BlockSpec((1, tile, MLA_HEAD_PAD), lambda b, h, i: (b, i, h)),
                  pl.BlockSpec((1, S_, MLA_HEAD_PAD), lambda b, h, i: (b, 0, h)),
                  pl.BlockSpec((1, S_, 2 * MLA_DV), lambda b, h, i: (b, 0, h // 2))],
        out_specs=pl.BlockSpec((1, tile, MLA_HEAD_PAD), lambda b, h, i: (b, i, h)),
        out_shape=jax.ShapeDtypeStruct((B_, S_, MLA_QK_W), F32),
        compiler_params=pltpu.CompilerParams(dimension_semantics=("arbitrary", "arbitrary", "arbitrary")),
        name="mla_causal_flash",
    )(q, k, v)


def _out_kernel(x_ref, oa_ref, ob_ref, oc_ref, od_ref, wabc_ref, wd_ref, gb_ref, gd_ref, ln_ref, wr_ref, br_ref,
                x1_ref, x1b_ref, logit_ref, *, alpha):
    ob = ob_ref[...]
    ob = ob * lax.rsqrt(jnp.mean(ob * ob, -1, keepdims=True) + NORM_EPS) * gb_ref[...]
    od = od_ref[...]
    od = od * lax.rsqrt(jnp.sum(od * od, -1, keepdims=True) * (1.0 / MLA_W) + NORM_EPS) * gd_ref[...]
    W = RET_W
    mix = jnp.dot(oa_ref[...].astype(BF16), wabc_ref[0:W, :], preferred_element_type=F32)
    mix += jnp.dot(ob.astype(BF16), wabc_ref[W:2 * W, :], preferred_element_type=F32)
    mix += jnp.dot(oc_ref[...].astype(BF16), wabc_ref[2 * W:3 * W, :], preferred_element_type=F32)
    mix += jnp.dot(od.astype(BF16), wd_ref[...], preferred_element_type=F32)
    x1 = _layer_norm_rows(alpha * x_ref[...] + mix, ln_ref)
    x1_ref[...] = x1
    x1b = x1.astype(BF16)
    x1b_ref[...] = x1b
    logit_ref[...] = jnp.dot(x1b, wr_ref[...], preferred_element_type=F32) + br_ref[...]


def fused_out_proj(x, o_a, o_b, o_c, o_d, w_abc, w_d, g_b, g_d, ln_gb, w_router, b_router, alpha, tile=OUT_TILE):
    N, D = x.shape
    tile = min(tile, N)
    assert N % tile == 0
    n_exp = w_router.shape[1]
    wr = jnp.zeros((D, ROUTER_PAD), F32).at[:, :n_exp].set(w_router).astype(BF16)
    br = jnp.zeros((1, ROUTER_PAD), F32).at[0, :n_exp].set(b_router)
    consts = [w_abc, w_d, g_b, g_d, ln_gb, wr, br]
    tok = lambda w: pl.BlockSpec((tile, w), lambda i: (i, 0))
    full = lambda a: pl.BlockSpec(a.shape, lambda i: (0,) * a.ndim)
    x1, x1b, logits = pl.pallas_call(
        functools.partial(_out_kernel, alpha=alpha),
        grid=(N // tile,),
        in_specs=[tok(D), tok(RET_W), tok(DIL_W), tok(RWKV_W), tok(MLA_QK_W)] + [full(c) for c in consts],
        out_specs=[tok(D), tok(D), tok(ROUTER_PAD)],
        out_shape=[jax.ShapeDtypeStruct((N, D), F32), jax.ShapeDtypeStruct((N, D), BF16),
                   jax.ShapeDtypeStruct((N, ROUTER_PAD), F32)],
        compiler_params=pltpu.CompilerParams(dimension_semantics=("arbitrary",), vmem_limit_bytes=VMEM_LIMIT),
        name="fused_out_proj",
    )(x, o_a, o_b, o_c, o_d, *consts)
    return x1, x1b, logits[:, :n_exp]


def _moe_expert_kernel(tile_e_ref, n_used_ref, x_ref, wg_ref, wl_ref, bg_ref, bl_ref, wd_ref, bd_ref, o_ref):
    del tile_e_ref
    i = pl.program_id(0)

    @pl.when(i < n_used_ref[0])
    def _():
        x = x_ref[...]
        glu = jnp.dot(x, wg_ref[0], preferred_element_type=F32) + bg_ref[0]
        lin = jnp.dot(x, wl_ref[0], preferred_element_type=F32) + bl_ref[0]
        glu = jnp.minimum(glu, SWIGLU_LIMIT)
        lin = jnp.clip(lin, -SWIGLU_LIMIT, SWIGLU_LIMIT)
        act = glu * jax.nn.sigmoid(SWIGLU_ALPHA * glu) * (lin + 1.0)
        o_ref[...] = jnp.dot(act.astype(wd_ref.dtype), wd_ref[0], preferred_element_type=F32) + bd_ref[0]

    @pl.when(i >= n_used_ref[0])
    def _():
        o_ref[...] = jnp.zeros_like(o_ref)


def _deinterleave_kernel(w_ref, even_ref, odd_ref):
    g = 2 * LANES
    i = lax.broadcasted_iota(jnp.int32, (g, g), 0)
    j = lax.broadcasted_iota(jnp.int32, (g, g), 1)
    perm = (i == jnp.where(j < LANES, 2 * j, 2 * (j - LANES) + 1)).astype(BF16)
    for c in range(w_ref.shape[2] // g):
        blk = w_ref[0, :, c * g:(c + 1) * g].astype(BF16)
        sorted_cols = jnp.dot(blk, perm, preferred_element_type=F32)
        even_ref[0, :, c * LANES:(c + 1) * LANES] = sorted_cols[:, :LANES].astype(even_ref.dtype)
        odd_ref[0, :, c * LANES:(c + 1) * LANES] = sorted_cols[:, LANES:].astype(odd_ref.dtype)


def deinterleave_to_bf16(w):
    E, D, F2 = w.shape
    rows = min(DEINT_ROWS, D)
    assert D % rows == 0 and F2 % (2 * LANES) == 0
    out = jax.ShapeDtypeStruct((E, D, F2 // 2), BF16)
    return pl.pallas_call(
        _deinterleave_kernel,
        grid=(E, D // rows),
        in_specs=[pl.BlockSpec((1, rows, F2), lambda e, r: (e, r, 0))],
        out_specs=[pl.BlockSpec((1, rows, F2 // 2), lambda e, r: (e, r, 0))] * 2,
        out_shape=[out, out],
        compiler_params=pltpu.CompilerParams(dimension_semantics=("arbitrary", "arbitrary")),
        name="deinterleave_cast",
    )(w)


def moe_experts(xs, tile_e, n_used, w_glu, w_lin, b_glu, b_lin, w_dn, b_dn, tile):
    n_rows, D = xs.shape
    E, _, F = w_glu.shape
    w_spec = lambda shape: pl.BlockSpec((1,) + shape, lambda i, te, nu: (te[i], 0, 0))
    return pl.pallas_call(
        _moe_expert_kernel,
        grid_spec=pltpu.PrefetchScalarGridSpec(
            num_scalar_prefetch=2,
            grid=(n_rows // tile,),
            in_specs=[pl.BlockSpec((tile, D), lambda i, te, nu: (i, 0)),
                      w_spec((D, F)), w_spec((D, F)), w_spec((1, F)), w_spec((1, F)),
                      w_spec((F, D)), w_spec((1, D))],
            out_specs=pl.BlockSpec((tile, D), lambda i, te, nu: (i, 0)),
        ),
        out_shape=jax.ShapeDtypeStruct((n_rows, D), F32),
        compiler_params=pltpu.CompilerParams(dimension_semantics=("arbitrary",), vmem_limit_bytes=VMEM_LIMIT),
        name="moe_expert_ffn",
    )(tile_e, n_used, xs, w_glu, w_lin, b_glu.reshape(E, 1, F), b_lin.reshape(E, 1, F), w_dn, b_dn.reshape(E, 1, D))


def _combine_kernel(x_ref, y0_ref, y1_ref, y2_ref, y3_ref, gate_ref, ln_ref, o_ref, *, alpha):
    gate = gate_ref[...]
    y = y0_ref[...] * gate[:, 0:1] + y1_ref[...] * gate[:, 1:2] + y2_ref[...] * gate[:, 2:3] + y3_ref[...] * gate[:, 3:4]
    o_ref[...] = _layer_norm_rows(alpha * x_ref[...] + y, ln_ref)


def moe_combine_ln(x1, ys, gate, ln_gb, alpha, tile=OUT_TILE):
    N, D = x1.shape
    tile = min(tile, N)
    assert N % tile == 0 and len(ys) == TOP_K
    tok = lambda w: pl.BlockSpec((tile, w), lambda i: (i, 0))
    return pl.pallas_call(
        functools.partial(_combine_kernel, alpha=alpha),
        grid=(N // tile,),
        in_specs=[tok(D)] * (1 + TOP_K) + [tok(TOP_K), pl.BlockSpec(ln_gb.shape, lambda i: (0, 0))],
        out_specs=tok(D),
        out_shape=jax.ShapeDtypeStruct((N, D), F32),
        compiler_params=pltpu.CompilerParams(dimension_semantics=("arbitrary",), vmem_limit_bytes=VMEM_LIMIT),
        name="moe_combine_ln",
    )(x1, *ys, gate, ln_gb)


def moe_layer(x1, x1b, logits, w_gu, b_gu, w_dn, b_dn, ln_gb, alpha, tile=MOE_TILE):
    n_tok, D = x1.shape
    n_exp = logits.shape[1]
    top_val, top_idx = lax.top_k(logits, TOP_K)
    gate = jax.nn.softmax(top_val, axis=-1)
    n_assign = n_tok * TOP_K
    e_flat = top_idx.reshape(-1).astype(jnp.int32)
    e_sorted, order = lax.sort((e_flat, jnp.arange(n_assign, dtype=jnp.int32)), num_keys=1)
    counts = jnp.sum((e_flat[:, None] == jnp.arange(n_exp, dtype=jnp.int32)[None, :]).astype(jnp.int32), axis=0)
    padded = (counts + tile - 1) // tile * tile
    start = jnp.cumsum(counts) - counts
    pend = jnp.cumsum(padded)
    pstart = pend - padded
    n_tiles = (n_assign + n_exp * (tile - 1) + tile - 1) // tile
    n_rows = n_tiles * tile
    tile_e = jnp.minimum(jnp.searchsorted(pend, jnp.arange(n_tiles, dtype=jnp.int32) * tile, side='right'),
                         n_exp - 1).astype(jnp.int32)
    n_used = (pend[-1] // tile).astype(jnp.int32).reshape(1)
    row = jnp.arange(n_rows, dtype=jnp.int32)
    row_e = jnp.repeat(tile_e, tile)
    within = row - pstart[row_e]
    valid = within < counts[row_e]
    row_assign = order[jnp.clip(start[row_e] + within, 0, n_assign - 1)]
    row_tok = jnp.where(valid, row_assign // TOP_K, 0)
    dest = pstart[e_sorted] + (jnp.arange(n_assign, dtype=jnp.int32) - start[e_sorted])
    _, pos = lax.sort((order, dest), num_keys=1)
    pos = pos.reshape(n_tok, TOP_K)
    w_glu, w_lin = deinterleave_to_bf16(w_gu)
    yb = moe_experts(x1b[row_tok], tile_e, n_used, w_glu, w_lin, b_gu[:, 0::2], b_gu[:, 1::2], w_dn.astype(BF16), b_dn, tile)
    return moe_combine_ln(x1, [yb[pos[:, s]] for s in range(TOP_K)], gate, ln_gb, alpha)


def kernel(x, w_in, w_out, ret_norm_g, dil_norm_g, rwkv_mu, rwkv_w0, rwkv_w_up, rwkv_a0, rwkv_a_up, rwkv_g_up, rwkv_k_k, rwkv_k_a, rwkv_r_k, rwkv_ln_g, rwkv_ln_b, rwkv_vres_down, rwkv_vres_mu, rwkv_v0, rwkv_v_up, mla_q_norm_g, mla_w_q_up, mla_kv_norm_g, mla_w_kv_up, mla_out_norm_g, ln1_g, ln1_b, router_w, router_b, exp_w_gu, exp_b_gu, exp_w_dn, exp_b_dn, ln2_g, ln2_b):
    depth = w_in.shape[0]
    B_, S_, D = x.shape
    alpha = (2 * depth) ** 0.25
    tables = rope_tables(S_)
    ret_tabs = retention_tables()
    v_first = None
    for l in range(depth):
        w_cat = prep_in_weights(w_in[l], rwkv_vres_down[l - 1] if l > 0 else None)
        rw_params = prep_rwkv_params(rwkv_mu[l], rwkv_w0[l], rwkv_w_up[l], rwkv_a0[l], rwkv_a_up[l], rwkv_g_up[l],
                                     rwkv_k_k[l], rwkv_k_a[l], rwkv_vres_mu[l - 1] if l > 0 else None,
                                     rwkv_v0[l - 1] if l > 0 else None, rwkv_v_up[l - 1] if l > 0 else None)
        mla_params = prep_mla_params(mla_q_norm_g[l], mla_w_q_up[l], mla_kv_norm_g[l], mla_w_kv_up[l])
        (ret_q, ret_k, ret_v, ret_g, dil_q, dil_k, dil_v, rw_r, rw_lw, rw_k, rw_v, rw_a, rw_b, rw_g,
         mla_q, mla_k, mla_v) = fused_in_proj(x, w_cat, tables, rw_params, mla_params, v_first)
        if l == 0:
            v_first = rw_v
        o_a = retention_fused(ret_q, ret_k, ret_v, ret_g, ret_norm_g[l], ret_tabs)
        o_b = dilated_fused(dil_q, dil_k, dil_v)
        gn = jnp.stack([rwkv_ln_g[l], rwkv_ln_b[l], rwkv_r_k[l].reshape(-1)])
        o_c = rwkv7_mixer_fused(rw_r, rw_lw, rw_k, rw_v, rw_a, rw_b, rw_g, gn)
        o_d = mla_attention(mla_q, mla_k, mla_v)
        w_abc, w_d, g_d = prep_out_weights(w_out[l], mla_out_norm_g[l])
        n_tok = B_ * S_
        flat = lambda t: t.reshape(n_tok, t.shape[-1])
        x1, x1b, logits = fused_out_proj(flat(x), flat(o_a), flat(o_b), flat(o_c), flat(o_d), w_abc, w_d,
                                         dil_norm_g[l].reshape(1, DIL_W), g_d, jnp.stack([ln1_g[l], ln1_b[l]]),
                                         router_w[l], router_b[l], alpha)
        x = moe_layer(x1, x1b, logits, exp_w_gu[l], exp_b_gu[l], exp_w_dn[l], exp_b_dn[l],
                      jnp.stack([ln2_g[l], ln2_b[l]]), alpha).reshape(B_, S_, D)
    return x
```

```python
import functools

import numpy as np
import jax
import jax.numpy as jnp
from jax import lax
from jax.experimental import pallas as pl
from jax.experimental.pallas import tpu as pltpu

F32 = jnp.float32
BF16 = jnp.bfloat16

LANES = 128
BLK = 128
LN_EPS = 1e-5
NORM_EPS = 1e-6

RET_HEADS, RET_DK, RET_DV = 4, 32, 64
RET_QK = RET_HEADS * RET_DK
RET_W = RET_HEADS * RET_DV
RET_THETA = 10000.0
DIL_HEADS, DIL_DH = 4, 64
DIL_W = DIL_HEADS * DIL_DH
DIL_PATTERNS = ((128, 1), (512, 4), (2048, 16))
ROPE_THETA = 500000.0
ROPE_ROT_DIM = DIL_DH // 4
RWKV_HEADS, RWKV_DH = 4, 64
RWKV_W = RWKV_HEADS * RWKV_DH
DECAY_LORA, AAA_LORA, MV_LORA, GATE_LORA = 64, 64, 32, 128
RWKV_GN_EPS = 64e-5
MLA_HEADS, MLA_NOPE, MLA_ROPE, MLA_DV = 4, 64, 32, 64
MLA_W = MLA_HEADS * MLA_DV
Q_LORA, KV_LORA = 256, 128
MLA_THETA = 10000.0
TOP_K = 4
SWIGLU_LIMIT, SWIGLU_ALPHA = 7.0, 1.702

RET_SPLITS = (RET_QK, RET_QK, RET_W, RET_W)
DIL_SPLITS = (DIL_W, DIL_W, DIL_W)
RWKV_SPLITS = (RWKV_W, RWKV_W, RWKV_W, DECAY_LORA, AAA_LORA, GATE_LORA)
MLA_SPLITS = (Q_LORA, KV_LORA, MLA_ROPE)
A_END = sum(RET_SPLITS)
B_END = A_END + sum(DIL_SPLITS)
C_END = B_END + sum(RWKV_SPLITS)
N_IN = C_END + sum(MLA_SPLITS)

RWKV_CHUNK = 64
RWKV_BLOCK = 256

MLA_HEAD_PAD = 128
MLA_QK_W = MLA_HEADS * MLA_HEAD_PAD
ATTN_TILE = 512

IN_RET_W = 4 * RET_QK + 2 * RET_W
IN_DIL_W = 5 * DIL_W
IN_RWKV_W = sum(RWKV_SPLITS) + LANES
IN_MLA_W = Q_LORA + KV_LORA + LANES
IN_COLS = IN_RET_W + IN_DIL_W + IN_RWKV_W + IN_MLA_W
IN_TILE = 256
OUT_TILE = 512
ROUTER_PAD = 128
DIL_PAIR_W = 2 * DIL_DH
DEINT_ROWS = 256
MOE_TILE = 512
VMEM_LIMIT = 48 * 1024 * 1024
DIL_VMEM_LIMIT = 56 * 1024 * 1024


def split_cols(p, sizes):
    idx = np.cumsum(sizes)[:-1].tolist()
    return jnp.split(p, idx, axis=-1)


def rope_table(n_pos, rot_dim, theta):
    inv_freq = 1.0 / (theta ** (jnp.arange(0, rot_dim, 2, dtype=F32) / rot_dim))
    ang = jnp.arange(n_pos, dtype=F32)[:, None] * inv_freq[None, :]
    return jnp.cos(ang), jnp.sin(ang)


def _bdot(a, b, dims):
    return lax.dot_general(a.astype(BF16), b.astype(BF16), ((dims[0], dims[1]), ((), ())), preferred_element_type=F32)


_NN = ((1,), (0,))
_NT = ((1,), (1,))
_TN = ((0,), (0,))


def _split3(x):
    h1 = x.astype(BF16)
    r1 = x - h1.astype(F32)
    h2 = r1.astype(BF16)
    h3 = (r1 - h2.astype(F32)).astype(BF16)
    return h1, h2, h3


def _head_sum(x, dh):
    w = x.shape[1]
    i = lax.broadcasted_iota(jnp.int32, (w, w), 0) // dh
    j = lax.broadcasted_iota(jnp.int32, (w, w), 1) // dh
    ones = (i == j).astype(BF16)
    return sum(jnp.dot(p, ones, preferred_element_type=F32) for p in _split3(x))


def _layer_norm_rows(h, ln_ref):
    mu = jnp.mean(h, -1, keepdims=True)
    hc = h - mu
    var = jnp.mean(hc * hc, -1, keepdims=True)
    return hc * lax.rsqrt(var + LN_EPS) * ln_ref[0:1, :] + ln_ref[1:2, :]


def _rot_half_cols(w, heads, dh, rot):
    d_in = w.shape[0]
    w = w.reshape(d_in, heads, dh)
    half = rot // 2
    sw = jnp.concatenate([-w[..., half:rot], w[..., :half], jnp.zeros((d_in, heads, dh - rot), w.dtype)], axis=-1)
    return sw.reshape(d_in, heads * dh)


def _rope_lanes(cs, heads, dh, rot, lead=0):
    cos, sin = cs
    S_ = cos.shape[0]
    c = jnp.concatenate([jnp.ones((S_, lead), F32), cos, cos, jnp.ones((S_, dh - lead - rot), F32)], axis=1)
    s = jnp.concatenate([jnp.zeros((S_, lead), F32), sin, sin, jnp.zeros((S_, dh - lead - rot), F32)], axis=1)
    return jnp.tile(c, (1, heads)), jnp.tile(s, (1, heads))


def rope_tables(S_):
    ret = _rope_lanes(rope_table(S_, RET_DK, RET_THETA), RET_HEADS, RET_DK, RET_DK)
    dil = _rope_lanes(rope_table(S_, ROPE_ROT_DIM, ROPE_THETA), DIL_HEADS, DIL_DH, ROPE_ROT_DIM)
    cos, sin = rope_table(S_, MLA_ROPE, MLA_THETA)
    mq = _rope_lanes((cos, sin), MLA_HEADS, MLA_HEAD_PAD, MLA_ROPE, lead=MLA_NOPE)
    mk = jnp.concatenate([cos, cos, sin, sin, jnp.zeros((S_, LANES - 2 * MLA_ROPE), F32)], axis=1)
    return ret + dil + mq + (mk,)


def prep_in_weights(w_in_l, vres_down_l):
    D = w_in_l.shape[0]
    a_q, a_k, a_v, a_g = split_cols(w_in_l[:, :A_END], RET_SPLITS)
    b_q, b_k, b_v = split_cols(w_in_l[:, A_END:B_END], DIL_SPLITS)
    d_cq, d_ckv, d_kr = split_cols(w_in_l[:, C_END:N_IN], MLA_SPLITS)
    vres = jnp.zeros((D, LANES), F32)
    if vres_down_l is not None:
        vres = vres.at[:, :MV_LORA].set(vres_down_l)
    cols = [a_q, _rot_half_cols(a_q, RET_HEADS, RET_DK, RET_DK), a_k, _rot_half_cols(a_k, RET_HEADS, RET_DK, RET_DK), a_v, a_g,
            b_q, _rot_half_cols(b_q, DIL_HEADS, DIL_DH, ROPE_ROT_DIM), b_k, _rot_half_cols(b_k, DIL_HEADS, DIL_DH, ROPE_ROT_DIM), b_v,
            w_in_l[:, B_END:C_END], vres,
            d_cq, d_ckv, d_kr, _rot_half_cols(d_kr, 1, MLA_ROPE, MLA_ROPE), jnp.zeros((D, LANES - 2 * MLA_ROPE), F32)]
    w = jnp.concatenate(cols, axis=1)
    assert w.shape[1] == IN_COLS
    return w.astype(BF16)


def prep_rwkv_params(mu, w0, w_up, a0, a_up, g_up, k_k, k_a, vres_mu, v0, v_up):
    W = RWKV_W
    mu_ext = jnp.zeros((1, IN_RWKV_W), F32).at[0, :mu.shape[0]].set(mu)
    v_up_ext = jnp.zeros((LANES, W), F32)
    v0_ext = jnp.zeros((1, W), F32)
    if vres_mu is not None:
        mu_ext = mu_ext.at[0, mu.shape[0]:mu.shape[0] + MV_LORA].set(vres_mu)
        v_up_ext = v_up_ext.at[:MV_LORA].set(v_up)
        v0_ext = v0.reshape(1, W)
    wa_up = jnp.zeros((LANES, 2 * W), F32).at[:DECAY_LORA, :W].set(w_up).at[DECAY_LORA:, W:].set(a_up)
    return [mu_ext, wa_up.astype(BF16), jnp.concatenate([w0, a0]).reshape(1, 2 * W), g_up.astype(BF16),
            jnp.stack([k_k, k_a]), v_up_ext.astype(BF16), v0_ext]


def prep_mla_params(q_norm_g, w_q_up, kv_norm_g, w_kv_up):
    ql = w_q_up.shape[0]
    wq = w_q_up.reshape(ql, MLA_HEADS, MLA_NOPE + MLA_ROPE)
    pad = jnp.zeros((ql, MLA_HEADS, MLA_HEAD_PAD - MLA_NOPE - MLA_ROPE), F32)
    q_main = jnp.concatenate([wq, pad], axis=-1).reshape(ql, MLA_QK_W)
    rope = wq[..., MLA_NOPE:]
    half = MLA_ROPE // 2
    q_rot = jnp.concatenate([jnp.zeros((ql, MLA_HEADS, MLA_NOPE), F32), -rope[..., half:], rope[..., :half], pad], axis=-1)
    w_q = jnp.concatenate([q_main, q_rot.reshape(ql, MLA_QK_W)], axis=1).astype(BF16)
    kl = w_kv_up.shape[0]
    wkv = w_kv_up.reshape(kl, MLA_HEADS, MLA_NOPE + MLA_DV)
    k_main = jnp.concatenate([wkv[..., :MLA_NOPE], jnp.zeros((kl, MLA_HEADS, MLA_HEAD_PAD - MLA_NOPE), F32)], axis=-1)
    w_kv = jnp.concatenate([k_main.reshape(kl, MLA_QK_W), wkv[..., MLA_NOPE:].reshape(kl, MLA_W)], axis=1).astype(BF16)
    return [q_norm_g.reshape(1, Q_LORA), w_q, kv_norm_g.reshape(1, KV_LORA), w_kv]


def prep_out_weights(w_out_l, mla_out_norm_g):
    W = RET_W
    w_abc = w_out_l[:3 * W].astype(BF16)
    w_d = w_out_l[3 * W:].reshape(MLA_HEADS, MLA_DV, -1)
    g_d = mla_out_norm_g.reshape(MLA_HEADS, MLA_DV)
    zw = jnp.zeros_like(w_d[0])
    zg = jnp.zeros_like(g_d[0])
    rows, gains = [], []
    for h in range(MLA_HEADS):
        rows += [w_d[h], zw] if h % 2 == 0 else [zw, w_d[h]]
        gains += [g_d[h], zg] if h % 2 == 0 else [zg, g_d[h]]
    return w_abc, jnp.concatenate(rows, axis=0).astype(BF16), jnp.concatenate(gains).reshape(1, MLA_QK_W)


def retention_tables():
    log_gamma = jnp.log(1.0 - 2.0 ** (-5.0 - jnp.arange(RET_HEADS, dtype=F32)))
    idx = jnp.arange(BLK, dtype=F32)
    dist = idx[:, None] - idx[None, :]
    inner = jnp.where(dist >= 0, jnp.exp(jnp.maximum(dist, 0.0)[None] * log_gamma[:, None, None]), 0.0)
    zeta = jnp.exp((BLK - 1 - idx)[None, :] * log_gamma[:, None])
    xi = jnp.exp((idx + 1.0)[None, :] * log_gamma[:, None])
    zeta_k = jnp.repeat(zeta.T, RET_DK, axis=1)
    xi_v = jnp.repeat(xi.T, RET_DV, axis=1)
    row_h = jnp.arange(RET_QK) // RET_DK
    col_h = jnp.arange(RET_W) // RET_DV
    same = row_h[:, None] == col_h[None, :]
    state_decay = jnp.where(same, jnp.exp(BLK * log_gamma)[row_h][:, None], 0.0)
    return inner, zeta_k, xi_v, state_decay, same.astype(F32)


def _in_kernel(x_ref, w_ref, cr_ref, sr_ref, cd_ref, sd_ref, cq_ref, sq_ref, ck_ref,
               mu_ref, wa_up_ref, w0a0_ref, g_up_ref, kk_ka_ref, v_up_ref, v0_ref, vfirst_ref,
               qn_ref, wq_ref, kvn_ref, wkv_ref,
               ret_q, ret_k, ret_v, ret_g, dil_q, dil_k, dil_v,
               rw_r, rw_lw, rw_k, rw_v, rw_a, rw_b, rw_g, mla_q, mla_k, mla_v,
               carry_ref, *, has_vres):
    tm = x_ref.shape[1]

    @pl.when(pl.program_id(1) == 0)
    def _():
        carry_ref[...] = jnp.zeros_like(carry_ref)

    xb = x_ref[0].astype(BF16)
    o1, o2, o3 = IN_RET_W, IN_RET_W + IN_DIL_W, IN_RET_W + IN_DIL_W + IN_RWKV_W
    pa = jnp.dot(xb, w_ref[:, :o1], preferred_element_type=F32)
    pb = jnp.dot(xb, w_ref[:, o1:o2], preferred_element_type=F32)
    pc = jnp.dot(xb, w_ref[:, o2:o3], preferred_element_type=F32)
    pd = jnp.dot(xb, w_ref[:, o3:], preferred_element_type=F32)

    qk = RET_QK
    cr, sr = cr_ref[...], sr_ref[...]
    ret_q[0] = (pa[:, 0:qk] * cr + pa[:, qk:2 * qk] * sr).astype(ret_q.dtype)
    ret_k[0] = ((pa[:, 2 * qk:3 * qk] * cr + pa[:, 3 * qk:4 * qk] * sr) * (RET_DK ** -0.5)).astype(ret_k.dtype)
    ret_v[0] = pa[:, 4 * qk:4 * qk + RET_W].astype(ret_v.dtype)
    ret_g[0] = pa[:, 4 * qk + RET_W:]

    cd, sd = cd_ref[...], sd_ref[...]
    dil_q[0] = pb[:, 0:DIL_W] * cd + pb[:, DIL_W:2 * DIL_W] * sd
    dil_k[0] = pb[:, 2 * DIL_W:3 * DIL_W] * cd + pb[:, 3 * DIL_W:4 * DIL_W] * sd
    dil_v[0] = pb[:, 4 * DIL_W:]

    row = lax.broadcasted_iota(jnp.int32, pc.shape, 0)
    prev = jnp.where(row == 0, carry_ref[...], pltpu.roll(pc, 1, 0))
    carry_ref[...] = pc[tm - 1:tm, :]
    ps = pc + (prev - pc) * mu_ref[...]
    W = RWKV_W
    r, k, v = ps[:, 0:W], ps[:, W:2 * W], ps[:, 2 * W:3 * W]
    wd_ad = ps[:, 3 * W:3 * W + LANES]
    lane = lax.broadcasted_iota(jnp.int32, wd_ad.shape, 1)
    lora_in = jnp.where(lane < DECAY_LORA, jnp.tanh(wd_ad), wd_ad)
    lora = jnp.dot(lora_in.astype(BF16), wa_up_ref[...], preferred_element_type=F32) + w0a0_ref[...]
    w_raw = -jax.nn.softplus(-lora[:, :W]) - 0.5
    a_sig = jax.nn.sigmoid(lora[:, W:])
    gd = ps[:, 3 * W + LANES:3 * W + 2 * LANES]
    rw_g[0] = jnp.dot(jax.nn.sigmoid(gd).astype(BF16), g_up_ref[...], preferred_element_type=F32)
    kk = k * kk_ka_ref[0:1, :]
    kk = kk / jnp.maximum(jnp.sqrt(_head_sum(kk * kk, RWKV_DH)), 1e-12)
    if has_vres:
        vd = ps[:, 3 * W + 2 * LANES:]
        mix = jax.nn.sigmoid(jnp.dot(vd.astype(BF16), v_up_ref[...], preferred_element_type=F32) + v0_ref[...])
        v = v + (vfirst_ref[0] - v) * mix
    rw_r[0] = r
    rw_lw[0] = -jnp.exp(w_raw)
    rw_k[0] = k * (1.0 + (a_sig - 1.0) * kk_ka_ref[1:2, :])
    rw_v[0] = v
    rw_a[0] = -kk
    rw_b[0] = kk * a_sig

    c_q = pd[:, :Q_LORA]
    c_q = c_q * lax.rsqrt(jnp.mean(c_q * c_q, -1, keepdims=True) + NORM_EPS) * qn_ref[...]
    q2 = jnp.dot(c_q.astype(BF16), wq_ref[...], preferred_element_type=F32)
    scale = (MLA_NOPE + MLA_ROPE) ** -0.5
    mla_q[0] = ((q2[:, :MLA_QK_W] * cq_ref[...] + q2[:, MLA_QK_W:] * sq_ref[...]) * scale).astype(mla_q.dtype)
    c_kv = pd[:, Q_LORA:Q_LORA + KV_LORA]
    c_kv = c_kv * lax.rsqrt(jnp.mean(c_kv * c_kv, -1, keepdims=True) + NORM_EPS) * kvn_ref[...]
    kv = jnp.dot(c_kv.astype(BF16), wkv_ref[...], preferred_element_type=F32)
    kr = pd[:, Q_LORA + KV_LORA:] * ck_ref[...]
    i = lax.broadcasted_iota(jnp.int32, (LANES, MLA_QK_W), 0)
    j = lax.broadcasted_iota(jnp.int32, (LANES, MLA_QK_W), 1) % MLA_HEAD_PAD
    place = ((i < 2 * MLA_ROPE) & (j == MLA_NOPE + i % MLA_ROPE)).astype(BF16)
    k_pe = sum(jnp.dot(p, place, preferred_element_type=F32) for p in _split3(kr)[:2])
    mla_k[0] = (kv[:, :MLA_QK_W] + k_pe).astype(mla_k.dtype)
    mla_v[0] = kv[:, MLA_QK_W:].astype(mla_v.dtype)


def fused_in_proj(x, w_cat, tables, rw_params, mla_params, v_first, tile=IN_TILE):
    B_, S_, D = x.shape
    tile = min(tile, S_)
    assert S_ % tile == 0
    has_vres = v_first is not None
    tok = lambda w: pl.BlockSpec((1, tile, w), lambda b, i: (b, i, 0))
    pos = lambda a: pl.BlockSpec((tile, a.shape[1]), lambda b, i: (i, 0))
    full = lambda a: pl.BlockSpec(a.shape, lambda b, i: (0,) * a.ndim)
    if has_vres:
        vf_spec = tok(RWKV_W)
    else:
        v_first = jnp.zeros((1, tile, RWKV_W), F32)
        vf_spec = pl.BlockSpec((1, tile, RWKV_W), lambda b, i: (0, 0, 0))
    f32o = lambda w: jax.ShapeDtypeStruct((B_, S_, w), F32)
    bfo = lambda w: jax.ShapeDtypeStruct((B_, S_, w), BF16)
    out_shape = ([bfo(RET_QK), bfo(RET_QK), bfo(RET_W), f32o(RET_W)] + [f32o(DIL_W)] * 3 + [f32o(RWKV_W)] * 7
                 + [bfo(MLA_QK_W), bfo(MLA_QK_W), bfo(MLA_W)])
    return pl.pallas_call(
        functools.partial(_in_kernel, has_vres=has_vres),
        grid=(B_, S_ // tile),
        in_specs=[tok(D), full(w_cat)] + [pos(t) for t in tables] + [full(p) for p in rw_params] + [vf_spec]
                 + [full(p) for p in mla_params],
        out_specs=[tok(s.shape[-1]) for s in out_shape],
        out_shape=out_shape,
        scratch_shapes=[pltpu.VMEM((1, IN_RWKV_W), F32)],
        compiler_params=pltpu.CompilerParams(dimension_semantics=("arbitrary", "arbitrary"), vmem_limit_bytes=VMEM_LIMIT),
        name="fused_in_proj",
    )(x, w_cat, *tables, *rw_params, v_first, *mla_params)


def _retention_kernel(q_ref, k_ref, v_ref, g_ref, ng_ref, inner_ref, zeta_ref, xi_ref, sdec_ref, smask_ref, o_ref, state_ref):
    @pl.when(pl.program_id(1) == 0)
    def _():
        state_ref[...] = jnp.zeros_like(state_ref)

    q, k, v = q_ref[0], k_ref[0], v_ref[0]
    q_head = lax.broadcasted_iota(jnp.int32, q.shape, 1) // RET_DK
    v_head = lax.broadcasted_iota(jnp.int32, v.shape, 1) // RET_DV
    zero_q = jnp.zeros_like(q)
    scores = [_bdot(jnp.where(q_head == h, q, zero_q), k, _NT) * inner_ref[h] for h in range(RET_HEADS)]
    o_heads = [_bdot(s, v, _NN) for s in scores]
    o = _bdot(q, state_ref[...], _NN) * xi_ref[...]
    for h in range(RET_HEADS):
        o = o + jnp.where(v_head == h, o_heads[h], 0.0)
    state_ref[...] = state_ref[...] * sdec_ref[...] + _bdot(k.astype(F32) * zeta_ref[...], v, _TN) * smask_ref[...]
    o = o * lax.rsqrt(_head_sum(o * o, RET_DV) * (1.0 / RET_DV) + NORM_EPS) * ng_ref[...]
    g = g_ref[0]
    o_ref[0] = g * jax.nn.sigmoid(g) * o


def retention_fused(q, k, v, g, norm_g, tables):
    B_, S_, _ = q.shape
    assert S_ % BLK == 0
    tok = lambda w: pl.BlockSpec((1, BLK, w), lambda b, c: (b, c, 0))
    full = lambda a: pl.BlockSpec(a.shape, lambda b, c: (0,) * a.ndim)
    ng = norm_g.reshape(1, RET_W)
    return pl.pallas_call(
        _retention_kernel,
        grid=(B_, S_ // BLK),
        in_specs=[tok(RET_QK), tok(RET_QK), tok(RET_W), tok(RET_W), full(ng)] + [full(t) for t in tables],
        out_specs=tok(RET_W),
        out_shape=jax.ShapeDtypeStruct((B_, S_, RET_W), F32),
        scratch_shapes=[pltpu.VMEM((RET_QK, RET_W), F32)],
        compiler_params=pltpu.CompilerParams(dimension_semantics=("arbitrary", "arbitrary")),
        name="retention_chunk",
    )(q, k, v, g, ng, *tables)


def _dilated_kernel(q_ref, k_ref, v_ref, o_ref, m_ref, l_ref, *, dils):
    S_ = q_ref.shape[1]
    lane_head = lax.broadcasted_iota(jnp.int32, (BLK, DIL_PAIR_W), 1) // DIL_DH
    i_idx = lax.broadcasted_iota(jnp.int32, (BLK, 2 * BLK), 0)
    j_idx = lax.broadcasted_iota(jnp.int32, (BLK, 2 * BLK), 1)
    scale = DIL_DH ** -0.5

    def rows(start, n, d):
        return pl.ds(start, n) if d == 1 else pl.ds(start, n, stride=d)

    for pi, d in enumerate(dils):
        def tile(t, carry, d=d, first=(pi == 0)):
            nb, r = t // d, t % d
            kb = jnp.maximum(nb - 1, 0)
            q_rows = rows(nb * (BLK * d) + r, BLK, d)
            k_rows = rows(kb * (BLK * d) + r, 2 * BLK, d)
            q = q_ref[0, q_rows, :]
            kk = k_ref[0, k_rows, :].astype(BF16)
            vv = v_ref[0, k_rows, :].astype(BF16)
            delta = (nb - kb) * BLK + i_idx - j_idx
            valid = (delta >= 0) & (delta <= BLK)
            zero_q = jnp.zeros_like(q)
            s = [jnp.where(valid, _bdot(jnp.where(lane_head == h, q, zero_q), kk, _NT) * scale, -jnp.inf) for h in range(2)]
            m = [jnp.max(x, axis=-1, keepdims=True) for x in s]
            p = [jnp.exp(s[h] - m[h]) for h in range(2)]
            l = [jnp.sum(x, axis=-1, keepdims=True) for x in p]
            pv = [jnp.dot(p[h].astype(BF16), vv, preferred_element_type=F32) for h in range(2)]
            m_t = jnp.where(lane_head == 0, m[0], m[1])
            l_t = jnp.where(lane_head == 0, l[0], l[1])
            pv_t = jnp.where(lane_head == 0, pv[0], pv[1])
            if first:
                m_ref[q_rows, :] = m_t
                l_ref[q_rows, :] = l_t
                o_ref[0, q_rows, :] = pv_t
            else:
                m_old = m_ref[q_rows, :]
                m_new = jnp.maximum(m_old, m_t)
                c_old = jnp.exp(m_old - m_new)
                c_t = jnp.exp(m_t - m_new)
                m_ref[q_rows, :] = m_new
                l_ref[q_rows, :] = l_ref[q_rows, :] * c_old + l_t * c_t
                o_ref[0, q_rows, :] = o_ref[0, q_rows, :] * c_old + pv_t * c_t
            return carry

        lax.fori_loop(0, S_ // BLK, tile, 0)

    def finish(t, carry):
        r0 = pl.multiple_of(t * BLK, BLK)
        o_ref[0, pl.ds(r0, BLK), :] = o_ref[0, pl.ds(r0, BLK), :] / l_ref[pl.ds(r0, BLK), :]
        return carry

    lax.fori_loop(0, S_ // BLK, finish, 0)


def dilated_fused(q, k, v):
    B_, S_, W_ = q.shape
    dils = tuple(d for _, d in DIL_PATTERNS)
    assert all(w // d == BLK for w, d in DIL_PATTERNS) and S_ % (2 * BLK * max(dils)) == 0
    spec = pl.BlockSpec((1, S_, DIL_PAIR_W), lambda b, hp: (b, 0, hp))
    return pl.pallas_call(
        functools.partial(_dilated_kernel, dils=dils),
        grid=(B_, W_ // DIL_PAIR_W),
        in_specs=[spec] * 3,
        out_specs=spec,
        out_shape=jax.ShapeDtypeStruct((B_, S_, W_), F32),
        scratch_shapes=[pltpu.VMEM((S_, DIL_PAIR_W), F32)] * 2,
        compiler_params=pltpu.CompilerParams(dimension_semantics=("arbitrary", "arbitrary"), vmem_limit_bytes=DIL_VMEM_LIMIT),
        name="dilated_attention",
    )(q, k, v)


def _rwkv7_mixer_kernel(r_ref, lw_ref, k_ref, v_ref, a_ref, b_ref, g_ref, gn_ref, o_ref, state_ref, *, heads, dh, chunk):
    rows = r_ref.shape[1]
    n_sub = rows // chunk
    n_double = int(np.log2(chunk))
    assert 2 ** n_double == chunk and n_sub * chunk == rows

    @pl.when(pl.program_id(1) == 0)
    def _():
        state_ref[...] = jnp.zeros_like(state_ref)

    ri = lax.broadcasted_iota(jnp.int32, (rows, rows), 0)
    ci = lax.broadcasted_iota(jnp.int32, (rows, rows), 1)
    tri = ((ci <= ri) & (ci >= (ri // chunk) * chunk)).astype(BF16)
    logw = lw_ref[0]
    cum = sum(jnp.dot(tri, piece, preferred_element_type=F32) for piece in _split3(logw))

    row = lax.broadcasted_iota(jnp.int32, (chunk, 2 * chunk), 0)
    col = lax.broadcasted_iota(jnp.int32, (chunk, 2 * chunk), 1) % chunk
    strict = col < row
    incl = col <= row
    zeros_cv = jnp.zeros((chunk, dh), F32)

    pairs = [(s, h) for s in range(n_sub) for h in range(heads)]
    ah, rh, vh, bk_rem, w_total = {}, {}, {}, {}, {}
    l_abk, m_rbk = {}, {}
    for s in range(n_sub):
        rs = slice(s * chunk, (s + 1) * chunk)
        cum_s, logw_s = cum[rs], logw[rs]
        total = cum_s[chunk - 1:chunk, :]
        e_in = jnp.exp(cum_s)
        e_neg = jnp.exp(-cum_s)
        e_rem = jnp.exp(total - cum_s)
        w_tot = jnp.exp(total)
        r_hat = r_ref[0, rs, :] * e_in
        a_hat = a_ref[0, rs, :] * jnp.exp(cum_s - logw_s)
        b_all, k_all, v_all = b_ref[0, rs, :], k_ref[0, rs, :], v_ref[0, rs, :]
        b_til, k_til = b_all * e_neg, k_all * e_neg
        b_rem, k_rem = b_all * e_rem, k_all * e_rem
        for h in range(heads):
            sl = slice(h * dh, (h + 1) * dh)
            ah[s, h], rh[s, h], vh[s, h] = a_hat[:, sl], r_hat[:, sl], v_all[:, sl]
            w_total[s, h] = w_tot[:, sl]
            bk_rem[s, h] = jnp.concatenate([b_rem[:, sl], k_rem[:, sl]], axis=0)
            bk_til = jnp.concatenate([b_til[:, sl], k_til[:, sl]], axis=0)
            l_abk[s, h] = jnp.where(strict, _bdot(ah[s, h], bk_til, _NT), 0.0)
            m_rbk[s, h] = jnp.where(incl, _bdot(rh[s, h], bk_til, _NT), 0.0)
    z = {p: jnp.concatenate([ah[p], _bdot(l_abk[p], jnp.concatenate([zeros_cv, vh[p]], axis=0), _NN)], axis=1)
         for p in pairs}
    lp = {p: l_abk[p][:, :chunk] for p in pairs}
    for m in range(n_double):
        z = {p: z[p] + _bdot(lp[p], z[p], _NN) for p in pairs}
        if m + 1 < n_double:
            lp = {p: _bdot(lp[p], lp[p], _NN) for p in pairs}
    state = [state_ref[h] for h in range(heads)]
    y_parts = []
    for s in range(n_sub):
        pr_s = [_bdot(jnp.concatenate([z[s, h][:, :dh], rh[s, h]], axis=0), state[h], _NT) for h in range(heads)]
        uv = [jnp.concatenate([pr_s[h][:chunk] + z[s, h][:, dh:], vh[s, h]], axis=0) for h in range(heads)]
        ys = [pr_s[h][chunk:] + _bdot(m_rbk[s, h], uv[h], _NN) for h in range(heads)]
        state = [state[h] * w_total[s, h] + _bdot(uv[h], bk_rem[s, h], _TN) for h in range(heads)]
        y_parts.append(jnp.concatenate(ys, axis=1))
    for h in range(heads):
        state_ref[h] = state[h]
    y = jnp.concatenate(y_parts, axis=0)
    mean = _head_sum(y, dh) * (1.0 / dh)
    yc = y - mean
    var = _head_sum(yc * yc, dh) * (1.0 / dh)
    yn = yc * lax.rsqrt(var + RWKV_GN_EPS) * gn_ref[0:1, :] + gn_ref[1:2, :]
    bonus = _head_sum(r_ref[0] * k_ref[0] * gn_ref[2:3, :], dh) * v_ref[0]
    o_ref[0] = (yn + bonus) * g_ref[0]


def rwkv7_mixer_fused(r, logw, k, v, a, b, g, gn, heads=RWKV_HEADS, dh=RWKV_DH, chunk=RWKV_CHUNK, block=RWKV_BLOCK):
    B_, S_, W_ = r.shape
    block = min(block, S_)
    assert W_ == heads * dh and S_ % block == 0 and block % chunk == 0
    spec = pl.BlockSpec((1, block, W_), lambda bi, ci: (bi, ci, 0))
    return pl.pallas_call(
        functools.partial(_rwkv7_mixer_kernel, heads=heads, dh=dh, chunk=chunk),
        grid=(B_, S_ // block),
        in_specs=[spec] * 7 + [pl.BlockSpec(gn.shape, lambda bi, ci: (0, 0))],
        out_specs=spec,
        out_shape=jax.ShapeDtypeStruct((B_, S_, W_), F32),
        scratch_shapes=[pltpu.VMEM((heads, dh, dh), F32)],
        compiler_params=pltpu.CompilerParams(dimension_semantics=("arbitrary", "arbitrary")),
        name="rwkv7_mixer",
    )(r, logw, k, v, a, b, g, gn)


def _mla_flash_kernel(q_ref, k_ref, v_ref, o_ref, *, tile):
    h = pl.program_id(1)
    qi = pl.program_id(2)
    q = q_ref[0]
    dv = v_ref.shape[2]

    def kv_tile(j):
        start = pl.multiple_of(j * tile, tile)
        return k_ref[0, pl.ds(start, tile), :], v_ref[0, pl.ds(start, tile), :]

    def update(carry, s, v_t):
        m, l, acc = carry
        m_new = jnp.maximum(m, jnp.max(s, axis=-1, keepdims=True))
        corr = jnp.exp(m - m_new)
        p = jnp.exp(s - m_new)
        l_new = corr * l + jnp.sum(p, axis=-1, keepdims=True)
        acc_new = corr * acc + jnp.dot(p.astype(v_t.dtype), v_t, preferred_element_type=F32)
        return m_new, l_new, acc_new

    def body(j, carry):
        k_t, v_t = kv_tile(j)
        s = lax.dot_general(q, k_t, (_NT, ((), ())), preferred_element_type=F32)
        return update(carry, s, v_t)

    init = (jnp.full((tile, 1), -jnp.inf, F32), jnp.zeros((tile, 1), F32), jnp.zeros((tile, dv), F32))
    carry = lax.fori_loop(0, qi, body, init)
    k_t, v_t = kv_tile(qi)
    s = lax.dot_general(q, k_t, (_NT, ((), ())), preferred_element_type=F32)
    row = lax.broadcasted_iota(jnp.int32, (tile, tile), 0)
    col = lax.broadcasted_iota(jnp.int32, (tile, tile), 1)
    s = jnp.where(col <= row, s, -jnp.inf)
    m, l, acc = update(carry, s, v_t)
    lane_head = lax.broadcasted_iota(jnp.int32, (tile, dv), 1) // MLA_DV
    o_ref[0] = jnp.where(lane_head == h % 2, acc / l, 0.0)


def mla_attention(q, k, v, tile=ATTN_TILE):
    B_, S_, _ = q.shape
    tile = min(tile, S_)
    assert S_ % tile == 0 and MLA_HEAD_PAD == 2 * MLA_DV
    return pl.pallas_call(
        functools.partial(_mla_flash_kernel, tile=tile),
        grid=(B_, MLA_HEADS, S_ // tile),
        in_specs=[pl.BlockSpec((1, tile, MLA_HEAD_PAD), lambda b, h, i: (b, i, h)),
                  pl.BlockSpec((1, S_, MLA_HEAD_PAD), lambda b, h, i: (b, 0, h)),
                  pl.BlockSpec((1, S_, 2 * MLA_DV), lambda b, h, i: (b, 0, h // 2))],
        out_specs=pl.BlockSpec((1, tile, MLA_HEAD_PAD), lambda b, h, i: (b, i, h)),
        out_shape=jax.ShapeDtypeStruct((B_, S_, MLA_QK_W), F32),
        compiler_params=pltpu.CompilerParams(dimension_semantics=("arbitrary", "arbitrary", "arbitrary")),
        name="mla_causal_flash",
    )(q, k, v)


def _out_kernel(x_ref, oa_ref, ob_ref, oc_ref, od_ref, wabc_ref, wd_ref, gb_ref, gd_ref, ln_ref, wr_ref, br_ref,
                x1_ref, x1b_ref, logit_ref, *, alpha):
    ob = ob_ref[...]
    ob = ob * lax.rsqrt(jnp.mean(ob * ob, -1, keepdims=True) + NORM_EPS) * gb_ref[...]
    od = od_ref[...]
    od = od * lax.rsqrt(jnp.sum(od * od, -1, keepdims=True) * (1.0 / MLA_W) + NORM_EPS) * gd_ref[...]
    W = RET_W
    mix = jnp.dot(oa_ref[...].astype(BF16), wabc_ref[0:W, :], preferred_element_type=F32)
    mix += jnp.dot(ob.astype(BF16), wabc_ref[W:2 * W, :], preferred_element_type=F32)
    mix += jnp.dot(oc_ref[...].astype(BF16), wabc_ref[2 * W:3 * W, :], preferred_element_type=F32)
    mix += jnp.dot(od.astype(BF16), wd_ref[...], preferred_element_type=F32)
    x1 = _layer_norm_rows(alpha * x_ref[...] + mix, ln_ref)
    x1_ref[...] = x1
    x1b = x1.astype(BF16)
    x1b_ref[...] = x1b
    logit_ref[...] = jnp.dot(x1b, wr_ref[...], preferred_element_type=F32) + br_ref[...]


def fused_out_proj(x, o_a, o_b, o_c, o_d, w_abc, w_d, g_b, g_d, ln_gb, w_router, b_router, alpha, tile=OUT_TILE):
    N, D = x.shape
    tile = min(tile, N)
    assert N % tile == 0
    n_exp = w_router.shape[1]
    wr = jnp.zeros((D, ROUTER_PAD), F32).at[:, :n_exp].set(w_router).astype(BF16)
    br = jnp.zeros((1, ROUTER_PAD), F32).at[0, :n_exp].set(b_router)
    consts = [w_abc, w_d, g_b, g_d, ln_gb, wr, br]
    tok = lambda w: pl.BlockSpec((tile, w), lambda i: (i, 0))
    full = lambda a: pl.BlockSpec(a.shape, lambda i: (0,) * a.ndim)
    x1, x1b, logits = pl.pallas_call(
        functools.partial(_out_kernel, alpha=alpha),
        grid=(N // tile,),
        in_specs=[tok(D), tok(RET_W), tok(DIL_W), tok(RWKV_W), tok(MLA_QK_W)] + [full(c) for c in consts],
        out_specs=[tok(D), tok(D), tok(ROUTER_PAD)],
        out_shape=[jax.ShapeDtypeStruct((N, D), F32), jax.ShapeDtypeStruct((N, D), BF16),
                   jax.ShapeDtypeStruct((N, ROUTER_PAD), F32)],
        compiler_params=pltpu.CompilerParams(dimension_semantics=("arbitrary",), vmem_limit_bytes=VMEM_LIMIT),
        name="fused_out_proj",
    )(x, o_a, o_b, o_c, o_d, *consts)
    return x1, x1b, logits[:, :n_exp]


def _moe_expert_kernel(tile_e_ref, n_used_ref, x_ref, wg_ref, wl_ref, bg_ref, bl_ref, wd_ref, bd_ref, o_ref):
    del tile_e_ref
    i = pl.program_id(0)

    @pl.when(i < n_used_ref[0])
    def _():
        x = x_ref[...]
        glu = jnp.dot(x, wg_ref[0], preferred_element_type=F32) + bg_ref[0]
        lin = jnp.dot(x, wl_ref[0], preferred_element_type=F32) + bl_ref[0]
        glu = jnp.minimum(glu, SWIGLU_LIMIT)
        lin = jnp.clip(lin, -SWIGLU_LIMIT, SWIGLU_LIMIT)
        act = glu * jax.nn.sigmoid(SWIGLU_ALPHA * glu) * (lin + 1.0)
        o_ref[...] = jnp.dot(act.astype(wd_ref.dtype), wd_ref[0], preferred_element_type=F32) + bd_ref[0]

    @pl.when(i >= n_used_ref[0])
    def _():
        o_ref[...] = jnp.zeros_like(o_ref)


def _deinterleave_kernel(w_ref, even_ref, odd_ref):
    g = 2 * LANES
    i = lax.broadcasted_iota(jnp.int32, (g, g), 0)
    j = lax.broadcasted_iota(jnp.int32, (g, g), 1)
    perm = (i == jnp.where(j < LANES, 2 * j, 2 * (j - LANES) + 1)).astype(BF16)
    for c in range(w_ref.shape[2] // g):
        blk = w_ref[0, :, c * g:(c + 1) * g].astype(BF16)
        sorted_cols = jnp.dot(blk, perm, preferred_element_type=F32)
        even_ref[0, :, c * LANES:(c + 1) * LANES] = sorted_cols[:, :LANES].astype(even_ref.dtype)
        odd_ref[0, :, c * LANES:(c + 1) * LANES] = sorted_cols[:, LANES:].astype(odd_ref.dtype)


def deinterleave_to_bf16(w):
    E, D, F2 = w.shape
    rows = min(DEINT_ROWS, D)
    assert D % rows == 0 and F2 % (2 * LANES) == 0
    out = jax.ShapeDtypeStruct((E, D, F2 // 2), BF16)
    return pl.pallas_call(
        _deinterleave_kernel,
        grid=(E, D // rows),
        in_specs=[pl.BlockSpec((1, rows, F2), lambda e, r: (e, r, 0))],
        out_specs=[pl.BlockSpec((1, rows, F2 // 2), lambda e, r: (e, r, 0))] * 2,
        out_shape=[out, out],
        compiler_params=pltpu.CompilerParams(dimension_semantics=("arbitrary", "arbitrary")),
        name="deinterleave_cast",
    )(w)


def moe_experts(xs, tile_e, n_used, w_glu, w_lin, b_glu, b_lin, w_dn, b_dn, tile):
    n_rows, D = xs.shape
    E, _, F = w_glu.shape
    w_spec = lambda shape: pl.BlockSpec((1,) + shape, lambda i, te, nu: (te[i], 0, 0))
    return pl.pallas_call(
        _moe_expert_kernel,
        grid_spec=pltpu.PrefetchScalarGridSpec(
            num_scalar_prefetch=2,
            grid=(n_rows // tile,),
            in_specs=[pl.BlockSpec((tile, D), lambda i, te, nu: (i, 0)),
                      w_spec((D, F)), w_spec((D, F)), w_spec((1, F)), w_spec((1, F)),
                      w_spec((F, D)), w_spec((1, D))],
            out_specs=pl.BlockSpec((tile, D), lambda i, te, nu: (i, 0)),
        ),
        out_shape=jax.ShapeDtypeStruct((n_rows, D), F32),
        compiler_params=pltpu.CompilerParams(dimension_semantics=("arbitrary",), vmem_limit_bytes=VMEM_LIMIT),
        name="moe_expert_ffn",
    )(tile_e, n_used, xs, w_glu, w_lin, b_glu.reshape(E, 1, F), b_lin.reshape(E, 1, F), w_dn, b_dn.reshape(E, 1, D))


def _combine_kernel(x_ref, y0_ref, y1_ref, y2_ref, y3_ref, gate_ref, ln_ref, o_ref, *, alpha):
    gate = gate_ref[...]
    y = y0_ref[...] * gate[:, 0:1] + y1_ref[...] * gate[:, 1:2] + y2_ref[...] * gate[:, 2:3] + y3_ref[...] * gate[:, 3:4]
    o_ref[...] = _layer_norm_rows(alpha * x_ref[...] + y, ln_ref)


def moe_combine_ln(x1, ys, gate, ln_gb, alpha, tile=OUT_TILE):
    N, D = x1.shape
    tile = min(tile, N)
    assert N % tile == 0 and len(ys) == TOP_K
    tok = lambda w: pl.BlockSpec((tile, w), lambda i: (i, 0))
    return pl.pallas_call(
        functools.partial(_combine_kernel, alpha=alpha),
        grid=(N // tile,),
        in_specs=[tok(D)] * (1 + TOP_K) + [tok(TOP_K), pl.BlockSpec(ln_gb.shape, lambda i: (0, 0))],
        out_specs=tok(D),
        out_shape=jax.ShapeDtypeStruct((N, D), F32),
        compiler_params=pltpu.CompilerParams(dimension_semantics=("arbitrary",), vmem_limit_bytes=VMEM_LIMIT),
        name="moe_combine_ln",
    )(x1, *ys, gate, ln_gb)


def moe_layer(x1, x1b, logits, expert_params, expert_base, ln_gb, alpha, tile=MOE_TILE):
    n_tok, D = x1.shape
    n_exp = logits.shape[1]
    top_val, top_idx = lax.top_k(logits, TOP_K)
    gate = jax.nn.softmax(top_val, axis=-1)
    n_assign = n_tok * TOP_K
    e_flat = top_idx.reshape(-1).astype(jnp.int32)
    experts = jnp.arange(n_exp, dtype=jnp.int32)
    e_sorted, order = lax.sort((e_flat, jnp.arange(n_assign, dtype=jnp.int32)), num_keys=1)
    counts = jnp.sum((e_flat[:, None] == experts[None, :]).astype(jnp.int32), axis=0)
    padded = (counts + tile - 1) // tile * tile
    start = jnp.cumsum(counts) - counts
    pend = jnp.cumsum(padded)
    pstart = pend - padded
    n_tiles = (n_assign + n_exp * (tile - 1) + tile - 1) // tile
    tile_first = jnp.arange(n_tiles, dtype=jnp.int32) * tile
    tile_e = jnp.minimum(jnp.sum((pend[None, :] <= tile_first[:, None]).astype(jnp.int32), axis=1), n_exp - 1)
    n_used = (pend[-1] // tile).astype(jnp.int32).reshape(1)
    within = (tile_first - pstart[tile_e])[:, None] + jnp.arange(tile, dtype=jnp.int32)[None, :]
    valid = within < counts[tile_e][:, None]
    sorted_idx = jnp.clip(start[tile_e][:, None] + within, 0, n_assign - 1)
    row_tok = jnp.where(valid, order[sorted_idx] // TOP_K, 0).reshape(-1)
    shift = jnp.sum(jnp.where(e_sorted[:, None] == experts[None, :], (pstart - start)[None, :], 0), axis=1)
    dest = jnp.arange(n_assign, dtype=jnp.int32) + shift
    _, pos = lax.sort((order, dest), num_keys=1)
    pos = pos.reshape(n_tok, TOP_K)
    yb = moe_experts(x1b[row_tok], tile_e + expert_base, n_used, *expert_params, tile)
    return moe_combine_ln(x1, [yb[pos[:, s]] for s in range(TOP_K)], gate, ln_gb, alpha)


def prep_expert_params(exp_w_gu, exp_b_gu, exp_w_dn, exp_b_dn):
    L, E, D, F2 = exp_w_gu.shape
    w_glu, w_lin = deinterleave_to_bf16(exp_w_gu.reshape(L * E, D, F2))
    b_gu = exp_b_gu.reshape(L * E, F2)
    return (w_glu, w_lin, b_gu[:, 0::2], b_gu[:, 1::2], exp_w_dn.astype(BF16).reshape(L * E, F2 // 2, D),
            exp_b_dn.reshape(L * E, D))


def kernel(x, w_in, w_out, ret_norm_g, dil_norm_g, rwkv_mu, rwkv_w0, rwkv_w_up, rwkv_a0, rwkv_a_up, rwkv_g_up, rwkv_k_k, rwkv_k_a, rwkv_r_k, rwkv_ln_g, rwkv_ln_b, rwkv_vres_down, rwkv_vres_mu, rwkv_v0, rwkv_v_up, mla_q_norm_g, mla_w_q_up, mla_kv_norm_g, mla_w_kv_up, mla_out_norm_g, ln1_g, ln1_b, router_w, router_b, exp_w_gu, exp_b_gu, exp_w_dn, exp_b_dn, ln2_g, ln2_b):
    depth = w_in.shape[0]
    B_, S_, D = x.shape
    alpha = (2 * depth) ** 0.25
    tables = rope_tables(S_)
    ret_tabs = retention_tables()
    expert_params = prep_expert_params(exp_w_gu, exp_b_gu, exp_w_dn, exp_b_dn)
    n_exp = exp_w_gu.shape[1]
    v_first = None
    for l in range(depth):
        w_cat = prep_in_weights(w_in[l], rwkv_vres_down[l - 1] if l > 0 else None)
        rw_params = prep_rwkv_params(rwkv_mu[l], rwkv_w0[l], rwkv_w_up[l], rwkv_a0[l], rwkv_a_up[l], rwkv_g_up[l],
                                     rwkv_k_k[l], rwkv_k_a[l], rwkv_vres_mu[l - 1] if l > 0 else None,
                                     rwkv_v0[l - 1] if l > 0 else None, rwkv_v_up[l - 1] if l > 0 else None)
        mla_params = prep_mla_params(mla_q_norm_g[l], mla_w_q_up[l], mla_kv_norm_g[l], mla_w_kv_up[l])
        (ret_q, ret_k, ret_v, ret_g, dil_q, dil_k, dil_v, rw_r, rw_lw, rw_k, rw_v, rw_a, rw_b, rw_g,
         mla_q, mla_k, mla_v) = fused_in_proj(x, w_cat, tables, rw_params, mla_params, v_first)
        if l == 0:
            v_first = rw_v
        o_a = retention_fused(ret_q, ret_k, ret_v, ret_g, ret_norm_g[l], ret_tabs)
        o_b = dilated_fused(dil_q, dil_k, dil_v)
        gn = jnp.stack([rwkv_ln_g[l], rwkv_ln_b[l], rwkv_r_k[l].reshape(-1)])
        o_c = rwkv7_mixer_fused(rw_r, rw_lw, rw_k, rw_v, rw_a, rw_b, rw_g, gn)
        o_d = mla_attention(mla_q, mla_k, mla_v)
        w_abc, w_d, g_d = prep_out_weights(w_out[l], mla_out_norm_g[l])
        n_tok = B_ * S_
        flat = lambda t: t.reshape(n_tok, t.shape[-1])
        x1, x1b, logits = fused_out_proj(flat(x), flat(o_a), flat(o_b), flat(o_c), flat(o_d), w_abc, w_d,
                                         dil_norm_g[l].reshape(1, DIL_W), g_d, jnp.stack([ln1_g[l], ln1_b[l]]),
                                         router_w[l], router_b[l], alpha)
        x = moe_layer(x1, x1b, logits, expert_params, l * n_exp, jnp.stack([ln2_g[l], ln2_b[l]]),
                      alpha).reshape(B_, S_, D)
    return x
```

```python
import functools

import numpy as np
import jax
import jax.numpy as jnp
from jax import lax
from jax.experimental import pallas as pl
from jax.experimental.pallas import tpu as pltpu

F32 = jnp.float32
BF16 = jnp.bfloat16

LANES = 128
BLK = 128
LN_EPS = 1e-5
NORM_EPS = 1e-6

RET_HEADS, RET_DK, RET_DV = 4, 32, 64
RET_QK = RET_HEADS * RET_DK
RET_W = RET_HEADS * RET_DV
RET_THETA = 10000.0
DIL_HEADS, DIL_DH = 4, 64
DIL_W = DIL_HEADS * DIL_DH
DIL_PATTERNS = ((128, 1), (512, 4), (2048, 16))
ROPE_THETA = 500000.0
ROPE_ROT_DIM = DIL_DH // 4
RWKV_HEADS, RWKV_DH = 4, 64
RWKV_W = RWKV_HEADS * RWKV_DH
DECAY_LORA, AAA_LORA, MV_LORA, GATE_LORA = 64, 64, 32, 128
RWKV_GN_EPS = 64e-5
MLA_HEADS, MLA_NOPE, MLA_ROPE, MLA_DV = 4, 64, 32, 64
MLA_W = MLA_HEADS * MLA_DV
Q_LORA, KV_LORA = 256, 128
MLA_THETA = 10000.0
TOP_K = 4
SWIGLU_LIMIT, SWIGLU_ALPHA = 7.0, 1.702

RET_SPLITS = (RET_QK, RET_QK, RET_W, RET_W)
DIL_SPLITS = (DIL_W, DIL_W, DIL_W)
RWKV_SPLITS = (RWKV_W, RWKV_W, RWKV_W, DECAY_LORA, AAA_LORA, GATE_LORA)
MLA_SPLITS = (Q_LORA, KV_LORA, MLA_ROPE)
A_END = sum(RET_SPLITS)
B_END = A_END + sum(DIL_SPLITS)
C_END = B_END + sum(RWKV_SPLITS)
N_IN = C_END + sum(MLA_SPLITS)

RWKV_CHUNK = 64
RWKV_BLOCK = 256

MLA_HEAD_PAD = 128
MLA_QK_W = MLA_HEADS * MLA_HEAD_PAD
ATTN_TILE = 512

IN_RET_W = 4 * RET_QK + 2 * RET_W
IN_DIL_W = 5 * DIL_W
IN_RWKV_W = sum(RWKV_SPLITS) + LANES
IN_MLA_W = Q_LORA + KV_LORA + LANES
IN_COLS = IN_RET_W + IN_DIL_W + IN_RWKV_W + IN_MLA_W
IN_TILE = 256
OUT_TILE = 512
ROUTER_PAD = 128
DIL_PAIR_W = 2 * DIL_DH
DEINT_ROWS = 512
MOE_TILE = 512
BATCH_GROUPS = 2
VMEM_LIMIT = 48 * 1024 * 1024
DIL_VMEM_LIMIT = 56 * 1024 * 1024


def split_cols(p, sizes):
    idx = np.cumsum(sizes)[:-1].tolist()
    return jnp.split(p, idx, axis=-1)


def rope_table(n_pos, rot_dim, theta):
    inv_freq = 1.0 / (theta ** (jnp.arange(0, rot_dim, 2, dtype=F32) / rot_dim))
    ang = jnp.arange(n_pos, dtype=F32)[:, None] * inv_freq[None, :]
    return jnp.cos(ang), jnp.sin(ang)


def _bdot(a, b, dims):
    return lax.dot_general(a.astype(BF16), b.astype(BF16), ((dims[0], dims[1]), ((), ())), preferred_element_type=F32)


_NN = ((1,), (0,))
_NT = ((1,), (1,))
_TN = ((0,), (0,))


def _split3(x):
    h1 = x.astype(BF16)
    r1 = x - h1.astype(F32)
    h2 = r1.astype(BF16)
    h3 = (r1 - h2.astype(F32)).astype(BF16)
    return h1, h2, h3


def _head_sum(x, dh):
    w = x.shape[1]
    i = lax.broadcasted_iota(jnp.int32, (w, w), 0) // dh
    j = lax.broadcasted_iota(jnp.int32, (w, w), 1) // dh
    ones = (i == j).astype(BF16)
    return sum(jnp.dot(p, ones, preferred_element_type=F32) for p in _split3(x))


def _layer_norm_rows(h, ln_ref):
    mu = jnp.mean(h, -1, keepdims=True)
    hc = h - mu
    var = jnp.mean(hc * hc, -1, keepdims=True)
    return hc * lax.rsqrt(var + LN_EPS) * ln_ref[0:1, :] + ln_ref[1:2, :]


def _rot_half_cols(w, heads, dh, rot):
    d_in = w.shape[0]
    w = w.reshape(d_in, heads, dh)
    half = rot // 2
    sw = jnp.concatenate([-w[..., half:rot], w[..., :half], jnp.zeros((d_in, heads, dh - rot), w.dtype)], axis=-1)
    return sw.reshape(d_in, heads * dh)


def _rope_lanes(cs, heads, dh, rot, lead=0):
    cos, sin = cs
    S_ = cos.shape[0]
    c = jnp.concatenate([jnp.ones((S_, lead), F32), cos, cos, jnp.ones((S_, dh - lead - rot), F32)], axis=1)
    s = jnp.concatenate([jnp.zeros((S_, lead), F32), sin, sin, jnp.zeros((S_, dh - lead - rot), F32)], axis=1)
    return jnp.tile(c, (1, heads)), jnp.tile(s, (1, heads))


def rope_tables(S_):
    ret = _rope_lanes(rope_table(S_, RET_DK, RET_THETA), RET_HEADS, RET_DK, RET_DK)
    dil = _rope_lanes(rope_table(S_, ROPE_ROT_DIM, ROPE_THETA), DIL_HEADS, DIL_DH, ROPE_ROT_DIM)
    cos, sin = rope_table(S_, MLA_ROPE, MLA_THETA)
    mq = _rope_lanes((cos, sin), MLA_HEADS, MLA_HEAD_PAD, MLA_ROPE, lead=MLA_NOPE)
    mk = jnp.concatenate([cos, cos, sin, sin, jnp.zeros((S_, LANES - 2 * MLA_ROPE), F32)], axis=1)
    return ret + dil + mq + (mk,)


def prep_in_weights(w_in_l, vres_down_l):
    D = w_in_l.shape[0]
    a_q, a_k, a_v, a_g = split_cols(w_in_l[:, :A_END], RET_SPLITS)
    b_q, b_k, b_v = split_cols(w_in_l[:, A_END:B_END], DIL_SPLITS)
    d_cq, d_ckv, d_kr = split_cols(w_in_l[:, C_END:N_IN], MLA_SPLITS)
    vres = jnp.zeros((D, LANES), F32)
    if vres_down_l is not None:
        vres = vres.at[:, :MV_LORA].set(vres_down_l)
    cols = [a_q, _rot_half_cols(a_q, RET_HEADS, RET_DK, RET_DK), a_k, _rot_half_cols(a_k, RET_HEADS, RET_DK, RET_DK), a_v, a_g,
            b_q, _rot_half_cols(b_q, DIL_HEADS, DIL_DH, ROPE_ROT_DIM), b_k, _rot_half_cols(b_k, DIL_HEADS, DIL_DH, ROPE_ROT_DIM), b_v,
            w_in_l[:, B_END:C_END], vres,
            d_cq, d_ckv, d_kr, _rot_half_cols(d_kr, 1, MLA_ROPE, MLA_ROPE), jnp.zeros((D, LANES - 2 * MLA_ROPE), F32)]
    w = jnp.concatenate(cols, axis=1)
    assert w.shape[1] == IN_COLS
    return w.astype(BF16)


def prep_rwkv_params(mu, w0, w_up, a0, a_up, g_up, k_k, k_a, vres_mu, v0, v_up):
    W = RWKV_W
    mu_ext = jnp.zeros((1, IN_RWKV_W), F32).at[0, :mu.shape[0]].set(mu)
    v_up_ext = jnp.zeros((LANES, W), F32)
    v0_ext = jnp.zeros((1, W), F32)
    if vres_mu is not None:
        mu_ext = mu_ext.at[0, mu.shape[0]:mu.shape[0] + MV_LORA].set(vres_mu)
        v_up_ext = v_up_ext.at[:MV_LORA].set(v_up)
        v0_ext = v0.reshape(1, W)
    wa_up = jnp.zeros((LANES, 2 * W), F32).at[:DECAY_LORA, :W].set(w_up).at[DECAY_LORA:, W:].set(a_up)
    return [mu_ext, wa_up.astype(BF16), jnp.concatenate([w0, a0]).reshape(1, 2 * W), g_up.astype(BF16),
            jnp.stack([k_k, k_a]), v_up_ext.astype(BF16), v0_ext]


def prep_mla_params(q_norm_g, w_q_up, kv_norm_g, w_kv_up):
    ql = w_q_up.shape[0]
    wq = w_q_up.reshape(ql, MLA_HEADS, MLA_NOPE + MLA_ROPE)
    pad = jnp.zeros((ql, MLA_HEADS, MLA_HEAD_PAD - MLA_NOPE - MLA_ROPE), F32)
    q_main = jnp.concatenate([wq, pad], axis=-1).reshape(ql, MLA_QK_W)
    rope = wq[..., MLA_NOPE:]
    half = MLA_ROPE // 2
    q_rot = jnp.concatenate([jnp.zeros((ql, MLA_HEADS, MLA_NOPE), F32), -rope[..., half:], rope[..., :half], pad], axis=-1)
    w_q = jnp.concatenate([q_main, q_rot.reshape(ql, MLA_QK_W)], axis=1).astype(BF16)
    kl = w_kv_up.shape[0]
    wkv = w_kv_up.reshape(kl, MLA_HEADS, MLA_NOPE + MLA_DV)
    k_main = jnp.concatenate([wkv[..., :MLA_NOPE], jnp.zeros((kl, MLA_HEADS, MLA_HEAD_PAD - MLA_NOPE), F32)], axis=-1)
    w_kv = jnp.concatenate([k_main.reshape(kl, MLA_QK_W), wkv[..., MLA_NOPE:].reshape(kl, MLA_W)], axis=1).astype(BF16)
    return [q_norm_g.reshape(1, Q_LORA), w_q, kv_norm_g.reshape(1, KV_LORA), w_kv]


def prep_out_weights(w_out_l, mla_out_norm_g):
    W = RET_W
    w_abc = w_out_l[:3 * W].astype(BF16)
    w_d = w_out_l[3 * W:].reshape(MLA_HEADS, MLA_DV, -1)
    g_d = mla_out_norm_g.reshape(MLA_HEADS, MLA_DV)
    zw = jnp.zeros_like(w_d[0])
    zg = jnp.zeros_like(g_d[0])
    rows, gains = [], []
    for h in range(MLA_HEADS):
        rows += [w_d[h], zw] if h % 2 == 0 else [zw, w_d[h]]
        gains += [g_d[h], zg] if h % 2 == 0 else [zg, g_d[h]]
    return w_abc, jnp.concatenate(rows, axis=0).astype(BF16), jnp.concatenate(gains).reshape(1, MLA_QK_W)


def retention_tables():
    log_gamma = jnp.log(1.0 - 2.0 ** (-5.0 - jnp.arange(RET_HEADS, dtype=F32)))
    idx = jnp.arange(BLK, dtype=F32)
    dist = idx[:, None] - idx[None, :]
    inner = jnp.where(dist >= 0, jnp.exp(jnp.maximum(dist, 0.0)[None] * log_gamma[:, None, None]), 0.0)
    zeta = jnp.exp((BLK - 1 - idx)[None, :] * log_gamma[:, None])
    xi = jnp.exp((idx + 1.0)[None, :] * log_gamma[:, None])
    zeta_k = jnp.repeat(zeta.T, RET_DK, axis=1)
    xi_v = jnp.repeat(xi.T, RET_DV, axis=1)
    row_h = jnp.arange(RET_QK) // RET_DK
    col_h = jnp.arange(RET_W) // RET_DV
    same = row_h[:, None] == col_h[None, :]
    state_decay = jnp.where(same, jnp.exp(BLK * log_gamma)[row_h][:, None], 0.0)
    return inner, zeta_k, xi_v, state_decay, same.astype(F32)


def _in_kernel(x_ref, w_ref, cr_ref, sr_ref, cd_ref, sd_ref, cq_ref, sq_ref, ck_ref,
               mu_ref, wa_up_ref, w0a0_ref, g_up_ref, kk_ka_ref, v_up_ref, v0_ref, vfirst_ref,
               qn_ref, wq_ref, kvn_ref, wkv_ref,
               ret_q, ret_k, ret_v, ret_g, dil_q, dil_k, dil_v,
               rw_r, rw_lw, rw_k, rw_v, rw_a, rw_b, rw_g, mla_q, mla_k, mla_v,
               carry_ref, *, has_vres):
    tm = x_ref.shape[1]

    @pl.when(pl.program_id(1) == 0)
    def _():
        carry_ref[...] = jnp.zeros_like(carry_ref)

    xb = x_ref[0].astype(BF16)
    o1, o2, o3 = IN_RET_W, IN_RET_W + IN_DIL_W, IN_RET_W + IN_DIL_W + IN_RWKV_W
    pa = jnp.dot(xb, w_ref[:, :o1], preferred_element_type=F32)
    pb = jnp.dot(xb, w_ref[:, o1:o2], preferred_element_type=F32)
    pc = jnp.dot(xb, w_ref[:, o2:o3], preferred_element_type=F32)
    pd = jnp.dot(xb, w_ref[:, o3:], preferred_element_type=F32)

    qk = RET_QK
    cr, sr = cr_ref[...], sr_ref[...]
    ret_q[0] = (pa[:, 0:qk] * cr + pa[:, qk:2 * qk] * sr).astype(ret_q.dtype)
    ret_k[0] = ((pa[:, 2 * qk:3 * qk] * cr + pa[:, 3 * qk:4 * qk] * sr) * (RET_DK ** -0.5)).astype(ret_k.dtype)
    ret_v[0] = pa[:, 4 * qk:4 * qk + RET_W].astype(ret_v.dtype)
    ret_g[0] = pa[:, 4 * qk + RET_W:]

    cd, sd = cd_ref[...], sd_ref[...]
    dil_q[0] = pb[:, 0:DIL_W] * cd + pb[:, DIL_W:2 * DIL_W] * sd
    dil_k[0] = pb[:, 2 * DIL_W:3 * DIL_W] * cd + pb[:, 3 * DIL_W:4 * DIL_W] * sd
    dil_v[0] = pb[:, 4 * DIL_W:]

    row = lax.broadcasted_iota(jnp.int32, pc.shape, 0)
    prev = jnp.where(row == 0, carry_ref[...], pltpu.roll(pc, 1, 0))
    carry_ref[...] = pc[tm - 1:tm, :]
    ps = pc + (prev - pc) * mu_ref[...]
    W = RWKV_W
    r, k, v = ps[:, 0:W], ps[:, W:2 * W], ps[:, 2 * W:3 * W]
    wd_ad = ps[:, 3 * W:3 * W + LANES]
    lane = lax.broadcasted_iota(jnp.int32, wd_ad.shape, 1)
    lora_in = jnp.where(lane < DECAY_LORA, jnp.tanh(wd_ad), wd_ad)
    lora = jnp.dot(lora_in.astype(BF16), wa_up_ref[...], preferred_element_type=F32) + w0a0_ref[...]
    w_raw = -jax.nn.softplus(-lora[:, :W]) - 0.5
    a_sig = jax.nn.sigmoid(lora[:, W:])
    gd = ps[:, 3 * W + LANES:3 * W + 2 * LANES]
    rw_g[0] = jnp.dot(jax.nn.sigmoid(gd).astype(BF16), g_up_ref[...], preferred_element_type=F32)
    kk = k * kk_ka_ref[0:1, :]
    kk = kk / jnp.maximum(jnp.sqrt(_head_sum(kk * kk, RWKV_DH)), 1e-12)
    if has_vres:
        vd = ps[:, 3 * W + 2 * LANES:]
        mix = jax.nn.sigmoid(jnp.dot(vd.astype(BF16), v_up_ref[...], preferred_element_type=F32) + v0_ref[...])
        v = v + (vfirst_ref[0] - v) * mix
    rw_r[0] = r
    rw_lw[0] = -jnp.exp(w_raw)
    rw_k[0] = k * (1.0 + (a_sig - 1.0) * kk_ka_ref[1:2, :])
    rw_v[0] = v
    rw_a[0] = -kk
    rw_b[0] = kk * a_sig

    c_q = pd[:, :Q_LORA]
    c_q = c_q * lax.rsqrt(jnp.mean(c_q * c_q, -1, keepdims=True) + NORM_EPS) * qn_ref[...]
    q2 = jnp.dot(c_q.astype(BF16), wq_ref[...], preferred_element_type=F32)
    scale = (MLA_NOPE + MLA_ROPE) ** -0.5
    mla_q[0] = ((q2[:, :MLA_QK_W] * cq_ref[...] + q2[:, MLA_QK_W:] * sq_ref[...]) * scale).astype(mla_q.dtype)
    c_kv = pd[:, Q_LORA:Q_LORA + KV_LORA]
    c_kv = c_kv * lax.rsqrt(jnp.mean(c_kv * c_kv, -1, keepdims=True) + NORM_EPS) * kvn_ref[...]
    kv = jnp.dot(c_kv.astype(BF16), wkv_ref[...], preferred_element_type=F32)
    kr = pd[:, Q_LORA + KV_LORA:] * ck_ref[...]
    i = lax.broadcasted_iota(jnp.int32, (LANES, MLA_QK_W), 0)
    j = lax.broadcasted_iota(jnp.int32, (LANES, MLA_QK_W), 1) % MLA_HEAD_PAD
    place = ((i < 2 * MLA_ROPE) & (j == MLA_NOPE + i % MLA_ROPE)).astype(BF16)
    k_pe = sum(jnp.dot(p, place, preferred_element_type=F32) for p in _split3(kr)[:2])
    mla_k[0] = (kv[:, :MLA_QK_W] + k_pe).astype(mla_k.dtype)
    mla_v[0] = kv[:, MLA_QK_W:].astype(mla_v.dtype)


def fused_in_proj(x, w_cat, tables, rw_params, mla_params, v_first, tile=IN_TILE):
    B_, S_, D = x.shape
    tile = min(tile, S_)
    assert S_ % tile == 0
    has_vres = v_first is not None
    tok = lambda w: pl.BlockSpec((1, tile, w), lambda b, i: (b, i, 0))
    pos = lambda a: pl.BlockSpec((tile, a.shape[1]), lambda b, i: (i, 0))
    full = lambda a: pl.BlockSpec(a.shape, lambda b, i: (0,) * a.ndim)
    if has_vres:
        vf_spec = tok(RWKV_W)
    else:
        v_first = jnp.zeros((1, tile, RWKV_W), F32)
        vf_spec = pl.BlockSpec((1, tile, RWKV_W), lambda b, i: (0, 0, 0))
    f32o = lambda w: jax.ShapeDtypeStruct((B_, S_, w), F32)
    bfo = lambda w: jax.ShapeDtypeStruct((B_, S_, w), BF16)
    out_shape = ([bfo(RET_QK), bfo(RET_QK), bfo(RET_W), f32o(RET_W)] + [f32o(DIL_W)] * 3 + [f32o(RWKV_W)] * 7
                 + [bfo(MLA_QK_W), bfo(MLA_QK_W), bfo(MLA_W)])
    return pl.pallas_call(
        functools.partial(_in_kernel, has_vres=has_vres),
        grid=(B_, S_ // tile),
        in_specs=[tok(D), full(w_cat)] + [pos(t) for t in tables] + [full(p) for p in rw_params] + [vf_spec]
                 + [full(p) for p in mla_params],
        out_specs=[tok(s.shape[-1]) for s in out_shape],
        out_shape=out_shape,
        scratch_shapes=[pltpu.VMEM((1, IN_RWKV_W), F32)],
        compiler_params=pltpu.CompilerParams(dimension_semantics=("arbitrary", "arbitrary"), vmem_limit_bytes=VMEM_LIMIT),
        name="fused_in_proj",
    )(x, w_cat, *tables, *rw_params, v_first, *mla_params)


def _retention_kernel(q_ref, k_ref, v_ref, g_ref, ng_ref, inner_ref, zeta_ref, xi_ref, sdec_ref, smask_ref, o_ref, state_ref):
    @pl.when(pl.program_id(1) == 0)
    def _():
        state_ref[...] = jnp.zeros_like(state_ref)

    q, k, v = q_ref[0], k_ref[0], v_ref[0]
    q_head = lax.broadcasted_iota(jnp.int32, q.shape, 1) // RET_DK
    v_head = lax.broadcasted_iota(jnp.int32, v.shape, 1) // RET_DV
    zero_q = jnp.zeros_like(q)
    scores = [_bdot(jnp.where(q_head == h, q, zero_q), k, _NT) * inner_ref[h] for h in range(RET_HEADS)]
    o_heads = [_bdot(s, v, _NN) for s in scores]
    o = _bdot(q, state_ref[...], _NN) * xi_ref[...]
    for h in range(RET_HEADS):
        o = o + jnp.where(v_head == h, o_heads[h], 0.0)
    state_ref[...] = state_ref[...] * sdec_ref[...] + _bdot(k.astype(F32) * zeta_ref[...], v, _TN) * smask_ref[...]
    o = o * lax.rsqrt(_head_sum(o * o, RET_DV) * (1.0 / RET_DV) + NORM_EPS) * ng_ref[...]
    g = g_ref[0]
    o_ref[0] = g * jax.nn.sigmoid(g) * o


def retention_fused(q, k, v, g, norm_g, tables):
    B_, S_, _ = q.shape
    assert S_ % BLK == 0
    tok = lambda w: pl.BlockSpec((1, BLK, w), lambda b, c: (b, c, 0))
    full = lambda a: pl.BlockSpec(a.shape, lambda b, c: (0,) * a.ndim)
    ng = norm_g.reshape(1, RET_W)
    return pl.pallas_call(
        _retention_kernel,
        grid=(B_, S_ // BLK),
        in_specs=[tok(RET_QK), tok(RET_QK), tok(RET_W), tok(RET_W), full(ng)] + [full(t) for t in tables],
        out_specs=tok(RET_W),
        out_shape=jax.ShapeDtypeStruct((B_, S_, RET_W), F32),
        scratch_shapes=[pltpu.VMEM((RET_QK, RET_W), F32)],
        compiler_params=pltpu.CompilerParams(dimension_semantics=("arbitrary", "arbitrary")),
        name="retention_chunk",
    )(q, k, v, g, ng, *tables)


def _dilated_kernel(q_ref, k_ref, v_ref, o_ref, m_ref, l_ref, *, dils):
    S_ = q_ref.shape[1]
    lane_head = lax.broadcasted_iota(jnp.int32, (BLK, DIL_PAIR_W), 1) // DIL_DH
    i_idx = lax.broadcasted_iota(jnp.int32, (BLK, 2 * BLK), 0)
    j_idx = lax.broadcasted_iota(jnp.int32, (BLK, 2 * BLK), 1)
    scale = DIL_DH ** -0.5

    def rows(start, n, d):
        return pl.ds(start, n) if d == 1 else pl.ds(start, n, stride=d)

    for pi, d in enumerate(dils):
        def tile(t, carry, d=d, first=(pi == 0)):
            nb, r = t // d, t % d
            kb = jnp.maximum(nb - 1, 0)
            q_rows = rows(nb * (BLK * d) + r, BLK, d)
            k_rows = rows(kb * (BLK * d) + r, 2 * BLK, d)
            q = q_ref[0, q_rows, :]
            kk = k_ref[0, k_rows, :].astype(BF16)
            vv = v_ref[0, k_rows, :].astype(BF16)
            delta = (nb - kb) * BLK + i_idx - j_idx
            valid = (delta >= 0) & (delta <= BLK)
            zero_q = jnp.zeros_like(q)
            s = [jnp.where(valid, _bdot(jnp.where(lane_head == h, q, zero_q), kk, _NT) * scale, -jnp.inf) for h in range(2)]
            m = [jnp.max(x, axis=-1, keepdims=True) for x in s]
            p = [jnp.exp(s[h] - m[h]) for h in range(2)]
            l = [jnp.sum(x, axis=-1, keepdims=True) for x in p]
            pv = [jnp.dot(p[h].astype(BF16), vv, preferred_element_type=F32) for h in range(2)]
            m_t = jnp.where(lane_head == 0, m[0], m[1])
            l_t = jnp.where(lane_head == 0, l[0], l[1])
            pv_t = jnp.where(lane_head == 0, pv[0], pv[1])
            if first:
                m_ref[q_rows, :] = m_t
                l_ref[q_rows, :] = l_t
                o_ref[0, q_rows, :] = pv_t
            else:
                m_old = m_ref[q_rows, :]
                m_new = jnp.maximum(m_old, m_t)
                c_old = jnp.exp(m_old - m_new)
                c_t = jnp.exp(m_t - m_new)
                m_ref[q_rows, :] = m_new
                l_ref[q_rows, :] = l_ref[q_rows, :] * c_old + l_t * c_t
                o_ref[0, q_rows, :] = o_ref[0, q_rows, :] * c_old + pv_t * c_t
            return carry

        lax.fori_loop(0, S_ // BLK, tile, 0)

    def finish(t, carry):
        r0 = pl.multiple_of(t * BLK, BLK)
        o_ref[0, pl.ds(r0, BLK), :] = o_ref[0, pl.ds(r0, BLK), :] / l_ref[pl.ds(r0, BLK), :]
        return carry

    lax.fori_loop(0, S_ // BLK, finish, 0)


def dilated_fused(q, k, v):
    B_, S_, W_ = q.shape
    dils = tuple(d for _, d in DIL_PATTERNS)
    assert all(w // d == BLK for w, d in DIL_PATTERNS) and S_ % (2 * BLK * max(dils)) == 0
    spec = pl.BlockSpec((1, S_, DIL_PAIR_W), lambda b, hp: (b, 0, hp))
    return pl.pallas_call(
        functools.partial(_dilated_kernel, dils=dils),
        grid=(B_, W_ // DIL_PAIR_W),
        in_specs=[spec] * 3,
        out_specs=spec,
        out_shape=jax.ShapeDtypeStruct((B_, S_, W_), F32),
        scratch_shapes=[pltpu.VMEM((S_, DIL_PAIR_W), F32)] * 2,
        compiler_params=pltpu.CompilerParams(dimension_semantics=("arbitrary", "arbitrary"), vmem_limit_bytes=DIL_VMEM_LIMIT),
        name="dilated_attention",
    )(q, k, v)


def _rwkv7_mixer_kernel(r_ref, lw_ref, k_ref, v_ref, a_ref, b_ref, g_ref, gn_ref, o_ref, state_ref, *, heads, dh, chunk):
    rows = r_ref.shape[1]
    n_sub = rows // chunk
    n_double = int(np.log2(chunk))
    assert 2 ** n_double == chunk and n_sub * chunk == rows

    @pl.when(pl.program_id(1) == 0)
    def _():
        state_ref[...] = jnp.zeros_like(state_ref)

    ri = lax.broadcasted_iota(jnp.int32, (rows, rows), 0)
    ci = lax.broadcasted_iota(jnp.int32, (rows, rows), 1)
    tri = ((ci <= ri) & (ci >= (ri // chunk) * chunk)).astype(BF16)
    logw = lw_ref[0]
    cum = sum(jnp.dot(tri, piece, preferred_element_type=F32) for piece in _split3(logw))

    row = lax.broadcasted_iota(jnp.int32, (chunk, 2 * chunk), 0)
    col = lax.broadcasted_iota(jnp.int32, (chunk, 2 * chunk), 1) % chunk
    strict = col < row
    incl = col <= row
    zeros_cv = jnp.zeros((chunk, dh), F32)

    pairs = [(s, h) for s in range(n_sub) for h in range(heads)]
    ah, rh, vh, bk_rem, w_total = {}, {}, {}, {}, {}
    l_abk, m_rbk = {}, {}
    for s in range(n_sub):
        rs = slice(s * chunk, (s + 1) * chunk)
        cum_s, logw_s = cum[rs], logw[rs]
        total = cum_s[chunk - 1:chunk, :]
        e_in = jnp.exp(cum_s)
        e_neg = jnp.exp(-cum_s)
        e_rem = jnp.exp(total - cum_s)
        w_tot = jnp.exp(total)
        r_hat = r_ref[0, rs, :] * e_in
        a_hat = a_ref[0, rs, :] * jnp.exp(cum_s - logw_s)
        b_all, k_all, v_all = b_ref[0, rs, :], k_ref[0, rs, :], v_ref[0, rs, :]
        b_til, k_til = b_all * e_neg, k_all * e_neg
        b_rem, k_rem = b_all * e_rem, k_all * e_rem
        for h in range(heads):
            sl = slice(h * dh, (h + 1) * dh)
            ah[s, h], rh[s, h], vh[s, h] = a_hat[:, sl], r_hat[:, sl], v_all[:, sl]
            w_total[s, h] = w_tot[:, sl]
            bk_rem[s, h] = jnp.concatenate([b_rem[:, sl], k_rem[:, sl]], axis=0)
            bk_til = jnp.concatenate([b_til[:, sl], k_til[:, sl]], axis=0)
            l_abk[s, h] = jnp.where(strict, _bdot(ah[s, h], bk_til, _NT), 0.0)
            m_rbk[s, h] = jnp.where(incl, _bdot(rh[s, h], bk_til, _NT), 0.0)
    z = {p: jnp.concatenate([ah[p], _bdot(l_abk[p], jnp.concatenate([zeros_cv, vh[p]], axis=0), _NN)], axis=1)
         for p in pairs}
    lp = {p: l_abk[p][:, :chunk] for p in pairs}
    for m in range(n_double):
        z = {p: z[p] + _bdot(lp[p], z[p], _NN) for p in pairs}
        if m + 1 < n_double:
            lp = {p: _bdot(lp[p], lp[p], _NN) for p in pairs}
    state = [state_ref[h] for h in range(heads)]
    y_parts = []
    for s in range(n_sub):
        pr_s = [_bdot(jnp.concatenate([z[s, h][:, :dh], rh[s, h]], axis=0), state[h], _NT) for h in range(heads)]
        uv = [jnp.concatenate([pr_s[h][:chunk] + z[s, h][:, dh:], vh[s, h]], axis=0) for h in range(heads)]
        ys = [pr_s[h][chunk:] + _bdot(m_rbk[s, h], uv[h], _NN) for h in range(heads)]
        state = [state[h] * w_total[s, h] + _bdot(uv[h], bk_rem[s, h], _TN) for h in range(heads)]
        y_parts.append(jnp.concatenate(ys, axis=1))
    for h in range(heads):
        state_ref[h] = state[h]
    y = jnp.concatenate(y_parts, axis=0)
    mean = _head_sum(y, dh) * (1.0 / dh)
    yc = y - mean
    var = _head_sum(yc * yc, dh) * (1.0 / dh)
    yn = yc * lax.rsqrt(var + RWKV_GN_EPS) * gn_ref[0:1, :] + gn_ref[1:2, :]
    bonus = _head_sum(r_ref[0] * k_ref[0] * gn_ref[2:3, :], dh) * v_ref[0]
    o_ref[0] = (yn + bonus) * g_ref[0]


def rwkv7_mixer_fused(r, logw, k, v, a, b, g, gn, heads=RWKV_HEADS, dh=RWKV_DH, chunk=RWKV_CHUNK, block=RWKV_BLOCK):
    B_, S_, W_ = r.shape
    block = min(block, S_)
    assert W_ == heads * dh and S_ % block == 0 and block % chunk == 0
    spec = pl.BlockSpec((1, block, W_), lambda bi, ci: (bi, ci, 0))
    return pl.pallas_call(
        functools.partial(_rwkv7_mixer_kernel, heads=heads, dh=dh, chunk=chunk),
        grid=(B_, S_ // block),
        in_specs=[spec] * 7 + [pl.BlockSpec(gn.shape, lambda bi, ci: (0, 0))],
        out_specs=spec,
        out_shape=jax.ShapeDtypeStruct((B_, S_, W_), F32),
        scratch_shapes=[pltpu.VMEM((heads, dh, dh), F32)],
        compiler_params=pltpu.CompilerParams(dimension_semantics=("arbitrary", "arbitrary")),
        name="rwkv7_mixer",
    )(r, logw, k, v, a, b, g, gn)


def _mla_flash_kernel(q_ref, k_ref, v_ref, o_ref, *, tile):
    h = pl.program_id(1)
    qi = pl.program_id(2)
    q = q_ref[0]
    dv = v_ref.shape[2]

    def kv_tile(j):
        start = pl.multiple_of(j * tile, tile)
        return k_ref[0, pl.ds(start, tile), :], v_ref[0, pl.ds(start, tile), :]

    def update(carry, s, v_t):
        m, l, acc = carry
        m_new = jnp.maximum(m, jnp.max(s, axis=-1, keepdims=True))
        corr = jnp.exp(m - m_new)
        p = jnp.exp(s - m_new)
        l_new = corr * l + jnp.sum(p, axis=-1, keepdims=True)
        acc_new = corr * acc + jnp.dot(p.astype(v_t.dtype), v_t, preferred_element_type=F32)
        return m_new, l_new, acc_new

    def body(j, carry):
        k_t, v_t = kv_tile(j)
        s = lax.dot_general(q, k_t, (_NT, ((), ())), preferred_element_type=F32)
        return update(carry, s, v_t)

    init = (jnp.full((tile, 1), -jnp.inf, F32), jnp.zeros((tile, 1), F32), jnp.zeros((tile, dv), F32))
    carry = lax.fori_loop(0, qi, body, init)
    k_t, v_t = kv_tile(qi)
    s = lax.dot_general(q, k_t, (_NT, ((), ())), preferred_element_type=F32)
    row = lax.broadcasted_iota(jnp.int32, (tile, tile), 0)
    col = lax.broadcasted_iota(jnp.int32, (tile, tile), 1)
    s = jnp.where(col <= row, s, -jnp.inf)
    m, l, acc = update(carry, s, v_t)
    lane_head = lax.broadcasted_iota(jnp.int32, (tile, dv), 1) // MLA_DV
    o_ref[0] = jnp.where(lane_head == h % 2, acc / l, 0.0)


def mla_attention(q, k, v, tile=ATTN_TILE):
    B_, S_, _ = q.shape
    tile = min(tile, S_)
    assert S_ % tile == 0 and MLA_HEAD_PAD == 2 * MLA_DV
    return pl.pallas_call(
        functools.partial(_mla_flash_kernel, tile=tile),
        grid=(B_, MLA_HEADS, S_ // tile),
        in_specs=[pl.BlockSpec((1, tile, MLA_HEAD_PAD), lambda b, h, i: (b, i, h)),
                  pl.BlockSpec((1, S_, MLA_HEAD_PAD), lambda b, h, i: (b, 0, h)),
                  pl.BlockSpec((1, S_, 2 * MLA_DV), lambda b, h, i: (b, 0, h // 2))],
        out_specs=pl.BlockSpec((1, tile, MLA_HEAD_PAD), lambda b, h, i: (b, i, h)),
        out_shape=jax.ShapeDtypeStruct((B_, S_, MLA_QK_W), F32),
        compiler_params=pltpu.CompilerParams(dimension_semantics=("arbitrary", "arbitrary", "arbitrary")),
        name="mla_causal_flash",
    )(q, k, v)


def _out_kernel(x_ref, oa_ref, ob_ref, oc_ref, od_ref, wabc_ref, wd_ref, gb_ref, gd_ref, ln_ref, wr_ref, br_ref,
                x1_ref, x1b_ref, logit_ref, *, alpha):
    ob = ob_ref[...]
    ob = ob * lax.rsqrt(jnp.mean(ob * ob, -1, keepdims=True) + NORM_EPS) * gb_ref[...]
    od = od_ref[...]
    od = od * lax.rsqrt(jnp.sum(od * od, -1, keepdims=True) * (1.0 / MLA_W) + NORM_EPS) * gd_ref[...]
    W = RET_W
    mix = jnp.dot(oa_ref[...].astype(BF16), wabc_ref[0:W, :], preferred_element_type=F32)
    mix += jnp.dot(ob.astype(BF16), wabc_ref[W:2 * W, :], preferred_element_type=F32)
    mix += jnp.dot(oc_ref[...].astype(BF16), wabc_ref[2 * W:3 * W, :], preferred_element_type=F32)
    mix += jnp.dot(od.astype(BF16), wd_ref[...], preferred_element_type=F32)
    x1 = _layer_norm_rows(alpha * x_ref[...] + mix, ln_ref)
    x1_ref[...] = x1
    x1b = x1.astype(BF16)
    x1b_ref[...] = x1b
    logit_ref[...] = jnp.dot(x1b, wr_ref[...], preferred_element_type=F32) + br_ref[...]


def fused_out_proj(x, o_a, o_b, o_c, o_d, w_abc, w_d, g_b, g_d, ln_gb, w_router, b_router, alpha, tile=OUT_TILE):
    N, D = x.shape
    tile = min(tile, N)
    assert N % tile == 0
    n_exp = w_router.shape[1]
    wr = jnp.zeros((D, ROUTER_PAD), F32).at[:, :n_exp].set(w_router).astype(BF16)
    br = jnp.zeros((1, ROUTER_PAD), F32).at[0, :n_exp].set(b_router)
    consts = [w_abc, w_d, g_b, g_d, ln_gb, wr, br]
    tok = lambda w: pl.BlockSpec((tile, w), lambda i: (i, 0))
    full = lambda a: pl.BlockSpec(a.shape, lambda i: (0,) * a.ndim)
    x1, x1b, logits = pl.pallas_call(
        functools.partial(_out_kernel, alpha=alpha),
        grid=(N // tile,),
        in_specs=[tok(D), tok(RET_W), tok(DIL_W), tok(RWKV_W), tok(MLA_QK_W)] + [full(c) for c in consts],
        out_specs=[tok(D), tok(D), tok(ROUTER_PAD)],
        out_shape=[jax.ShapeDtypeStruct((N, D), F32), jax.ShapeDtypeStruct((N, D), BF16),
                   jax.ShapeDtypeStruct((N, ROUTER_PAD), F32)],
        compiler_params=pltpu.CompilerParams(dimension_semantics=("arbitrary",), vmem_limit_bytes=VMEM_LIMIT),
        name="fused_out_proj",
    )(x, o_a, o_b, o_c, o_d, *consts)
    return x1, x1b, logits[:, :n_exp]


def _moe_expert_kernel(tile_e_ref, n_used_ref, x_ref, wg_ref, wl_ref, bg_ref, bl_ref, wd_ref, bd_ref, o_ref):
    del tile_e_ref
    i = pl.program_id(0)

    @pl.when(i < n_used_ref[0])
    def _():
        x = x_ref[...]
        glu = jnp.dot(x, wg_ref[0], preferred_element_type=F32) + bg_ref[0]
        lin = jnp.dot(x, wl_ref[0], preferred_element_type=F32) + bl_ref[0]
        glu = jnp.minimum(glu, SWIGLU_LIMIT)
        lin = jnp.clip(lin, -SWIGLU_LIMIT, SWIGLU_LIMIT)
        act = glu * jax.nn.sigmoid(SWIGLU_ALPHA * glu) * (lin + 1.0)
        o_ref[...] = jnp.dot(act.astype(wd_ref.dtype), wd_ref[0], preferred_element_type=F32) + bd_ref[0]

    @pl.when(i >= n_used_ref[0])
    def _():
        o_ref[...] = jnp.zeros_like(o_ref)


def _deinterleave_kernel(w_ref, even_ref, odd_ref):
    g = 2 * LANES
    i = lax.broadcasted_iota(jnp.int32, (g, g), 0)
    j = lax.broadcasted_iota(jnp.int32, (g, g), 1)
    perm = (i == jnp.where(j < LANES, 2 * j, 2 * (j - LANES) + 1)).astype(BF16)
    for c in range(w_ref.shape[2] // g):
        blk = w_ref[0, :, c * g:(c + 1) * g].astype(BF16)
        sorted_cols = jnp.dot(blk, perm, preferred_element_type=F32)
        even_ref[0, :, c * LANES:(c + 1) * LANES] = sorted_cols[:, :LANES].astype(even_ref.dtype)
        odd_ref[0, :, c * LANES:(c + 1) * LANES] = sorted_cols[:, LANES:].astype(odd_ref.dtype)


def deinterleave_to_bf16(w):
    E, D, F2 = w.shape
    rows = min(DEINT_ROWS, D)
    assert D % rows == 0 and F2 % (2 * LANES) == 0
    out = jax.ShapeDtypeStruct((E, D, F2 // 2), BF16)
    return pl.pallas_call(
        _deinterleave_kernel,
        grid=(E, D // rows),
        in_specs=[pl.BlockSpec((1, rows, F2), lambda e, r: (e, r, 0))],
        out_specs=[pl.BlockSpec((1, rows, F2 // 2), lambda e, r: (e, r, 0))] * 2,
        out_shape=[out, out],
        compiler_params=pltpu.CompilerParams(dimension_semantics=("arbitrary", "arbitrary"), vmem_limit_bytes=VMEM_LIMIT),
        name="deinterleave_cast",
    )(w)


def moe_experts(xs, tile_e, n_used, w_glu, w_lin, b_glu, b_lin, w_dn, b_dn, tile):
    n_rows, D = xs.shape
    E, _, F = w_glu.shape
    w_spec = lambda shape: pl.BlockSpec((1,) + shape, lambda i, te, nu: (te[i], 0, 0))
    return pl.pallas_call(
        _moe_expert_kernel,
        grid_spec=pltpu.PrefetchScalarGridSpec(
            num_scalar_prefetch=2,
            grid=(n_rows // tile,),
            in_specs=[pl.BlockSpec((tile, D), lambda i, te, nu: (i, 0)),
                      w_spec((D, F)), w_spec((D, F)), w_spec((1, F)), w_spec((1, F)),
                      w_spec((F, D)), w_spec((1, D))],
            out_specs=pl.BlockSpec((tile, D), lambda i, te, nu: (i, 0)),
        ),
        out_shape=jax.ShapeDtypeStruct((n_rows, D), F32),
        compiler_params=pltpu.CompilerParams(dimension_semantics=("arbitrary",), vmem_limit_bytes=VMEM_LIMIT),
        name="moe_expert_ffn",
    )(tile_e, n_used, xs, w_glu, w_lin, b_glu.reshape(E, 1, F), b_lin.reshape(E, 1, F), w_dn, b_dn.reshape(E, 1, D))


def _combine_kernel(x_ref, y0_ref, y1_ref, y2_ref, y3_ref, gate_ref, ln_ref, o_ref, *, alpha):
    gate = gate_ref[...]
    y = y0_ref[...] * gate[:, 0:1] + y1_ref[...] * gate[:, 1:2] + y2_ref[...] * gate[:, 2:3] + y3_ref[...] * gate[:, 3:4]
    o_ref[...] = _layer_norm_rows(alpha * x_ref[...] + y, ln_ref)


def moe_combine_ln(x1, ys, gate, ln_gb, alpha, tile=OUT_TILE):
    N, D = x1.shape
    tile = min(tile, N)
    assert N % tile == 0 and len(ys) == TOP_K
    tok = lambda w: pl.BlockSpec((tile, w), lambda i: (i, 0))
    return pl.pallas_call(
        functools.partial(_combine_kernel, alpha=alpha),
        grid=(N // tile,),
        in_specs=[tok(D)] * (1 + TOP_K) + [tok(TOP_K), pl.BlockSpec(ln_gb.shape, lambda i: (0, 0))],
        out_specs=tok(D),
        out_shape=jax.ShapeDtypeStruct((N, D), F32),
        compiler_params=pltpu.CompilerParams(dimension_semantics=("arbitrary",), vmem_limit_bytes=VMEM_LIMIT),
        name="moe_combine_ln",
    )(x1, *ys, gate, ln_gb)


def moe_layer(x1, x1b, logits, expert_params, expert_base, ln_gb, alpha, tile=MOE_TILE):
    n_tok, D = x1.shape
    n_exp = logits.shape[1]
    top_val, top_idx = lax.top_k(logits, TOP_K)
    gate = jax.nn.softmax(top_val, axis=-1)
    n_assign = n_tok * TOP_K
    e_flat = top_idx.reshape(-1).astype(jnp.int32)
    experts = jnp.arange(n_exp, dtype=jnp.int32)
    e_sorted, order = lax.sort((e_flat, jnp.arange(n_assign, dtype=jnp.int32)), num_keys=1)
    counts = jnp.sum((e_flat[:, None] == experts[None, :]).astype(jnp.int32), axis=0)
    padded = (counts + tile - 1) // tile * tile
    start = jnp.cumsum(counts) - counts
    pend = jnp.cumsum(padded)
    pstart = pend - padded
    n_tiles = (n_assign + n_exp * (tile - 1) + tile - 1) // tile
    tile_first = jnp.arange(n_tiles, dtype=jnp.int32) * tile
    tile_e = jnp.minimum(jnp.sum((pend[None, :] <= tile_first[:, None]).astype(jnp.int32), axis=1), n_exp - 1)
    n_used = (pend[-1] // tile).astype(jnp.int32).reshape(1)
    within = (tile_first - pstart[tile_e])[:, None] + jnp.arange(tile, dtype=jnp.int32)[None, :]
    valid = within < counts[tile_e][:, None]
    sorted_idx = jnp.clip(start[tile_e][:, None] + within, 0, n_assign - 1)
    row_tok = jnp.where(valid, order[sorted_idx] // TOP_K, 0).reshape(-1)
    shift = jnp.sum(jnp.where(e_sorted[:, None] == experts[None, :], (pstart - start)[None, :], 0), axis=1)
    dest = jnp.arange(n_assign, dtype=jnp.int32) + shift
    _, pos = lax.sort((order, dest), num_keys=1)
    pos = pos.reshape(n_tok, TOP_K)
    yb = moe_experts(x1b[row_tok], tile_e + expert_base, n_used, *expert_params, tile)
    return moe_combine_ln(x1, [yb[pos[:, s]] for s in range(TOP_K)], gate, ln_gb, alpha)


def prep_expert_params(exp_w_gu, exp_b_gu, exp_w_dn, exp_b_dn):
    L, E, D, F2 = exp_w_gu.shape
    w_glu, w_lin = deinterleave_to_bf16(exp_w_gu.reshape(L * E, D, F2))
    b_gu = exp_b_gu.reshape(L * E, F2)
    return (w_glu, w_lin, b_gu[:, 0::2], b_gu[:, 1::2], exp_w_dn.astype(BF16).reshape(L * E, F2 // 2, D),
            exp_b_dn.reshape(L * E, D))


def kernel(x, w_in, w_out, ret_norm_g, dil_norm_g, rwkv_mu, rwkv_w0, rwkv_w_up, rwkv_a0, rwkv_a_up, rwkv_g_up, rwkv_k_k, rwkv_k_a, rwkv_r_k, rwkv_ln_g, rwkv_ln_b, rwkv_vres_down, rwkv_vres_mu, rwkv_v0, rwkv_v_up, mla_q_norm_g, mla_w_q_up, mla_kv_norm_g, mla_w_kv_up, mla_out_norm_g, ln1_g, ln1_b, router_w, router_b, exp_w_gu, exp_b_gu, exp_w_dn, exp_b_dn, ln2_g, ln2_b):
    depth = w_in.shape[0]
    B_, S_, D = x.shape
    alpha = (2 * depth) ** 0.25
    tables = rope_tables(S_)
    ret_tabs = retention_tables()
    expert_params = prep_expert_params(exp_w_gu, exp_b_gu, exp_w_dn, exp_b_dn)
    n_exp = exp_w_gu.shape[1]
    n_groups = BATCH_GROUPS if B_ % BATCH_GROUPS == 0 else 1
    bg = B_ // n_groups
    xs = [x[g * bg:(g + 1) * bg] for g in range(n_groups)]
    v_first = [None] * n_groups
    n_tok = bg * S_
    flat = lambda t: t.reshape(n_tok, t.shape[-1])
    for l in range(depth):
        w_cat = prep_in_weights(w_in[l], rwkv_vres_down[l - 1] if l > 0 else None)
        rw_params = prep_rwkv_params(rwkv_mu[l], rwkv_w0[l], rwkv_w_up[l], rwkv_a0[l], rwkv_a_up[l], rwkv_g_up[l],
                                     rwkv_k_k[l], rwkv_k_a[l], rwkv_vres_mu[l - 1] if l > 0 else None,
                                     rwkv_v0[l - 1] if l > 0 else None, rwkv_v_up[l - 1] if l > 0 else None)
        mla_params = prep_mla_params(mla_q_norm_g[l], mla_w_q_up[l], mla_kv_norm_g[l], mla_w_kv_up[l])
        gn = jnp.stack([rwkv_ln_g[l], rwkv_ln_b[l], rwkv_r_k[l].reshape(-1)])
        w_abc, w_d, g_d = prep_out_weights(w_out[l], mla_out_norm_g[l])
        for g in range(n_groups):
            (ret_q, ret_k, ret_v, ret_g, dil_q, dil_k, dil_v, rw_r, rw_lw, rw_k, rw_v, rw_a, rw_b, rw_g,
             mla_q, mla_k, mla_v) = fused_in_proj(xs[g], w_cat, tables, rw_params, mla_params, v_first[g])
            if l == 0:
                v_first[g] = rw_v
            o_a = retention_fused(ret_q, ret_k, ret_v, ret_g, ret_norm_g[l], ret_tabs)
            o_b = dilated_fused(dil_q, dil_k, dil_v)
            o_c = rwkv7_mixer_fused(rw_r, rw_lw, rw_k, rw_v, rw_a, rw_b, rw_g, gn)
            o_d = mla_attention(mla_q, mla_k, mla_v)
            x1, x1b, logits = fused_out_proj(flat(xs[g]), flat(o_a), flat(o_b), flat(o_c), flat(o_d), w_abc, w_d,
                                             dil_norm_g[l].reshape(1, DIL_W), g_d, jnp.stack([ln1_g[l], ln1_b[l]]),
                                             router_w[l], router_b[l], alpha)
            xs[g] = moe_layer(x1, x1b, logits, expert_params, l * n_exp, jnp.stack([ln2_g[l], ln2_b[l]]),
                              alpha).reshape(bg, S_, D)
    return jnp.concatenate(xs, axis=0)
```

```python
import functools

import numpy as np
import jax
import jax.numpy as jnp
from jax import lax
from jax.experimental import pallas as pl
from jax.experimental.pallas import tpu as pltpu

F32 = jnp.float32
BF16 = jnp.bfloat16

LANES = 128
BLK = 128
LN_EPS = 1e-5
NORM_EPS = 1e-6

RET_HEADS, RET_DK, RET_DV = 4, 32, 64
RET_QK = RET_HEADS * RET_DK
RET_W = RET_HEADS * RET_DV
RET_THETA = 10000.0
DIL_HEADS, DIL_DH = 4, 64
DIL_W = DIL_HEADS * DIL_DH
DIL_PATTERNS = ((128, 1), (512, 4), (2048, 16))
ROPE_THETA = 500000.0
ROPE_ROT_DIM = DIL_DH // 4
RWKV_HEADS, RWKV_DH = 4, 64
RWKV_W = RWKV_HEADS * RWKV_DH
DECAY_LORA, AAA_LORA, MV_LORA, GATE_LORA = 64, 64, 32, 128
RWKV_GN_EPS = 64e-5
MLA_HEADS, MLA_NOPE, MLA_ROPE, MLA_DV = 4, 64, 32, 64
MLA_W = MLA_HEADS * MLA_DV
Q_LORA, KV_LORA = 256, 128
MLA_THETA = 10000.0
TOP_K = 4
SWIGLU_LIMIT, SWIGLU_ALPHA = 7.0, 1.702

RET_SPLITS = (RET_QK, RET_QK, RET_W, RET_W)
DIL_SPLITS = (DIL_W, DIL_W, DIL_W)
RWKV_SPLITS = (RWKV_W, RWKV_W, RWKV_W, DECAY_LORA, AAA_LORA, GATE_LORA)
MLA_SPLITS = (Q_LORA, KV_LORA, MLA_ROPE)
A_END = sum(RET_SPLITS)
B_END = A_END + sum(DIL_SPLITS)
C_END = B_END + sum(RWKV_SPLITS)
N_IN = C_END + sum(MLA_SPLITS)

RWKV_CHUNK = 64
RWKV_BLOCK = 256

MLA_HEAD_PAD = 128
MLA_QK_W = MLA_HEADS * MLA_HEAD_PAD
ATTN_TILE = 512

IN_RET_W = 4 * RET_QK + 2 * RET_W
IN_DIL_W = 5 * DIL_W
IN_RWKV_W = sum(RWKV_SPLITS) + LANES
IN_MLA_W = Q_LORA + KV_LORA + LANES
IN_COLS = IN_RET_W + IN_DIL_W + IN_RWKV_W + IN_MLA_W
IN_TILE = 256
OUT_TILE = 512
ROUTER_PAD = 128
DIL_PAIR_W = 2 * DIL_DH
DEINT_ROWS = 512
MOE_TILE = 512
DIL_UNROLL = 4
VMEM_LIMIT = 48 * 1024 * 1024
DIL_VMEM_LIMIT = 56 * 1024 * 1024


def split_cols(p, sizes):
    idx = np.cumsum(sizes)[:-1].tolist()
    return jnp.split(p, idx, axis=-1)


def rope_table(n_pos, rot_dim, theta):
    inv_freq = 1.0 / (theta ** (jnp.arange(0, rot_dim, 2, dtype=F32) / rot_dim))
    ang = jnp.arange(n_pos, dtype=F32)[:, None] * inv_freq[None, :]
    return jnp.cos(ang), jnp.sin(ang)


def _bdot(a, b, dims):
    return lax.dot_general(a.astype(BF16), b.astype(BF16), ((dims[0], dims[1]), ((), ())), preferred_element_type=F32)


_NN = ((1,), (0,))
_NT = ((1,), (1,))
_TN = ((0,), (0,))


def _split3(x):
    h1 = x.astype(BF16)
    r1 = x - h1.astype(F32)
    h2 = r1.astype(BF16)
    h3 = (r1 - h2.astype(F32)).astype(BF16)
    return h1, h2, h3


def _head_sum(x, dh):
    w = x.shape[1]
    i = lax.broadcasted_iota(jnp.int32, (w, w), 0) // dh
    j = lax.broadcasted_iota(jnp.int32, (w, w), 1) // dh
    ones = (i == j).astype(BF16)
    return sum(jnp.dot(p, ones, preferred_element_type=F32) for p in _split3(x))


def _layer_norm_rows(h, ln_ref):
    mu = jnp.mean(h, -1, keepdims=True)
    hc = h - mu
    var = jnp.mean(hc * hc, -1, keepdims=True)
    return hc * lax.rsqrt(var + LN_EPS) * ln_ref[0:1, :] + ln_ref[1:2, :]


def _rot_half_cols(w, heads, dh, rot):
    d_in = w.shape[0]
    w = w.reshape(d_in, heads, dh)
    half = rot // 2
    sw = jnp.concatenate([-w[..., half:rot], w[..., :half], jnp.zeros((d_in, heads, dh - rot), w.dtype)], axis=-1)
    return sw.reshape(d_in, heads * dh)


def _rope_lanes(cs, heads, dh, rot, lead=0):
    cos, sin = cs
    S_ = cos.shape[0]
    c = jnp.concatenate([jnp.ones((S_, lead), F32), cos, cos, jnp.ones((S_, dh - lead - rot), F32)], axis=1)
    s = jnp.concatenate([jnp.zeros((S_, lead), F32), sin, sin, jnp.zeros((S_, dh - lead - rot), F32)], axis=1)
    return jnp.tile(c, (1, heads)), jnp.tile(s, (1, heads))


def rope_tables(S_):
    ret = _rope_lanes(rope_table(S_, RET_DK, RET_THETA), RET_HEADS, RET_DK, RET_DK)
    dil = _rope_lanes(rope_table(S_, ROPE_ROT_DIM, ROPE_THETA), DIL_HEADS, DIL_DH, ROPE_ROT_DIM)
    cos, sin = rope_table(S_, MLA_ROPE, MLA_THETA)
    mq = _rope_lanes((cos, sin), MLA_HEADS, MLA_HEAD_PAD, MLA_ROPE, lead=MLA_NOPE)
    mk = jnp.concatenate([cos, cos, sin, sin, jnp.zeros((S_, LANES - 2 * MLA_ROPE), F32)], axis=1)
    return ret + dil + mq + (mk,)


def prep_in_weights(w_in_l, vres_down_l):
    D = w_in_l.shape[0]
    a_q, a_k, a_v, a_g = split_cols(w_in_l[:, :A_END], RET_SPLITS)
    b_q, b_k, b_v = split_cols(w_in_l[:, A_END:B_END], DIL_SPLITS)
    d_cq, d_ckv, d_kr = split_cols(w_in_l[:, C_END:N_IN], MLA_SPLITS)
    vres = jnp.zeros((D, LANES), F32)
    if vres_down_l is not None:
        vres = vres.at[:, :MV_LORA].set(vres_down_l)
    cols = [a_q, _rot_half_cols(a_q, RET_HEADS, RET_DK, RET_DK), a_k, _rot_half_cols(a_k, RET_HEADS, RET_DK, RET_DK), a_v, a_g,
            b_q, _rot_half_cols(b_q, DIL_HEADS, DIL_DH, ROPE_ROT_DIM), b_k, _rot_half_cols(b_k, DIL_HEADS, DIL_DH, ROPE_ROT_DIM), b_v,
            w_in_l[:, B_END:C_END], vres,
            d_cq, d_ckv, d_kr, _rot_half_cols(d_kr, 1, MLA_ROPE, MLA_ROPE), jnp.zeros((D, LANES - 2 * MLA_ROPE), F32)]
    w = jnp.concatenate(cols, axis=1)
    assert w.shape[1] == IN_COLS
    return w.astype(BF16)


def prep_rwkv_params(mu, w0, w_up, a0, a_up, g_up, k_k, k_a, vres_mu, v0, v_up):
    W = RWKV_W
    mu_ext = jnp.zeros((1, IN_RWKV_W), F32).at[0, :mu.shape[0]].set(mu)
    v_up_ext = jnp.zeros((LANES, W), F32)
    v0_ext = jnp.zeros((1, W), F32)
    if vres_mu is not None:
        mu_ext = mu_ext.at[0, mu.shape[0]:mu.shape[0] + MV_LORA].set(vres_mu)
        v_up_ext = v_up_ext.at[:MV_LORA].set(v_up)
        v0_ext = v0.reshape(1, W)
    wa_up = jnp.zeros((LANES, 2 * W), F32).at[:DECAY_LORA, :W].set(w_up).at[DECAY_LORA:, W:].set(a_up)
    return [mu_ext, wa_up.astype(BF16), jnp.concatenate([w0, a0]).reshape(1, 2 * W), g_up.astype(BF16),
            jnp.stack([k_k, k_a]), v_up_ext.astype(BF16), v0_ext]


def prep_mla_params(q_norm_g, w_q_up, kv_norm_g, w_kv_up):
    ql = w_q_up.shape[0]
    wq = w_q_up.reshape(ql, MLA_HEADS, MLA_NOPE + MLA_ROPE)
    pad = jnp.zeros((ql, MLA_HEADS, MLA_HEAD_PAD - MLA_NOPE - MLA_ROPE), F32)
    q_main = jnp.concatenate([wq, pad], axis=-1).reshape(ql, MLA_QK_W)
    rope = wq[..., MLA_NOPE:]
    half = MLA_ROPE // 2
    q_rot = jnp.concatenate([jnp.zeros((ql, MLA_HEADS, MLA_NOPE), F32), -rope[..., half:], rope[..., :half], pad], axis=-1)
    w_q = jnp.concatenate([q_main, q_rot.reshape(ql, MLA_QK_W)], axis=1).astype(BF16)
    kl = w_kv_up.shape[0]
    wkv = w_kv_up.reshape(kl, MLA_HEADS, MLA_NOPE + MLA_DV)
    k_main = jnp.concatenate([wkv[..., :MLA_NOPE], jnp.zeros((kl, MLA_HEADS, MLA_HEAD_PAD - MLA_NOPE), F32)], axis=-1)
    w_kv = jnp.concatenate([k_main.reshape(kl, MLA_QK_W), wkv[..., MLA_NOPE:].reshape(kl, MLA_W)], axis=1).astype(BF16)
    return [q_norm_g.reshape(1, Q_LORA), w_q, kv_norm_g.reshape(1, KV_LORA), w_kv]


def prep_out_weights(w_out_l, mla_out_norm_g):
    W = RET_W
    w_abc = w_out_l[:3 * W].astype(BF16)
    w_d = w_out_l[3 * W:].reshape(MLA_HEADS, MLA_DV, -1)
    g_d = mla_out_norm_g.reshape(MLA_HEADS, MLA_DV)
    zw = jnp.zeros_like(w_d[0])
    zg = jnp.zeros_like(g_d[0])
    rows, gains = [], []
    for h in range(MLA_HEADS):
        rows += [w_d[h], zw] if h % 2 == 0 else [zw, w_d[h]]
        gains += [g_d[h], zg] if h % 2 == 0 else [zg, g_d[h]]
    return w_abc, jnp.concatenate(rows, axis=0).astype(BF16), jnp.concatenate(gains).reshape(1, MLA_QK_W)


def retention_tables():
    log_gamma = jnp.log(1.0 - 2.0 ** (-5.0 - jnp.arange(RET_HEADS, dtype=F32)))
    idx = jnp.arange(BLK, dtype=F32)
    dist = idx[:, None] - idx[None, :]
    inner = jnp.where(dist >= 0, jnp.exp(jnp.maximum(dist, 0.0)[None] * log_gamma[:, None, None]), 0.0)
    zeta = jnp.exp((BLK - 1 - idx)[None, :] * log_gamma[:, None])
    xi = jnp.exp((idx + 1.0)[None, :] * log_gamma[:, None])
    zeta_k = jnp.repeat(zeta.T, RET_DK, axis=1)
    xi_v = jnp.repeat(xi.T, RET_DV, axis=1)
    row_h = jnp.arange(RET_QK) // RET_DK
    col_h = jnp.arange(RET_W) // RET_DV
    same = row_h[:, None] == col_h[None, :]
    state_decay = jnp.where(same, jnp.exp(BLK * log_gamma)[row_h][:, None], 0.0)
    return inner, zeta_k, xi_v, state_decay, same.astype(F32)


def _in_kernel(x_ref, w_ref, cr_ref, sr_ref, cd_ref, sd_ref, cq_ref, sq_ref, ck_ref,
               mu_ref, wa_up_ref, w0a0_ref, g_up_ref, kk_ka_ref, v_up_ref, v0_ref, vfirst_ref,
               qn_ref, wq_ref, kvn_ref, wkv_ref,
               ret_q, ret_k, ret_v, ret_g, dil_q, dil_k, dil_v,
               rw_r, rw_lw, rw_k, rw_v, rw_a, rw_b, rw_g, mla_q, mla_k, mla_v,
               carry_ref, *, has_vres):
    tm = x_ref.shape[1]

    @pl.when(pl.program_id(1) == 0)
    def _():
        carry_ref[...] = jnp.zeros_like(carry_ref)

    xb = x_ref[0].astype(BF16)
    o1, o2, o3 = IN_RET_W, IN_RET_W + IN_DIL_W, IN_RET_W + IN_DIL_W + IN_RWKV_W
    pa = jnp.dot(xb, w_ref[:, :o1], preferred_element_type=F32)
    pb = jnp.dot(xb, w_ref[:, o1:o2], preferred_element_type=F32)
    pc = jnp.dot(xb, w_ref[:, o2:o3], preferred_element_type=F32)
    pd = jnp.dot(xb, w_ref[:, o3:], preferred_element_type=F32)

    qk = RET_QK
    cr, sr = cr_ref[...], sr_ref[...]
    ret_q[0] = (pa[:, 0:qk] * cr + pa[:, qk:2 * qk] * sr).astype(ret_q.dtype)
    ret_k[0] = ((pa[:, 2 * qk:3 * qk] * cr + pa[:, 3 * qk:4 * qk] * sr) * (RET_DK ** -0.5)).astype(ret_k.dtype)
    ret_v[0] = pa[:, 4 * qk:4 * qk + RET_W].astype(ret_v.dtype)
    ret_g[0] = pa[:, 4 * qk + RET_W:]

    cd, sd = cd_ref[...], sd_ref[...]
    dil_q[0] = pb[:, 0:DIL_W] * cd + pb[:, DIL_W:2 * DIL_W] * sd
    dil_k[0] = pb[:, 2 * DIL_W:3 * DIL_W] * cd + pb[:, 3 * DIL_W:4 * DIL_W] * sd
    dil_v[0] = pb[:, 4 * DIL_W:]

    row = lax.broadcasted_iota(jnp.int32, pc.shape, 0)
    prev = jnp.where(row == 0, carry_ref[...], pltpu.roll(pc, 1, 0))
    carry_ref[...] = pc[tm - 1:tm, :]
    ps = pc + (prev - pc) * mu_ref[...]
    W = RWKV_W
    r, k, v = ps[:, 0:W], ps[:, W:2 * W], ps[:, 2 * W:3 * W]
    wd_ad = ps[:, 3 * W:3 * W + LANES]
    lane = lax.broadcasted_iota(jnp.int32, wd_ad.shape, 1)
    lora_in = jnp.where(lane < DECAY_LORA, jnp.tanh(wd_ad), wd_ad)
    lora = jnp.dot(lora_in.astype(BF16), wa_up_ref[...], preferred_element_type=F32) + w0a0_ref[...]
    w_raw = -jax.nn.softplus(-lora[:, :W]) - 0.5
    a_sig = jax.nn.sigmoid(lora[:, W:])
    gd = ps[:, 3 * W + LANES:3 * W + 2 * LANES]
    rw_g[0] = jnp.dot(jax.nn.sigmoid(gd).astype(BF16), g_up_ref[...], preferred_element_type=F32)
    kk = k * kk_ka_ref[0:1, :]
    kk = kk / jnp.maximum(jnp.sqrt(_head_sum(kk * kk, RWKV_DH)), 1e-12)
    if has_vres:
        vd = ps[:, 3 * W + 2 * LANES:]
        mix = jax.nn.sigmoid(jnp.dot(vd.astype(BF16), v_up_ref[...], preferred_element_type=F32) + v0_ref[...])
        v = v + (vfirst_ref[0] - v) * mix
    rw_r[0] = r
    rw_lw[0] = -jnp.exp(w_raw)
    rw_k[0] = k * (1.0 + (a_sig - 1.0) * kk_ka_ref[1:2, :])
    rw_v[0] = v
    rw_a[0] = -kk
    rw_b[0] = kk * a_sig

    c_q = pd[:, :Q_LORA]
    c_q = c_q * lax.rsqrt(jnp.mean(c_q * c_q, -1, keepdims=True) + NORM_EPS) * qn_ref[...]
    q2 = jnp.dot(c_q.astype(BF16), wq_ref[...], preferred_element_type=F32)
    scale = (MLA_NOPE + MLA_ROPE) ** -0.5
    mla_q[0] = ((q2[:, :MLA_QK_W] * cq_ref[...] + q2[:, MLA_QK_W:] * sq_ref[...]) * scale).astype(mla_q.dtype)
    c_kv = pd[:, Q_LORA:Q_LORA + KV_LORA]
    c_kv = c_kv * lax.rsqrt(jnp.mean(c_kv * c_kv, -1, keepdims=True) + NORM_EPS) * kvn_ref[...]
    kv = jnp.dot(c_kv.astype(BF16), wkv_ref[...], preferred_element_type=F32)
    kr = pd[:, Q_LORA + KV_LORA:] * ck_ref[...]
    i = lax.broadcasted_iota(jnp.int32, (LANES, MLA_QK_W), 0)
    j = lax.broadcasted_iota(jnp.int32, (LANES, MLA_QK_W), 1) % MLA_HEAD_PAD
    place = ((i < 2 * MLA_ROPE) & (j == MLA_NOPE + i % MLA_ROPE)).astype(BF16)
    k_pe = sum(jnp.dot(p, place, preferred_element_type=F32) for p in _split3(kr)[:2])
    mla_k[0] = (kv[:, :MLA_QK_W] + k_pe).astype(mla_k.dtype)
    mla_v[0] = kv[:, MLA_QK_W:].astype(mla_v.dtype)


def fused_in_proj(x, w_cat, tables, rw_params, mla_params, v_first, tile=IN_TILE):
    B_, S_, D = x.shape
    tile = min(tile, S_)
    assert S_ % tile == 0
    has_vres = v_first is not None
    tok = lambda w: pl.BlockSpec((1, tile, w), lambda b, i: (b, i, 0))
    pos = lambda a: pl.BlockSpec((tile, a.shape[1]), lambda b, i: (i, 0))
    full = lambda a: pl.BlockSpec(a.shape, lambda b, i: (0,) * a.ndim)
    if has_vres:
        vf_spec = tok(RWKV_W)
    else:
        v_first = jnp.zeros((1, tile, RWKV_W), F32)
        vf_spec = pl.BlockSpec((1, tile, RWKV_W), lambda b, i: (0, 0, 0))
    f32o = lambda w: jax.ShapeDtypeStruct((B_, S_, w), F32)
    bfo = lambda w: jax.ShapeDtypeStruct((B_, S_, w), BF16)
    out_shape = ([bfo(RET_QK), bfo(RET_QK), bfo(RET_W), f32o(RET_W)] + [f32o(DIL_W)] * 3 + [f32o(RWKV_W)] * 7
                 + [bfo(MLA_QK_W), bfo(MLA_QK_W), bfo(MLA_W)])
    return pl.pallas_call(
        functools.partial(_in_kernel, has_vres=has_vres),
        grid=(B_, S_ // tile),
        in_specs=[tok(D), full(w_cat)] + [pos(t) for t in tables] + [full(p) for p in rw_params] + [vf_spec]
                 + [full(p) for p in mla_params],
        out_specs=[tok(s.shape[-1]) for s in out_shape],
        out_shape=out_shape,
        scratch_shapes=[pltpu.VMEM((1, IN_RWKV_W), F32)],
        compiler_params=pltpu.CompilerParams(dimension_semantics=("arbitrary", "arbitrary"), vmem_limit_bytes=VMEM_LIMIT),
        name="fused_in_proj",
    )(x, w_cat, *tables, *rw_params, v_first, *mla_params)


def _retention_kernel(q_ref, k_ref, v_ref, g_ref, ng_ref, inner_ref, zeta_ref, xi_ref, sdec_ref, smask_ref, o_ref, state_ref):
    @pl.when(pl.program_id(1) == 0)
    def _():
        state_ref[...] = jnp.zeros_like(state_ref)

    q, k, v = q_ref[0], k_ref[0], v_ref[0]
    q_head = lax.broadcasted_iota(jnp.int32, q.shape, 1) // RET_DK
    v_head = lax.broadcasted_iota(jnp.int32, v.shape, 1) // RET_DV
    zero_q = jnp.zeros_like(q)
    scores = [_bdot(jnp.where(q_head == h, q, zero_q), k, _NT) * inner_ref[h] for h in range(RET_HEADS)]
    o_heads = [_bdot(s, v, _NN) for s in scores]
    o = _bdot(q, state_ref[...], _NN) * xi_ref[...]
    for h in range(RET_HEADS):
        o = o + jnp.where(v_head == h, o_heads[h], 0.0)
    state_ref[...] = state_ref[...] * sdec_ref[...] + _bdot(k.astype(F32) * zeta_ref[...], v, _TN) * smask_ref[...]
    o = o * lax.rsqrt(_head_sum(o * o, RET_DV) * (1.0 / RET_DV) + NORM_EPS) * ng_ref[...]
    g = g_ref[0]
    o_ref[0] = g * jax.nn.sigmoid(g) * o


def retention_fused(q, k, v, g, norm_g, tables):
    B_, S_, _ = q.shape
    assert S_ % BLK == 0
    tok = lambda w: pl.BlockSpec((1, BLK, w), lambda b, c: (b, c, 0))
    full = lambda a: pl.BlockSpec(a.shape, lambda b, c: (0,) * a.ndim)
    ng = norm_g.reshape(1, RET_W)
    return pl.pallas_call(
        _retention_kernel,
        grid=(B_, S_ // BLK),
        in_specs=[tok(RET_QK), tok(RET_QK), tok(RET_W), tok(RET_W), full(ng)] + [full(t) for t in tables],
        out_specs=tok(RET_W),
        out_shape=jax.ShapeDtypeStruct((B_, S_, RET_W), F32),
        scratch_shapes=[pltpu.VMEM((RET_QK, RET_W), F32)],
        compiler_params=pltpu.CompilerParams(dimension_semantics=("arbitrary", "arbitrary")),
        name="retention_chunk",
    )(q, k, v, g, ng, *tables)


def _dilated_kernel(q_ref, k_ref, v_ref, o_ref, m_ref, l_ref, *, dils):
    S_ = q_ref.shape[1]
    lane_head = lax.broadcasted_iota(jnp.int32, (BLK, DIL_PAIR_W), 1) // DIL_DH
    i_idx = lax.broadcasted_iota(jnp.int32, (BLK, 2 * BLK), 0)
    j_idx = lax.broadcasted_iota(jnp.int32, (BLK, 2 * BLK), 1)
    scale = DIL_DH ** -0.5

    def rows(start, n, d):
        return pl.ds(start, n) if d == 1 else pl.ds(start, n, stride=d)

    n_un = DIL_UNROLL
    for pi, d in enumerate(dils):
        def tiles(tt, carry, d=d, first=(pi == 0)):
            nb = [(tt * n_un + u) // d for u in range(n_un)]
            r = [(tt * n_un + u) % d for u in range(n_un)]
            kb = [jnp.maximum(n - 1, 0) for n in nb]
            q_rows = [rows(nb[u] * (BLK * d) + r[u], BLK, d) for u in range(n_un)]
            k_rows = [rows(kb[u] * (BLK * d) + r[u], 2 * BLK, d) for u in range(n_un)]
            q = [q_ref[0, q_rows[u], :] for u in range(n_un)]
            kk = [k_ref[0, k_rows[u], :].astype(BF16) for u in range(n_un)]
            vv = [v_ref[0, k_rows[u], :].astype(BF16) for u in range(n_un)]
            if not first:
                m_old = [m_ref[q_rows[u], :] for u in range(n_un)]
                l_old = [l_ref[q_rows[u], :] for u in range(n_un)]
                o_old = [o_ref[0, q_rows[u], :] for u in range(n_un)]
            delta = [(nb[u] - kb[u]) * BLK + i_idx - j_idx for u in range(n_un)]
            valid = [(dl >= 0) & (dl <= BLK) for dl in delta]
            pairs = [(u, h) for u in range(n_un) for h in range(2)]
            s = {(u, h): jnp.where(valid[u], _bdot(jnp.where(lane_head == h, q[u], jnp.zeros_like(q[u])), kk[u], _NT) * scale,
                                   -jnp.inf) for u, h in pairs}
            m = {p: jnp.max(s[p], axis=-1, keepdims=True) for p in pairs}
            pr = {p: jnp.exp(s[p] - m[p]) for p in pairs}
            l = {p: jnp.sum(pr[p], axis=-1, keepdims=True) for p in pairs}
            pv = {(u, h): jnp.dot(pr[u, h].astype(BF16), vv[u], preferred_element_type=F32) for u, h in pairs}
            for u in range(n_un):
                m_t = jnp.where(lane_head == 0, m[u, 0], m[u, 1])
                l_t = jnp.where(lane_head == 0, l[u, 0], l[u, 1])
                pv_t = jnp.where(lane_head == 0, pv[u, 0], pv[u, 1])
                if first:
                    m_ref[q_rows[u], :] = m_t
                    l_ref[q_rows[u], :] = l_t
                    o_ref[0, q_rows[u], :] = pv_t
                else:
                    m_new = jnp.maximum(m_old[u], m_t)
                    c_old = jnp.exp(m_old[u] - m_new)
                    c_t = jnp.exp(m_t - m_new)
                    m_ref[q_rows[u], :] = m_new
                    l_ref[q_rows[u], :] = l_old[u] * c_old + l_t * c_t
                    o_ref[0, q_rows[u], :] = o_old[u] * c_old + pv_t * c_t
            return carry

        lax.fori_loop(0, S_ // (BLK * n_un), tiles, 0)

    def finish(t, carry):
        r0 = pl.multiple_of(t * BLK, BLK)
        o_ref[0, pl.ds(r0, BLK), :] = o_ref[0, pl.ds(r0, BLK), :] / l_ref[pl.ds(r0, BLK), :]
        return carry

    lax.fori_loop(0, S_ // BLK, finish, 0)


def dilated_fused(q, k, v):
    B_, S_, W_ = q.shape
    dils = tuple(d for _, d in DIL_PATTERNS)
    assert all(w // d == BLK for w, d in DIL_PATTERNS) and S_ % (2 * BLK * max(dils)) == 0 and S_ % (BLK * DIL_UNROLL) == 0
    spec = pl.BlockSpec((1, S_, DIL_PAIR_W), lambda b, hp: (b, 0, hp))
    return pl.pallas_call(
        functools.partial(_dilated_kernel, dils=dils),
        grid=(B_, W_ // DIL_PAIR_W),
        in_specs=[spec] * 3,
        out_specs=spec,
        out_shape=jax.ShapeDtypeStruct((B_, S_, W_), F32),
        scratch_shapes=[pltpu.VMEM((S_, DIL_PAIR_W), F32)] * 2,
        compiler_params=pltpu.CompilerParams(dimension_semantics=("arbitrary", "arbitrary"), vmem_limit_bytes=DIL_VMEM_LIMIT),
        name="dilated_attention",
    )(q, k, v)


def _rwkv7_mixer_kernel(r_ref, lw_ref, k_ref, v_ref, a_ref, b_ref, g_ref, gn_ref, o_ref, state_ref, *, heads, dh, chunk):
    rows = r_ref.shape[1]
    n_sub = rows // chunk
    n_double = int(np.log2(chunk))
    assert 2 ** n_double == chunk and n_sub * chunk == rows

    @pl.when(pl.program_id(1) == 0)
    def _():
        state_ref[...] = jnp.zeros_like(state_ref)

    ri = lax.broadcasted_iota(jnp.int32, (rows, rows), 0)
    ci = lax.broadcasted_iota(jnp.int32, (rows, rows), 1)
    tri = ((ci <= ri) & (ci >= (ri // chunk) * chunk)).astype(BF16)
    logw = lw_ref[0]
    cum = sum(jnp.dot(tri, piece, preferred_element_type=F32) for piece in _split3(logw))

    row = lax.broadcasted_iota(jnp.int32, (chunk, 2 * chunk), 0)
    col = lax.broadcasted_iota(jnp.int32, (chunk, 2 * chunk), 1) % chunk
    strict = col < row
    incl = col <= row
    zeros_cv = jnp.zeros((chunk, dh), F32)

    pairs = [(s, h) for s in range(n_sub) for h in range(heads)]
    ah, rh, vh, bk_rem, w_total = {}, {}, {}, {}, {}
    l_abk, m_rbk = {}, {}
    for s in range(n_sub):
        rs = slice(s * chunk, (s + 1) * chunk)
        cum_s, logw_s = cum[rs], logw[rs]
        total = cum_s[chunk - 1:chunk, :]
        e_in = jnp.exp(cum_s)
        e_neg = jnp.exp(-cum_s)
        e_rem = jnp.exp(total - cum_s)
        w_tot = jnp.exp(total)
        r_hat = r_ref[0, rs, :] * e_in
        a_hat = a_ref[0, rs, :] * jnp.exp(cum_s - logw_s)
        b_all, k_all, v_all = b_ref[0, rs, :], k_ref[0, rs, :], v_ref[0, rs, :]
        b_til, k_til = b_all * e_neg, k_all * e_neg
        b_rem, k_rem = b_all * e_rem, k_all * e_rem
        for h in range(heads):
            sl = slice(h * dh, (h + 1) * dh)
            ah[s, h], rh[s, h], vh[s, h] = a_hat[:, sl], r_hat[:, sl], v_all[:, sl]
            w_total[s, h] = w_tot[:, sl]
            bk_rem[s, h] = jnp.concatenate([b_rem[:, sl], k_rem[:, sl]], axis=0)
            bk_til = jnp.concatenate([b_til[:, sl], k_til[:, sl]], axis=0)
            l_abk[s, h] = jnp.where(strict, _bdot(ah[s, h], bk_til, _NT), 0.0)
            m_rbk[s, h] = jnp.where(incl, _bdot(rh[s, h], bk_til, _NT), 0.0)
    z = {p: jnp.concatenate([ah[p], _bdot(l_abk[p], jnp.concatenate([zeros_cv, vh[p]], axis=0), _NN)], axis=1)
         for p in pairs}
    lp = {p: l_abk[p][:, :chunk] for p in pairs}
    for m in range(n_double):
        z = {p: z[p] + _bdot(lp[p], z[p], _NN) for p in pairs}
        if m + 1 < n_double:
            lp = {p: _bdot(lp[p], lp[p], _NN) for p in pairs}
    state = [state_ref[h] for h in range(heads)]
    y_parts = []
    for s in range(n_sub):
        pr_s = [_bdot(jnp.concatenate([z[s, h][:, :dh], rh[s, h]], axis=0), state[h], _NT) for h in range(heads)]
        uv = [jnp.concatenate([pr_s[h][:chunk] + z[s, h][:, dh:], vh[s, h]], axis=0) for h in range(heads)]
        ys = [pr_s[h][chunk:] + _bdot(m_rbk[s, h], uv[h], _NN) for h in range(heads)]
        state = [state[h] * w_total[s, h] + _bdot(uv[h], bk_rem[s, h], _TN) for h in range(heads)]
        y_parts.append(jnp.concatenate(ys, axis=1))
    for h in range(heads):
        state_ref[h] = state[h]
    y = jnp.concatenate(y_parts, axis=0)
    mean = _head_sum(y, dh) * (1.0 / dh)
    yc = y - mean
    var = _head_sum(yc * yc, dh) * (1.0 / dh)
    yn = yc * lax.rsqrt(var + RWKV_GN_EPS) * gn_ref[0:1, :] + gn_ref[1:2, :]
    bonus = _head_sum(r_ref[0] * k_ref[0] * gn_ref[2:3, :], dh) * v_ref[0]
    o_ref[0] = (yn + bonus) * g_ref[0]


def rwkv7_mixer_fused(r, logw, k, v, a, b, g, gn, heads=RWKV_HEADS, dh=RWKV_DH, chunk=RWKV_CHUNK, block=RWKV_BLOCK):
    B_, S_, W_ = r.shape
    block = min(block, S_)
    assert W_ == heads * dh and S_ % block == 0 and block % chunk == 0
    spec = pl.BlockSpec((1, block, W_), lambda bi, ci: (bi, ci, 0))
    return pl.pallas_call(
        functools.partial(_rwkv7_mixer_kernel, heads=heads, dh=dh, chunk=chunk),
        grid=(B_, S_ // block),
        in_specs=[spec] * 7 + [pl.BlockSpec(gn.shape, lambda bi, ci: (0, 0))],
        out_specs=spec,
        out_shape=jax.ShapeDtypeStruct((B_, S_, W_), F32),
        scratch_shapes=[pltpu.VMEM((heads, dh, dh), F32)],
        compiler_params=pltpu.CompilerParams(dimension_semantics=("arbitrary", "arbitrary")),
        name="rwkv7_mixer",
    )(r, logw, k, v, a, b, g, gn)


def _mla_flash_kernel(q_ref, k_ref, v_ref, o_ref, *, tile):
    h = pl.program_id(1)
    qi = pl.program_id(2)
    q = q_ref[0]
    dv = v_ref.shape[2]

    def kv_tile(j):
        start = pl.multiple_of(j * tile, tile)
        return k_ref[0, pl.ds(start, tile), :], v_ref[0, pl.ds(start, tile), :]

    def update(carry, s, v_t):
        m, l, acc = carry
        m_new = jnp.maximum(m, jnp.max(s, axis=-1, keepdims=True))
        corr = jnp.exp(m - m_new)
        p = jnp.exp(s - m_new)
        l_new = corr * l + jnp.sum(p, axis=-1, keepdims=True)
        acc_new = corr * acc + jnp.dot(p.astype(v_t.dtype), v_t, preferred_element_type=F32)
        return m_new, l_new, acc_new

    def body(j, carry):
        k_t, v_t = kv_tile(j)
        s = lax.dot_general(q, k_t, (_NT, ((), ())), preferred_element_type=F32)
        return update(carry, s, v_t)

    init = (jnp.full((tile, 1), -jnp.inf, F32), jnp.zeros((tile, 1), F32), jnp.zeros((tile, dv), F32))
    carry = lax.fori_loop(0, qi, body, init)
    k_t, v_t = kv_tile(qi)
    s = lax.dot_general(q, k_t, (_NT, ((), ())), preferred_element_type=F32)
    row = lax.broadcasted_iota(jnp.int32, (tile, tile), 0)
    col = lax.broadcasted_iota(jnp.int32, (tile, tile), 1)
    s = jnp.where(col <= row, s, -jnp.inf)
    m, l, acc = update(carry, s, v_t)
    lane_head = lax.broadcasted_iota(jnp.int32, (tile, dv), 1) // MLA_DV
    o_ref[0] = jnp.where(lane_head == h % 2, acc / l, 0.0)


def mla_attention(q, k, v, tile=ATTN_TILE):
    B_, S_, _ = q.shape
    tile = min(tile, S_)
    assert S_ % tile == 0 and MLA_HEAD_PAD == 2 * MLA_DV
    return pl.pallas_call(
        functools.partial(_mla_flash_kernel, tile=tile),
        grid=(B_, MLA_HEADS, S_ // tile),
        in_specs=[pl.BlockSpec((1, tile, MLA_HEAD_PAD), lambda b, h, i: (b, i, h)),
                  pl.BlockSpec((1, S_, MLA_HEAD_PAD), lambda b, h, i: (b, 0, h)),
                  pl.BlockSpec((1, S_, 2 * MLA_DV), lambda b, h, i: (b, 0, h // 2))],
        out_specs=pl.BlockSpec((1, tile, MLA_HEAD_PAD), lambda b, h, i: (b, i, h)),
        out_shape=jax.ShapeDtypeStruct((B_, S_, MLA_QK_W), F32),
        compiler_params=pltpu.CompilerParams(dimension_semantics=("arbitrary", "arbitrary", "arbitrary")),
        name="mla_causal_flash",
    )(q, k, v)


def _out_kernel(x_ref, oa_ref, ob_ref, oc_ref, od_ref, wabc_ref, wd_ref, gb_ref, gd_ref, ln_ref, wr_ref, br_ref,
                x1_ref, x1b_ref, logit_ref, *, alpha):
    ob = ob_ref[...]
    ob = ob * lax.rsqrt(jnp.mean(ob * ob, -1, keepdims=True) + NORM_EPS) * gb_ref[...]
    od = od_ref[...]
    od = od * lax.rsqrt(jnp.sum(od * od, -1, keepdims=True) * (1.0 / MLA_W) + NORM_EPS) * gd_ref[...]
    W = RET_W
    mix = jnp.dot(oa_ref[...].astype(BF16), wabc_ref[0:W, :], preferred_element_type=F32)
    mix += jnp.dot(ob.astype(BF16), wabc_ref[W:2 * W, :], preferred_element_type=F32)
    mix += jnp.dot(oc_ref[...].astype(BF16), wabc_ref[2 * W:3 * W, :], preferred_element_type=F32)
    mix += jnp.dot(od.astype(BF16), wd_ref[...], preferred_element_type=F32)
    x1 = _layer_norm_rows(alpha * x_ref[...] + mix, ln_ref)
    x1_ref[...] = x1
    x1b = x1.astype(BF16)
    x1b_ref[...] = x1b
    logit_ref[...] = jnp.dot(x1b, wr_ref[...], preferred_element_type=F32) + br_ref[...]


def fused_out_proj(x, o_a, o_b, o_c, o_d, w_abc, w_d, g_b, g_d, ln_gb, w_router, b_router, alpha, tile=OUT_TILE):
    N, D = x.shape
    tile = min(tile, N)
    assert N % tile == 0
    n_exp = w_router.shape[1]
    wr = jnp.zeros((D, ROUTER_PAD), F32).at[:, :n_exp].set(w_router).astype(BF16)
    br = jnp.zeros((1, ROUTER_PAD), F32).at[0, :n_exp].set(b_router)
    consts = [w_abc, w_d, g_b, g_d, ln_gb, wr, br]
    tok = lambda w: pl.BlockSpec((tile, w), lambda i: (i, 0))
    full = lambda a: pl.BlockSpec(a.shape, lambda i: (0,) * a.ndim)
    x1, x1b, logits = pl.pallas_call(
        functools.partial(_out_kernel, alpha=alpha),
        grid=(N // tile,),
        in_specs=[tok(D), tok(RET_W), tok(DIL_W), tok(RWKV_W), tok(MLA_QK_W)] + [full(c) for c in consts],
        out_specs=[tok(D), tok(D), tok(ROUTER_PAD)],
        out_shape=[jax.ShapeDtypeStruct((N, D), F32), jax.ShapeDtypeStruct((N, D), BF16),
                   jax.ShapeDtypeStruct((N, ROUTER_PAD), F32)],
        compiler_params=pltpu.CompilerParams(dimension_semantics=("arbitrary",), vmem_limit_bytes=VMEM_LIMIT),
        name="fused_out_proj",
    )(x, o_a, o_b, o_c, o_d, *consts)
    return x1, x1b, logits[:, :n_exp]


def _moe_expert_kernel(tile_e_ref, n_used_ref, x_ref, wg_ref, wl_ref, bg_ref, bl_ref, wd_ref, bd_ref, o_ref):
    del tile_e_ref
    i = pl.program_id(0)

    @pl.when(i < n_used_ref[0])
    def _():
        x = x_ref[...]
        glu = jnp.dot(x, wg_ref[0], preferred_element_type=F32) + bg_ref[0]
        lin = jnp.dot(x, wl_ref[0], preferred_element_type=F32) + bl_ref[0]
        glu = jnp.minimum(glu, SWIGLU_LIMIT)
        lin = jnp.clip(lin, -SWIGLU_LIMIT, SWIGLU_LIMIT)
        act = glu * jax.nn.sigmoid(SWIGLU_ALPHA * glu) * (lin + 1.0)
        y = jnp.dot(act.astype(wd_ref.dtype), wd_ref[0], preferred_element_type=F32) + bd_ref[0]
        o_ref[...] = y.astype(o_ref.dtype)

    @pl.when(i >= n_used_ref[0])
    def _():
        o_ref[...] = jnp.zeros_like(o_ref)


def _deinterleave_kernel(w_ref, even_ref, odd_ref):
    g = 2 * LANES
    i = lax.broadcasted_iota(jnp.int32, (g, g), 0)
    j = lax.broadcasted_iota(jnp.int32, (g, g), 1)
    perm = (i == jnp.where(j < LANES, 2 * j, 2 * (j - LANES) + 1)).astype(BF16)
    for c in range(w_ref.shape[2] // g):
        blk = w_ref[0, :, c * g:(c + 1) * g].astype(BF16)
        sorted_cols = jnp.dot(blk, perm, preferred_element_type=F32)
        even_ref[0, :, c * LANES:(c + 1) * LANES] = sorted_cols[:, :LANES].astype(even_ref.dtype)
        odd_ref[0, :, c * LANES:(c + 1) * LANES] = sorted_cols[:, LANES:].astype(odd_ref.dtype)


def deinterleave_to_bf16(w):
    E, D, F2 = w.shape
    rows = min(DEINT_ROWS, D)
    assert D % rows == 0 and F2 % (2 * LANES) == 0
    out = jax.ShapeDtypeStruct((E, D, F2 // 2), BF16)
    return pl.pallas_call(
        _deinterleave_kernel,
        grid=(E, D // rows),
        in_specs=[pl.BlockSpec((1, rows, F2), lambda e, r: (e, r, 0))],
        out_specs=[pl.BlockSpec((1, rows, F2 // 2), lambda e, r: (e, r, 0))] * 2,
        out_shape=[out, out],
        compiler_params=pltpu.CompilerParams(dimension_semantics=("arbitrary", "arbitrary"), vmem_limit_bytes=VMEM_LIMIT),
        name="deinterleave_cast",
    )(w)


def moe_experts(xs, tile_e, n_used, w_glu, w_lin, b_glu, b_lin, w_dn, b_dn, tile):
    n_rows, D = xs.shape
    E, _, F = w_glu.shape
    w_spec = lambda shape: pl.BlockSpec((1,) + shape, lambda i, te, nu: (te[i], 0, 0))
    return pl.pallas_call(
        _moe_expert_kernel,
        grid_spec=pltpu.PrefetchScalarGridSpec(
            num_scalar_prefetch=2,
            grid=(n_rows // tile,),
            in_specs=[pl.BlockSpec((tile, D), lambda i, te, nu: (i, 0)),
                      w_spec((D, F)), w_spec((D, F)), w_spec((1, F)), w_spec((1, F)),
                      w_spec((F, D)), w_spec((1, D))],
            out_specs=pl.BlockSpec((tile, D), lambda i, te, nu: (i, 0)),
        ),
        out_shape=jax.ShapeDtypeStruct((n_rows, D), BF16),
        compiler_params=pltpu.CompilerParams(dimension_semantics=("arbitrary",), vmem_limit_bytes=VMEM_LIMIT),
        name="moe_expert_ffn",
    )(tile_e, n_used, xs, w_glu, w_lin, b_glu.reshape(E, 1, F), b_lin.reshape(E, 1, F), w_dn, b_dn.reshape(E, 1, D))


def _combine_kernel(x_ref, y0_ref, y1_ref, y2_ref, y3_ref, gate_ref, ln_ref, o_ref, *, alpha):
    gate = gate_ref[...]
    y = sum(y_ref[...].astype(F32) * gate[:, s:s + 1] for s, y_ref in enumerate((y0_ref, y1_ref, y2_ref, y3_ref)))
    o_ref[...] = _layer_norm_rows(alpha * x_ref[...] + y, ln_ref)


def moe_combine_ln(x1, ys, gate, ln_gb, alpha, tile=OUT_TILE):
    N, D = x1.shape
    tile = min(tile, N)
    assert N % tile == 0 and len(ys) == TOP_K
    tok = lambda w: pl.BlockSpec((tile, w), lambda i: (i, 0))
    return pl.pallas_call(
        functools.partial(_combine_kernel, alpha=alpha),
        grid=(N // tile,),
        in_specs=[tok(D)] * (1 + TOP_K) + [tok(TOP_K), pl.BlockSpec(ln_gb.shape, lambda i: (0, 0))],
        out_specs=tok(D),
        out_shape=jax.ShapeDtypeStruct((N, D), F32),
        compiler_params=pltpu.CompilerParams(dimension_semantics=("arbitrary",), vmem_limit_bytes=VMEM_LIMIT),
        name="moe_combine_ln",
    )(x1, *ys, gate, ln_gb)


def moe_layer(x1, x1b, logits, expert_params, expert_base, ln_gb, alpha, tile=MOE_TILE):
    n_tok, D = x1.shape
    n_exp = logits.shape[1]
    top_val, top_idx = lax.top_k(logits, TOP_K)
    gate = jax.nn.softmax(top_val, axis=-1)
    n_assign = n_tok * TOP_K
    e_flat = top_idx.reshape(-1).astype(jnp.int32)
    experts = jnp.arange(n_exp, dtype=jnp.int32)
    e_sorted, order = lax.sort((e_flat, jnp.arange(n_assign, dtype=jnp.int32)), num_keys=1)
    counts = jnp.sum((e_flat[:, None] == experts[None, :]).astype(jnp.int32), axis=0)
    padded = (counts + tile - 1) // tile * tile
    start = jnp.cumsum(counts) - counts
    pend = jnp.cumsum(padded)
    pstart = pend - padded
    n_tiles = (n_assign + n_exp * (tile - 1) + tile - 1) // tile
    tile_first = jnp.arange(n_tiles, dtype=jnp.int32) * tile
    tile_e = jnp.minimum(jnp.sum((pend[None, :] <= tile_first[:, None]).astype(jnp.int32), axis=1), n_exp - 1)
    n_used = (pend[-1] // tile).astype(jnp.int32).reshape(1)
    within = (tile_first - pstart[tile_e])[:, None] + jnp.arange(tile, dtype=jnp.int32)[None, :]
    valid = within < counts[tile_e][:, None]
    sorted_idx = jnp.clip(start[tile_e][:, None] + within, 0, n_assign - 1)
    row_tok = jnp.where(valid, order[sorted_idx] // TOP_K, 0).reshape(-1)
    shift = jnp.sum(jnp.where(e_sorted[:, None] == experts[None, :], (pstart - start)[None, :], 0), axis=1)
    dest = jnp.arange(n_assign, dtype=jnp.int32) + shift
    _, pos = lax.sort((order, dest), num_keys=1)
    pos = pos.reshape(n_tok, TOP_K)
    yb = moe_experts(x1b[row_tok], tile_e + expert_base, n_used, *expert_params, tile)
    return moe_combine_ln(x1, [yb[pos[:, s]] for s in range(TOP_K)], gate, ln_gb, alpha)


def prep_expert_params(exp_w_gu, exp_b_gu, exp_w_dn, exp_b_dn):
    L, E, D, F2 = exp_w_gu.shape
    w_glu, w_lin = deinterleave_to_bf16(exp_w_gu.reshape(L * E, D, F2))
    b_gu = exp_b_gu.reshape(L * E, F2)
    return (w_glu, w_lin, b_gu[:, 0::2], b_gu[:, 1::2], exp_w_dn.astype(BF16).reshape(L * E, F2 // 2, D),
            exp_b_dn.reshape(L * E, D))


def kernel(x, w_in, w_out, ret_norm_g, dil_norm_g, rwkv_mu, rwkv_w0, rwkv_w_up, rwkv_a0, rwkv_a_up, rwkv_g_up, rwkv_k_k, rwkv_k_a, rwkv_r_k, rwkv_ln_g, rwkv_ln_b, rwkv_vres_down, rwkv_vres_mu, rwkv_v0, rwkv_v_up, mla_q_norm_g, mla_w_q_up, mla_kv_norm_g, mla_w_kv_up, mla_out_norm_g, ln1_g, ln1_b, router_w, router_b, exp_w_gu, exp_b_gu, exp_w_dn, exp_b_dn, ln2_g, ln2_b):
    depth = w_in.shape[0]
    B_, S_, D = x.shape
    alpha = (2 * depth) ** 0.25
    tables = rope_tables(S_)
    ret_tabs = retention_tables()
    expert_params = prep_expert_params(exp_w_gu, exp_b_gu, exp_w_dn, exp_b_dn)
    n_exp = exp_w_gu.shape[1]
    v_first = None
    n_tok = B_ * S_
    flat = lambda t: t.reshape(n_tok, t.shape[-1])
    for l in range(depth):
        w_cat = prep_in_weights(w_in[l], rwkv_vres_down[l - 1] if l > 0 else None)
        rw_params = prep_rwkv_params(rwkv_mu[l], rwkv_w0[l], rwkv_w_up[l], rwkv_a0[l], rwkv_a_up[l], rwkv_g_up[l],
                                     rwkv_k_k[l], rwkv_k_a[l], rwkv_vres_mu[l - 1] if l > 0 else None,
                                     rwkv_v0[l - 1] if l > 0 else None, rwkv_v_up[l - 1] if l > 0 else None)
        mla_params = prep_mla_params(mla_q_norm_g[l], mla_w_q_up[l], mla_kv_norm_g[l], mla_w_kv_up[l])
        gn = jnp.stack([rwkv_ln_g[l], rwkv_ln_b[l], rwkv_r_k[l].reshape(-1)])
        w_abc, w_d, g_d = prep_out_weights(w_out[l], mla_out_norm_g[l])
        (ret_q, ret_k, ret_v, ret_g, dil_q, dil_k, dil_v, rw_r, rw_lw, rw_k, rw_v, rw_a, rw_b, rw_g,
         mla_q, mla_k, mla_v) = fused_in_proj(x, w_cat, tables, rw_params, mla_params, v_first)
        if l == 0:
            v_first = rw_v
        o_a = retention_fused(ret_q, ret_k, ret_v, ret_g, ret_norm_g[l], ret_tabs)
        o_b = dilated_fused(dil_q, dil_k, dil_v)
        o_c = rwkv7_mixer_fused(rw_r, rw_lw, rw_k, rw_v, rw_a, rw_b, rw_g, gn)
        o_d = mla_attention(mla_q, mla_k, mla_v)
        x1, x1b, logits = fused_out_proj(flat(x), flat(o_a), flat(o_b), flat(o_c), flat(o_d), w_abc, w_d,
                                         dil_norm_g[l].reshape(1, DIL_W), g_d, jnp.stack([ln1_g[l], ln1_b[l]]),
                                         router_w[l], router_b[l], alpha)
        x = moe_layer(x1, x1b, logits, expert_params, l * n_exp, jnp.stack([ln2_g[l], ln2_b[l]]),
                      alpha).reshape(B_, S_, D)
    return x
```

```python
import functools

import numpy as np
import jax
import jax.numpy as jnp
from jax import lax
from jax.experimental import pallas as pl
from jax.experimental.pallas import tpu as pltpu

F32 = jnp.float32
BF16 = jnp.bfloat16

LANES = 128
BLK = 128
LN_EPS = 1e-5
NORM_EPS = 1e-6

RET_HEADS, RET_DK, RET_DV = 4, 32, 64
RET_QK = RET_HEADS * RET_DK
RET_W = RET_HEADS * RET_DV
RET_THETA = 10000.0
DIL_HEADS, DIL_DH = 4, 64
DIL_W = DIL_HEADS * DIL_DH
DIL_PATTERNS = ((128, 1), (512, 4), (2048, 16))
ROPE_THETA = 500000.0
ROPE_ROT_DIM = DIL_DH // 4
RWKV_HEADS, RWKV_DH = 4, 64
RWKV_W = RWKV_HEADS * RWKV_DH
DECAY_LORA, AAA_LORA, MV_LORA, GATE_LORA = 64, 64, 32, 128
RWKV_GN_EPS = 64e-5
MLA_HEADS, MLA_NOPE, MLA_ROPE, MLA_DV = 4, 64, 32, 64
MLA_W = MLA_HEADS * MLA_DV
Q_LORA, KV_LORA = 256, 128
MLA_THETA = 10000.0
TOP_K = 4
SWIGLU_LIMIT, SWIGLU_ALPHA = 7.0, 1.702

RET_SPLITS = (RET_QK, RET_QK, RET_W, RET_W)
DIL_SPLITS = (DIL_W, DIL_W, DIL_W)
RWKV_SPLITS = (RWKV_W, RWKV_W, RWKV_W, DECAY_LORA, AAA_LORA, GATE_LORA)
MLA_SPLITS = (Q_LORA, KV_LORA, MLA_ROPE)
A_END = sum(RET_SPLITS)
B_END = A_END + sum(DIL_SPLITS)
C_END = B_END + sum(RWKV_SPLITS)
N_IN = C_END + sum(MLA_SPLITS)

RWKV_CHUNK = 64
RWKV_BLOCK = 256

MLA_HEAD_PAD = 128
MLA_QK_W = MLA_HEADS * MLA_HEAD_PAD
ATTN_TILE = 512

IN_RET_W = 4 * RET_QK + 2 * RET_W
IN_DIL_W = 5 * DIL_W
IN_RWKV_W = sum(RWKV_SPLITS) + LANES
IN_MLA_W = Q_LORA + KV_LORA + LANES
IN_COLS = IN_RET_W + IN_DIL_W + IN_RWKV_W + IN_MLA_W
IN_TILE = 256
OUT_TILE = 512
ROUTER_PAD = 128
DIL_PAIR_W = 2 * DIL_DH
DEINT_ROWS = 512
MOE_TILE = 512
DIL_UNROLL = 4
MOE_GATHER_CHUNKS = 4
VMEM_LIMIT = 48 * 1024 * 1024
DIL_VMEM_LIMIT = 56 * 1024 * 1024


def split_cols(p, sizes):
    idx = np.cumsum(sizes)[:-1].tolist()
    return jnp.split(p, idx, axis=-1)


def rope_table(n_pos, rot_dim, theta):
    inv_freq = 1.0 / (theta ** (jnp.arange(0, rot_dim, 2, dtype=F32) / rot_dim))
    ang = jnp.arange(n_pos, dtype=F32)[:, None] * inv_freq[None, :]
    return jnp.cos(ang), jnp.sin(ang)


def _bdot(a, b, dims):
    return lax.dot_general(a.astype(BF16), b.astype(BF16), ((dims[0], dims[1]), ((), ())), preferred_element_type=F32)


_NN = ((1,), (0,))
_NT = ((1,), (1,))
_TN = ((0,), (0,))


def _split3(x):
    h1 = x.astype(BF16)
    r1 = x - h1.astype(F32)
    h2 = r1.astype(BF16)
    h3 = (r1 - h2.astype(F32)).astype(BF16)
    return h1, h2, h3


def _head_sum(x, dh):
    w = x.shape[1]
    i = lax.broadcasted_iota(jnp.int32, (w, w), 0) // dh
    j = lax.broadcasted_iota(jnp.int32, (w, w), 1) // dh
    ones = (i == j).astype(BF16)
    return sum(jnp.dot(p, ones, preferred_element_type=F32) for p in _split3(x))


def _layer_norm_rows(h, ln_ref):
    mu = jnp.mean(h, -1, keepdims=True)
    hc = h - mu
    var = jnp.mean(hc * hc, -1, keepdims=True)
    return hc * lax.rsqrt(var + LN_EPS) * ln_ref[0:1, :] + ln_ref[1:2, :]


def _rot_half_cols(w, heads, dh, rot):
    d_in = w.shape[0]
    w = w.reshape(d_in, heads, dh)
    half = rot // 2
    sw = jnp.concatenate([-w[..., half:rot], w[..., :half], jnp.zeros((d_in, heads, dh - rot), w.dtype)], axis=-1)
    return sw.reshape(d_in, heads * dh)


def _rope_lanes(cs, heads, dh, rot, lead=0):
    cos, sin = cs
    S_ = cos.shape[0]
    c = jnp.concatenate([jnp.ones((S_, lead), F32), cos, cos, jnp.ones((S_, dh - lead - rot), F32)], axis=1)
    s = jnp.concatenate([jnp.zeros((S_, lead), F32), sin, sin, jnp.zeros((S_, dh - lead - rot), F32)], axis=1)
    return jnp.tile(c, (1, heads)), jnp.tile(s, (1, heads))


def rope_tables(S_):
    ret = _rope_lanes(rope_table(S_, RET_DK, RET_THETA), RET_HEADS, RET_DK, RET_DK)
    dil = _rope_lanes(rope_table(S_, ROPE_ROT_DIM, ROPE_THETA), DIL_HEADS, DIL_DH, ROPE_ROT_DIM)
    cos, sin = rope_table(S_, MLA_ROPE, MLA_THETA)
    mq = _rope_lanes((cos, sin), MLA_HEADS, MLA_HEAD_PAD, MLA_ROPE, lead=MLA_NOPE)
    mk = jnp.concatenate([cos, cos, sin, sin, jnp.zeros((S_, LANES - 2 * MLA_ROPE), F32)], axis=1)
    return ret + dil + mq + (mk,)


def prep_in_weights(w_in_l, vres_down_l):
    D = w_in_l.shape[0]
    a_q, a_k, a_v, a_g = split_cols(w_in_l[:, :A_END], RET_SPLITS)
    b_q, b_k, b_v = split_cols(w_in_l[:, A_END:B_END], DIL_SPLITS)
    d_cq, d_ckv, d_kr = split_cols(w_in_l[:, C_END:N_IN], MLA_SPLITS)
    vres = jnp.zeros((D, LANES), F32)
    if vres_down_l is not None:
        vres = vres.at[:, :MV_LORA].set(vres_down_l)
    cols = [a_q, _rot_half_cols(a_q, RET_HEADS, RET_DK, RET_DK), a_k, _rot_half_cols(a_k, RET_HEADS, RET_DK, RET_DK), a_v, a_g,
            b_q, _rot_half_cols(b_q, DIL_HEADS, DIL_DH, ROPE_ROT_DIM), b_k, _rot_half_cols(b_k, DIL_HEADS, DIL_DH, ROPE_ROT_DIM), b_v,
            w_in_l[:, B_END:C_END], vres,
            d_cq, d_ckv, d_kr, _rot_half_cols(d_kr, 1, MLA_ROPE, MLA_ROPE), jnp.zeros((D, LANES - 2 * MLA_ROPE), F32)]
    w = jnp.concatenate(cols, axis=1)
    assert w.shape[1] == IN_COLS
    return w.astype(BF16)


def prep_rwkv_params(mu, w0, w_up, a0, a_up, g_up, k_k, k_a, vres_mu, v0, v_up):
    W = RWKV_W
    mu_ext = jnp.zeros((1, IN_RWKV_W), F32).at[0, :mu.shape[0]].set(mu)
    v_up_ext = jnp.zeros((LANES, W), F32)
    v0_ext = jnp.zeros((1, W), F32)
    if vres_mu is not None:
        mu_ext = mu_ext.at[0, mu.shape[0]:mu.shape[0] + MV_LORA].set(vres_mu)
        v_up_ext = v_up_ext.at[:MV_LORA].set(v_up)
        v0_ext = v0.reshape(1, W)
    wa_up = jnp.zeros((LANES, 2 * W), F32).at[:DECAY_LORA, :W].set(w_up).at[DECAY_LORA:, W:].set(a_up)
    return [mu_ext, wa_up.astype(BF16), jnp.concatenate([w0, a0]).reshape(1, 2 * W), g_up.astype(BF16),
            jnp.stack([k_k, k_a]), v_up_ext.astype(BF16), v0_ext]


def prep_mla_params(q_norm_g, w_q_up, kv_norm_g, w_kv_up):
    ql = w_q_up.shape[0]
    wq = w_q_up.reshape(ql, MLA_HEADS, MLA_NOPE + MLA_ROPE)
    pad = jnp.zeros((ql, MLA_HEADS, MLA_HEAD_PAD - MLA_NOPE - MLA_ROPE), F32)
    q_main = jnp.concatenate([wq, pad], axis=-1).reshape(ql, MLA_QK_W)
    rope = wq[..., MLA_NOPE:]
    half = MLA_ROPE // 2
    q_rot = jnp.concatenate([jnp.zeros((ql, MLA_HEADS, MLA_NOPE), F32), -rope[..., half:], rope[..., :half], pad], axis=-1)
    w_q = jnp.concatenate([q_main, q_rot.reshape(ql, MLA_QK_W)], axis=1).astype(BF16)
    kl = w_kv_up.shape[0]
    wkv = w_kv_up.reshape(kl, MLA_HEADS, MLA_NOPE + MLA_DV)
    k_main = jnp.concatenate([wkv[..., :MLA_NOPE], jnp.zeros((kl, MLA_HEADS, MLA_HEAD_PAD - MLA_NOPE), F32)], axis=-1)
    w_kv = jnp.concatenate([k_main.reshape(kl, MLA_QK_W), wkv[..., MLA_NOPE:].reshape(kl, MLA_W)], axis=1).astype(BF16)
    return [q_norm_g.reshape(1, Q_LORA), w_q, kv_norm_g.reshape(1, KV_LORA), w_kv]


def prep_out_weights(w_out_l, mla_out_norm_g):
    W = RET_W
    w_abc = w_out_l[:3 * W].astype(BF16)
    w_d = w_out_l[3 * W:].reshape(MLA_HEADS, MLA_DV, -1)
    g_d = mla_out_norm_g.reshape(MLA_HEADS, MLA_DV)
    zw = jnp.zeros_like(w_d[0])
    zg = jnp.zeros_like(g_d[0])
    rows, gains = [], []
    for h in range(MLA_HEADS):
        rows += [w_d[h], zw] if h % 2 == 0 else [zw, w_d[h]]
        gains += [g_d[h], zg] if h % 2 == 0 else [zg, g_d[h]]
    return w_abc, jnp.concatenate(rows, axis=0).astype(BF16), jnp.concatenate(gains).reshape(1, MLA_QK_W)


def retention_tables():
    log_gamma = jnp.log(1.0 - 2.0 ** (-5.0 - jnp.arange(RET_HEADS, dtype=F32)))
    idx = jnp.arange(BLK, dtype=F32)
    dist = idx[:, None] - idx[None, :]
    inner = jnp.where(dist >= 0, jnp.exp(jnp.maximum(dist, 0.0)[None] * log_gamma[:, None, None]), 0.0)
    zeta = jnp.exp((BLK - 1 - idx)[None, :] * log_gamma[:, None])
    xi = jnp.exp((idx + 1.0)[None, :] * log_gamma[:, None])
    zeta_k = jnp.repeat(zeta.T, RET_DK, axis=1)
    xi_v = jnp.repeat(xi.T, RET_DV, axis=1)
    row_h = jnp.arange(RET_QK) // RET_DK
    col_h = jnp.arange(RET_W) // RET_DV
    same = row_h[:, None] == col_h[None, :]
    state_decay = jnp.where(same, jnp.exp(BLK * log_gamma)[row_h][:, None], 0.0)
    return inner, zeta_k, xi_v, state_decay, same.astype(F32)


def _in_kernel(x_ref, w_ref, cr_ref, sr_ref, cd_ref, sd_ref, cq_ref, sq_ref, ck_ref,
               mu_ref, wa_up_ref, w0a0_ref, g_up_ref, kk_ka_ref, v_up_ref, v0_ref, vfirst_ref,
               qn_ref, wq_ref, kvn_ref, wkv_ref,
               ret_q, ret_k, ret_v, ret_g, dil_q, dil_k, dil_v,
               rw_r, rw_lw, rw_k, rw_v, rw_a, rw_b, rw_g, mla_q, mla_k, mla_v,
               carry_ref, *, has_vres):
    tm = x_ref.shape[1]

    @pl.when(pl.program_id(1) == 0)
    def _():
        carry_ref[...] = jnp.zeros_like(carry_ref)

    xb = x_ref[0].astype(BF16)
    o1, o2, o3 = IN_RET_W, IN_RET_W + IN_DIL_W, IN_RET_W + IN_DIL_W + IN_RWKV_W
    pa = jnp.dot(xb, w_ref[:, :o1], preferred_element_type=F32)
    pb = jnp.dot(xb, w_ref[:, o1:o2], preferred_element_type=F32)
    pc = jnp.dot(xb, w_ref[:, o2:o3], preferred_element_type=F32)
    pd = jnp.dot(xb, w_ref[:, o3:], preferred_element_type=F32)

    qk = RET_QK
    cr, sr = cr_ref[...], sr_ref[...]
    ret_q[0] = (pa[:, 0:qk] * cr + pa[:, qk:2 * qk] * sr).astype(ret_q.dtype)
    ret_k[0] = ((pa[:, 2 * qk:3 * qk] * cr + pa[:, 3 * qk:4 * qk] * sr) * (RET_DK ** -0.5)).astype(ret_k.dtype)
    ret_v[0] = pa[:, 4 * qk:4 * qk + RET_W].astype(ret_v.dtype)
    ret_g[0] = pa[:, 4 * qk + RET_W:]

    cd, sd = cd_ref[...], sd_ref[...]
    dil_q[0] = pb[:, 0:DIL_W] * cd + pb[:, DIL_W:2 * DIL_W] * sd
    dil_k[0] = pb[:, 2 * DIL_W:3 * DIL_W] * cd + pb[:, 3 * DIL_W:4 * DIL_W] * sd
    dil_v[0] = pb[:, 4 * DIL_W:]

    row = lax.broadcasted_iota(jnp.int32, pc.shape, 0)
    prev = jnp.where(row == 0, carry_ref[...], pltpu.roll(pc, 1, 0))
    carry_ref[...] = pc[tm - 1:tm, :]
    ps = pc + (prev - pc) * mu_ref[...]
    W = RWKV_W
    r, k, v = ps[:, 0:W], ps[:, W:2 * W], ps[:, 2 * W:3 * W]
    wd_ad = ps[:, 3 * W:3 * W + LANES]
    lane = lax.broadcasted_iota(jnp.int32, wd_ad.shape, 1)
    lora_in = jnp.where(lane < DECAY_LORA, jnp.tanh(wd_ad), wd_ad)
    lora = jnp.dot(lora_in.astype(BF16), wa_up_ref[...], preferred_element_type=F32) + w0a0_ref[...]
    w_raw = -jax.nn.softplus(-lora[:, :W]) - 0.5
    a_sig = jax.nn.sigmoid(lora[:, W:])
    gd = ps[:, 3 * W + LANES:3 * W + 2 * LANES]
    rw_g[0] = jnp.dot(jax.nn.sigmoid(gd).astype(BF16), g_up_ref[...], preferred_element_type=F32)
    kk = k * kk_ka_ref[0:1, :]
    kk = kk / jnp.maximum(jnp.sqrt(_head_sum(kk * kk, RWKV_DH)), 1e-12)
    if has_vres:
        vd = ps[:, 3 * W + 2 * LANES:]
        mix = jax.nn.sigmoid(jnp.dot(vd.astype(BF16), v_up_ref[...], preferred_element_type=F32) + v0_ref[...])
        v = v + (vfirst_ref[0] - v) * mix
    rw_r[0] = r
    rw_lw[0] = -jnp.exp(w_raw)
    rw_k[0] = k * (1.0 + (a_sig - 1.0) * kk_ka_ref[1:2, :])
    rw_v[0] = v
    rw_a[0] = -kk
    rw_b[0] = kk * a_sig

    c_q = pd[:, :Q_LORA]
    c_q = c_q * lax.rsqrt(jnp.mean(c_q * c_q, -1, keepdims=True) + NORM_EPS) * qn_ref[...]
    q2 = jnp.dot(c_q.astype(BF16), wq_ref[...], preferred_element_type=F32)
    scale = (MLA_NOPE + MLA_ROPE) ** -0.5
    mla_q[0] = ((q2[:, :MLA_QK_W] * cq_ref[...] + q2[:, MLA_QK_W:] * sq_ref[...]) * scale).astype(mla_q.dtype)
    c_kv = pd[:, Q_LORA:Q_LORA + KV_LORA]
    c_kv = c_kv * lax.rsqrt(jnp.mean(c_kv * c_kv, -1, keepdims=True) + NORM_EPS) * kvn_ref[...]
    kv = jnp.dot(c_kv.astype(BF16), wkv_ref[...], preferred_element_type=F32)
    kr = pd[:, Q_LORA + KV_LORA:] * ck_ref[...]
    i = lax.broadcasted_iota(jnp.int32, (LANES, MLA_QK_W), 0)
    j = lax.broadcasted_iota(jnp.int32, (LANES, MLA_QK_W), 1) % MLA_HEAD_PAD
    place = ((i < 2 * MLA_ROPE) & (j == MLA_NOPE + i % MLA_ROPE)).astype(BF16)
    k_pe = sum(jnp.dot(p, place, preferred_element_type=F32) for p in _split3(kr)[:2])
    mla_k[0] = (kv[:, :MLA_QK_W] + k_pe).astype(mla_k.dtype)
    mla_v[0] = kv[:, MLA_QK_W:].astype(mla_v.dtype)


def fused_in_proj(x, w_cat, tables, rw_params, mla_params, v_first, tile=IN_TILE):
    B_, S_, D = x.shape
    tile = min(tile, S_)
    assert S_ % tile == 0
    has_vres = v_first is not None
    tok = lambda w: pl.BlockSpec((1, tile, w), lambda b, i: (b, i, 0))
    pos = lambda a: pl.BlockSpec((tile, a.shape[1]), lambda b, i: (i, 0))
    full = lambda a: pl.BlockSpec(a.shape, lambda b, i: (0,) * a.ndim)
    if has_vres:
        vf_spec = tok(RWKV_W)
    else:
        v_first = jnp.zeros((1, tile, RWKV_W), F32)
        vf_spec = pl.BlockSpec((1, tile, RWKV_W), lambda b, i: (0, 0, 0))
    f32o = lambda w: jax.ShapeDtypeStruct((B_, S_, w), F32)
    bfo = lambda w: jax.ShapeDtypeStruct((B_, S_, w), BF16)
    out_shape = ([bfo(RET_QK), bfo(RET_QK), bfo(RET_W), f32o(RET_W)] + [f32o(DIL_W)] * 3 + [f32o(RWKV_W)] * 7
                 + [bfo(MLA_QK_W), bfo(MLA_QK_W), bfo(MLA_W)])
    return pl.pallas_call(
        functools.partial(_in_kernel, has_vres=has_vres),
        grid=(B_, S_ // tile),
        in_specs=[tok(D), full(w_cat)] + [pos(t) for t in tables] + [full(p) for p in rw_params] + [vf_spec]
                 + [full(p) for p in mla_params],
        out_specs=[tok(s.shape[-1]) for s in out_shape],
        out_shape=out_shape,
        scratch_shapes=[pltpu.VMEM((1, IN_RWKV_W), F32)],
        compiler_params=pltpu.CompilerParams(dimension_semantics=("arbitrary", "arbitrary"), vmem_limit_bytes=VMEM_LIMIT),
        name="fused_in_proj",
    )(x, w_cat, *tables, *rw_params, v_first, *mla_params)


def _retention_kernel(q_ref, k_ref, v_ref, g_ref, ng_ref, inner_ref, zeta_ref, xi_ref, sdec_ref, smask_ref, o_ref, state_ref):
    @pl.when(pl.program_id(1) == 0)
    def _():
        state_ref[...] = jnp.zeros_like(state_ref)

    q, k, v = q_ref[0], k_ref[0], v_ref[0]
    q_head = lax.broadcasted_iota(jnp.int32, q.shape, 1) // RET_DK
    v_head = lax.broadcasted_iota(jnp.int32, v.shape, 1) // RET_DV
    zero_q = jnp.zeros_like(q)
    scores = [_bdot(jnp.where(q_head == h, q, zero_q), k, _NT) * inner_ref[h] for h in range(RET_HEADS)]
    o_heads = [_bdot(s, v, _NN) for s in scores]
    o = _bdot(q, state_ref[...], _NN) * xi_ref[...]
    for h in range(RET_HEADS):
        o = o + jnp.where(v_head == h, o_heads[h], 0.0)
    state_ref[...] = state_ref[...] * sdec_ref[...] + _bdot(k.astype(F32) * zeta_ref[...], v, _TN) * smask_ref[...]
    o = o * lax.rsqrt(_head_sum(o * o, RET_DV) * (1.0 / RET_DV) + NORM_EPS) * ng_ref[...]
    g = g_ref[0]
    o_ref[0] = g * jax.nn.sigmoid(g) * o


def retention_fused(q, k, v, g, norm_g, tables):
    B_, S_, _ = q.shape
    assert S_ % BLK == 0
    tok = lambda w: pl.BlockSpec((1, BLK, w), lambda b, c: (b, c, 0))
    full = lambda a: pl.BlockSpec(a.shape, lambda b, c: (0,) * a.ndim)
    ng = norm_g.reshape(1, RET_W)
    return pl.pallas_call(
        _retention_kernel,
        grid=(B_, S_ // BLK),
        in_specs=[tok(RET_QK), tok(RET_QK), tok(RET_W), tok(RET_W), full(ng)] + [full(t) for t in tables],
        out_specs=tok(RET_W),
        out_shape=jax.ShapeDtypeStruct((B_, S_, RET_W), F32),
        scratch_shapes=[pltpu.VMEM((RET_QK, RET_W), F32)],
        compiler_params=pltpu.CompilerParams(dimension_semantics=("arbitrary", "arbitrary")),
        name="retention_chunk",
    )(q, k, v, g, ng, *tables)


def _dilated_kernel(q_ref, k_ref, v_ref, o_ref, m_ref, l_ref, *, dils):
    S_ = q_ref.shape[1]
    lane_head = lax.broadcasted_iota(jnp.int32, (BLK, DIL_PAIR_W), 1) // DIL_DH
    i_idx = lax.broadcasted_iota(jnp.int32, (BLK, 2 * BLK), 0)
    j_idx = lax.broadcasted_iota(jnp.int32, (BLK, 2 * BLK), 1)
    scale = DIL_DH ** -0.5

    def rows(start, n, d):
        return pl.ds(start, n) if d == 1 else pl.ds(start, n, stride=d)

    n_un = DIL_UNROLL
    for pi, d in enumerate(dils):
        def tiles(tt, carry, d=d, first=(pi == 0)):
            nb = [(tt * n_un + u) // d for u in range(n_un)]
            r = [(tt * n_un + u) % d for u in range(n_un)]
            kb = [jnp.maximum(n - 1, 0) for n in nb]
            q_rows = [rows(nb[u] * (BLK * d) + r[u], BLK, d) for u in range(n_un)]
            k_rows = [rows(kb[u] * (BLK * d) + r[u], 2 * BLK, d) for u in range(n_un)]
            q = [q_ref[0, q_rows[u], :] for u in range(n_un)]
            kk = [k_ref[0, k_rows[u], :].astype(BF16) for u in range(n_un)]
            vv = [v_ref[0, k_rows[u], :].astype(BF16) for u in range(n_un)]
            if not first:
                m_old = [m_ref[q_rows[u], :] for u in range(n_un)]
                l_old = [l_ref[q_rows[u], :] for u in range(n_un)]
                o_old = [o_ref[0, q_rows[u], :] for u in range(n_un)]
            delta = [(nb[u] - kb[u]) * BLK + i_idx - j_idx for u in range(n_un)]
            valid = [(dl >= 0) & (dl <= BLK) for dl in delta]
            pairs = [(u, h) for u in range(n_un) for h in range(2)]
            s = {(u, h): jnp.where(valid[u], _bdot(jnp.where(lane_head == h, q[u], jnp.zeros_like(q[u])), kk[u], _NT) * scale,
                                   -jnp.inf) for u, h in pairs}
            m = {p: jnp.max(s[p], axis=-1, keepdims=True) for p in pairs}
            pr = {p: jnp.exp(s[p] - m[p]) for p in pairs}
            l = {p: jnp.sum(pr[p], axis=-1, keepdims=True) for p in pairs}
            pv = {(u, h): jnp.dot(pr[u, h].astype(BF16), vv[u], preferred_element_type=F32) for u, h in pairs}
            for u in range(n_un):
                m_t = jnp.where(lane_head == 0, m[u, 0], m[u, 1])
                l_t = jnp.where(lane_head == 0, l[u, 0], l[u, 1])
                pv_t = jnp.where(lane_head == 0, pv[u, 0], pv[u, 1])
                if first:
                    m_ref[q_rows[u], :] = m_t
                    l_ref[q_rows[u], :] = l_t
                    o_ref[0, q_rows[u], :] = pv_t
                else:
                    m_new = jnp.maximum(m_old[u], m_t)
                    c_old = jnp.exp(m_old[u] - m_new)
                    c_t = jnp.exp(m_t - m_new)
                    m_ref[q_rows[u], :] = m_new
                    l_ref[q_rows[u], :] = l_old[u] * c_old + l_t * c_t
                    o_ref[0, q_rows[u], :] = o_old[u] * c_old + pv_t * c_t
            return carry

        lax.fori_loop(0, S_ // (BLK * n_un), tiles, 0)

    def finish(t, carry):
        r0 = pl.multiple_of(t * BLK, BLK)
        o_ref[0, pl.ds(r0, BLK), :] = o_ref[0, pl.ds(r0, BLK), :] / l_ref[pl.ds(r0, BLK), :]
        return carry

    lax.fori_loop(0, S_ // BLK, finish, 0)


def dilated_fused(q, k, v):
    B_, S_, W_ = q.shape
    dils = tuple(d for _, d in DIL_PATTERNS)
    assert all(w // d == BLK for w, d in DIL_PATTERNS) and S_ % (2 * BLK * max(dils)) == 0 and S_ % (BLK * DIL_UNROLL) == 0
    spec = pl.BlockSpec((1, S_, DIL_PAIR_W), lambda b, hp: (b, 0, hp))
    return pl.pallas_call(
        functools.partial(_dilated_kernel, dils=dils),
        grid=(B_, W_ // DIL_PAIR_W),
        in_specs=[spec] * 3,
        out_specs=spec,
        out_shape=jax.ShapeDtypeStruct((B_, S_, W_), F32),
        scratch_shapes=[pltpu.VMEM((S_, DIL_PAIR_W), F32)] * 2,
        compiler_params=pltpu.CompilerParams(dimension_semantics=("arbitrary", "arbitrary"), vmem_limit_bytes=DIL_VMEM_LIMIT),
        name="dilated_attention",
    )(q, k, v)


def _rwkv7_mixer_kernel(r_ref, lw_ref, k_ref, v_ref, a_ref, b_ref, g_ref, gn_ref, o_ref, state_ref, *, heads, dh, chunk):
    rows = r_ref.shape[1]
    n_sub = rows // chunk
    n_double = int(np.log2(chunk))
    assert 2 ** n_double == chunk and n_sub * chunk == rows

    @pl.when(pl.program_id(1) == 0)
    def _():
        state_ref[...] = jnp.zeros_like(state_ref)

    ri = lax.broadcasted_iota(jnp.int32, (rows, rows), 0)
    ci = lax.broadcasted_iota(jnp.int32, (rows, rows), 1)
    tri = ((ci <= ri) & (ci >= (ri // chunk) * chunk)).astype(BF16)
    logw = lw_ref[0]
    cum = sum(jnp.dot(tri, piece, preferred_element_type=F32) for piece in _split3(logw))

    row = lax.broadcasted_iota(jnp.int32, (chunk, 2 * chunk), 0)
    col = lax.broadcasted_iota(jnp.int32, (chunk, 2 * chunk), 1) % chunk
    strict = col < row
    incl = col <= row
    zeros_cv = jnp.zeros((chunk, dh), F32)

    pairs = [(s, h) for s in range(n_sub) for h in range(heads)]
    ah, rh, vh, bk_rem, w_total = {}, {}, {}, {}, {}
    l_abk, m_rbk = {}, {}
    for s in range(n_sub):
        rs = slice(s * chunk, (s + 1) * chunk)
        cum_s, logw_s = cum[rs], logw[rs]
        total = cum_s[chunk - 1:chunk, :]
        e_in = jnp.exp(cum_s)
        e_neg = jnp.exp(-cum_s)
        e_rem = jnp.exp(total - cum_s)
        w_tot = jnp.exp(total)
        r_hat = r_ref[0, rs, :] * e_in
        a_hat = a_ref[0, rs, :] * jnp.exp(cum_s - logw_s)
        b_all, k_all, v_all = b_ref[0, rs, :], k_ref[0, rs, :], v_ref[0, rs, :]
        b_til, k_til = b_all * e_neg, k_all * e_neg
        b_rem, k_rem = b_all * e_rem, k_all * e_rem
        for h in range(heads):
            sl = slice(h * dh, (h + 1) * dh)
            ah[s, h], rh[s, h], vh[s, h] = a_hat[:, sl], r_hat[:, sl], v_all[:, sl]
            w_total[s, h] = w_tot[:, sl]
            bk_rem[s, h] = jnp.concatenate([b_rem[:, sl], k_rem[:, sl]], axis=0)
            bk_til = jnp.concatenate([b_til[:, sl], k_til[:, sl]], axis=0)
            l_abk[s, h] = jnp.where(strict, _bdot(ah[s, h], bk_til, _NT), 0.0)
            m_rbk[s, h] = jnp.where(incl, _bdot(rh[s, h], bk_til, _NT), 0.0)
    z = {p: jnp.concatenate([ah[p], _bdot(l_abk[p], jnp.concatenate([zeros_cv, vh[p]], axis=0), _NN)], axis=1)
         for p in pairs}
    lp = {p: l_abk[p][:, :chunk] for p in pairs}
    for m in range(n_double):
        z = {p: z[p] + _bdot(lp[p], z[p], _NN) for p in pairs}
        if m + 1 < n_double:
            lp = {p: _bdot(lp[p], lp[p], _NN) for p in pairs}
    state = [state_ref[h] for h in range(heads)]
    y_parts = []
    for s in range(n_sub):
        pr_s = [_bdot(jnp.concatenate([z[s, h][:, :dh], rh[s, h]], axis=0), state[h], _NT) for h in range(heads)]
        uv = [jnp.concatenate([pr_s[h][:chunk] + z[s, h][:, dh:], vh[s, h]], axis=0) for h in range(heads)]
        ys = [pr_s[h][chunk:] + _bdot(m_rbk[s, h], uv[h], _NN) for h in range(heads)]
        state = [state[h] * w_total[s, h] + _bdot(uv[h], bk_rem[s, h], _TN) for h in range(heads)]
        y_parts.append(jnp.concatenate(ys, axis=1))
    for h in range(heads):
        state_ref[h] = state[h]
    y = jnp.concatenate(y_parts, axis=0)
    mean = _head_sum(y, dh) * (1.0 / dh)
    yc = y - mean
    var = _head_sum(yc * yc, dh) * (1.0 / dh)
    yn = yc * lax.rsqrt(var + RWKV_GN_EPS) * gn_ref[0:1, :] + gn_ref[1:2, :]
    bonus = _head_sum(r_ref[0] * k_ref[0] * gn_ref[2:3, :], dh) * v_ref[0]
    o_ref[0] = (yn + bonus) * g_ref[0]


def rwkv7_mixer_fused(r, logw, k, v, a, b, g, gn, heads=RWKV_HEADS, dh=RWKV_DH, chunk=RWKV_CHUNK, block=RWKV_BLOCK):
    B_, S_, W_ = r.shape
    block = min(block, S_)
    assert W_ == heads * dh and S_ % block == 0 and block % chunk == 0
    spec = pl.BlockSpec((1, block, W_), lambda bi, ci: (bi, ci, 0))
    return pl.pallas_call(
        functools.partial(_rwkv7_mixer_kernel, heads=heads, dh=dh, chunk=chunk),
        grid=(B_, S_ // block),
        in_specs=[spec] * 7 + [pl.BlockSpec(gn.shape, lambda bi, ci: (0, 0))],
        out_specs=spec,
        out_shape=jax.ShapeDtypeStruct((B_, S_, W_), F32),
        scratch_shapes=[pltpu.VMEM((heads, dh, dh), F32)],
        compiler_params=pltpu.CompilerParams(dimension_semantics=("arbitrary", "arbitrary")),
        name="rwkv7_mixer",
    )(r, logw, k, v, a, b, g, gn)


def _mla_flash_kernel(q_ref, k_ref, v_ref, o_ref, *, tile):
    h = pl.program_id(1)
    qi = pl.program_id(2)
    q = q_ref[0]
    dv = v_ref.shape[2]

    def kv_tile(j):
        start = pl.multiple_of(j * tile, tile)
        return k_ref[0, pl.ds(start, tile), :], v_ref[0, pl.ds(start, tile), :]

    def update(carry, s, v_t):
        m, l, acc = carry
        m_new = jnp.maximum(m, jnp.max(s, axis=-1, keepdims=True))
        corr = jnp.exp(m - m_new)
        p = jnp.exp(s - m_new)
        l_new = corr * l + jnp.sum(p, axis=-1, keepdims=True)
        acc_new = corr * acc + jnp.dot(p.astype(v_t.dtype), v_t, preferred_element_type=F32)
        return m_new, l_new, acc_new

    def body(j, carry):
        k_t, v_t = kv_tile(j)
        s = lax.dot_general(q, k_t, (_NT, ((), ())), preferred_element_type=F32)
        return update(carry, s, v_t)

    init = (jnp.full((tile, 1), -jnp.inf, F32), jnp.zeros((tile, 1), F32), jnp.zeros((tile, dv), F32))
    carry = lax.fori_loop(0, qi, body, init)
    k_t, v_t = kv_tile(qi)
    s = lax.dot_general(q, k_t, (_NT, ((), ())), preferred_element_type=F32)
    row = lax.broadcasted_iota(jnp.int32, (tile, tile), 0)
    col = lax.broadcasted_iota(jnp.int32, (tile, tile), 1)
    s = jnp.where(col <= row, s, -jnp.inf)
    m, l, acc = update(carry, s, v_t)
    lane_head = lax.broadcasted_iota(jnp.int32, (tile, dv), 1) // MLA_DV
    o_ref[0] = jnp.where(lane_head == h % 2, acc / l, 0.0)


def mla_attention(q, k, v, tile=ATTN_TILE):
    B_, S_, _ = q.shape
    tile = min(tile, S_)
    assert S_ % tile == 0 and MLA_HEAD_PAD == 2 * MLA_DV
    return pl.pallas_call(
        functools.partial(_mla_flash_kernel, tile=tile),
        grid=(B_, MLA_HEADS, S_ // tile),
        in_specs=[pl.BlockSpec((1, tile, MLA_HEAD_PAD), lambda b, h, i: (b, i, h)),
                  pl.BlockSpec((1, S_, MLA_HEAD_PAD), lambda b, h, i: (b, 0, h)),
                  pl.BlockSpec((1, S_, 2 * MLA_DV), lambda b, h, i: (b, 0, h // 2))],
        out_specs=pl.BlockSpec((1, tile, MLA_HEAD_PAD), lambda b, h, i: (b, i, h)),
        out_shape=jax.ShapeDtypeStruct((B_, S_, MLA_QK_W), F32),
        compiler_params=pltpu.CompilerParams(dimension_semantics=("arbitrary", "arbitrary", "arbitrary")),
        name="mla_causal_flash",
    )(q, k, v)


def _out_kernel(x_ref, oa_ref, ob_ref, oc_ref, od_ref, wabc_ref, wd_ref, gb_ref, gd_ref, ln_ref, wr_ref, br_ref,
                x1_ref, x1b_ref, logit_ref, *, alpha):
    ob = ob_ref[...]
    ob = ob * lax.rsqrt(jnp.mean(ob * ob, -1, keepdims=True) + NORM_EPS) * gb_ref[...]
    od = od_ref[...]
    od = od * lax.rsqrt(jnp.sum(od * od, -1, keepdims=True) * (1.0 / MLA_W) + NORM_EPS) * gd_ref[...]
    W = RET_W
    mix = jnp.dot(oa_ref[...].astype(BF16), wabc_ref[0:W, :], preferred_element_type=F32)
    mix += jnp.dot(ob.astype(BF16), wabc_ref[W:2 * W, :], preferred_element_type=F32)
    mix += jnp.dot(oc_ref[...].astype(BF16), wabc_ref[2 * W:3 * W, :], preferred_element_type=F32)
    mix += jnp.dot(od.astype(BF16), wd_ref[...], preferred_element_type=F32)
    x1 = _layer_norm_rows(alpha * x_ref[...] + mix, ln_ref)
    x1_ref[...] = x1
    x1b = x1.astype(BF16)
    x1b_ref[...] = x1b
    logit_ref[...] = jnp.dot(x1b, wr_ref[...], preferred_element_type=F32) + br_ref[...]


def fused_out_proj(x, o_a, o_b, o_c, o_d, w_abc, w_d, g_b, g_d, ln_gb, w_router, b_router, alpha, tile=OUT_TILE):
    N, D = x.shape
    tile = min(tile, N)
    assert N % tile == 0
    n_exp = w_router.shape[1]
    wr = jnp.zeros((D, ROUTER_PAD), F32).at[:, :n_exp].set(w_router).astype(BF16)
    br = jnp.zeros((1, ROUTER_PAD), F32).at[0, :n_exp].set(b_router)
    consts = [w_abc, w_d, g_b, g_d, ln_gb, wr, br]
    tok = lambda w: pl.BlockSpec((tile, w), lambda i: (i, 0))
    full = lambda a: pl.BlockSpec(a.shape, lambda i: (0,) * a.ndim)
    x1, x1b, logits = pl.pallas_call(
        functools.partial(_out_kernel, alpha=alpha),
        grid=(N // tile,),
        in_specs=[tok(D), tok(RET_W), tok(DIL_W), tok(RWKV_W), tok(MLA_QK_W)] + [full(c) for c in consts],
        out_specs=[tok(D), tok(D), tok(ROUTER_PAD)],
        out_shape=[jax.ShapeDtypeStruct((N, D), F32), jax.ShapeDtypeStruct((N, D), BF16),
                   jax.ShapeDtypeStruct((N, ROUTER_PAD), F32)],
        compiler_params=pltpu.CompilerParams(dimension_semantics=("arbitrary",), vmem_limit_bytes=VMEM_LIMIT),
        name="fused_out_proj",
    )(x, o_a, o_b, o_c, o_d, *consts)
    return x1, x1b, logits[:, :n_exp]


def _moe_expert_kernel(tile_e_ref, n_used_ref, x_ref, wg_ref, wl_ref, bg_ref, bl_ref, wd_ref, bd_ref, o_ref):
    del tile_e_ref
    i = pl.program_id(0)

    @pl.when(i < n_used_ref[0])
    def _():
        x = x_ref[...]
        glu = jnp.dot(x, wg_ref[0], preferred_element_type=F32) + bg_ref[0]
        lin = jnp.dot(x, wl_ref[0], preferred_element_type=F32) + bl_ref[0]
        glu = jnp.minimum(glu, SWIGLU_LIMIT)
        lin = jnp.clip(lin, -SWIGLU_LIMIT, SWIGLU_LIMIT)
        act = glu * jax.nn.sigmoid(SWIGLU_ALPHA * glu) * (lin + 1.0)
        y = jnp.dot(act.astype(wd_ref.dtype), wd_ref[0], preferred_element_type=F32) + bd_ref[0]
        o_ref[...] = y.astype(o_ref.dtype)

    @pl.when(i >= n_used_ref[0])
    def _():
        o_ref[...] = jnp.zeros_like(o_ref)


def _deinterleave_kernel(w_ref, even_ref, odd_ref):
    g = 2 * LANES
    i = lax.broadcasted_iota(jnp.int32, (g, g), 0)
    j = lax.broadcasted_iota(jnp.int32, (g, g), 1)
    perm = (i == jnp.where(j < LANES, 2 * j, 2 * (j - LANES) + 1)).astype(BF16)
    for c in range(w_ref.shape[2] // g):
        blk = w_ref[0, :, c * g:(c + 1) * g].astype(BF16)
        sorted_cols = jnp.dot(blk, perm, preferred_element_type=F32)
        even_ref[0, :, c * LANES:(c + 1) * LANES] = sorted_cols[:, :LANES].astype(even_ref.dtype)
        odd_ref[0, :, c * LANES:(c + 1) * LANES] = sorted_cols[:, LANES:].astype(odd_ref.dtype)


def deinterleave_to_bf16(w):
    E, D, F2 = w.shape
    rows = min(DEINT_ROWS, D)
    assert D % rows == 0 and F2 % (2 * LANES) == 0
    out = jax.ShapeDtypeStruct((E, D, F2 // 2), BF16)
    return pl.pallas_call(
        _deinterleave_kernel,
        grid=(E, D // rows),
        in_specs=[pl.BlockSpec((1, rows, F2), lambda e, r: (e, r, 0))],
        out_specs=[pl.BlockSpec((1, rows, F2 // 2), lambda e, r: (e, r, 0))] * 2,
        out_shape=[out, out],
        compiler_params=pltpu.CompilerParams(dimension_semantics=("arbitrary", "arbitrary"), vmem_limit_bytes=VMEM_LIMIT),
        name="deinterleave_cast",
    )(w)


def moe_experts(xs, tile_e, n_used, w_glu, w_lin, b_glu, b_lin, w_dn, b_dn, tile):
    n_rows, D = xs.shape
    E, _, F = w_glu.shape
    w_spec = lambda shape: pl.BlockSpec((1,) + shape, lambda i, te, nu: (te[i], 0, 0))
    return pl.pallas_call(
        _moe_expert_kernel,
        grid_spec=pltpu.PrefetchScalarGridSpec(
            num_scalar_prefetch=2,
            grid=(n_rows // tile,),
            in_specs=[pl.BlockSpec((tile, D), lambda i, te, nu: (i, 0)),
                      w_spec((D, F)), w_spec((D, F)), w_spec((1, F)), w_spec((1, F)),
                      w_spec((F, D)), w_spec((1, D))],
            out_specs=pl.BlockSpec((tile, D), lambda i, te, nu: (i, 0)),
        ),
        out_shape=jax.ShapeDtypeStruct((n_rows, D), BF16),
        compiler_params=pltpu.CompilerParams(dimension_semantics=("arbitrary",), vmem_limit_bytes=VMEM_LIMIT),
        name="moe_expert_ffn",
    )(tile_e, n_used, xs, w_glu, w_lin, b_glu.reshape(E, 1, F), b_lin.reshape(E, 1, F), w_dn, b_dn.reshape(E, 1, D))


def _combine_kernel(x_ref, y0_ref, y1_ref, y2_ref, y3_ref, gate_ref, ln_ref, o_ref, *, alpha):
    gate = gate_ref[...]
    y = sum(y_ref[...].astype(F32) * gate[:, s:s + 1] for s, y_ref in enumerate((y0_ref, y1_ref, y2_ref, y3_ref)))
    o_ref[...] = _layer_norm_rows(alpha * x_ref[...] + y, ln_ref)


def moe_combine_ln(x1, ys, gate, ln_gb, alpha, tile=OUT_TILE):
    N, D = x1.shape
    tile = min(tile, N)
    assert N % tile == 0 and len(ys) == TOP_K
    tok = lambda w: pl.BlockSpec((tile, w), lambda i: (i, 0))
    return pl.pallas_call(
        functools.partial(_combine_kernel, alpha=alpha),
        grid=(N // tile,),
        in_specs=[tok(D)] * (1 + TOP_K) + [tok(TOP_K), pl.BlockSpec(ln_gb.shape, lambda i: (0, 0))],
        out_specs=tok(D),
        out_shape=jax.ShapeDtypeStruct((N, D), F32),
        compiler_params=pltpu.CompilerParams(dimension_semantics=("arbitrary",), vmem_limit_bytes=VMEM_LIMIT),
        name="moe_combine_ln",
    )(x1, *ys, gate, ln_gb)


def moe_layer(x1, x1b, logits, expert_params, expert_base, ln_gb, alpha, tile=MOE_TILE):
    n_tok, D = x1.shape
    n_exp = logits.shape[1]
    top_val, top_idx = lax.top_k(logits, TOP_K)
    gate = jax.nn.softmax(top_val, axis=-1)
    n_assign = n_tok * TOP_K
    e_flat = top_idx.reshape(-1).astype(jnp.int32)
    experts = jnp.arange(n_exp, dtype=jnp.int32)
    e_sorted, order = lax.sort((e_flat, jnp.arange(n_assign, dtype=jnp.int32)), num_keys=1)
    counts = jnp.sum((e_flat[:, None] == experts[None, :]).astype(jnp.int32), axis=0)
    padded = (counts + tile - 1) // tile * tile
    start = jnp.cumsum(counts) - counts
    pend = jnp.cumsum(padded)
    pstart = pend - padded
    n_tiles = (n_assign + n_exp * (tile - 1) + tile - 1) // tile
    tile_first = jnp.arange(n_tiles, dtype=jnp.int32) * tile
    tile_e = jnp.minimum(jnp.sum((pend[None, :] <= tile_first[:, None]).astype(jnp.int32), axis=1), n_exp - 1)
    n_used = (pend[-1] // tile).astype(jnp.int32).reshape(1)
    within = (tile_first - pstart[tile_e])[:, None] + jnp.arange(tile, dtype=jnp.int32)[None, :]
    valid = within < counts[tile_e][:, None]
    sorted_idx = jnp.clip(start[tile_e][:, None] + within, 0, n_assign - 1)
    row_tok = jnp.where(valid, order[sorted_idx] // TOP_K, 0).reshape(-1)
    shift = jnp.sum(jnp.where(e_sorted[:, None] == experts[None, :], (pstart - start)[None, :], 0), axis=1)
    dest = jnp.arange(n_assign, dtype=jnp.int32) + shift
    _, pos = lax.sort((order, dest), num_keys=1)
    pos = pos.reshape(n_tok, TOP_K)
    n_chunks = MOE_GATHER_CHUNKS if n_tiles % MOE_GATHER_CHUNKS == 0 else 1
    tpc = n_tiles // n_chunks
    row_tok = row_tok.reshape(n_chunks, tpc * tile)
    tile_e = (tile_e + expert_base).reshape(n_chunks, tpc)
    yb = jnp.concatenate([moe_experts(x1b[row_tok[c]], tile_e[c], jnp.clip(n_used - c * tpc, 0, tpc), *expert_params, tile)
                          for c in range(n_chunks)], axis=0)
    return moe_combine_ln(x1, [yb[pos[:, s]] for s in range(TOP_K)], gate, ln_gb, alpha)


def prep_expert_params(exp_w_gu, exp_b_gu, exp_w_dn, exp_b_dn):
    L, E, D, F2 = exp_w_gu.shape
    w_glu, w_lin = deinterleave_to_bf16(exp_w_gu.reshape(L * E, D, F2))
    b_gu = exp_b_gu.reshape(L * E, F2)
    return (w_glu, w_lin, b_gu[:, 0::2], b_gu[:, 1::2], exp_w_dn.astype(BF16).reshape(L * E, F2 // 2, D),
            exp_b_dn.reshape(L * E, D))


def kernel(x, w_in, w_out, ret_norm_g, dil_norm_g, rwkv_mu, rwkv_w0, rwkv_w_up, rwkv_a0, rwkv_a_up, rwkv_g_up, rwkv_k_k, rwkv_k_a, rwkv_r_k, rwkv_ln_g, rwkv_ln_b, rwkv_vres_down, rwkv_vres_mu, rwkv_v0, rwkv_v_up, mla_q_norm_g, mla_w_q_up, mla_kv_norm_g, mla_w_kv_up, mla_out_norm_g, ln1_g, ln1_b, router_w, router_b, exp_w_gu, exp_b_gu, exp_w_dn, exp_b_dn, ln2_g, ln2_b):
    depth = w_in.shape[0]
    B_, S_, D = x.shape
    alpha = (2 * depth) ** 0.25
    tables = rope_tables(S_)
    ret_tabs = retention_tables()
    expert_params = prep_expert_params(exp_w_gu, exp_b_gu, exp_w_dn, exp_b_dn)
    n_exp = exp_w_gu.shape[1]
    v_first = None
    n_tok = B_ * S_
    flat = lambda t: t.reshape(n_tok, t.shape[-1])
    for l in range(depth):
        w_cat = prep_in_weights(w_in[l], rwkv_vres_down[l - 1] if l > 0 else None)
        rw_params = prep_rwkv_params(rwkv_mu[l], rwkv_w0[l], rwkv_w_up[l], rwkv_a0[l], rwkv_a_up[l], rwkv_g_up[l],
                                     rwkv_k_k[l], rwkv_k_a[l], rwkv_vres_mu[l - 1] if l > 0 else None,
                                     rwkv_v0[l - 1] if l > 0 else None, rwkv_v_up[l - 1] if l > 0 else None)
        mla_params = prep_mla_params(mla_q_norm_g[l], mla_w_q_up[l], mla_kv_norm_g[l], mla_w_kv_up[l])
        gn = jnp.stack([rwkv_ln_g[l], rwkv_ln_b[l], rwkv_r_k[l].reshape(-1)])
        w_abc, w_d, g_d = prep_out_weights(w_out[l], mla_out_norm_g[l])
        (ret_q, ret_k, ret_v, ret_g, dil_q, dil_k, dil_v, rw_r, rw_lw, rw_k, rw_v, rw_a, rw_b, rw_g,
         mla_q, mla_k, mla_v) = fused_in_proj(x, w_cat, tables, rw_params, mla_params, v_first)
        if l == 0:
            v_first = rw_v
        o_a = retention_fused(ret_q, ret_k, ret_v, ret_g, ret_norm_g[l], ret_tabs)
        o_b = dilated_fused(dil_q, dil_k, dil_v)
        o_c = rwkv7_mixer_fused(rw_r, rw_lw, rw_k, rw_v, rw_a, rw_b, rw_g, gn)
        o_d = mla_attention(mla_q, mla_k, mla_v)
        x1, x1b, logits = fused_out_proj(flat(x), flat(o_a), flat(o_b), flat(o_c), flat(o_d), w_abc, w_d,
                                         dil_norm_g[l].reshape(1, DIL_W), g_d, jnp.stack([ln1_g[l], ln1_b[l]]),
                                         router_w[l], router_b[l], alpha)
        x = moe_layer(x1, x1b, logits, expert_params, l * n_exp, jnp.stack([ln2_g[l], ln2_b[l]]),
                      alpha).reshape(B_, S_, D)
    return x
```

```python
import functools

import numpy as np
import jax
import jax.numpy as jnp
from jax import lax
from jax.experimental import pallas as pl
from jax.experimental.pallas import tpu as pltpu

F32 = jnp.float32
BF16 = jnp.bfloat16

LANES = 128
BLK = 128
LN_EPS = 1e-5
NORM_EPS = 1e-6

RET_HEADS, RET_DK, RET_DV = 4, 32, 64
RET_QK = RET_HEADS * RET_DK
RET_W = RET_HEADS * RET_DV
RET_THETA = 10000.0
DIL_HEADS, DIL_DH = 4, 64
DIL_W = DIL_HEADS * DIL_DH
DIL_PATTERNS = ((128, 1), (512, 4), (2048, 16))
ROPE_THETA = 500000.0
ROPE_ROT_DIM = DIL_DH // 4
RWKV_HEADS, RWKV_DH = 4, 64
RWKV_W = RWKV_HEADS * RWKV_DH
DECAY_LORA, AAA_LORA, MV_LORA, GATE_LORA = 64, 64, 32, 128
RWKV_GN_EPS = 64e-5
MLA_HEADS, MLA_NOPE, MLA_ROPE, MLA_DV = 4, 64, 32, 64
MLA_W = MLA_HEADS * MLA_DV
Q_LORA, KV_LORA = 256, 128
MLA_THETA = 10000.0
TOP_K = 4
SWIGLU_LIMIT, SWIGLU_ALPHA = 7.0, 1.702

RET_SPLITS = (RET_QK, RET_QK, RET_W, RET_W)
DIL_SPLITS = (DIL_W, DIL_W, DIL_W)
RWKV_SPLITS = (RWKV_W, RWKV_W, RWKV_W, DECAY_LORA, AAA_LORA, GATE_LORA)
MLA_SPLITS = (Q_LORA, KV_LORA, MLA_ROPE)
A_END = sum(RET_SPLITS)
B_END = A_END + sum(DIL_SPLITS)
C_END = B_END + sum(RWKV_SPLITS)
N_IN = C_END + sum(MLA_SPLITS)

RWKV_CHUNK = 64
RWKV_BLOCK = 256

MLA_HEAD_PAD = 128
MLA_QK_W = MLA_HEADS * MLA_HEAD_PAD
ATTN_TILE = 512

IN_RET_W = 4 * RET_QK + 2 * RET_W
IN_DIL_W = 5 * DIL_W
IN_RWKV_W = sum(RWKV_SPLITS) + LANES
IN_MLA_W = Q_LORA + KV_LORA + LANES
IN_COLS = IN_RET_W + IN_DIL_W + IN_RWKV_W + IN_MLA_W
IN_TILE = 256
OUT_TILE = 512
ROUTER_PAD = 128
DIL_PAIR_W = 2 * DIL_DH
DEINT_ROWS = 512
MOE_TILE = 512
DIL_UNROLL = 4
MOE_GATHER_CHUNKS = 4
VMEM_LIMIT = 48 * 1024 * 1024
DIL_VMEM_LIMIT = 56 * 1024 * 1024


def split_cols(p, sizes):
    idx = np.cumsum(sizes)[:-1].tolist()
    return jnp.split(p, idx, axis=-1)


def rope_table(n_pos, rot_dim, theta):
    inv_freq = 1.0 / (theta ** (jnp.arange(0, rot_dim, 2, dtype=F32) / rot_dim))
    ang = jnp.arange(n_pos, dtype=F32)[:, None] * inv_freq[None, :]
    return jnp.cos(ang), jnp.sin(ang)


def _bdot(a, b, dims):
    return lax.dot_general(a.astype(BF16), b.astype(BF16), ((dims[0], dims[1]), ((), ())), preferred_element_type=F32)


_NN = ((1,), (0,))
_NT = ((1,), (1,))
_TN = ((0,), (0,))


def _split3(x):
    h1 = x.astype(BF16)
    r1 = x - h1.astype(F32)
    h2 = r1.astype(BF16)
    h3 = (r1 - h2.astype(F32)).astype(BF16)
    return h1, h2, h3


def _head_sum(x, dh):
    w = x.shape[1]
    i = lax.broadcasted_iota(jnp.int32, (w, w), 0) // dh
    j = lax.broadcasted_iota(jnp.int32, (w, w), 1) // dh
    ones = (i == j).astype(BF16)
    return sum(jnp.dot(p, ones, preferred_element_type=F32) for p in _split3(x))


def _layer_norm_rows(h, ln_ref):
    mu = jnp.mean(h, -1, keepdims=True)
    hc = h - mu
    var = jnp.mean(hc * hc, -1, keepdims=True)
    return hc * lax.rsqrt(var + LN_EPS) * ln_ref[0:1, :] + ln_ref[1:2, :]


def _rot_half_cols(w, heads, dh, rot):
    d_in = w.shape[0]
    w = w.reshape(d_in, heads, dh)
    half = rot // 2
    sw = jnp.concatenate([-w[..., half:rot], w[..., :half], jnp.zeros((d_in, heads, dh - rot), w.dtype)], axis=-1)
    return sw.reshape(d_in, heads * dh)


def _rope_lanes(cs, heads, dh, rot, lead=0):
    cos, sin = cs
    S_ = cos.shape[0]
    c = jnp.concatenate([jnp.ones((S_, lead), F32), cos, cos, jnp.ones((S_, dh - lead - rot), F32)], axis=1)
    s = jnp.concatenate([jnp.zeros((S_, lead), F32), sin, sin, jnp.zeros((S_, dh - lead - rot), F32)], axis=1)
    return jnp.tile(c, (1, heads)), jnp.tile(s, (1, heads))


def rope_tables(S_):
    ret = _rope_lanes(rope_table(S_, RET_DK, RET_THETA), RET_HEADS, RET_DK, RET_DK)
    dil = _rope_lanes(rope_table(S_, ROPE_ROT_DIM, ROPE_THETA), DIL_HEADS, DIL_DH, ROPE_ROT_DIM)
    cos, sin = rope_table(S_, MLA_ROPE, MLA_THETA)
    mq = _rope_lanes((cos, sin), MLA_HEADS, MLA_HEAD_PAD, MLA_ROPE, lead=MLA_NOPE)
    mk = jnp.concatenate([cos, cos, sin, sin, jnp.zeros((S_, LANES - 2 * MLA_ROPE), F32)], axis=1)
    return ret + dil + mq + (mk,)


def prep_in_weights(w_in_l, vres_down_l):
    D = w_in_l.shape[0]
    a_q, a_k, a_v, a_g = split_cols(w_in_l[:, :A_END], RET_SPLITS)
    b_q, b_k, b_v = split_cols(w_in_l[:, A_END:B_END], DIL_SPLITS)
    d_cq, d_ckv, d_kr = split_cols(w_in_l[:, C_END:N_IN], MLA_SPLITS)
    vres = jnp.zeros((D, LANES), F32)
    if vres_down_l is not None:
        vres = vres.at[:, :MV_LORA].set(vres_down_l)
    cols = [a_q, _rot_half_cols(a_q, RET_HEADS, RET_DK, RET_DK), a_k, _rot_half_cols(a_k, RET_HEADS, RET_DK, RET_DK), a_v, a_g,
            b_q, _rot_half_cols(b_q, DIL_HEADS, DIL_DH, ROPE_ROT_DIM), b_k, _rot_half_cols(b_k, DIL_HEADS, DIL_DH, ROPE_ROT_DIM), b_v,
            w_in_l[:, B_END:C_END], vres,
            d_cq, d_ckv, d_kr, _rot_half_cols(d_kr, 1, MLA_ROPE, MLA_ROPE), jnp.zeros((D, LANES - 2 * MLA_ROPE), F32)]
    w = jnp.concatenate(cols, axis=1)
    assert w.shape[1] == IN_COLS
    return w.astype(BF16)


def prep_rwkv_params(mu, w0, w_up, a0, a_up, g_up, k_k, k_a, vres_mu, v0, v_up):
    W = RWKV_W
    mu_ext = jnp.zeros((1, IN_RWKV_W), F32).at[0, :mu.shape[0]].set(mu)
    v_up_ext = jnp.zeros((LANES, W), F32)
    v0_ext = jnp.zeros((1, W), F32)
    if vres_mu is not None:
        mu_ext = mu_ext.at[0, mu.shape[0]:mu.shape[0] + MV_LORA].set(vres_mu)
        v_up_ext = v_up_ext.at[:MV_LORA].set(v_up)
        v0_ext = v0.reshape(1, W)
    wa_up = jnp.zeros((LANES, 2 * W), F32).at[:DECAY_LORA, :W].set(w_up).at[DECAY_LORA:, W:].set(a_up)
    return [mu_ext, wa_up.astype(BF16), jnp.concatenate([w0, a0]).reshape(1, 2 * W), g_up.astype(BF16),
            jnp.stack([k_k, k_a]), v_up_ext.astype(BF16), v0_ext]


def prep_mla_params(q_norm_g, w_q_up, kv_norm_g, w_kv_up):
    ql = w_q_up.shape[0]
    wq = w_q_up.reshape(ql, MLA_HEADS, MLA_NOPE + MLA_ROPE)
    pad = jnp.zeros((ql, MLA_HEADS, MLA_HEAD_PAD - MLA_NOPE - MLA_ROPE), F32)
    q_main = jnp.concatenate([wq, pad], axis=-1).reshape(ql, MLA_QK_W)
    rope = wq[..., MLA_NOPE:]
    half = MLA_ROPE // 2
    q_rot = jnp.concatenate([jnp.zeros((ql, MLA_HEADS, MLA_NOPE), F32), -rope[..., half:], rope[..., :half], pad], axis=-1)
    w_q = jnp.concatenate([q_main, q_rot.reshape(ql, MLA_QK_W)], axis=1).astype(BF16)
    kl = w_kv_up.shape[0]
    wkv = w_kv_up.reshape(kl, MLA_HEADS, MLA_NOPE + MLA_DV)
    k_main = jnp.concatenate([wkv[..., :MLA_NOPE], jnp.zeros((kl, MLA_HEADS, MLA_HEAD_PAD - MLA_NOPE), F32)], axis=-1)
    w_kv = jnp.concatenate([k_main.reshape(kl, MLA_QK_W), wkv[..., MLA_NOPE:].reshape(kl, MLA_W)], axis=1).astype(BF16)
    return [q_norm_g.reshape(1, Q_LORA), w_q, kv_norm_g.reshape(1, KV_LORA), w_kv]


def prep_out_weights(w_out_l, mla_out_norm_g):
    W = RET_W
    w_abc = w_out_l[:3 * W].astype(BF16)
    w_d = w_out_l[3 * W:].reshape(MLA_HEADS, MLA_DV, -1)
    g_d = mla_out_norm_g.reshape(MLA_HEADS, MLA_DV)
    zw = jnp.zeros_like(w_d[0])
    zg = jnp.zeros_like(g_d[0])
    rows, gains = [], []
    for h in range(MLA_HEADS):
        rows += [w_d[h], zw] if h % 2 == 0 else [zw, w_d[h]]
        gains += [g_d[h], zg] if h % 2 == 0 else [zg, g_d[h]]
    return w_abc, jnp.concatenate(rows, axis=0).astype(BF16), jnp.concatenate(gains).reshape(1, MLA_QK_W)


def retention_tables():
    log_gamma = jnp.log(1.0 - 2.0 ** (-5.0 - jnp.arange(RET_HEADS, dtype=F32)))
    idx = jnp.arange(BLK, dtype=F32)
    dist = idx[:, None] - idx[None, :]
    inner = jnp.where(dist >= 0, jnp.exp(jnp.maximum(dist, 0.0)[None] * log_gamma[:, None, None]), 0.0)
    zeta = jnp.exp((BLK - 1 - idx)[None, :] * log_gamma[:, None])
    xi = jnp.exp((idx + 1.0)[None, :] * log_gamma[:, None])
    zeta_k = jnp.repeat(zeta.T, RET_DK, axis=1)
    xi_v = jnp.repeat(xi.T, RET_DV, axis=1)
    row_h = jnp.arange(RET_QK) // RET_DK
    col_h = jnp.arange(RET_W) // RET_DV
    same = row_h[:, None] == col_h[None, :]
    state_decay = jnp.where(same, jnp.exp(BLK * log_gamma)[row_h][:, None], 0.0)
    return inner, zeta_k, xi_v, state_decay, same.astype(F32)


def _in_kernel(x_ref, w_ref, cr_ref, sr_ref, cd_ref, sd_ref, cq_ref, sq_ref, ck_ref,
               mu_ref, wa_up_ref, w0a0_ref, g_up_ref, kk_ka_ref, v_up_ref, v0_ref, vfirst_ref,
               qn_ref, wq_ref, kvn_ref, wkv_ref,
               ret_q, ret_k, ret_v, ret_g, dil_q, dil_k, dil_v,
               rw_r, rw_lw, rw_k, rw_v, rw_a, rw_b, rw_g, mla_q, mla_k, mla_v,
               carry_ref, *, has_vres):
    tm = x_ref.shape[1]

    @pl.when(pl.program_id(1) == 0)
    def _():
        carry_ref[...] = jnp.zeros_like(carry_ref)

    xb = x_ref[0].astype(BF16)
    o1, o2, o3 = IN_RET_W, IN_RET_W + IN_DIL_W, IN_RET_W + IN_DIL_W + IN_RWKV_W
    pa = jnp.dot(xb, w_ref[:, :o1], preferred_element_type=F32)
    pb = jnp.dot(xb, w_ref[:, o1:o2], preferred_element_type=F32)
    pc = jnp.dot(xb, w_ref[:, o2:o3], preferred_element_type=F32)
    pd = jnp.dot(xb, w_ref[:, o3:], preferred_element_type=F32)

    qk = RET_QK
    cr, sr = cr_ref[...], sr_ref[...]
    ret_q[0] = (pa[:, 0:qk] * cr + pa[:, qk:2 * qk] * sr).astype(ret_q.dtype)
    ret_k[0] = ((pa[:, 2 * qk:3 * qk] * cr + pa[:, 3 * qk:4 * qk] * sr) * (RET_DK ** -0.5)).astype(ret_k.dtype)
    ret_v[0] = pa[:, 4 * qk:4 * qk + RET_W].astype(ret_v.dtype)
    ret_g[0] = pa[:, 4 * qk + RET_W:]

    cd, sd = cd_ref[...], sd_ref[...]
    dil_q[0] = pb[:, 0:DIL_W] * cd + pb[:, DIL_W:2 * DIL_W] * sd
    dil_k[0] = pb[:, 2 * DIL_W:3 * DIL_W] * cd + pb[:, 3 * DIL_W:4 * DIL_W] * sd
    dil_v[0] = pb[:, 4 * DIL_W:]

    row = lax.broadcasted_iota(jnp.int32, pc.shape, 0)
    prev = jnp.where(row == 0, carry_ref[...], pltpu.roll(pc, 1, 0))
    carry_ref[...] = pc[tm - 1:tm, :]
    ps = pc + (prev - pc) * mu_ref[...]
    W = RWKV_W
    r, k, v = ps[:, 0:W], ps[:, W:2 * W], ps[:, 2 * W:3 * W]
    wd_ad = ps[:, 3 * W:3 * W + LANES]
    lane = lax.broadcasted_iota(jnp.int32, wd_ad.shape, 1)
    lora_in = jnp.where(lane < DECAY_LORA, jnp.tanh(wd_ad), wd_ad)
    lora = jnp.dot(lora_in.astype(BF16), wa_up_ref[...], preferred_element_type=F32) + w0a0_ref[...]
    w_raw = -jax.nn.softplus(-lora[:, :W]) - 0.5
    a_sig = jax.nn.sigmoid(lora[:, W:])
    gd = ps[:, 3 * W + LANES:3 * W + 2 * LANES]
    rw_g[0] = jnp.dot(jax.nn.sigmoid(gd).astype(BF16), g_up_ref[...], preferred_element_type=F32)
    kk = k * kk_ka_ref[0:1, :]
    kk = kk / jnp.maximum(jnp.sqrt(_head_sum(kk * kk, RWKV_DH)), 1e-12)
    if has_vres:
        vd = ps[:, 3 * W + 2 * LANES:]
        mix = jax.nn.sigmoid(jnp.dot(vd.astype(BF16), v_up_ref[...], preferred_element_type=F32) + v0_ref[...])
        v = v + (vfirst_ref[0] - v) * mix
    rw_r[0] = r
    rw_lw[0] = -jnp.exp(w_raw)
    rw_k[0] = k * (1.0 + (a_sig - 1.0) * kk_ka_ref[1:2, :])
    rw_v[0] = v
    rw_a[0] = -kk
    rw_b[0] = kk * a_sig

    c_q = pd[:, :Q_LORA]
    c_q = c_q * lax.rsqrt(jnp.mean(c_q * c_q, -1, keepdims=True) + NORM_EPS) * qn_ref[...]
    q2 = jnp.dot(c_q.astype(BF16), wq_ref[...], preferred_element_type=F32)
    scale = (MLA_NOPE + MLA_ROPE) ** -0.5
    mla_q[0] = ((q2[:, :MLA_QK_W] * cq_ref[...] + q2[:, MLA_QK_W:] * sq_ref[...]) * scale).astype(mla_q.dtype)
    c_kv = pd[:, Q_LORA:Q_LORA + KV_LORA]
    c_kv = c_kv * lax.rsqrt(jnp.mean(c_kv * c_kv, -1, keepdims=True) + NORM_EPS) * kvn_ref[...]
    kv = jnp.dot(c_kv.astype(BF16), wkv_ref[...], preferred_element_type=F32)
    kr = pd[:, Q_LORA + KV_LORA:] * ck_ref[...]
    i = lax.broadcasted_iota(jnp.int32, (LANES, MLA_QK_W), 0)
    j = lax.broadcasted_iota(jnp.int32, (LANES, MLA_QK_W), 1) % MLA_HEAD_PAD
    place = ((i < 2 * MLA_ROPE) & (j == MLA_NOPE + i % MLA_ROPE)).astype(BF16)
    k_pe = sum(jnp.dot(p, place, preferred_element_type=F32) for p in _split3(kr)[:2])
    mla_k[0] = (kv[:, :MLA_QK_W] + k_pe).astype(mla_k.dtype)
    mla_v[0] = kv[:, MLA_QK_W:].astype(mla_v.dtype)


def fused_in_proj(x, w_cat, tables, rw_params, mla_params, v_first, tile=IN_TILE):
    B_, S_, D = x.shape
    tile = min(tile, S_)
    assert S_ % tile == 0
    has_vres = v_first is not None
    tok = lambda w: pl.BlockSpec((1, tile, w), lambda b, i: (b, i, 0))
    pos = lambda a: pl.BlockSpec((tile, a.shape[1]), lambda b, i: (i, 0))
    full = lambda a: pl.BlockSpec(a.shape, lambda b, i: (0,) * a.ndim)
    if has_vres:
        vf_spec = tok(RWKV_W)
    else:
        v_first = jnp.zeros((1, tile, RWKV_W), F32)
        vf_spec = pl.BlockSpec((1, tile, RWKV_W), lambda b, i: (0, 0, 0))
    f32o = lambda w: jax.ShapeDtypeStruct((B_, S_, w), F32)
    bfo = lambda w: jax.ShapeDtypeStruct((B_, S_, w), BF16)
    out_shape = ([bfo(RET_QK), bfo(RET_QK), bfo(RET_W), f32o(RET_W)] + [f32o(DIL_W)] * 3 + [f32o(RWKV_W)] * 7
                 + [bfo(MLA_QK_W), bfo(MLA_QK_W), bfo(MLA_W)])
    return pl.pallas_call(
        functools.partial(_in_kernel, has_vres=has_vres),
        grid=(B_, S_ // tile),
        in_specs=[tok(D), full(w_cat)] + [pos(t) for t in tables] + [full(p) for p in rw_params] + [vf_spec]
                 + [full(p) for p in mla_params],
        out_specs=[tok(s.shape[-1]) for s in out_shape],
        out_shape=out_shape,
        scratch_shapes=[pltpu.VMEM((1, IN_RWKV_W), F32)],
        compiler_params=pltpu.CompilerParams(dimension_semantics=("arbitrary", "arbitrary"), vmem_limit_bytes=VMEM_LIMIT),
        name="fused_in_proj",
    )(x, w_cat, *tables, *rw_params, v_first, *mla_params)


def _retention_kernel(q_ref, k_ref, v_ref, g_ref, ng_ref, inner_ref, zeta_ref, xi_ref, sdec_ref, smask_ref, o_ref, state_ref):
    @pl.when(pl.program_id(1) == 0)
    def _():
        state_ref[...] = jnp.zeros_like(state_ref)

    q, k, v = q_ref[0], k_ref[0], v_ref[0]
    q_head = lax.broadcasted_iota(jnp.int32, q.shape, 1) // RET_DK
    v_head = lax.broadcasted_iota(jnp.int32, v.shape, 1) // RET_DV
    zero_q = jnp.zeros_like(q)
    scores = [_bdot(jnp.where(q_head == h, q, zero_q), k, _NT) * inner_ref[h] for h in range(RET_HEADS)]
    o_heads = [_bdot(s, v, _NN) for s in scores]
    o = _bdot(q, state_ref[...], _NN) * xi_ref[...]
    for h in range(RET_HEADS):
        o = o + jnp.where(v_head == h, o_heads[h], 0.0)
    state_ref[...] = state_ref[...] * sdec_ref[...] + _bdot(k.astype(F32) * zeta_ref[...], v, _TN) * smask_ref[...]
    o = o * lax.rsqrt(_head_sum(o * o, RET_DV) * (1.0 / RET_DV) + NORM_EPS) * ng_ref[...]
    g = g_ref[0]
    o_ref[0] = g * jax.nn.sigmoid(g) * o


def retention_fused(q, k, v, g, norm_g, tables):
    B_, S_, _ = q.shape
    assert S_ % BLK == 0
    tok = lambda w: pl.BlockSpec((1, BLK, w), lambda b, c: (b, c, 0))
    full = lambda a: pl.BlockSpec(a.shape, lambda b, c: (0,) * a.ndim)
    ng = norm_g.reshape(1, RET_W)
    return pl.pallas_call(
        _retention_kernel,
        grid=(B_, S_ // BLK),
        in_specs=[tok(RET_QK), tok(RET_QK), tok(RET_W), tok(RET_W), full(ng)] + [full(t) for t in tables],
        out_specs=tok(RET_W),
        out_shape=jax.ShapeDtypeStruct((B_, S_, RET_W), F32),
        scratch_shapes=[pltpu.VMEM((RET_QK, RET_W), F32)],
        compiler_params=pltpu.CompilerParams(dimension_semantics=("arbitrary", "arbitrary")),
        name="retention_chunk",
    )(q, k, v, g, ng, *tables)


def _dilated_kernel(q_ref, k_ref, v_ref, o_ref, m_ref, l_ref, *, dils):
    S_ = q_ref.shape[1]
    lane_head = lax.broadcasted_iota(jnp.int32, (BLK, DIL_PAIR_W), 1) // DIL_DH
    i_idx = lax.broadcasted_iota(jnp.int32, (BLK, 2 * BLK), 0)
    j_idx = lax.broadcasted_iota(jnp.int32, (BLK, 2 * BLK), 1)
    scale = DIL_DH ** -0.5

    def rows(start, n, d):
        return pl.ds(start, n) if d == 1 else pl.ds(start, n, stride=d)

    n_un = DIL_UNROLL
    for pi, d in enumerate(dils):
        def tiles(tt, carry, d=d, first=(pi == 0)):
            nb = [(tt * n_un + u) // d for u in range(n_un)]
            r = [(tt * n_un + u) % d for u in range(n_un)]
            kb = [jnp.maximum(n - 1, 0) for n in nb]
            q_rows = [rows(nb[u] * (BLK * d) + r[u], BLK, d) for u in range(n_un)]
            k_rows = [rows(kb[u] * (BLK * d) + r[u], 2 * BLK, d) for u in range(n_un)]
            q = [q_ref[0, q_rows[u], :] for u in range(n_un)]
            kk = [k_ref[0, k_rows[u], :].astype(BF16) for u in range(n_un)]
            vv = [v_ref[0, k_rows[u], :].astype(BF16) for u in range(n_un)]
            if not first:
                m_old = [m_ref[q_rows[u], :] for u in range(n_un)]
                l_old = [l_ref[q_rows[u], :] for u in range(n_un)]
                o_old = [o_ref[0, q_rows[u], :] for u in range(n_un)]
            delta = [(nb[u] - kb[u]) * BLK + i_idx - j_idx for u in range(n_un)]
            valid = [(dl >= 0) & (dl <= BLK) for dl in delta]
            pairs = [(u, h) for u in range(n_un) for h in range(2)]
            s = {(u, h): jnp.where(valid[u], _bdot(jnp.where(lane_head == h, q[u], jnp.zeros_like(q[u])), kk[u], _NT) * scale,
                                   -jnp.inf) for u, h in pairs}
            m = {p: jnp.max(s[p], axis=-1, keepdims=True) for p in pairs}
            pr = {p: jnp.exp(s[p] - m[p]) for p in pairs}
            l = {p: jnp.sum(pr[p], axis=-1, keepdims=True) for p in pairs}
            pv = {(u, h): jnp.dot(pr[u, h].astype(BF16), vv[u], preferred_element_type=F32) for u, h in pairs}
            for u in range(n_un):
                m_t = jnp.where(lane_head == 0, m[u, 0], m[u, 1])
                l_t = jnp.where(lane_head == 0, l[u, 0], l[u, 1])
                pv_t = jnp.where(lane_head == 0, pv[u, 0], pv[u, 1])
                if first:
                    m_ref[q_rows[u], :] = m_t
                    l_ref[q_rows[u], :] = l_t
                    o_ref[0, q_rows[u], :] = pv_t
                else:
                    m_new = jnp.maximum(m_old[u], m_t)
                    c_old = jnp.exp(m_old[u] - m_new)
                    c_t = jnp.exp(m_t - m_new)
                    m_ref[q_rows[u], :] = m_new
                    l_ref[q_rows[u], :] = l_old[u] * c_old + l_t * c_t
                    o_ref[0, q_rows[u], :] = o_old[u] * c_old + pv_t * c_t
            return carry

        lax.fori_loop(0, S_ // (BLK * n_un), tiles, 0)

    def finish(t, carry):
        r0 = pl.multiple_of(t * BLK, BLK)
        o_ref[0, pl.ds(r0, BLK), :] = o_ref[0, pl.ds(r0, BLK), :] / l_ref[pl.ds(r0, BLK), :]
        return carry

    lax.fori_loop(0, S_ // BLK, finish, 0)


def dilated_fused(q, k, v):
    B_, S_, W_ = q.shape
    dils = tuple(d for _, d in DIL_PATTERNS)
    assert all(w // d == BLK for w, d in DIL_PATTERNS) and S_ % (2 * BLK * max(dils)) == 0 and S_ % (BLK * DIL_UNROLL) == 0
    spec = pl.BlockSpec((1, S_, DIL_PAIR_W), lambda b, hp: (b, 0, hp))
    return pl.pallas_call(
        functools.partial(_dilated_kernel, dils=dils),
        grid=(B_, W_ // DIL_PAIR_W),
        in_specs=[spec] * 3,
        out_specs=spec,
        out_shape=jax.ShapeDtypeStruct((B_, S_, W_), F32),
        scratch_shapes=[pltpu.VMEM((S_, DIL_PAIR_W), F32)] * 2,
        compiler_params=pltpu.CompilerParams(dimension_semantics=("arbitrary", "arbitrary"), vmem_limit_bytes=DIL_VMEM_LIMIT),
        name="dilated_attention",
    )(q, k, v)


def _rwkv7_mixer_kernel(r_ref, lw_ref, k_ref, v_ref, a_ref, b_ref, g_ref, gn_ref, o_ref, state_ref, *, heads, dh, chunk):
    rows = r_ref.shape[1]
    n_sub = rows // chunk
    n_double = int(np.log2(chunk))
    assert 2 ** n_double == chunk and n_sub * chunk == rows

    @pl.when(pl.program_id(1) == 0)
    def _():
        state_ref[...] = jnp.zeros_like(state_ref)

    ri = lax.broadcasted_iota(jnp.int32, (rows, rows), 0)
    ci = lax.broadcasted_iota(jnp.int32, (rows, rows), 1)
    tri = ((ci <= ri) & (ci >= (ri // chunk) * chunk)).astype(BF16)
    logw = lw_ref[0]
    cum = sum(jnp.dot(tri, piece, preferred_element_type=F32) for piece in _split3(logw))

    row = lax.broadcasted_iota(jnp.int32, (chunk, 2 * chunk), 0)
    col = lax.broadcasted_iota(jnp.int32, (chunk, 2 * chunk), 1) % chunk
    strict = col < row
    incl = col <= row
    zeros_cv = jnp.zeros((chunk, dh), F32)

    pairs = [(s, h) for s in range(n_sub) for h in range(heads)]
    ah, rh, vh, bk_rem, w_total = {}, {}, {}, {}, {}
    l_abk, m_rbk = {}, {}
    for s in range(n_sub):
        rs = slice(s * chunk, (s + 1) * chunk)
        cum_s, logw_s = cum[rs], logw[rs]
        total = cum_s[chunk - 1:chunk, :]
        e_in = jnp.exp(cum_s)
        e_neg = jnp.exp(-cum_s)
        e_rem = jnp.exp(total - cum_s)
        w_tot = jnp.exp(total)
        r_hat = r_ref[0, rs, :] * e_in
        a_hat = a_ref[0, rs, :] * jnp.exp(cum_s - logw_s)
        b_all, k_all, v_all = b_ref[0, rs, :], k_ref[0, rs, :], v_ref[0, rs, :]
        b_til, k_til = b_all * e_neg, k_all * e_neg
        b_rem, k_rem = b_all * e_rem, k_all * e_rem
        for h in range(heads):
            sl = slice(h * dh, (h + 1) * dh)
            ah[s, h], rh[s, h], vh[s, h] = a_hat[:, sl], r_hat[:, sl], v_all[:, sl]
            w_total[s, h] = w_tot[:, sl]
            bk_rem[s, h] = jnp.concatenate([b_rem[:, sl], k_rem[:, sl]], axis=0)
            bk_til = jnp.concatenate([b_til[:, sl], k_til[:, sl]], axis=0)
            l_abk[s, h] = jnp.where(strict, _bdot(ah[s, h], bk_til, _NT), 0.0)
            m_rbk[s, h] = jnp.where(incl, _bdot(rh[s, h], bk_til, _NT), 0.0)
    z = {p: jnp.concatenate([ah[p], _bdot(l_abk[p], jnp.concatenate([zeros_cv, vh[p]], axis=0), _NN)], axis=1)
         for p in pairs}
    lp = {p: l_abk[p][:, :chunk] for p in pairs}
    for m in range(n_double):
        z = {p: z[p] + _bdot(lp[p], z[p], _NN) for p in pairs}
        if m + 1 < n_double:
            lp = {p: _bdot(lp[p], lp[p], _NN) for p in pairs}
    state = [state_ref[h] for h in range(heads)]
    y_parts = []
    for s in range(n_sub):
        pr_s = [_bdot(jnp.concatenate([z[s, h][:, :dh], rh[s, h]], axis=0), state[h], _NT) for h in range(heads)]
        uv = [jnp.concatenate([pr_s[h][:chunk] + z[s, h][:, dh:], vh[s, h]], axis=0) for h in range(heads)]
        ys = [pr_s[h][chunk:] + _bdot(m_rbk[s, h], uv[h], _NN) for h in range(heads)]
        state = [state[h] * w_total[s, h] + _bdot(uv[h], bk_rem[s, h], _TN) for h in range(heads)]
        y_parts.append(jnp.concatenate(ys, axis=1))
    for h in range(heads):
        state_ref[h] = state[h]
    y = jnp.concatenate(y_parts, axis=0)
    mean = _head_sum(y, dh) * (1.0 / dh)
    yc = y - mean
    var = _head_sum(yc * yc, dh) * (1.0 / dh)
    yn = yc * lax.rsqrt(var + RWKV_GN_EPS) * gn_ref[0:1, :] + gn_ref[1:2, :]
    bonus = _head_sum(r_ref[0] * k_ref[0] * gn_ref[2:3, :], dh) * v_ref[0]
    o_ref[0] = (yn + bonus) * g_ref[0]


def rwkv7_mixer_fused(r, logw, k, v, a, b, g, gn, heads=RWKV_HEADS, dh=RWKV_DH, chunk=RWKV_CHUNK, block=RWKV_BLOCK):
    B_, S_, W_ = r.shape
    block = min(block, S_)
    assert W_ == heads * dh and S_ % block == 0 and block % chunk == 0
    spec = pl.BlockSpec((1, block, W_), lambda bi, ci: (bi, ci, 0))
    return pl.pallas_call(
        functools.partial(_rwkv7_mixer_kernel, heads=heads, dh=dh, chunk=chunk),
        grid=(B_, S_ // block),
        in_specs=[spec] * 7 + [pl.BlockSpec(gn.shape, lambda bi, ci: (0, 0))],
        out_specs=spec,
        out_shape=jax.ShapeDtypeStruct((B_, S_, W_), F32),
        scratch_shapes=[pltpu.VMEM((heads, dh, dh), F32)],
        compiler_params=pltpu.CompilerParams(dimension_semantics=("arbitrary", "arbitrary")),
        name="rwkv7_mixer",
    )(r, logw, k, v, a, b, g, gn)


def _mla_flash_kernel(q_ref, k_ref, v_ref, o_ref, *, tile):
    h = pl.program_id(1)
    qi = pl.program_id(2)
    q = q_ref[0]
    dv = v_ref.shape[2]

    def kv_tile(j):
        start = pl.multiple_of(j * tile, tile)
        return k_ref[0, pl.ds(start, tile), :], v_ref[0, pl.ds(start, tile), :]

    def update(carry, s, v_t):
        m, l, acc = carry
        m_new = jnp.maximum(m, jnp.max(s, axis=-1, keepdims=True))
        corr = jnp.exp(m - m_new)
        p = jnp.exp(s - m_new)
        l_new = corr * l + jnp.sum(p, axis=-1, keepdims=True)
        acc_new = corr * acc + jnp.dot(p.astype(v_t.dtype), v_t, preferred_element_type=F32)
        return m_new, l_new, acc_new

    def body(j, carry):
        k_t, v_t = kv_tile(j)
        s = lax.dot_general(q, k_t, (_NT, ((), ())), preferred_element_type=F32)
        return update(carry, s, v_t)

    init = (jnp.full((tile, 1), -jnp.inf, F32), jnp.zeros((tile, 1), F32), jnp.zeros((tile, dv), F32))
    carry = lax.fori_loop(0, qi, body, init)
    k_t, v_t = kv_tile(qi)
    s = lax.dot_general(q, k_t, (_NT, ((), ())), preferred_element_type=F32)
    row = lax.broadcasted_iota(jnp.int32, (tile, tile), 0)
    col = lax.broadcasted_iota(jnp.int32, (tile, tile), 1)
    s = jnp.where(col <= row, s, -jnp.inf)
    m, l, acc = update(carry, s, v_t)
    lane_head = lax.broadcasted_iota(jnp.int32, (tile, dv), 1) // MLA_DV
    o_ref[0] = jnp.where(lane_head == h % 2, acc / l, 0.0)


def mla_attention(q, k, v, tile=ATTN_TILE):
    B_, S_, _ = q.shape
    tile = min(tile, S_)
    assert S_ % tile == 0 and MLA_HEAD_PAD == 2 * MLA_DV
    return pl.pallas_call(
        functools.partial(_mla_flash_kernel, tile=tile),
        grid=(B_, MLA_HEADS, S_ // tile),
        in_specs=[pl.BlockSpec((1, tile, MLA_HEAD_PAD), lambda b, h, i: (b, i, h)),
                  pl.BlockSpec((1, S_, MLA_HEAD_PAD), lambda b, h, i: (b, 0, h)),
                  pl.BlockSpec((1, S_, 2 * MLA_DV), lambda b, h, i: (b, 0, h // 2))],
        out_specs=pl.BlockSpec((1, tile, MLA_HEAD_PAD), lambda b, h, i: (b, i, h)),
        out_shape=jax.ShapeDtypeStruct((B_, S_, MLA_QK_W), F32),
        compiler_params=pltpu.CompilerParams(dimension_semantics=("arbitrary", "arbitrary", "arbitrary")),
        name="mla_causal_flash",
    )(q, k, v)


def _out_kernel(x_ref, oa_ref, ob_ref, oc_ref, od_ref, wabc_ref, wd_ref, gb_ref, gd_ref, ln_ref, wr_ref, br_ref,
                x1_ref, x1b_ref, logit_ref, *, alpha):
    ob = ob_ref[...]
    ob = ob * lax.rsqrt(jnp.mean(ob * ob, -1, keepdims=True) + NORM_EPS) * gb_ref[...]
    od = od_ref[...]
    od = od * lax.rsqrt(jnp.sum(od * od, -1, keepdims=True) * (1.0 / MLA_W) + NORM_EPS) * gd_ref[...]
    W = RET_W
    mix = jnp.dot(oa_ref[...].astype(BF16), wabc_ref[0:W, :], preferred_element_type=F32)
    mix += jnp.dot(ob.astype(BF16), wabc_ref[W:2 * W, :], preferred_element_type=F32)
    mix += jnp.dot(oc_ref[...].astype(BF16), wabc_ref[2 * W:3 * W, :], preferred_element_type=F32)
    mix += jnp.dot(od.astype(BF16), wd_ref[...], preferred_element_type=F32)
    x1 = _layer_norm_rows(alpha * x_ref[...] + mix, ln_ref)
    x1_ref[...] = x1
    x1b = x1.astype(BF16)
    x1b_ref[...] = x1b
    logit_ref[...] = jnp.dot(x1b, wr_ref[...], preferred_element_type=F32) + br_ref[...]


def fused_out_proj(x, o_a, o_b, o_c, o_d, w_abc, w_d, g_b, g_d, ln_gb, w_router, b_router, alpha, tile=OUT_TILE):
    N, D = x.shape
    tile = min(tile, N)
    assert N % tile == 0
    n_exp = w_router.shape[1]
    wr = jnp.zeros((D, ROUTER_PAD), F32).at[:, :n_exp].set(w_router).astype(BF16)
    br = jnp.zeros((1, ROUTER_PAD), F32).at[0, :n_exp].set(b_router)
    consts = [w_abc, w_d, g_b, g_d, ln_gb, wr, br]
    tok = lambda w: pl.BlockSpec((tile, w), lambda i: (i, 0))
    full = lambda a: pl.BlockSpec(a.shape, lambda i: (0,) * a.ndim)
    x1, x1b, logits = pl.pallas_call(
        functools.partial(_out_kernel, alpha=alpha),
        grid=(N // tile,),
        in_specs=[tok(D), tok(RET_W), tok(DIL_W), tok(RWKV_W), tok(MLA_QK_W)] + [full(c) for c in consts],
        out_specs=[tok(D), tok(D), tok(ROUTER_PAD)],
        out_shape=[jax.ShapeDtypeStruct((N, D), F32), jax.ShapeDtypeStruct((N, D), BF16),
                   jax.ShapeDtypeStruct((N, ROUTER_PAD), F32)],
        compiler_params=pltpu.CompilerParams(dimension_semantics=("arbitrary",), vmem_limit_bytes=VMEM_LIMIT),
        name="fused_out_proj",
    )(x, o_a, o_b, o_c, o_d, *consts)
    return x1, x1b, logits[:, :n_exp]


def _moe_expert_kernel(tile_e_ref, n_used_ref, x_ref, wg_ref, wl_ref, bg_ref, bl_ref, wd_ref, bd_ref, buf_ref, o_ref):
    del tile_e_ref, buf_ref
    i = pl.program_id(0)

    @pl.when(i < n_used_ref[0])
    def _():
        x = x_ref[...]
        glu = jnp.dot(x, wg_ref[0], preferred_element_type=F32) + bg_ref[0]
        lin = jnp.dot(x, wl_ref[0], preferred_element_type=F32) + bl_ref[0]
        glu = jnp.minimum(glu, SWIGLU_LIMIT)
        lin = jnp.clip(lin, -SWIGLU_LIMIT, SWIGLU_LIMIT)
        act = glu * jax.nn.sigmoid(SWIGLU_ALPHA * glu) * (lin + 1.0)
        y = jnp.dot(act.astype(wd_ref.dtype), wd_ref[0], preferred_element_type=F32) + bd_ref[0]
        o_ref[...] = y.astype(o_ref.dtype)

    @pl.when(i >= n_used_ref[0])
    def _():
        o_ref[...] = jnp.zeros_like(o_ref)


def _deinterleave_kernel(w_ref, even_ref, odd_ref):
    g = 2 * LANES
    i = lax.broadcasted_iota(jnp.int32, (g, g), 0)
    j = lax.broadcasted_iota(jnp.int32, (g, g), 1)
    perm = (i == jnp.where(j < LANES, 2 * j, 2 * (j - LANES) + 1)).astype(BF16)
    for c in range(w_ref.shape[2] // g):
        blk = w_ref[0, :, c * g:(c + 1) * g].astype(BF16)
        sorted_cols = jnp.dot(blk, perm, preferred_element_type=F32)
        even_ref[0, :, c * LANES:(c + 1) * LANES] = sorted_cols[:, :LANES].astype(even_ref.dtype)
        odd_ref[0, :, c * LANES:(c + 1) * LANES] = sorted_cols[:, LANES:].astype(odd_ref.dtype)


def deinterleave_to_bf16(w):
    E, D, F2 = w.shape
    rows = min(DEINT_ROWS, D)
    assert D % rows == 0 and F2 % (2 * LANES) == 0
    out = jax.ShapeDtypeStruct((E, D, F2 // 2), BF16)
    return pl.pallas_call(
        _deinterleave_kernel,
        grid=(E, D // rows),
        in_specs=[pl.BlockSpec((1, rows, F2), lambda e, r: (e, r, 0))],
        out_specs=[pl.BlockSpec((1, rows, F2 // 2), lambda e, r: (e, r, 0))] * 2,
        out_shape=[out, out],
        compiler_params=pltpu.CompilerParams(dimension_semantics=("arbitrary", "arbitrary"), vmem_limit_bytes=VMEM_LIMIT),
        name="deinterleave_cast",
    )(w)


def moe_experts(xs, tile_e, n_used, w_glu, w_lin, b_glu, b_lin, w_dn, b_dn, tile, first_tile, out_buf):
    n_rows, D = xs.shape
    E, _, F = w_glu.shape
    w_spec = lambda shape: pl.BlockSpec((1,) + shape, lambda i, te, nu: (te[i], 0, 0))
    operands = [tile_e, n_used, xs, w_glu, w_lin, b_glu.reshape(E, 1, F), b_lin.reshape(E, 1, F), w_dn, b_dn.reshape(E, 1, D),
                out_buf]
    return pl.pallas_call(
        _moe_expert_kernel,
        grid_spec=pltpu.PrefetchScalarGridSpec(
            num_scalar_prefetch=2,
            grid=(n_rows // tile,),
            in_specs=[pl.BlockSpec((tile, D), lambda i, te, nu: (i, 0)),
                      w_spec((D, F)), w_spec((D, F)), w_spec((1, F)), w_spec((1, F)), w_spec((F, D)), w_spec((1, D)),
                      pl.BlockSpec(memory_space=pl.ANY)],
            out_specs=pl.BlockSpec((tile, D), lambda i, te, nu: (i + first_tile, 0)),
        ),
        out_shape=jax.ShapeDtypeStruct(out_buf.shape, out_buf.dtype),
        input_output_aliases={len(operands) - 1: 0},
        compiler_params=pltpu.CompilerParams(dimension_semantics=("arbitrary",), vmem_limit_bytes=VMEM_LIMIT),
        name="moe_expert_ffn",
    )(*operands)


def _combine_kernel(x_ref, y0_ref, y1_ref, y2_ref, y3_ref, gate_ref, ln_ref, o_ref, *, alpha):
    gate = gate_ref[...]
    y = sum(y_ref[...].astype(F32) * gate[:, s:s + 1] for s, y_ref in enumerate((y0_ref, y1_ref, y2_ref, y3_ref)))
    o_ref[...] = _layer_norm_rows(alpha * x_ref[...] + y, ln_ref)


def moe_combine_ln(x1, ys, gate, ln_gb, alpha, tile=OUT_TILE):
    N, D = x1.shape
    tile = min(tile, N)
    assert N % tile == 0 and len(ys) == TOP_K
    tok = lambda w: pl.BlockSpec((tile, w), lambda i: (i, 0))
    return pl.pallas_call(
        functools.partial(_combine_kernel, alpha=alpha),
        grid=(N // tile,),
        in_specs=[tok(D)] * (1 + TOP_K) + [tok(TOP_K), pl.BlockSpec(ln_gb.shape, lambda i: (0, 0))],
        out_specs=tok(D),
        out_shape=jax.ShapeDtypeStruct((N, D), F32),
        compiler_params=pltpu.CompilerParams(dimension_semantics=("arbitrary",), vmem_limit_bytes=VMEM_LIMIT),
        name="moe_combine_ln",
    )(x1, *ys, gate, ln_gb)


def moe_layer(x1, x1b, logits, expert_params, expert_base, ln_gb, alpha, row_buf, tile=MOE_TILE):
    n_tok, D = x1.shape
    n_exp = logits.shape[1]
    top_val, top_idx = lax.top_k(logits, TOP_K)
    gate = jax.nn.softmax(top_val, axis=-1)
    n_assign = n_tok * TOP_K
    e_flat = top_idx.reshape(-1).astype(jnp.int32)
    experts = jnp.arange(n_exp, dtype=jnp.int32)
    e_sorted, order = lax.sort((e_flat, jnp.arange(n_assign, dtype=jnp.int32)), num_keys=1)
    counts = jnp.sum((e_flat[:, None] == experts[None, :]).astype(jnp.int32), axis=0)
    padded = (counts + tile - 1) // tile * tile
    start = jnp.cumsum(counts) - counts
    pend = jnp.cumsum(padded)
    pstart = pend - padded
    n_tiles = (n_assign + n_exp * (tile - 1) + tile - 1) // tile
    tile_first = jnp.arange(n_tiles, dtype=jnp.int32) * tile
    tile_e = jnp.minimum(jnp.sum((pend[None, :] <= tile_first[:, None]).astype(jnp.int32), axis=1), n_exp - 1)
    n_used = (pend[-1] // tile).astype(jnp.int32).reshape(1)
    within = (tile_first - pstart[tile_e])[:, None] + jnp.arange(tile, dtype=jnp.int32)[None, :]
    valid = within < counts[tile_e][:, None]
    sorted_idx = jnp.clip(start[tile_e][:, None] + within, 0, n_assign - 1)
    row_tok = jnp.where(valid, order[sorted_idx] // TOP_K, 0).reshape(-1)
    shift = jnp.sum(jnp.where(e_sorted[:, None] == experts[None, :], (pstart - start)[None, :], 0), axis=1)
    dest = jnp.arange(n_assign, dtype=jnp.int32) + shift
    _, pos = lax.sort((order, dest), num_keys=1)
    pos = pos.reshape(n_tok, TOP_K)
    n_chunks = MOE_GATHER_CHUNKS if n_tiles % MOE_GATHER_CHUNKS == 0 else 1
    tpc = n_tiles // n_chunks
    row_tok = row_tok.reshape(n_chunks, tpc * tile)
    tile_e = (tile_e + expert_base).reshape(n_chunks, tpc)
    yb = row_buf
    for c in range(n_chunks):
        yb = moe_experts(x1b[row_tok[c]], tile_e[c], jnp.clip(n_used - c * tpc, 0, tpc), *expert_params, tile, c * tpc, yb)
    return moe_combine_ln(x1, [yb[pos[:, s]] for s in range(TOP_K)], gate, ln_gb, alpha), yb


def moe_row_buffer(n_tok, D, n_exp, tile=MOE_TILE):
    n_tiles = (n_tok * TOP_K + n_exp * (tile - 1) + tile - 1) // tile
    return jnp.zeros((n_tiles * tile, D), BF16)


def prep_expert_params(exp_w_gu, exp_b_gu, exp_w_dn, exp_b_dn):
    L, E, D, F2 = exp_w_gu.shape
    w_glu, w_lin = deinterleave_to_bf16(exp_w_gu.reshape(L * E, D, F2))
    b_gu = exp_b_gu.reshape(L * E, F2)
    return (w_glu, w_lin, b_gu[:, 0::2], b_gu[:, 1::2], exp_w_dn.astype(BF16).reshape(L * E, F2 // 2, D),
            exp_b_dn.reshape(L * E, D))


def kernel(x, w_in, w_out, ret_norm_g, dil_norm_g, rwkv_mu, rwkv_w0, rwkv_w_up, rwkv_a0, rwkv_a_up, rwkv_g_up, rwkv_k_k, rwkv_k_a, rwkv_r_k, rwkv_ln_g, rwkv_ln_b, rwkv_vres_down, rwkv_vres_mu, rwkv_v0, rwkv_v_up, mla_q_norm_g, mla_w_q_up, mla_kv_norm_g, mla_w_kv_up, mla_out_norm_g, ln1_g, ln1_b, router_w, router_b, exp_w_gu, exp_b_gu, exp_w_dn, exp_b_dn, ln2_g, ln2_b):
    depth = w_in.shape[0]
    B_, S_, D = x.shape
    alpha = (2 * depth) ** 0.25
    tables = rope_tables(S_)
    ret_tabs = retention_tables()
    expert_params = prep_expert_params(exp_w_gu, exp_b_gu, exp_w_dn, exp_b_dn)
    n_exp = exp_w_gu.shape[1]
    v_first = None
    n_tok = B_ * S_
    row_buf = moe_row_buffer(n_tok, D, n_exp)
    flat = lambda t: t.reshape(n_tok, t.shape[-1])
    for l in range(depth):
        w_cat = prep_in_weights(w_in[l], rwkv_vres_down[l - 1] if l > 0 else None)
        rw_params = prep_rwkv_params(rwkv_mu[l], rwkv_w0[l], rwkv_w_up[l], rwkv_a0[l], rwkv_a_up[l], rwkv_g_up[l],
                                     rwkv_k_k[l], rwkv_k_a[l], rwkv_vres_mu[l - 1] if l > 0 else None,
                                     rwkv_v0[l - 1] if l > 0 else None, rwkv_v_up[l - 1] if l > 0 else None)
        mla_params = prep_mla_params(mla_q_norm_g[l], mla_w_q_up[l], mla_kv_norm_g[l], mla_w_kv_up[l])
        gn = jnp.stack([rwkv_ln_g[l], rwkv_ln_b[l], rwkv_r_k[l].reshape(-1)])
        w_abc, w_d, g_d = prep_out_weights(w_out[l], mla_out_norm_g[l])
        (ret_q, ret_k, ret_v, ret_g, dil_q, dil_k, dil_v, rw_r, rw_lw, rw_k, rw_v, rw_a, rw_b, rw_g,
         mla_q, mla_k, mla_v) = fused_in_proj(x, w_cat, tables, rw_params, mla_params, v_first)
        if l == 0:
            v_first = rw_v
        o_a = retention_fused(ret_q, ret_k, ret_v, ret_g, ret_norm_g[l], ret_tabs)
        o_b = dilated_fused(dil_q, dil_k, dil_v)
        o_c = rwkv7_mixer_fused(rw_r, rw_lw, rw_k, rw_v, rw_a, rw_b, rw_g, gn)
        o_d = mla_attention(mla_q, mla_k, mla_v)
        x1, x1b, logits = fused_out_proj(flat(x), flat(o_a), flat(o_b), flat(o_c), flat(o_d), w_abc, w_d,
                                         dil_norm_g[l].reshape(1, DIL_W), g_d, jnp.stack([ln1_g[l], ln1_b[l]]),
                                         router_w[l], router_b[l], alpha)
        x, row_buf = moe_layer(x1, x1b, logits, expert_params, l * n_exp, jnp.stack([ln2_g[l], ln2_b[l]]), alpha, row_buf)
        x = x.reshape(B_, S_, D)
    return x
```

```python
import functools

import numpy as np
import jax
import jax.numpy as jnp
from jax import lax
from jax.experimental import pallas as pl
from jax.experimental.pallas import tpu as pltpu

F32 = jnp.float32
BF16 = jnp.bfloat16

LANES = 128
BLK = 128
LN_EPS = 1e-5
NORM_EPS = 1e-6

RET_HEADS, RET_DK, RET_DV = 4, 32, 64
RET_QK = RET_HEADS * RET_DK
RET_W = RET_HEADS * RET_DV
RET_THETA = 10000.0
DIL_HEADS, DIL_DH = 4, 64
DIL_W = DIL_HEADS * DIL_DH
DIL_PATTERNS = ((128, 1), (512, 4), (2048, 16))
ROPE_THETA = 500000.0
ROPE_ROT_DIM = DIL_DH // 4
RWKV_HEADS, RWKV_DH = 4, 64
RWKV_W = RWKV_HEADS * RWKV_DH
DECAY_LORA, AAA_LORA, MV_LORA, GATE_LORA = 64, 64, 32, 128
RWKV_GN_EPS = 64e-5
MLA_HEADS, MLA_NOPE, MLA_ROPE, MLA_DV = 4, 64, 32, 64
MLA_W = MLA_HEADS * MLA_DV
Q_LORA, KV_LORA = 256, 128
MLA_THETA = 10000.0
TOP_K = 4
SWIGLU_LIMIT, SWIGLU_ALPHA = 7.0, 1.702

RET_SPLITS = (RET_QK, RET_QK, RET_W, RET_W)
DIL_SPLITS = (DIL_W, DIL_W, DIL_W)
RWKV_SPLITS = (RWKV_W, RWKV_W, RWKV_W, DECAY_LORA, AAA_LORA, GATE_LORA)
MLA_SPLITS = (Q_LORA, KV_LORA, MLA_ROPE)
A_END = sum(RET_SPLITS)
B_END = A_END + sum(DIL_SPLITS)
C_END = B_END + sum(RWKV_SPLITS)
N_IN = C_END + sum(MLA_SPLITS)

RWKV_CHUNK = 64
RWKV_BLOCK = 256

MLA_HEAD_PAD = 128
MLA_QK_W = MLA_HEADS * MLA_HEAD_PAD
ATTN_TILE = 512

IN_RET_W = 4 * RET_QK + 2 * RET_W
IN_DIL_W = 5 * DIL_W
IN_RWKV_W = sum(RWKV_SPLITS) + LANES
IN_MLA_W = Q_LORA + KV_LORA + LANES
IN_COLS = IN_RET_W + IN_DIL_W + IN_RWKV_W + IN_MLA_W
IN_TILE = 256
OUT_TILE = 512
ROUTER_PAD = 128
DIL_PAIR_W = 2 * DIL_DH
DEINT_ROWS = 512
MOE_TILE = 512
DIL_UNROLL = 4
MOE_GATHER_CHUNKS = 4
VMEM_LIMIT = 48 * 1024 * 1024
DIL_VMEM_LIMIT = 56 * 1024 * 1024


def split_cols(p, sizes):
    idx = np.cumsum(sizes)[:-1].tolist()
    return jnp.split(p, idx, axis=-1)


def rope_table(n_pos, rot_dim, theta):
    inv_freq = 1.0 / (theta ** (jnp.arange(0, rot_dim, 2, dtype=F32) / rot_dim))
    ang = jnp.arange(n_pos, dtype=F32)[:, None] * inv_freq[None, :]
    return jnp.cos(ang), jnp.sin(ang)


def _bdot(a, b, dims):
    return lax.dot_general(a.astype(BF16), b.astype(BF16), ((dims[0], dims[1]), ((), ())), preferred_element_type=F32)


_NN = ((1,), (0,))
_NT = ((1,), (1,))
_TN = ((0,), (0,))


def _split3(x):
    h1 = x.astype(BF16)
    r1 = x - h1.astype(F32)
    h2 = r1.astype(BF16)
    h3 = (r1 - h2.astype(F32)).astype(BF16)
    return h1, h2, h3


def _head_sum(x, dh):
    w = x.shape[1]
    i = lax.broadcasted_iota(jnp.int32, (w, w), 0) // dh
    j = lax.broadcasted_iota(jnp.int32, (w, w), 1) // dh
    ones = (i == j).astype(BF16)
    return sum(jnp.dot(p, ones, preferred_element_type=F32) for p in _split3(x))


def _layer_norm_rows(h, ln_ref):
    mu = jnp.mean(h, -1, keepdims=True)
    hc = h - mu
    var = jnp.mean(hc * hc, -1, keepdims=True)
    return hc * lax.rsqrt(var + LN_EPS) * ln_ref[0:1, :] + ln_ref[1:2, :]


def _rot_half_cols(w, heads, dh, rot):
    d_in = w.shape[0]
    w = w.reshape(d_in, heads, dh)
    half = rot // 2
    sw = jnp.concatenate([-w[..., half:rot], w[..., :half], jnp.zeros((d_in, heads, dh - rot), w.dtype)], axis=-1)
    return sw.reshape(d_in, heads * dh)


def _rope_lanes(cs, heads, dh, rot, lead=0):
    cos, sin = cs
    S_ = cos.shape[0]
    c = jnp.concatenate([jnp.ones((S_, lead), F32), cos, cos, jnp.ones((S_, dh - lead - rot), F32)], axis=1)
    s = jnp.concatenate([jnp.zeros((S_, lead), F32), sin, sin, jnp.zeros((S_, dh - lead - rot), F32)], axis=1)
    return jnp.tile(c, (1, heads)), jnp.tile(s, (1, heads))


def rope_tables(S_):
    ret = _rope_lanes(rope_table(S_, RET_DK, RET_THETA), RET_HEADS, RET_DK, RET_DK)
    dil = _rope_lanes(rope_table(S_, ROPE_ROT_DIM, ROPE_THETA), DIL_HEADS, DIL_DH, ROPE_ROT_DIM)
    cos, sin = rope_table(S_, MLA_ROPE, MLA_THETA)
    mq = _rope_lanes((cos, sin), MLA_HEADS, MLA_HEAD_PAD, MLA_ROPE, lead=MLA_NOPE)
    mk = jnp.concatenate([cos, cos, sin, sin, jnp.zeros((S_, LANES - 2 * MLA_ROPE), F32)], axis=1)
    return ret + dil + mq + (mk,)


def prep_in_weights(w_in_l, vres_down_l):
    D = w_in_l.shape[0]
    a_q, a_k, a_v, a_g = split_cols(w_in_l[:, :A_END], RET_SPLITS)
    b_q, b_k, b_v = split_cols(w_in_l[:, A_END:B_END], DIL_SPLITS)
    d_cq, d_ckv, d_kr = split_cols(w_in_l[:, C_END:N_IN], MLA_SPLITS)
    vres = jnp.zeros((D, LANES), F32)
    if vres_down_l is not None:
        vres = vres.at[:, :MV_LORA].set(vres_down_l)
    cols = [a_q, _rot_half_cols(a_q, RET_HEADS, RET_DK, RET_DK), a_k, _rot_half_cols(a_k, RET_HEADS, RET_DK, RET_DK), a_v, a_g,
            b_q, _rot_half_cols(b_q, DIL_HEADS, DIL_DH, ROPE_ROT_DIM), b_k, _rot_half_cols(b_k, DIL_HEADS, DIL_DH, ROPE_ROT_DIM), b_v,
            w_in_l[:, B_END:C_END], vres,
            d_cq, d_ckv, d_kr, _rot_half_cols(d_kr, 1, MLA_ROPE, MLA_ROPE), jnp.zeros((D, LANES - 2 * MLA_ROPE), F32)]
    w = jnp.concatenate(cols, axis=1)
    assert w.shape[1] == IN_COLS
    return w.astype(BF16)


def prep_rwkv_params(mu, w0, w_up, a0, a_up, g_up, k_k, k_a, vres_mu, v0, v_up):
    W = RWKV_W
    mu_ext = jnp.zeros((1, IN_RWKV_W), F32).at[0, :mu.shape[0]].set(mu)
    v_up_ext = jnp.zeros((LANES, W), F32)
    v0_ext = jnp.zeros((1, W), F32)
    if vres_mu is not None:
        mu_ext = mu_ext.at[0, mu.shape[0]:mu.shape[0] + MV_LORA].set(vres_mu)
        v_up_ext = v_up_ext.at[:MV_LORA].set(v_up)
        v0_ext = v0.reshape(1, W)
    wa_up = jnp.zeros((LANES, 2 * W), F32).at[:DECAY_LORA, :W].set(w_up).at[DECAY_LORA:, W:].set(a_up)
    return [mu_ext, wa_up.astype(BF16), jnp.concatenate([w0, a0]).reshape(1, 2 * W), g_up.astype(BF16),
            jnp.stack([k_k, k_a]), v_up_ext.astype(BF16), v0_ext]


def prep_mla_params(q_norm_g, w_q_up, kv_norm_g, w_kv_up):
    ql = w_q_up.shape[0]
    wq = w_q_up.reshape(ql, MLA_HEADS, MLA_NOPE + MLA_ROPE)
    pad = jnp.zeros((ql, MLA_HEADS, MLA_HEAD_PAD - MLA_NOPE - MLA_ROPE), F32)
    q_main = jnp.concatenate([wq, pad], axis=-1).reshape(ql, MLA_QK_W)
    rope = wq[..., MLA_NOPE:]
    half = MLA_ROPE // 2
    q_rot = jnp.concatenate([jnp.zeros((ql, MLA_HEADS, MLA_NOPE), F32), -rope[..., half:], rope[..., :half], pad], axis=-1)
    w_q = jnp.concatenate([q_main, q_rot.reshape(ql, MLA_QK_W)], axis=1).astype(BF16)
    kl = w_kv_up.shape[0]
    wkv = w_kv_up.reshape(kl, MLA_HEADS, MLA_NOPE + MLA_DV)
    k_main = jnp.concatenate([wkv[..., :MLA_NOPE], jnp.zeros((kl, MLA_HEADS, MLA_HEAD_PAD - MLA_NOPE), F32)], axis=-1)
    w_kv = jnp.concatenate([k_main.reshape(kl, MLA_QK_W), wkv[..., MLA_NOPE:].reshape(kl, MLA_W)], axis=1).astype(BF16)
    return [q_norm_g.reshape(1, Q_LORA), w_q, kv_norm_g.reshape(1, KV_LORA), w_kv]


def prep_out_weights(w_out_l, mla_out_norm_g):
    W = RET_W
    w_abc = w_out_l[:3 * W].astype(BF16)
    w_d = w_out_l[3 * W:].reshape(MLA_HEADS, MLA_DV, -1)
    g_d = mla_out_norm_g.reshape(MLA_HEADS, MLA_DV)
    zw = jnp.zeros_like(w_d[0])
    zg = jnp.zeros_like(g_d[0])
    rows, gains = [], []
    for h in range(MLA_HEADS):
        rows += [w_d[h], zw] if h % 2 == 0 else [zw, w_d[h]]
        gains += [g_d[h], zg] if h % 2 == 0 else [zg, g_d[h]]
    return w_abc, jnp.concatenate(rows, axis=0).astype(BF16), jnp.concatenate(gains).reshape(1, MLA_QK_W)


def retention_tables():
    log_gamma = jnp.log(1.0 - 2.0 ** (-5.0 - jnp.arange(RET_HEADS, dtype=F32)))
    idx = jnp.arange(BLK, dtype=F32)
    dist = idx[:, None] - idx[None, :]
    inner = jnp.where(dist >= 0, jnp.exp(jnp.maximum(dist, 0.0)[None] * log_gamma[:, None, None]), 0.0)
    zeta = jnp.exp((BLK - 1 - idx)[None, :] * log_gamma[:, None])
    xi = jnp.exp((idx + 1.0)[None, :] * log_gamma[:, None])
    zeta_k = jnp.repeat(zeta.T, RET_DK, axis=1)
    xi_v = jnp.repeat(xi.T, RET_DV, axis=1)
    row_h = jnp.arange(RET_QK) // RET_DK
    col_h = jnp.arange(RET_W) // RET_DV
    same = row_h[:, None] == col_h[None, :]
    state_decay = jnp.where(same, jnp.exp(BLK * log_gamma)[row_h][:, None], 0.0)
    return inner, zeta_k, xi_v, state_decay, same.astype(F32)


def _in_kernel(x_ref, w_ref, cr_ref, sr_ref, cd_ref, sd_ref, cq_ref, sq_ref, ck_ref,
               mu_ref, wa_up_ref, w0a0_ref, g_up_ref, kk_ka_ref, v_up_ref, v0_ref, vfirst_ref,
               qn_ref, wq_ref, kvn_ref, wkv_ref,
               ret_q, ret_k, ret_v, ret_g, dil_q, dil_k, dil_v,
               rw_r, rw_lw, rw_k, rw_v, rw_a, rw_b, rw_g, mla_q, mla_k, mla_v,
               carry_ref, *, has_vres):
    tm = x_ref.shape[1]

    @pl.when(pl.program_id(1) == 0)
    def _():
        carry_ref[...] = jnp.zeros_like(carry_ref)

    xb = x_ref[0].astype(BF16)
    o1, o2, o3 = IN_RET_W, IN_RET_W + IN_DIL_W, IN_RET_W + IN_DIL_W + IN_RWKV_W
    pa = jnp.dot(xb, w_ref[:, :o1], preferred_element_type=F32)
    pb = jnp.dot(xb, w_ref[:, o1:o2], preferred_element_type=F32)
    pc = jnp.dot(xb, w_ref[:, o2:o3], preferred_element_type=F32)
    pd = jnp.dot(xb, w_ref[:, o3:], preferred_element_type=F32)

    qk = RET_QK
    cr, sr = cr_ref[...], sr_ref[...]
    ret_q[0] = (pa[:, 0:qk] * cr + pa[:, qk:2 * qk] * sr).astype(ret_q.dtype)
    ret_k[0] = ((pa[:, 2 * qk:3 * qk] * cr + pa[:, 3 * qk:4 * qk] * sr) * (RET_DK ** -0.5)).astype(ret_k.dtype)
    ret_v[0] = pa[:, 4 * qk:4 * qk + RET_W].astype(ret_v.dtype)
    ret_g[0] = pa[:, 4 * qk + RET_W:]

    cd, sd = cd_ref[...], sd_ref[...]
    dil_q[0] = pb[:, 0:DIL_W] * cd + pb[:, DIL_W:2 * DIL_W] * sd
    dil_k[0] = pb[:, 2 * DIL_W:3 * DIL_W] * cd + pb[:, 3 * DIL_W:4 * DIL_W] * sd
    dil_v[0] = pb[:, 4 * DIL_W:]

    row = lax.broadcasted_iota(jnp.int32, pc.shape, 0)
    prev = jnp.where(row == 0, carry_ref[...], pltpu.roll(pc, 1, 0))
    carry_ref[...] = pc[tm - 1:tm, :]
    ps = pc + (prev - pc) * mu_ref[...]
    W = RWKV_W
    r, k, v = ps[:, 0:W], ps[:, W:2 * W], ps[:, 2 * W:3 * W]
    wd_ad = ps[:, 3 * W:3 * W + LANES]
    lane = lax.broadcasted_iota(jnp.int32, wd_ad.shape, 1)
    lora_in = jnp.where(lane < DECAY_LORA, jnp.tanh(wd_ad), wd_ad)
    lora = jnp.dot(lora_in.astype(BF16), wa_up_ref[...], preferred_element_type=F32) + w0a0_ref[...]
    w_raw = -jax.nn.softplus(-lora[:, :W]) - 0.5
    a_sig = jax.nn.sigmoid(lora[:, W:])
    gd = ps[:, 3 * W + LANES:3 * W + 2 * LANES]
    rw_g[0] = jnp.dot(jax.nn.sigmoid(gd).astype(BF16), g_up_ref[...], preferred_element_type=F32)
    kk = k * kk_ka_ref[0:1, :]
    kk = kk / jnp.maximum(jnp.sqrt(_head_sum(kk * kk, RWKV_DH)), 1e-12)
    if has_vres:
        vd = ps[:, 3 * W + 2 * LANES:]
        mix = jax.nn.sigmoid(jnp.dot(vd.astype(BF16), v_up_ref[...], preferred_element_type=F32) + v0_ref[...])
        v = v + (vfirst_ref[0] - v) * mix
    rw_r[0] = r
    rw_lw[0] = -jnp.exp(w_raw)
    rw_k[0] = k * (1.0 + (a_sig - 1.0) * kk_ka_ref[1:2, :])
    rw_v[0] = v
    rw_a[0] = -kk
    rw_b[0] = kk * a_sig

    c_q = pd[:, :Q_LORA]
    c_q = c_q * lax.rsqrt(jnp.mean(c_q * c_q, -1, keepdims=True) + NORM_EPS) * qn_ref[...]
    q2 = jnp.dot(c_q.astype(BF16), wq_ref[...], preferred_element_type=F32)
    scale = (MLA_NOPE + MLA_ROPE) ** -0.5
    mla_q[0] = ((q2[:, :MLA_QK_W] * cq_ref[...] + q2[:, MLA_QK_W:] * sq_ref[...]) * scale).astype(mla_q.dtype)
    c_kv = pd[:, Q_LORA:Q_LORA + KV_LORA]
    c_kv = c_kv * lax.rsqrt(jnp.mean(c_kv * c_kv, -1, keepdims=True) + NORM_EPS) * kvn_ref[...]
    kv = jnp.dot(c_kv.astype(BF16), wkv_ref[...], preferred_element_type=F32)
    kr = pd[:, Q_LORA + KV_LORA:] * ck_ref[...]
    i = lax.broadcasted_iota(jnp.int32, (LANES, MLA_QK_W), 0)
    j = lax.broadcasted_iota(jnp.int32, (LANES, MLA_QK_W), 1) % MLA_HEAD_PAD
    place = ((i < 2 * MLA_ROPE) & (j == MLA_NOPE + i % MLA_ROPE)).astype(BF16)
    k_pe = sum(jnp.dot(p, place, preferred_element_type=F32) for p in _split3(kr)[:2])
    mla_k[0] = (kv[:, :MLA_QK_W] + k_pe).astype(mla_k.dtype)
    mla_v[0] = kv[:, MLA_QK_W:].astype(mla_v.dtype)


def fused_in_proj(x, w_cat, tables, rw_params, mla_params, v_first, tile=IN_TILE):
    B_, S_, D = x.shape
    tile = min(tile, S_)
    assert S_ % tile == 0
    has_vres = v_first is not None
    tok = lambda w: pl.BlockSpec((1, tile, w), lambda b, i: (b, i, 0))
    pos = lambda a: pl.BlockSpec((tile, a.shape[1]), lambda b, i: (i, 0))
    full = lambda a: pl.BlockSpec(a.shape, lambda b, i: (0,) * a.ndim)
    if has_vres:
        vf_spec = tok(RWKV_W)
    else:
        v_first = jnp.zeros((1, tile, RWKV_W), F32)
        vf_spec = pl.BlockSpec((1, tile, RWKV_W), lambda b, i: (0, 0, 0))
    f32o = lambda w: jax.ShapeDtypeStruct((B_, S_, w), F32)
    bfo = lambda w: jax.ShapeDtypeStruct((B_, S_, w), BF16)
    out_shape = ([bfo(RET_QK), bfo(RET_QK), bfo(RET_W), f32o(RET_W)] + [f32o(DIL_W)] * 3 + [f32o(RWKV_W)] * 7
                 + [bfo(MLA_QK_W), bfo(MLA_QK_W), bfo(MLA_W)])
    return pl.pallas_call(
        functools.partial(_in_kernel, has_vres=has_vres),
        grid=(B_, S_ // tile),
        in_specs=[tok(D), full(w_cat)] + [pos(t) for t in tables] + [full(p) for p in rw_params] + [vf_spec]
                 + [full(p) for p in mla_params],
        out_specs=[tok(s.shape[-1]) for s in out_shape],
        out_shape=out_shape,
        scratch_shapes=[pltpu.VMEM((1, IN_RWKV_W), F32)],
        compiler_params=pltpu.CompilerParams(dimension_semantics=("arbitrary", "arbitrary"), vmem_limit_bytes=VMEM_LIMIT),
        name="fused_in_proj",
    )(x, w_cat, *tables, *rw_params, v_first, *mla_params)


def _retention_kernel(q_ref, k_ref, v_ref, g_ref, ng_ref, inner_ref, zeta_ref, xi_ref, sdec_ref, smask_ref, o_ref, state_ref):
    @pl.when(pl.program_id(1) == 0)
    def _():
        state_ref[...] = jnp.zeros_like(state_ref)

    q, k, v = q_ref[0], k_ref[0], v_ref[0]
    q_head = lax.broadcasted_iota(jnp.int32, q.shape, 1) // RET_DK
    v_head = lax.broadcasted_iota(jnp.int32, v.shape, 1) // RET_DV
    zero_q = jnp.zeros_like(q)
    scores = [_bdot(jnp.where(q_head == h, q, zero_q), k, _NT) * inner_ref[h] for h in range(RET_HEADS)]
    o_heads = [_bdot(s, v, _NN) for s in scores]
    o = _bdot(q, state_ref[...], _NN) * xi_ref[...]
    for h in range(RET_HEADS):
        o = o + jnp.where(v_head == h, o_heads[h], 0.0)
    state_ref[...] = state_ref[...] * sdec_ref[...] + _bdot(k.astype(F32) * zeta_ref[...], v, _TN) * smask_ref[...]
    o = o * lax.rsqrt(_head_sum(o * o, RET_DV) * (1.0 / RET_DV) + NORM_EPS) * ng_ref[...]
    g = g_ref[0]
    o_ref[0] = g * jax.nn.sigmoid(g) * o


def retention_fused(q, k, v, g, norm_g, tables):
    B_, S_, _ = q.shape
    assert S_ % BLK == 0
    tok = lambda w: pl.BlockSpec((1, BLK, w), lambda b, c: (b, c, 0))
    full = lambda a: pl.BlockSpec(a.shape, lambda b, c: (0,) * a.ndim)
    ng = norm_g.reshape(1, RET_W)
    return pl.pallas_call(
        _retention_kernel,
        grid=(B_, S_ // BLK),
        in_specs=[tok(RET_QK), tok(RET_QK), tok(RET_W), tok(RET_W), full(ng)] + [full(t) for t in tables],
        out_specs=tok(RET_W),
        out_shape=jax.ShapeDtypeStruct((B_, S_, RET_W), F32),
        scratch_shapes=[pltpu.VMEM((RET_QK, RET_W), F32)],
        compiler_params=pltpu.CompilerParams(dimension_semantics=("arbitrary", "arbitrary")),
        name="retention_chunk",
    )(q, k, v, g, ng, *tables)


def _dilated_kernel(q_ref, k_ref, v_ref, o_ref, m_ref, l_ref, *, dils):
    S_ = q_ref.shape[1]
    lane_head = lax.broadcasted_iota(jnp.int32, (BLK, DIL_PAIR_W), 1) // DIL_DH
    i_idx = lax.broadcasted_iota(jnp.int32, (BLK, 2 * BLK), 0)
    j_idx = lax.broadcasted_iota(jnp.int32, (BLK, 2 * BLK), 1)
    scale = DIL_DH ** -0.5

    def rows(start, n, d):
        return pl.ds(start, n) if d == 1 else pl.ds(start, n, stride=d)

    n_un = DIL_UNROLL
    for pi, d in enumerate(dils):
        def tiles(tt, carry, d=d, first=(pi == 0)):
            nb = [(tt * n_un + u) // d for u in range(n_un)]
            r = [(tt * n_un + u) % d for u in range(n_un)]
            kb = [jnp.maximum(n - 1, 0) for n in nb]
            q_rows = [rows(nb[u] * (BLK * d) + r[u], BLK, d) for u in range(n_un)]
            k_rows = [rows(kb[u] * (BLK * d) + r[u], 2 * BLK, d) for u in range(n_un)]
            q = [q_ref[0, q_rows[u], :] for u in range(n_un)]
            kk = [k_ref[0, k_rows[u], :].astype(BF16) for u in range(n_un)]
            vv = [v_ref[0, k_rows[u], :].astype(BF16) for u in range(n_un)]
            if not first:
                m_old = [m_ref[q_rows[u], :] for u in range(n_un)]
                l_old = [l_ref[q_rows[u], :] for u in range(n_un)]
                o_old = [o_ref[0, q_rows[u], :] for u in range(n_un)]
            delta = [(nb[u] - kb[u]) * BLK + i_idx - j_idx for u in range(n_un)]
            valid = [(dl >= 0) & (dl <= BLK) for dl in delta]
            pairs = [(u, h) for u in range(n_un) for h in range(2)]
            s = {(u, h): jnp.where(valid[u], _bdot(jnp.where(lane_head == h, q[u], jnp.zeros_like(q[u])), kk[u], _NT) * scale,
                                   -jnp.inf) for u, h in pairs}
            m = {p: jnp.max(s[p], axis=-1, keepdims=True) for p in pairs}
            pr = {p: jnp.exp(s[p] - m[p]) for p in pairs}
            l = {p: jnp.sum(pr[p], axis=-1, keepdims=True) for p in pairs}
            pv = {(u, h): jnp.dot(pr[u, h].astype(BF16), vv[u], preferred_element_type=F32) for u, h in pairs}
            for u in range(n_un):
                m_t = jnp.where(lane_head == 0, m[u, 0], m[u, 1])
                l_t = jnp.where(lane_head == 0, l[u, 0], l[u, 1])
                pv_t = jnp.where(lane_head == 0, pv[u, 0], pv[u, 1])
                if first:
                    m_ref[q_rows[u], :] = m_t
                    l_ref[q_rows[u], :] = l_t
                    o_ref[0, q_rows[u], :] = pv_t
                else:
                    m_new = jnp.maximum(m_old[u], m_t)
                    c_old = jnp.exp(m_old[u] - m_new)
                    c_t = jnp.exp(m_t - m_new)
                    m_ref[q_rows[u], :] = m_new
                    l_ref[q_rows[u], :] = l_old[u] * c_old + l_t * c_t
                    o_ref[0, q_rows[u], :] = o_old[u] * c_old + pv_t * c_t
            return carry

        lax.fori_loop(0, S_ // (BLK * n_un), tiles, 0)

    def finish(t, carry):
        r0 = pl.multiple_of(t * BLK, BLK)
        o_ref[0, pl.ds(r0, BLK), :] = o_ref[0, pl.ds(r0, BLK), :] / l_ref[pl.ds(r0, BLK), :]
        return carry

    lax.fori_loop(0, S_ // BLK, finish, 0)


def dilated_fused(q, k, v):
    B_, S_, W_ = q.shape
    dils = tuple(d for _, d in DIL_PATTERNS)
    assert all(w // d == BLK for w, d in DIL_PATTERNS) and S_ % (2 * BLK * max(dils)) == 0 and S_ % (BLK * DIL_UNROLL) == 0
    spec = pl.BlockSpec((1, S_, DIL_PAIR_W), lambda b, hp: (b, 0, hp))
    return pl.pallas_call(
        functools.partial(_dilated_kernel, dils=dils),
        grid=(B_, W_ // DIL_PAIR_W),
        in_specs=[spec] * 3,
        out_specs=spec,
        out_shape=jax.ShapeDtypeStruct((B_, S_, W_), F32),
        scratch_shapes=[pltpu.VMEM((S_, DIL_PAIR_W), F32)] * 2,
        compiler_params=pltpu.CompilerParams(dimension_semantics=("arbitrary", "arbitrary"), vmem_limit_bytes=DIL_VMEM_LIMIT),
        name="dilated_attention",
    )(q, k, v)


def _rwkv7_mixer_kernel(r_ref, lw_ref, k_ref, v_ref, a_ref, b_ref, g_ref, gn_ref, o_ref, state_ref, *, heads, dh, chunk):
    rows = r_ref.shape[1]
    n_sub = rows // chunk
    n_double = int(np.log2(chunk))
    assert 2 ** n_double == chunk and n_sub * chunk == rows

    @pl.when(pl.program_id(1) == 0)
    def _():
        state_ref[...] = jnp.zeros_like(state_ref)

    ri = lax.broadcasted_iota(jnp.int32, (rows, rows), 0)
    ci = lax.broadcasted_iota(jnp.int32, (rows, rows), 1)
    tri = ((ci <= ri) & (ci >= (ri // chunk) * chunk)).astype(BF16)
    logw = lw_ref[0]
    cum = sum(jnp.dot(tri, piece, preferred_element_type=F32) for piece in _split3(logw))

    row = lax.broadcasted_iota(jnp.int32, (chunk, 2 * chunk), 0)
    col = lax.broadcasted_iota(jnp.int32, (chunk, 2 * chunk), 1) % chunk
    strict = col < row
    incl = col <= row
    zeros_cv = jnp.zeros((chunk, dh), F32)

    pairs = [(s, h) for s in range(n_sub) for h in range(heads)]
    ah, rh, vh, bk_rem, w_total = {}, {}, {}, {}, {}
    l_abk, m_rbk = {}, {}
    for s in range(n_sub):
        rs = slice(s * chunk, (s + 1) * chunk)
        cum_s, logw_s = cum[rs], logw[rs]
        total = cum_s[chunk - 1:chunk, :]
        e_in = jnp.exp(cum_s)
        e_neg = jnp.exp(-cum_s)
        e_rem = jnp.exp(total - cum_s)
        w_tot = jnp.exp(total)
        r_hat = r_ref[0, rs, :] * e_in
        a_hat = a_ref[0, rs, :] * jnp.exp(cum_s - logw_s)
        b_all, k_all, v_all = b_ref[0, rs, :], k_ref[0, rs, :], v_ref[0, rs, :]
        b_til, k_til = b_all * e_neg, k_all * e_neg
        b_rem, k_rem = b_all * e_rem, k_all * e_rem
        for h in range(heads):
            sl = slice(h * dh, (h + 1) * dh)
            ah[s, h], rh[s, h], vh[s, h] = a_hat[:, sl], r_hat[:, sl], v_all[:, sl]
            w_total[s, h] = w_tot[:, sl]
            bk_rem[s, h] = jnp.concatenate([b_rem[:, sl], k_rem[:, sl]], axis=0)
            bk_til = jnp.concatenate([b_til[:, sl], k_til[:, sl]], axis=0)
            l_abk[s, h] = jnp.where(strict, _bdot(ah[s, h], bk_til, _NT), 0.0)
            m_rbk[s, h] = jnp.where(incl, _bdot(rh[s, h], bk_til, _NT), 0.0)
    z = {p: jnp.concatenate([ah[p], _bdot(l_abk[p], jnp.concatenate([zeros_cv, vh[p]], axis=0), _NN)], axis=1)
         for p in pairs}
    lp = {p: l_abk[p][:, :chunk] for p in pairs}
    for m in range(n_double):
        z = {p: z[p] + _bdot(lp[p], z[p], _NN) for p in pairs}
        if m + 1 < n_double:
            lp = {p: _bdot(lp[p], lp[p], _NN) for p in pairs}
    state = [state_ref[h] for h in range(heads)]
    y_parts = []
    for s in range(n_sub):
        pr_s = [_bdot(jnp.concatenate([z[s, h][:, :dh], rh[s, h]], axis=0), state[h], _NT) for h in range(heads)]
        uv = [jnp.concatenate([pr_s[h][:chunk] + z[s, h][:, dh:], vh[s, h]], axis=0) for h in range(heads)]
        ys = [pr_s[h][chunk:] + _bdot(m_rbk[s, h], uv[h], _NN) for h in range(heads)]
        state = [state[h] * w_total[s, h] + _bdot(uv[h], bk_rem[s, h], _TN) for h in range(heads)]
        y_parts.append(jnp.concatenate(ys, axis=1))
    for h in range(heads):
        state_ref[h] = state[h]
    y = jnp.concatenate(y_parts, axis=0)
    mean = _head_sum(y, dh) * (1.0 / dh)
    yc = y - mean
    var = _head_sum(yc * yc, dh) * (1.0 / dh)
    yn = yc * lax.rsqrt(var + RWKV_GN_EPS) * gn_ref[0:1, :] + gn_ref[1:2, :]
    bonus = _head_sum(r_ref[0] * k_ref[0] * gn_ref[2:3, :], dh) * v_ref[0]
    o_ref[0] = (yn + bonus) * g_ref[0]


def rwkv7_mixer_fused(r, logw, k, v, a, b, g, gn, heads=RWKV_HEADS, dh=RWKV_DH, chunk=RWKV_CHUNK, block=RWKV_BLOCK):
    B_, S_, W_ = r.shape
    block = min(block, S_)
    assert W_ == heads * dh and S_ % block == 0 and block % chunk == 0
    spec = pl.BlockSpec((1, block, W_), lambda bi, ci: (bi, ci, 0))
    return pl.pallas_call(
        functools.partial(_rwkv7_mixer_kernel, heads=heads, dh=dh, chunk=chunk),
        grid=(B_, S_ // block),
        in_specs=[spec] * 7 + [pl.BlockSpec(gn.shape, lambda bi, ci: (0, 0))],
        out_specs=spec,
        out_shape=jax.ShapeDtypeStruct((B_, S_, W_), F32),
        scratch_shapes=[pltpu.VMEM((heads, dh, dh), F32)],
        compiler_params=pltpu.CompilerParams(dimension_semantics=("arbitrary", "arbitrary")),
        name="rwkv7_mixer",
    )(r, logw, k, v, a, b, g, gn)


def _mla_flash_kernel(q_ref, k_ref, v_ref, o_ref, *, tile):
    h = pl.program_id(1)
    qi = pl.program_id(2)
    q = q_ref[0]
    dv = v_ref.shape[2]

    def kv_tile(j):
        start = pl.multiple_of(j * tile, tile)
        return k_ref[0, pl.ds(start, tile), :], v_ref[0, pl.ds(start, tile), :]

    def update(carry, s, v_t):
        m, l, acc = carry
        m_new = jnp.maximum(m, jnp.max(s, axis=-1, keepdims=True))
        corr = jnp.exp(m - m_new)
        p = jnp.exp(s - m_new)
        l_new = corr * l + jnp.sum(p, axis=-1, keepdims=True)
        acc_new = corr * acc + jnp.dot(p.astype(v_t.dtype), v_t, preferred_element_type=F32)
        return m_new, l_new, acc_new

    def body(j, carry):
        k_t, v_t = kv_tile(j)
        s = lax.dot_general(q, k_t, (_NT, ((), ())), preferred_element_type=F32)
        return update(carry, s, v_t)

    init = (jnp.full((tile, 1), -jnp.inf, F32), jnp.zeros((tile, 1), F32), jnp.zeros((tile, dv), F32))
    carry = lax.fori_loop(0, qi, body, init)
    k_t, v_t = kv_tile(qi)
    s = lax.dot_general(q, k_t, (_NT, ((), ())), preferred_element_type=F32)
    row = lax.broadcasted_iota(jnp.int32, (tile, tile), 0)
    col = lax.broadcasted_iota(jnp.int32, (tile, tile), 1)
    s = jnp.where(col <= row, s, -jnp.inf)
    m, l, acc = update(carry, s, v_t)
    lane_head = lax.broadcasted_iota(jnp.int32, (tile, dv), 1) // MLA_DV
    o_ref[0] = jnp.where(lane_head == h % 2, acc / l, 0.0)


def mla_attention(q, k, v, tile=ATTN_TILE):
    B_, S_, _ = q.shape
    tile = min(tile, S_)
    assert S_ % tile == 0 and MLA_HEAD_PAD == 2 * MLA_DV
    return pl.pallas_call(
        functools.partial(_mla_flash_kernel, tile=tile),
        grid=(B_, MLA_HEADS, S_ // tile),
        in_specs=[pl.BlockSpec((1, tile, MLA_HEAD_PAD), lambda b, h, i: (b, i, h)),
                  pl.BlockSpec((1, S_, MLA_HEAD_PAD), lambda b, h, i: (b, 0, h)),
                  pl.BlockSpec((1, S_, 2 * MLA_DV), lambda b, h, i: (b, 0, h // 2))],
        out_specs=pl.BlockSpec((1, tile, MLA_HEAD_PAD), lambda b, h, i: (b, i, h)),
        out_shape=jax.ShapeDtypeStruct((B_, S_, MLA_QK_W), F32),
        compiler_params=pltpu.CompilerParams(dimension_semantics=("arbitrary", "arbitrary", "arbitrary")),
        name="mla_causal_flash",
    )(q, k, v)


def _out_kernel(x_ref, oa_ref, ob_ref, oc_ref, od_ref, wabc_ref, wd_ref, gb_ref, gd_ref, ln_ref, wr_ref, br_ref,
                x1_ref, x1b_ref, logit_ref, *, alpha):
    ob = ob_ref[...]
    ob = ob * lax.rsqrt(jnp.mean(ob * ob, -1, keepdims=True) + NORM_EPS) * gb_ref[...]
    od = od_ref[...]
    od = od * lax.rsqrt(jnp.sum(od * od, -1, keepdims=True) * (1.0 / MLA_W) + NORM_EPS) * gd_ref[...]
    W = RET_W
    mix = jnp.dot(oa_ref[...].astype(BF16), wabc_ref[0:W, :], preferred_element_type=F32)
    mix += jnp.dot(ob.astype(BF16), wabc_ref[W:2 * W, :], preferred_element_type=F32)
    mix += jnp.dot(oc_ref[...].astype(BF16), wabc_ref[2 * W:3 * W, :], preferred_element_type=F32)
    mix += jnp.dot(od.astype(BF16), wd_ref[...], preferred_element_type=F32)
    x1 = _layer_norm_rows(alpha * x_ref[...] + mix, ln_ref)
    x1_ref[...] = x1
    x1b = x1.astype(BF16)
    x1b_ref[...] = x1b
    logit_ref[...] = jnp.dot(x1b, wr_ref[...], preferred_element_type=F32) + br_ref[...]


def fused_out_proj(x, o_a, o_b, o_c, o_d, w_abc, w_d, g_b, g_d, ln_gb, w_router, b_router, alpha, tile=OUT_TILE):
    N, D = x.shape
    tile = min(tile, N)
    assert N % tile == 0
    n_exp = w_router.shape[1]
    wr = jnp.zeros((D, ROUTER_PAD), F32).at[:, :n_exp].set(w_router).astype(BF16)
    br = jnp.zeros((1, ROUTER_PAD), F32).at[0, :n_exp].set(b_router)
    consts = [w_abc, w_d, g_b, g_d, ln_gb, wr, br]
    tok = lambda w: pl.BlockSpec((tile, w), lambda i: (i, 0))
    full = lambda a: pl.BlockSpec(a.shape, lambda i: (0,) * a.ndim)
    x1, x1b, logits = pl.pallas_call(
        functools.partial(_out_kernel, alpha=alpha),
        grid=(N // tile,),
        in_specs=[tok(D), tok(RET_W), tok(DIL_W), tok(RWKV_W), tok(MLA_QK_W)] + [full(c) for c in consts],
        out_specs=[tok(D), tok(D), tok(ROUTER_PAD)],
        out_shape=[jax.ShapeDtypeStruct((N, D), F32), jax.ShapeDtypeStruct((N, D), BF16),
                   jax.ShapeDtypeStruct((N, ROUTER_PAD), F32)],
        compiler_params=pltpu.CompilerParams(dimension_semantics=("arbitrary",), vmem_limit_bytes=VMEM_LIMIT),
        name="fused_out_proj",
    )(x, o_a, o_b, o_c, o_d, *consts)
    return x1, x1b, logits[:, :n_exp]


def _moe_expert_kernel(tile_e_ref, n_used_ref, x_ref, wg_ref, wl_ref, bg_ref, bl_ref, wd_ref, bd_ref, buf_ref, o_ref):
    del tile_e_ref, buf_ref
    i = pl.program_id(0)

    @pl.when(i < n_used_ref[0])
    def _():
        x = x_ref[...]
        glu = jnp.dot(x, wg_ref[0], preferred_element_type=F32) + bg_ref[0]
        lin = jnp.dot(x, wl_ref[0], preferred_element_type=F32) + bl_ref[0]
        glu = jnp.minimum(glu, SWIGLU_LIMIT)
        lin = jnp.clip(lin, -SWIGLU_LIMIT, SWIGLU_LIMIT)
        act = glu * jax.nn.sigmoid(SWIGLU_ALPHA * glu) * (lin + 1.0)
        y = jnp.dot(act.astype(wd_ref.dtype), wd_ref[0], preferred_element_type=F32) + bd_ref[0]
        o_ref[...] = y.astype(o_ref.dtype)

    @pl.when(i >= n_used_ref[0])
    def _():
        o_ref[...] = jnp.zeros_like(o_ref)


def _deinterleave_kernel(w_ref, even_ref, odd_ref):
    g = 2 * LANES
    i = lax.broadcasted_iota(jnp.int32, (g, g), 0)
    j = lax.broadcasted_iota(jnp.int32, (g, g), 1)
    perm = (i == jnp.where(j < LANES, 2 * j, 2 * (j - LANES) + 1)).astype(BF16)
    for c in range(w_ref.shape[2] // g):
        blk = w_ref[0, :, c * g:(c + 1) * g].astype(BF16)
        sorted_cols = jnp.dot(blk, perm, preferred_element_type=F32)
        even_ref[0, :, c * LANES:(c + 1) * LANES] = sorted_cols[:, :LANES].astype(even_ref.dtype)
        odd_ref[0, :, c * LANES:(c + 1) * LANES] = sorted_cols[:, LANES:].astype(odd_ref.dtype)


def deinterleave_to_bf16(w):
    E, D, F2 = w.shape
    rows = min(DEINT_ROWS, D)
    assert D % rows == 0 and F2 % (2 * LANES) == 0
    out = jax.ShapeDtypeStruct((E, D, F2 // 2), BF16)
    return pl.pallas_call(
        _deinterleave_kernel,
        grid=(E, D // rows),
        in_specs=[pl.BlockSpec((1, rows, F2), lambda e, r: (e, r, 0))],
        out_specs=[pl.BlockSpec((1, rows, F2 // 2), lambda e, r: (e, r, 0))] * 2,
        out_shape=[out, out],
        compiler_params=pltpu.CompilerParams(dimension_semantics=("arbitrary", "arbitrary"), vmem_limit_bytes=VMEM_LIMIT),
        name="deinterleave_cast",
    )(w)


def moe_experts(xs, tile_e, n_used, w_glu, w_lin, b_glu, b_lin, w_dn, b_dn, tile, first_tile, out_buf):
    n_rows, D = xs.shape
    E, _, F = w_glu.shape
    w_spec = lambda shape: pl.BlockSpec((1,) + shape, lambda i, te, nu: (te[i], 0, 0))
    operands = [tile_e, n_used, xs, w_glu, w_lin, b_glu.reshape(E, 1, F), b_lin.reshape(E, 1, F), w_dn, b_dn.reshape(E, 1, D),
                out_buf]
    return pl.pallas_call(
        _moe_expert_kernel,
        grid_spec=pltpu.PrefetchScalarGridSpec(
            num_scalar_prefetch=2,
            grid=(n_rows // tile,),
            in_specs=[pl.BlockSpec((tile, D), lambda i, te, nu: (i, 0)),
                      w_spec((D, F)), w_spec((D, F)), w_spec((1, F)), w_spec((1, F)), w_spec((F, D)), w_spec((1, D)),
                      pl.BlockSpec(memory_space=pl.ANY)],
            out_specs=pl.BlockSpec((tile, D), lambda i, te, nu: (i + first_tile, 0)),
        ),
        out_shape=jax.ShapeDtypeStruct(out_buf.shape, out_buf.dtype),
        input_output_aliases={len(operands) - 1: 0},
        compiler_params=pltpu.CompilerParams(dimension_semantics=("arbitrary",), vmem_limit_bytes=VMEM_LIMIT),
        name="moe_expert_ffn",
    )(*operands)


def _combine_kernel(x_ref, y0_ref, y1_ref, y2_ref, y3_ref, gate_ref, ln_ref, o_ref, *, alpha):
    gate = gate_ref[...]
    y = sum(y_ref[...].astype(F32) * gate[:, s:s + 1] for s, y_ref in enumerate((y0_ref, y1_ref, y2_ref, y3_ref)))
    o_ref[...] = _layer_norm_rows(alpha * x_ref[...] + y, ln_ref)


def moe_combine_ln(x1, ys, gate, ln_gb, alpha, tile=OUT_TILE):
    N, D = x1.shape
    tile = min(tile, N)
    assert N % tile == 0 and len(ys) == TOP_K
    tok = lambda w: pl.BlockSpec((tile, w), lambda i: (i, 0))
    return pl.pallas_call(
        functools.partial(_combine_kernel, alpha=alpha),
        grid=(N // tile,),
        in_specs=[tok(D)] * (1 + TOP_K) + [tok(TOP_K), pl.BlockSpec(ln_gb.shape, lambda i: (0, 0))],
        out_specs=tok(D),
        out_shape=jax.ShapeDtypeStruct((N, D), F32),
        compiler_params=pltpu.CompilerParams(dimension_semantics=("arbitrary",), vmem_limit_bytes=VMEM_LIMIT),
        name="moe_combine_ln",
    )(x1, *ys, gate, ln_gb)


def moe_layer(x1, x1b, logits, expert_params, expert_base, ln_gb, alpha, row_buf, tile=MOE_TILE):
    n_tok, D = x1.shape
    n_exp = logits.shape[1]
    top_val, top_idx = lax.top_k(logits, TOP_K)
    gate = jax.nn.softmax(top_val, axis=-1)
    n_assign = n_tok * TOP_K
    e_flat = top_idx.reshape(-1).astype(jnp.int32)
    experts = jnp.arange(n_exp, dtype=jnp.int32)
    e_sorted, order = lax.sort((e_flat, jnp.arange(n_assign, dtype=jnp.int32)), num_keys=1)
    counts = jnp.sum((e_flat[:, None] == experts[None, :]).astype(jnp.int32), axis=0)
    padded = (counts + tile - 1) // tile * tile
    start = jnp.cumsum(counts) - counts
    pend = jnp.cumsum(padded)
    pstart = pend - padded
    n_tiles = (n_assign + n_exp * (tile - 1) + tile - 1) // tile
    tile_first = jnp.arange(n_tiles, dtype=jnp.int32) * tile
    tile_e = jnp.minimum(jnp.sum((pend[None, :] <= tile_first[:, None]).astype(jnp.int32), axis=1), n_exp - 1)
    n_used = (pend[-1] // tile).astype(jnp.int32).reshape(1)
    within = (tile_first - pstart[tile_e])[:, None] + jnp.arange(tile, dtype=jnp.int32)[None, :]
    valid = within < counts[tile_e][:, None]
    sorted_idx = jnp.clip(start[tile_e][:, None] + within, 0, n_assign - 1)
    spread = (tile_first[:, None] + jnp.arange(tile, dtype=jnp.int32)[None, :]) % n_tok
    row_tok = jnp.where(valid, order[sorted_idx] // TOP_K, spread).reshape(-1)
    shift = jnp.sum(jnp.where(e_sorted[:, None] == experts[None, :], (pstart - start)[None, :], 0), axis=1)
    dest = jnp.arange(n_assign, dtype=jnp.int32) + shift
    _, pos = lax.sort((order, dest), num_keys=1)
    pos = pos.reshape(n_tok, TOP_K)
    n_chunks = MOE_GATHER_CHUNKS if n_tiles % MOE_GATHER_CHUNKS == 0 else 1
    tpc = n_tiles // n_chunks
    row_tok = row_tok.reshape(n_chunks, tpc * tile)
    tile_e = (tile_e + expert_base).reshape(n_chunks, tpc)
    yb = row_buf
    for c in range(n_chunks):
        yb = moe_experts(x1b[row_tok[c]], tile_e[c], jnp.clip(n_used - c * tpc, 0, tpc), *expert_params, tile, c * tpc, yb)
    return moe_combine_ln(x1, [yb[pos[:, s]] for s in range(TOP_K)], gate, ln_gb, alpha), yb


def moe_row_buffer(n_tok, D, n_exp, tile=MOE_TILE):
    n_tiles = (n_tok * TOP_K + n_exp * (tile - 1) + tile - 1) // tile
    return jnp.zeros((n_tiles * tile, D), BF16)


def prep_expert_params(exp_w_gu, exp_b_gu, exp_w_dn, exp_b_dn):
    L, E, D, F2 = exp_w_gu.shape
    w_glu, w_lin = deinterleave_to_bf16(exp_w_gu.reshape(L * E, D, F2))
    b_gu = exp_b_gu.reshape(L * E, F2)
    return (w_glu, w_lin, b_gu[:, 0::2], b_gu[:, 1::2], exp_w_dn.astype(BF16).reshape(L * E, F2 // 2, D),
            exp_b_dn.reshape(L * E, D))


def kernel(x, w_in, w_out, ret_norm_g, dil_norm_g, rwkv_mu, rwkv_w0, rwkv_w_up, rwkv_a0, rwkv_a_up, rwkv_g_up, rwkv_k_k, rwkv_k_a, rwkv_r_k, rwkv_ln_g, rwkv_ln_b, rwkv_vres_down, rwkv_vres_mu, rwkv_v0, rwkv_v_up, mla_q_norm_g, mla_w_q_up, mla_kv_norm_g, mla_w_kv_up, mla_out_norm_g, ln1_g, ln1_b, router_w, router_b, exp_w_gu, exp_b_gu, exp_w_dn, exp_b_dn, ln2_g, ln2_b):
    depth = w_in.shape[0]
    B_, S_, D = x.shape
    alpha = (2 * depth) ** 0.25
    tables = rope_tables(S_)
    ret_tabs = retention_tables()
    expert_params = prep_expert_params(exp_w_gu, exp_b_gu, exp_w_dn, exp_b_dn)
    n_exp = exp_w_gu.shape[1]
    v_first = None
    n_tok = B_ * S_
    row_buf = moe_row_buffer(n_tok, D, n_exp)
    flat = lambda t: t.reshape(n_tok, t.shape[-1])
    for l in range(depth):
        w_cat = prep_in_weights(w_in[l], rwkv_vres_down[l - 1] if l > 0 else None)
        rw_params = prep_rwkv_params(rwkv_mu[l], rwkv_w0[l], rwkv_w_up[l], rwkv_a0[l], rwkv_a_up[l], rwkv_g_up[l],
                                     rwkv_k_k[l], rwkv_k_a[l], rwkv_vres_mu[l - 1] if l > 0 else None,
                                     rwkv_v0[l - 1] if l > 0 else None, rwkv_v_up[l - 1] if l > 0 else None)
        mla_params = prep_mla_params(mla_q_norm_g[l], mla_w_q_up[l], mla_kv_norm_g[l], mla_w_kv_up[l])
        gn = jnp.stack([rwkv_ln_g[l], rwkv_ln_b[l], rwkv_r_k[l].reshape(-1)])
        w_abc, w_d, g_d = prep_out_weights(w_out[l], mla_out_norm_g[l])
        (ret_q, ret_k, ret_v, ret_g, dil_q, dil_k, dil_v, rw_r, rw_lw, rw_k, rw_v, rw_a, rw_b, rw_g,
         mla_q, mla_k, mla_v) = fused_in_proj(x, w_cat, tables, rw_params, mla_params, v_first)
        if l == 0:
            v_first = rw_v
        o_a = retention_fused(ret_q, ret_k, ret_v, ret_g, ret_norm_g[l], ret_tabs)
        o_b = dilated_fused(dil_q, dil_k, dil_v)
        o_c = rwkv7_mixer_fused(rw_r, rw_lw, rw_k, rw_v, rw_a, rw_b, rw_g, gn)
        o_d = mla_attention(mla_q, mla_k, mla_v)
        x1, x1b, logits = fused_out_proj(flat(x), flat(o_a), flat(o_b), flat(o_c), flat(o_d), w_abc, w_d,
                                         dil_norm_g[l].reshape(1, DIL_W), g_d, jnp.stack([ln1_g[l], ln1_b[l]]),
                                         router_w[l], router_b[l], alpha)
        x, row_buf = moe_layer(x1, x1b, logits, expert_params, l * n_exp, jnp.stack([ln2_g[l], ln2_b[l]]), alpha, row_buf)
        x = x.reshape(B_, S_, D)
    return x
```

```python
import functools

import numpy as np
import jax
import jax.numpy as jnp
from jax import lax
from jax.experimental import pallas as pl
from jax.experimental.pallas import tpu as pltpu

F32 = jnp.float32
BF16 = jnp.bfloat16

LANES = 128
BLK = 128
LN_EPS = 1e-5
NORM_EPS = 1e-6

RET_HEADS, RET_DK, RET_DV = 4, 32, 64
RET_QK = RET_HEADS * RET_DK
RET_W = RET_HEADS * RET_DV
RET_THETA = 10000.0
DIL_HEADS, DIL_DH = 4, 64
DIL_W = DIL_HEADS * DIL_DH
DIL_PATTERNS = ((128, 1), (512, 4), (2048, 16))
ROPE_THETA = 500000.0
ROPE_ROT_DIM = DIL_DH // 4
RWKV_HEADS, RWKV_DH = 4, 64
RWKV_W = RWKV_HEADS * RWKV_DH
DECAY_LORA, AAA_LORA, MV_LORA, GATE_LORA = 64, 64, 32, 128
RWKV_GN_EPS = 64e-5
MLA_HEADS, MLA_NOPE, MLA_ROPE, MLA_DV = 4, 64, 32, 64
MLA_W = MLA_HEADS * MLA_DV
Q_LORA, KV_LORA = 256, 128
MLA_THETA = 10000.0
TOP_K = 4
SWIGLU_LIMIT, SWIGLU_ALPHA = 7.0, 1.702

RET_SPLITS = (RET_QK, RET_QK, RET_W, RET_W)
DIL_SPLITS = (DIL_W, DIL_W, DIL_W)
RWKV_SPLITS = (RWKV_W, RWKV_W, RWKV_W, DECAY_LORA, AAA_LORA, GATE_LORA)
MLA_SPLITS = (Q_LORA, KV_LORA, MLA_ROPE)
A_END = sum(RET_SPLITS)
B_END = A_END + sum(DIL_SPLITS)
C_END = B_END + sum(RWKV_SPLITS)
N_IN = C_END + sum(MLA_SPLITS)

RWKV_CHUNK = 64
RWKV_BLOCK = 512

MLA_HEAD_PAD = 128
MLA_QK_W = MLA_HEADS * MLA_HEAD_PAD
ATTN_TILE = 512

IN_RET_W = 4 * RET_QK + 2 * RET_W
IN_DIL_W = 5 * DIL_W
IN_RWKV_W = sum(RWKV_SPLITS) + LANES
IN_MLA_W = Q_LORA + KV_LORA + LANES
IN_COLS = IN_RET_W + IN_DIL_W + IN_RWKV_W + IN_MLA_W
IN_TILE = 256
OUT_TILE = 512
ROUTER_PAD = 128
DIL_PAIR_W = 2 * DIL_DH
DEINT_ROWS = 512
MOE_TILE = 512
DIL_UNROLL = 4
MOE_GATHER_CHUNKS = 4
VMEM_LIMIT = 48 * 1024 * 1024
DIL_VMEM_LIMIT = 56 * 1024 * 1024


def split_cols(p, sizes):
    idx = np.cumsum(sizes)[:-1].tolist()
    return jnp.split(p, idx, axis=-1)


def rope_table(n_pos, rot_dim, theta):
    inv_freq = 1.0 / (theta ** (jnp.arange(0, rot_dim, 2, dtype=F32) / rot_dim))
    ang = jnp.arange(n_pos, dtype=F32)[:, None] * inv_freq[None, :]
    return jnp.cos(ang), jnp.sin(ang)


def _bdot(a, b, dims):
    return lax.dot_general(a.astype(BF16), b.astype(BF16), ((dims[0], dims[1]), ((), ())), preferred_element_type=F32)


_NN = ((1,), (0,))
_NT = ((1,), (1,))
_TN = ((0,), (0,))


def _split3(x):
    h1 = x.astype(BF16)
    r1 = x - h1.astype(F32)
    h2 = r1.astype(BF16)
    h3 = (r1 - h2.astype(F32)).astype(BF16)
    return h1, h2, h3


def _head_sum(x, dh):
    w = x.shape[1]
    i = lax.broadcasted_iota(jnp.int32, (w, w), 0) // dh
    j = lax.broadcasted_iota(jnp.int32, (w, w), 1) // dh
    ones = (i == j).astype(BF16)
    return sum(jnp.dot(p, ones, preferred_element_type=F32) for p in _split3(x))


def _layer_norm_rows(h, ln_ref):
    mu = jnp.mean(h, -1, keepdims=True)
    hc = h - mu
    var = jnp.mean(hc * hc, -1, keepdims=True)
    return hc * lax.rsqrt(var + LN_EPS) * ln_ref[0:1, :] + ln_ref[1:2, :]


def _rot_half_cols(w, heads, dh, rot):
    d_in = w.shape[0]
    w = w.reshape(d_in, heads, dh)
    half = rot // 2
    sw = jnp.concatenate([-w[..., half:rot], w[..., :half], jnp.zeros((d_in, heads, dh - rot), w.dtype)], axis=-1)
    return sw.reshape(d_in, heads * dh)


def _rope_lanes(cs, heads, dh, rot, lead=0):
    cos, sin = cs
    S_ = cos.shape[0]
    c = jnp.concatenate([jnp.ones((S_, lead), F32), cos, cos, jnp.ones((S_, dh - lead - rot), F32)], axis=1)
    s = jnp.concatenate([jnp.zeros((S_, lead), F32), sin, sin, jnp.zeros((S_, dh - lead - rot), F32)], axis=1)
    return jnp.tile(c, (1, heads)), jnp.tile(s, (1, heads))


def rope_tables(S_):
    ret = _rope_lanes(rope_table(S_, RET_DK, RET_THETA), RET_HEADS, RET_DK, RET_DK)
    dil = _rope_lanes(rope_table(S_, ROPE_ROT_DIM, ROPE_THETA), DIL_HEADS, DIL_DH, ROPE_ROT_DIM)
    cos, sin = rope_table(S_, MLA_ROPE, MLA_THETA)
    mq = _rope_lanes((cos, sin), MLA_HEADS, MLA_HEAD_PAD, MLA_ROPE, lead=MLA_NOPE)
    mk = jnp.concatenate([cos, cos, sin, sin, jnp.zeros((S_, LANES - 2 * MLA_ROPE), F32)], axis=1)
    return ret + dil + mq + (mk,)


def prep_in_weights(w_in_l, vres_down_l):
    D = w_in_l.shape[0]
    a_q, a_k, a_v, a_g = split_cols(w_in_l[:, :A_END], RET_SPLITS)
    b_q, b_k, b_v = split_cols(w_in_l[:, A_END:B_END], DIL_SPLITS)
    d_cq, d_ckv, d_kr = split_cols(w_in_l[:, C_END:N_IN], MLA_SPLITS)
    vres = jnp.zeros((D, LANES), F32)
    if vres_down_l is not None:
        vres = vres.at[:, :MV_LORA].set(vres_down_l)
    cols = [a_q, _rot_half_cols(a_q, RET_HEADS, RET_DK, RET_DK), a_k, _rot_half_cols(a_k, RET_HEADS, RET_DK, RET_DK), a_v, a_g,
            b_q, _rot_half_cols(b_q, DIL_HEADS, DIL_DH, ROPE_ROT_DIM), b_k, _rot_half_cols(b_k, DIL_HEADS, DIL_DH, ROPE_ROT_DIM), b_v,
            w_in_l[:, B_END:C_END], vres,
            d_cq, d_ckv, d_kr, _rot_half_cols(d_kr, 1, MLA_ROPE, MLA_ROPE), jnp.zeros((D, LANES - 2 * MLA_ROPE), F32)]
    w = jnp.concatenate(cols, axis=1)
    assert w.shape[1] == IN_COLS
    return w.astype(BF16)


def prep_rwkv_params(mu, w0, w_up, a0, a_up, g_up, k_k, k_a, vres_mu, v0, v_up):
    W = RWKV_W
    mu_ext = jnp.zeros((1, IN_RWKV_W), F32).at[0, :mu.shape[0]].set(mu)
    v_up_ext = jnp.zeros((LANES, W), F32)
    v0_ext = jnp.zeros((1, W), F32)
    if vres_mu is not None:
        mu_ext = mu_ext.at[0, mu.shape[0]:mu.shape[0] + MV_LORA].set(vres_mu)
        v_up_ext = v_up_ext.at[:MV_LORA].set(v_up)
        v0_ext = v0.reshape(1, W)
    wa_up = jnp.zeros((LANES, 2 * W), F32).at[:DECAY_LORA, :W].set(w_up).at[DECAY_LORA:, W:].set(a_up)
    return [mu_ext, wa_up.astype(BF16), jnp.concatenate([w0, a0]).reshape(1, 2 * W), g_up.astype(BF16),
            jnp.stack([k_k, k_a]), v_up_ext.astype(BF16), v0_ext]


def prep_mla_params(q_norm_g, w_q_up, kv_norm_g, w_kv_up):
    ql = w_q_up.shape[0]
    wq = w_q_up.reshape(ql, MLA_HEADS, MLA_NOPE + MLA_ROPE)
    pad = jnp.zeros((ql, MLA_HEADS, MLA_HEAD_PAD - MLA_NOPE - MLA_ROPE), F32)
    q_main = jnp.concatenate([wq, pad], axis=-1).reshape(ql, MLA_QK_W)
    rope = wq[..., MLA_NOPE:]
    half = MLA_ROPE // 2
    q_rot = jnp.concatenate([jnp.zeros((ql, MLA_HEADS, MLA_NOPE), F32), -rope[..., half:], rope[..., :half], pad], axis=-1)
    w_q = jnp.concatenate([q_main, q_rot.reshape(ql, MLA_QK_W)], axis=1).astype(BF16)
    kl = w_kv_up.shape[0]
    wkv = w_kv_up.reshape(kl, MLA_HEADS, MLA_NOPE + MLA_DV)
    k_main = jnp.concatenate([wkv[..., :MLA_NOPE], jnp.zeros((kl, MLA_HEADS, MLA_HEAD_PAD - MLA_NOPE), F32)], axis=-1)
    w_kv = jnp.concatenate([k_main.reshape(kl, MLA_QK_W), wkv[..., MLA_NOPE:].reshape(kl, MLA_W)], axis=1).astype(BF16)
    return [q_norm_g.reshape(1, Q_LORA), w_q, kv_norm_g.reshape(1, KV_LORA), w_kv]


def prep_out_weights(w_out_l, mla_out_norm_g):
    W = RET_W
    w_abc = w_out_l[:3 * W].astype(BF16)
    w_d = w_out_l[3 * W:].reshape(MLA_HEADS, MLA_DV, -1)
    g_d = mla_out_norm_g.reshape(MLA_HEADS, MLA_DV)
    zw = jnp.zeros_like(w_d[0])
    zg = jnp.zeros_like(g_d[0])
    rows, gains = [], []
    for h in range(MLA_HEADS):
        rows += [w_d[h], zw] if h % 2 == 0 else [zw, w_d[h]]
        gains += [g_d[h], zg] if h % 2 == 0 else [zg, g_d[h]]
    return w_abc, jnp.concatenate(rows, axis=0).astype(BF16), jnp.concatenate(gains).reshape(1, MLA_QK_W)


def retention_tables():
    log_gamma = jnp.log(1.0 - 2.0 ** (-5.0 - jnp.arange(RET_HEADS, dtype=F32)))
    idx = jnp.arange(BLK, dtype=F32)
    dist = idx[:, None] - idx[None, :]
    inner = jnp.where(dist >= 0, jnp.exp(jnp.maximum(dist, 0.0)[None] * log_gamma[:, None, None]), 0.0)
    zeta = jnp.exp((BLK - 1 - idx)[None, :] * log_gamma[:, None])
    xi = jnp.exp((idx + 1.0)[None, :] * log_gamma[:, None])
    zeta_k = jnp.repeat(zeta.T, RET_DK, axis=1)
    xi_v = jnp.repeat(xi.T, RET_DV, axis=1)
    row_h = jnp.arange(RET_QK) // RET_DK
    col_h = jnp.arange(RET_W) // RET_DV
    same = row_h[:, None] == col_h[None, :]
    state_decay = jnp.where(same, jnp.exp(BLK * log_gamma)[row_h][:, None], 0.0)
    return inner, zeta_k, xi_v, state_decay, same.astype(F32)


def _in_kernel(x_ref, w_ref, cr_ref, sr_ref, cd_ref, sd_ref, cq_ref, sq_ref, ck_ref,
               mu_ref, wa_up_ref, w0a0_ref, g_up_ref, kk_ka_ref, v_up_ref, v0_ref, vfirst_ref,
               qn_ref, wq_ref, kvn_ref, wkv_ref,
               ret_q, ret_k, ret_v, ret_g, dil_q, dil_k, dil_v,
               rw_r, rw_lw, rw_k, rw_v, rw_a, rw_b, rw_g, mla_q, mla_k, mla_v,
               carry_ref, *, has_vres):
    tm = x_ref.shape[1]

    @pl.when(pl.program_id(1) == 0)
    def _():
        carry_ref[...] = jnp.zeros_like(carry_ref)

    xb = x_ref[0].astype(BF16)
    o1, o2, o3 = IN_RET_W, IN_RET_W + IN_DIL_W, IN_RET_W + IN_DIL_W + IN_RWKV_W
    pa = jnp.dot(xb, w_ref[:, :o1], preferred_element_type=F32)
    pb = jnp.dot(xb, w_ref[:, o1:o2], preferred_element_type=F32)
    pc = jnp.dot(xb, w_ref[:, o2:o3], preferred_element_type=F32)
    pd = jnp.dot(xb, w_ref[:, o3:], preferred_element_type=F32)

    qk = RET_QK
    cr, sr = cr_ref[...], sr_ref[...]
    ret_q[0] = (pa[:, 0:qk] * cr + pa[:, qk:2 * qk] * sr).astype(ret_q.dtype)
    ret_k[0] = ((pa[:, 2 * qk:3 * qk] * cr + pa[:, 3 * qk:4 * qk] * sr) * (RET_DK ** -0.5)).astype(ret_k.dtype)
    ret_v[0] = pa[:, 4 * qk:4 * qk + RET_W].astype(ret_v.dtype)
    ret_g[0] = pa[:, 4 * qk + RET_W:]

    cd, sd = cd_ref[...], sd_ref[...]
    dil_q[0] = pb[:, 0:DIL_W] * cd + pb[:, DIL_W:2 * DIL_W] * sd
    dil_k[0] = pb[:, 2 * DIL_W:3 * DIL_W] * cd + pb[:, 3 * DIL_W:4 * DIL_W] * sd
    dil_v[0] = pb[:, 4 * DIL_W:]

    row = lax.broadcasted_iota(jnp.int32, pc.shape, 0)
    prev = jnp.where(row == 0, carry_ref[...], pltpu.roll(pc, 1, 0))
    carry_ref[...] = pc[tm - 1:tm, :]
    ps = pc + (prev - pc) * mu_ref[...]
    W = RWKV_W
    r, k, v = ps[:, 0:W], ps[:, W:2 * W], ps[:, 2 * W:3 * W]
    wd_ad = ps[:, 3 * W:3 * W + LANES]
    lane = lax.broadcasted_iota(jnp.int32, wd_ad.shape, 1)
    lora_in = jnp.where(lane < DECAY_LORA, jnp.tanh(wd_ad), wd_ad)
    lora = jnp.dot(lora_in.astype(BF16), wa_up_ref[...], preferred_element_type=F32) + w0a0_ref[...]
    w_raw = -jax.nn.softplus(-lora[:, :W]) - 0.5
    a_sig = jax.nn.sigmoid(lora[:, W:])
    gd = ps[:, 3 * W + LANES:3 * W + 2 * LANES]
    rw_g[0] = jnp.dot(jax.nn.sigmoid(gd).astype(BF16), g_up_ref[...], preferred_element_type=F32)
    kk = k * kk_ka_ref[0:1, :]
    kk = kk / jnp.maximum(jnp.sqrt(_head_sum(kk * kk, RWKV_DH)), 1e-12)
    if has_vres:
        vd = ps[:, 3 * W + 2 * LANES:]
        mix = jax.nn.sigmoid(jnp.dot(vd.astype(BF16), v_up_ref[...], preferred_element_type=F32) + v0_ref[...])
        v = v + (vfirst_ref[0] - v) * mix
    rw_r[0] = r
    rw_lw[0] = -jnp.exp(w_raw)
    rw_k[0] = k * (1.0 + (a_sig - 1.0) * kk_ka_ref[1:2, :])
    rw_v[0] = v
    rw_a[0] = -kk
    rw_b[0] = kk * a_sig

    c_q = pd[:, :Q_LORA]
    c_q = c_q * lax.rsqrt(jnp.mean(c_q * c_q, -1, keepdims=True) + NORM_EPS) * qn_ref[...]
    q2 = jnp.dot(c_q.astype(BF16), wq_ref[...], preferred_element_type=F32)
    scale = (MLA_NOPE + MLA_ROPE) ** -0.5
    mla_q[0] = ((q2[:, :MLA_QK_W] * cq_ref[...] + q2[:, MLA_QK_W:] * sq_ref[...]) * scale).astype(mla_q.dtype)
    c_kv = pd[:, Q_LORA:Q_LORA + KV_LORA]
    c_kv = c_kv * lax.rsqrt(jnp.mean(c_kv * c_kv, -1, keepdims=True) + NORM_EPS) * kvn_ref[...]
    kv = jnp.dot(c_kv.astype(BF16), wkv_ref[...], preferred_element_type=F32)
    kr = pd[:, Q_LORA + KV_LORA:] * ck_ref[...]
    i = lax.broadcasted_iota(jnp.int32, (LANES, MLA_QK_W), 0)
    j = lax.broadcasted_iota(jnp.int32, (LANES, MLA_QK_W), 1) % MLA_HEAD_PAD
    place = ((i < 2 * MLA_ROPE) & (j == MLA_NOPE + i % MLA_ROPE)).astype(BF16)
    k_pe = sum(jnp.dot(p, place, preferred_element_type=F32) for p in _split3(kr)[:2])
    mla_k[0] = (kv[:, :MLA_QK_W] + k_pe).astype(mla_k.dtype)
    mla_v[0] = kv[:, MLA_QK_W:].astype(mla_v.dtype)


def fused_in_proj(x, w_cat, tables, rw_params, mla_params, v_first, tile=IN_TILE):
    B_, S_, D = x.shape
    tile = min(tile, S_)
    assert S_ % tile == 0
    has_vres = v_first is not None
    tok = lambda w: pl.BlockSpec((1, tile, w), lambda b, i: (b, i, 0))
    pos = lambda a: pl.BlockSpec((tile, a.shape[1]), lambda b, i: (i, 0))
    full = lambda a: pl.BlockSpec(a.shape, lambda b, i: (0,) * a.ndim)
    if has_vres:
        vf_spec = tok(RWKV_W)
    else:
        v_first = jnp.zeros((1, tile, RWKV_W), F32)
        vf_spec = pl.BlockSpec((1, tile, RWKV_W), lambda b, i: (0, 0, 0))
    f32o = lambda w: jax.ShapeDtypeStruct((B_, S_, w), F32)
    bfo = lambda w: jax.ShapeDtypeStruct((B_, S_, w), BF16)
    out_shape = ([bfo(RET_QK), bfo(RET_QK), bfo(RET_W), f32o(RET_W)] + [f32o(DIL_W)] * 3 + [f32o(RWKV_W)] * 7
                 + [bfo(MLA_QK_W), bfo(MLA_QK_W), bfo(MLA_W)])
    return pl.pallas_call(
        functools.partial(_in_kernel, has_vres=has_vres),
        grid=(B_, S_ // tile),
        in_specs=[tok(D), full(w_cat)] + [pos(t) for t in tables] + [full(p) for p in rw_params] + [vf_spec]
                 + [full(p) for p in mla_params],
        out_specs=[tok(s.shape[-1]) for s in out_shape],
        out_shape=out_shape,
        scratch_shapes=[pltpu.VMEM((1, IN_RWKV_W), F32)],
        compiler_params=pltpu.CompilerParams(dimension_semantics=("arbitrary", "arbitrary"), vmem_limit_bytes=VMEM_LIMIT),
        name="fused_in_proj",
    )(x, w_cat, *tables, *rw_params, v_first, *mla_params)


def _retention_kernel(q_ref, k_ref, v_ref, g_ref, ng_ref, inner_ref, zeta_ref, xi_ref, sdec_ref, smask_ref, o_ref, state_ref):
    @pl.when(pl.program_id(1) == 0)
    def _():
        state_ref[...] = jnp.zeros_like(state_ref)

    q, k, v = q_ref[0], k_ref[0], v_ref[0]
    q_head = lax.broadcasted_iota(jnp.int32, q.shape, 1) // RET_DK
    v_head = lax.broadcasted_iota(jnp.int32, v.shape, 1) // RET_DV
    zero_q = jnp.zeros_like(q)
    scores = [_bdot(jnp.where(q_head == h, q, zero_q), k, _NT) * inner_ref[h] for h in range(RET_HEADS)]
    o_heads = [_bdot(s, v, _NN) for s in scores]
    o = _bdot(q, state_ref[...], _NN) * xi_ref[...]
    for h in range(RET_HEADS):
        o = o + jnp.where(v_head == h, o_heads[h], 0.0)
    state_ref[...] = state_ref[...] * sdec_ref[...] + _bdot(k.astype(F32) * zeta_ref[...], v, _TN) * smask_ref[...]
    o = o * lax.rsqrt(_head_sum(o * o, RET_DV) * (1.0 / RET_DV) + NORM_EPS) * ng_ref[...]
    g = g_ref[0]
    o_ref[0] = g * jax.nn.sigmoid(g) * o


def retention_fused(q, k, v, g, norm_g, tables):
    B_, S_, _ = q.shape
    assert S_ % BLK == 0
    tok = lambda w: pl.BlockSpec((1, BLK, w), lambda b, c: (b, c, 0))
    full = lambda a: pl.BlockSpec(a.shape, lambda b, c: (0,) * a.ndim)
    ng = norm_g.reshape(1, RET_W)
    return pl.pallas_call(
        _retention_kernel,
        grid=(B_, S_ // BLK),
        in_specs=[tok(RET_QK), tok(RET_QK), tok(RET_W), tok(RET_W), full(ng)] + [full(t) for t in tables],
        out_specs=tok(RET_W),
        out_shape=jax.ShapeDtypeStruct((B_, S_, RET_W), F32),
        scratch_shapes=[pltpu.VMEM((RET_QK, RET_W), F32)],
        compiler_params=pltpu.CompilerParams(dimension_semantics=("arbitrary", "arbitrary")),
        name="retention_chunk",
    )(q, k, v, g, ng, *tables)


def _dilated_kernel(q_ref, k_ref, v_ref, o_ref, m_ref, l_ref, *, dils):
    S_ = q_ref.shape[1]
    lane_head = lax.broadcasted_iota(jnp.int32, (BLK, DIL_PAIR_W), 1) // DIL_DH
    i_idx = lax.broadcasted_iota(jnp.int32, (BLK, 2 * BLK), 0)
    j_idx = lax.broadcasted_iota(jnp.int32, (BLK, 2 * BLK), 1)
    scale = DIL_DH ** -0.5

    def rows(start, n, d):
        return pl.ds(start, n) if d == 1 else pl.ds(start, n, stride=d)

    n_un = DIL_UNROLL
    for pi, d in enumerate(dils):
        def tiles(tt, carry, d=d, first=(pi == 0)):
            nb = [(tt * n_un + u) // d for u in range(n_un)]
            r = [(tt * n_un + u) % d for u in range(n_un)]
            kb = [jnp.maximum(n - 1, 0) for n in nb]
            q_rows = [rows(nb[u] * (BLK * d) + r[u], BLK, d) for u in range(n_un)]
            k_rows = [rows(kb[u] * (BLK * d) + r[u], 2 * BLK, d) for u in range(n_un)]
            q = [q_ref[0, q_rows[u], :] for u in range(n_un)]
            kk = [k_ref[0, k_rows[u], :].astype(BF16) for u in range(n_un)]
            vv = [v_ref[0, k_rows[u], :].astype(BF16) for u in range(n_un)]
            if not first:
                m_old = [m_ref[q_rows[u], :] for u in range(n_un)]
                l_old = [l_ref[q_rows[u], :] for u in range(n_un)]
                o_old = [o_ref[0, q_rows[u], :] for u in range(n_un)]
            delta = [(nb[u] - kb[u]) * BLK + i_idx - j_idx for u in range(n_un)]
            valid = [(dl >= 0) & (dl <= BLK) for dl in delta]
            pairs = [(u, h) for u in range(n_un) for h in range(2)]
            s = {(u, h): jnp.where(valid[u], _bdot(jnp.where(lane_head == h, q[u], jnp.zeros_like(q[u])), kk[u], _NT) * scale,
                                   -jnp.inf) for u, h in pairs}
            m = {p: jnp.max(s[p], axis=-1, keepdims=True) for p in pairs}
            pr = {p: jnp.exp(s[p] - m[p]) for p in pairs}
            l = {p: jnp.sum(pr[p], axis=-1, keepdims=True) for p in pairs}
            pv = {(u, h): jnp.dot(pr[u, h].astype(BF16), vv[u], preferred_element_type=F32) for u, h in pairs}
            for u in range(n_un):
                m_t = jnp.where(lane_head == 0, m[u, 0], m[u, 1])
                l_t = jnp.where(lane_head == 0, l[u, 0], l[u, 1])
                pv_t = jnp.where(lane_head == 0, pv[u, 0], pv[u, 1])
                if first:
                    m_ref[q_rows[u], :] = m_t
                    l_ref[q_rows[u], :] = l_t
                    o_ref[0, q_rows[u], :] = pv_t
                else:
                    m_new = jnp.maximum(m_old[u], m_t)
                    c_old = jnp.exp(m_old[u] - m_new)
                    c_t = jnp.exp(m_t - m_new)
                    m_ref[q_rows[u], :] = m_new
                    l_ref[q_rows[u], :] = l_old[u] * c_old + l_t * c_t
                    o_ref[0, q_rows[u], :] = o_old[u] * c_old + pv_t * c_t
            return carry

        lax.fori_loop(0, S_ // (BLK * n_un), tiles, 0)

    def finish(t, carry):
        r0 = pl.multiple_of(t * BLK, BLK)
        o_ref[0, pl.ds(r0, BLK), :] = o_ref[0, pl.ds(r0, BLK), :] / l_ref[pl.ds(r0, BLK), :]
        return carry

    lax.fori_loop(0, S_ // BLK, finish, 0)


def dilated_fused(q, k, v):
    B_, S_, W_ = q.shape
    dils = tuple(d for _, d in DIL_PATTERNS)
    assert all(w // d == BLK for w, d in DIL_PATTERNS) and S_ % (2 * BLK * max(dils)) == 0 and S_ % (BLK * DIL_UNROLL) == 0
    spec = pl.BlockSpec((1, S_, DIL_PAIR_W), lambda b, hp: (b, 0, hp))
    return pl.pallas_call(
        functools.partial(_dilated_kernel, dils=dils),
        grid=(B_, W_ // DIL_PAIR_W),
        in_specs=[spec] * 3,
        out_specs=spec,
        out_shape=jax.ShapeDtypeStruct((B_, S_, W_), F32),
        scratch_shapes=[pltpu.VMEM((S_, DIL_PAIR_W), F32)] * 2,
        compiler_params=pltpu.CompilerParams(dimension_semantics=("arbitrary", "arbitrary"), vmem_limit_bytes=DIL_VMEM_LIMIT),
        name="dilated_attention",
    )(q, k, v)


def _rwkv7_mixer_kernel(r_ref, lw_ref, k_ref, v_ref, a_ref, b_ref, g_ref, gn_ref, o_ref, state_ref, *, heads, dh, chunk):
    rows = r_ref.shape[1]
    n_sub = rows // chunk
    n_double = int(np.log2(chunk))
    assert 2 ** n_double == chunk and n_sub * chunk == rows

    @pl.when(pl.program_id(1) == 0)
    def _():
        state_ref[...] = jnp.zeros_like(state_ref)

    ri = lax.broadcasted_iota(jnp.int32, (rows, rows), 0)
    ci = lax.broadcasted_iota(jnp.int32, (rows, rows), 1)
    tri = ((ci <= ri) & (ci >= (ri // chunk) * chunk)).astype(BF16)
    logw = lw_ref[0]
    cum = sum(jnp.dot(tri, piece, preferred_element_type=F32) for piece in _split3(logw))

    row = lax.broadcasted_iota(jnp.int32, (chunk, 2 * chunk), 0)
    col = lax.broadcasted_iota(jnp.int32, (chunk, 2 * chunk), 1) % chunk
    strict = col < row
    incl = col <= row
    zeros_cv = jnp.zeros((chunk, dh), F32)

    pairs = [(s, h) for s in range(n_sub) for h in range(heads)]
    ah, rh, vh, bk_rem, w_total = {}, {}, {}, {}, {}
    l_abk, m_rbk = {}, {}
    for s in range(n_sub):
        rs = slice(s * chunk, (s + 1) * chunk)
        cum_s, logw_s = cum[rs], logw[rs]
        total = cum_s[chunk - 1:chunk, :]
        e_in = jnp.exp(cum_s)
        e_neg = jnp.exp(-cum_s)
        e_rem = jnp.exp(total - cum_s)
        w_tot = jnp.exp(total)
        r_hat = r_ref[0, rs, :] * e_in
        a_hat = a_ref[0, rs, :] * jnp.exp(cum_s - logw_s)
        b_all, k_all, v_all = b_ref[0, rs, :], k_ref[0, rs, :], v_ref[0, rs, :]
        b_til, k_til = b_all * e_neg, k_all * e_neg
        b_rem, k_rem = b_all * e_rem, k_all * e_rem
        for h in range(heads):
            sl = slice(h * dh, (h + 1) * dh)
            ah[s, h], rh[s, h], vh[s, h] = a_hat[:, sl], r_hat[:, sl], v_all[:, sl]
            w_total[s, h] = w_tot[:, sl]
            bk_rem[s, h] = jnp.concatenate([b_rem[:, sl], k_rem[:, sl]], axis=0)
            bk_til = jnp.concatenate([b_til[:, sl], k_til[:, sl]], axis=0)
            l_abk[s, h] = jnp.where(strict, _bdot(ah[s, h], bk_til, _NT), 0.0)
            m_rbk[s, h] = jnp.where(incl, _bdot(rh[s, h], bk_til, _NT), 0.0)
    z = {p: jnp.concatenate([ah[p], _bdot(l_abk[p], jnp.concatenate([zeros_cv, vh[p]], axis=0), _NN)], axis=1)
         for p in pairs}
    lp = {p: l_abk[p][:, :chunk] for p in pairs}
    for m in range(n_double):
        z = {p: z[p] + _bdot(lp[p], z[p], _NN) for p in pairs}
        if m + 1 < n_double:
            lp = {p: _bdot(lp[p], lp[p], _NN) for p in pairs}
    state = [state_ref[h] for h in range(heads)]
    y_parts = []
    for s in range(n_sub):
        pr_s = [_bdot(jnp.concatenate([z[s, h][:, :dh], rh[s, h]], axis=0), state[h], _NT) for h in range(heads)]
        uv = [jnp.concatenate([pr_s[h][:chunk] + z[s, h][:, dh:], vh[s, h]], axis=0) for h in range(heads)]
        ys = [pr_s[h][chunk:] + _bdot(m_rbk[s, h], uv[h], _NN) for h in range(heads)]
        state = [state[h] * w_total[s, h] + _bdot(uv[h], bk_rem[s, h], _TN) for h in range(heads)]
        y_parts.append(jnp.concatenate(ys, axis=1))
    for h in range(heads):
        state_ref[h] = state[h]
    y = jnp.concatenate(y_parts, axis=0)
    mean = _head_sum(y, dh) * (1.0 / dh)
    yc = y - mean
    var = _head_sum(yc * yc, dh) * (1.0 / dh)
    yn = yc * lax.rsqrt(var + RWKV_GN_EPS) * gn_ref[0:1, :] + gn_ref[1:2, :]
    bonus = _head_sum(r_ref[0] * k_ref[0] * gn_ref[2:3, :], dh) * v_ref[0]
    o_ref[0] = (yn + bonus) * g_ref[0]


def rwkv7_mixer_fused(r, logw, k, v, a, b, g, gn, heads=RWKV_HEADS, dh=RWKV_DH, chunk=RWKV_CHUNK, block=RWKV_BLOCK):
    B_, S_, W_ = r.shape
    block = min(block, S_)
    assert W_ == heads * dh and S_ % block == 0 and block % chunk == 0
    spec = pl.BlockSpec((1, block, W_), lambda bi, ci: (bi, ci, 0))
    return pl.pallas_call(
        functools.partial(_rwkv7_mixer_kernel, heads=heads, dh=dh, chunk=chunk),
        grid=(B_, S_ // block),
        in_specs=[spec] * 7 + [pl.BlockSpec(gn.shape, lambda bi, ci: (0, 0))],
        out_specs=spec,
        out_shape=jax.ShapeDtypeStruct((B_, S_, W_), F32),
        scratch_shapes=[pltpu.VMEM((heads, dh, dh), F32)],
        compiler_params=pltpu.CompilerParams(dimension_semantics=("arbitrary", "arbitrary")),
        name="rwkv7_mixer",
    )(r, logw, k, v, a, b, g, gn)


def _mla_flash_kernel(q_ref, k_ref, v_ref, o_ref, *, tile):
    h = pl.program_id(1)
    qi = pl.program_id(2)
    q = q_ref[0]
    dv = v_ref.shape[2]

    def kv_tile(j):
        start = pl.multiple_of(j * tile, tile)
        return k_ref[0, pl.ds(start, tile), :], v_ref[0, pl.ds(start, tile), :]

    def update(carry, s, v_t):
        m, l, acc = carry
        m_new = jnp.maximum(m, jnp.max(s, axis=-1, keepdims=True))
        corr = jnp.exp(m - m_new)
        p = jnp.exp(s - m_new)
        l_new = corr * l + jnp.sum(p, axis=-1, keepdims=True)
        acc_new = corr * acc + jnp.dot(p.astype(v_t.dtype), v_t, preferred_element_type=F32)
        return m_new, l_new, acc_new

    def body(j, carry):
        k_t, v_t = kv_tile(j)
        s = lax.dot_general(q, k_t, (_NT, ((), ())), preferred_element_type=F32)
        return update(carry, s, v_t)

    init = (jnp.full((tile, 1), -jnp.inf, F32), jnp.zeros((tile, 1), F32), jnp.zeros((tile, dv), F32))
    carry = lax.fori_loop(0, qi, body, init)
    k_t, v_t = kv_tile(qi)
    s = lax.dot_general(q, k_t, (_NT, ((), ())), preferred_element_type=F32)
    row = lax.broadcasted_iota(jnp.int32, (tile, tile), 0)
    col = lax.broadcasted_iota(jnp.int32, (tile, tile), 1)
    s = jnp.where(col <= row, s, -jnp.inf)
    m, l, acc = update(carry, s, v_t)
    lane_head = lax.broadcasted_iota(jnp.int32, (tile, dv), 1) // MLA_DV
    o_ref[0] = jnp.where(lane_head == h % 2, acc / l, 0.0)


def mla_attention(q, k, v, tile=ATTN_TILE):
    B_, S_, _ = q.shape
    tile = min(tile, S_)
    assert S_ % tile == 0 and MLA_HEAD_PAD == 2 * MLA_DV
    return pl.pallas_call(
        functools.partial(_mla_flash_kernel, tile=tile),
        grid=(B_, MLA_HEADS, S_ // tile),
        in_specs=[pl.BlockSpec((1, tile, MLA_HEAD_PAD), lambda b, h, i: (b, i, h)),
                  pl.BlockSpec((1, S_, MLA_HEAD_PAD), lambda b, h, i: (b, 0, h)),
                  pl.BlockSpec((1, S_, 2 * MLA_DV), lambda b, h, i: (b, 0, h // 2))],
        out_specs=pl.BlockSpec((1, tile, MLA_HEAD_PAD), lambda b, h, i: (b, i, h)),
        out_shape=jax.ShapeDtypeStruct((B_, S_, MLA_QK_W), F32),
        compiler_params=pltpu.CompilerParams(dimension_semantics=("arbitrary", "arbitrary", "arbitrary")),
        name="mla_causal_flash",
    )(q, k, v)


def _out_kernel(x_ref, oa_ref, ob_ref, oc_ref, od_ref, wabc_ref, wd_ref, gb_ref, gd_ref, ln_ref, wr_ref, br_ref,
                x1_ref, x1b_ref, route_ref, *, alpha):
    ob = ob_ref[...]
    ob = ob * lax.rsqrt(jnp.mean(ob * ob, -1, keepdims=True) + NORM_EPS) * gb_ref[...]
    od = od_ref[...]
    od = od * lax.rsqrt(jnp.sum(od * od, -1, keepdims=True) * (1.0 / MLA_W) + NORM_EPS) * gd_ref[...]
    W = RET_W
    mix = jnp.dot(oa_ref[...].astype(BF16), wabc_ref[0:W, :], preferred_element_type=F32)
    mix += jnp.dot(ob.astype(BF16), wabc_ref[W:2 * W, :], preferred_element_type=F32)
    mix += jnp.dot(oc_ref[...].astype(BF16), wabc_ref[2 * W:3 * W, :], preferred_element_type=F32)
    mix += jnp.dot(od.astype(BF16), wd_ref[...], preferred_element_type=F32)
    x1 = _layer_norm_rows(alpha * x_ref[...] + mix, ln_ref)
    x1_ref[...] = x1
    x1b = x1.astype(BF16)
    x1b_ref[...] = x1b
    logits = jnp.dot(x1b, wr_ref[...], preferred_element_type=F32) + br_ref[...]
    lane = lax.broadcasted_iota(jnp.int32, logits.shape, 1).astype(F32)
    vals, idxs = [], []
    for _ in range(TOP_K):
        m = jnp.max(logits, -1, keepdims=True)
        idx = jnp.min(jnp.where(logits == m, lane, float(ROUTER_PAD)), -1, keepdims=True)
        vals.append(m)
        idxs.append(idx)
        logits = jnp.where(lane == idx, -jnp.inf, logits)
    es = [jnp.exp(v - vals[0]) for v in vals]
    inv = 1.0 / sum(es)
    route = jnp.zeros_like(logits)
    for r in range(TOP_K):
        route = jnp.where(lane == float(r), es[r] * inv, route)
        route = jnp.where(lane == float(TOP_K + r), idxs[r], route)
    route_ref[...] = route


def fused_out_proj(x, o_a, o_b, o_c, o_d, w_abc, w_d, g_b, g_d, ln_gb, w_router, b_router, alpha, tile=OUT_TILE):
    N, D = x.shape
    tile = min(tile, N)
    n_exp = w_router.shape[1]
    assert N % tile == 0 and TOP_K <= n_exp <= ROUTER_PAD and 2 * TOP_K <= ROUTER_PAD
    wr = jnp.zeros((D, ROUTER_PAD), F32).at[:, :n_exp].set(w_router).astype(BF16)
    br = jnp.full((1, ROUTER_PAD), -jnp.inf, F32).at[0, :n_exp].set(b_router)
    consts = [w_abc, w_d, g_b, g_d, ln_gb, wr, br]
    tok = lambda w: pl.BlockSpec((tile, w), lambda i: (i, 0))
    full = lambda a: pl.BlockSpec(a.shape, lambda i: (0,) * a.ndim)
    x1, x1b, route = pl.pallas_call(
        functools.partial(_out_kernel, alpha=alpha),
        grid=(N // tile,),
        in_specs=[tok(D), tok(RET_W), tok(DIL_W), tok(RWKV_W), tok(MLA_QK_W)] + [full(c) for c in consts],
        out_specs=[tok(D), tok(D), tok(ROUTER_PAD)],
        out_shape=[jax.ShapeDtypeStruct((N, D), F32), jax.ShapeDtypeStruct((N, D), BF16),
                   jax.ShapeDtypeStruct((N, ROUTER_PAD), F32)],
        compiler_params=pltpu.CompilerParams(dimension_semantics=("arbitrary",), vmem_limit_bytes=VMEM_LIMIT),
        name="fused_out_proj",
    )(x, o_a, o_b, o_c, o_d, *consts)
    return x1, x1b, route[:, :TOP_K], route[:, TOP_K:2 * TOP_K].astype(jnp.int32)


def _moe_expert_kernel(tile_e_ref, n_used_ref, x_ref, wg_ref, wl_ref, bg_ref, bl_ref, wd_ref, bd_ref, buf_ref, o_ref):
    del tile_e_ref, buf_ref
    i = pl.program_id(0)

    @pl.when(i < n_used_ref[0])
    def _():
        x = x_ref[...]
        glu = jnp.dot(x, wg_ref[0], preferred_element_type=F32) + bg_ref[0]
        lin = jnp.dot(x, wl_ref[0], preferred_element_type=F32) + bl_ref[0]
        glu = jnp.minimum(glu, SWIGLU_LIMIT)
        lin = jnp.clip(lin, -SWIGLU_LIMIT, SWIGLU_LIMIT)
        act = glu * jax.nn.sigmoid(SWIGLU_ALPHA * glu) * (lin + 1.0)
        y = jnp.dot(act.astype(wd_ref.dtype), wd_ref[0], preferred_element_type=F32) + bd_ref[0]
        o_ref[...] = y.astype(o_ref.dtype)

    @pl.when(i >= n_used_ref[0])
    def _():
        o_ref[...] = jnp.zeros_like(o_ref)


def _deinterleave_kernel(w_ref, even_ref, odd_ref):
    g = 2 * LANES
    i = lax.broadcasted_iota(jnp.int32, (g, g), 0)
    j = lax.broadcasted_iota(jnp.int32, (g, g), 1)
    perm = (i == jnp.where(j < LANES, 2 * j, 2 * (j - LANES) + 1)).astype(BF16)
    for c in range(w_ref.shape[2] // g):
        blk = w_ref[0, :, c * g:(c + 1) * g].astype(BF16)
        sorted_cols = jnp.dot(blk, perm, preferred_element_type=F32)
        even_ref[0, :, c * LANES:(c + 1) * LANES] = sorted_cols[:, :LANES].astype(even_ref.dtype)
        odd_ref[0, :, c * LANES:(c + 1) * LANES] = sorted_cols[:, LANES:].astype(odd_ref.dtype)


def deinterleave_to_bf16(w):
    E, D, F2 = w.shape
    rows = min(DEINT_ROWS, D)
    assert D % rows == 0 and F2 % (2 * LANES) == 0
    out = jax.ShapeDtypeStruct((E, D, F2 // 2), BF16)
    return pl.pallas_call(
        _deinterleave_kernel,
        grid=(E, D // rows),
        in_specs=[pl.BlockSpec((1, rows, F2), lambda e, r: (e, r, 0))],
        out_specs=[pl.BlockSpec((1, rows, F2 // 2), lambda e, r: (e, r, 0))] * 2,
        out_shape=[out, out],
        compiler_params=pltpu.CompilerParams(dimension_semantics=("arbitrary", "arbitrary"), vmem_limit_bytes=VMEM_LIMIT),
        name="deinterleave_cast",
    )(w)


def moe_experts(xs, tile_e, n_used, w_glu, w_lin, b_glu, b_lin, w_dn, b_dn, tile, first_tile, out_buf):
    n_rows, D = xs.shape
    E, _, F = w_glu.shape
    w_spec = lambda shape: pl.BlockSpec((1,) + shape, lambda i, te, nu: (te[i], 0, 0))
    operands = [tile_e, n_used, xs, w_glu, w_lin, b_glu.reshape(E, 1, F), b_lin.reshape(E, 1, F), w_dn, b_dn.reshape(E, 1, D),
                out_buf]
    return pl.pallas_call(
        _moe_expert_kernel,
        grid_spec=pltpu.PrefetchScalarGridSpec(
            num_scalar_prefetch=2,
            grid=(n_rows // tile,),
            in_specs=[pl.BlockSpec((tile, D), lambda i, te, nu: (i, 0)),
                      w_spec((D, F)), w_spec((D, F)), w_spec((1, F)), w_spec((1, F)), w_spec((F, D)), w_spec((1, D)),
                      pl.BlockSpec(memory_space=pl.ANY)],
            out_specs=pl.BlockSpec((tile, D), lambda i, te, nu: (i + first_tile, 0)),
        ),
        out_shape=jax.ShapeDtypeStruct(out_buf.shape, out_buf.dtype),
        input_output_aliases={len(operands) - 1: 0},
        compiler_params=pltpu.CompilerParams(dimension_semantics=("arbitrary",), vmem_limit_bytes=VMEM_LIMIT),
        name="moe_expert_ffn",
    )(*operands)


def _combine_kernel(x_ref, y0_ref, y1_ref, y2_ref, y3_ref, gate_ref, ln_ref, o_ref, *, alpha):
    gate = gate_ref[...]
    y = sum(y_ref[...].astype(F32) * gate[:, s:s + 1] for s, y_ref in enumerate((y0_ref, y1_ref, y2_ref, y3_ref)))
    o_ref[...] = _layer_norm_rows(alpha * x_ref[...] + y, ln_ref)


def moe_combine_ln(x1, ys, gate, ln_gb, alpha, tile=OUT_TILE):
    N, D = x1.shape
    tile = min(tile, N)
    assert N % tile == 0 and len(ys) == TOP_K
    tok = lambda w: pl.BlockSpec((tile, w), lambda i: (i, 0))
    return pl.pallas_call(
        functools.partial(_combine_kernel, alpha=alpha),
        grid=(N // tile,),
        in_specs=[tok(D)] * (1 + TOP_K) + [tok(TOP_K), pl.BlockSpec(ln_gb.shape, lambda i: (0, 0))],
        out_specs=tok(D),
        out_shape=jax.ShapeDtypeStruct((N, D), F32),
        compiler_params=pltpu.CompilerParams(dimension_semantics=("arbitrary",), vmem_limit_bytes=VMEM_LIMIT),
        name="moe_combine_ln",
    )(x1, *ys, gate, ln_gb)


def moe_layer(x1, x1b, gate, top_idx, n_exp, expert_params, expert_base, ln_gb, alpha, row_buf, tile=MOE_TILE):
    n_tok, D = x1.shape
    n_assign = n_tok * TOP_K
    e_flat = top_idx.T.reshape(-1)
    experts = jnp.arange(n_exp, dtype=jnp.int32)
    e_sorted, order = lax.sort((e_flat, jnp.arange(n_assign, dtype=jnp.int32)), num_keys=1)
    counts = jnp.sum((e_flat[:, None] == experts[None, :]).astype(jnp.int32), axis=0)
    padded = (counts + tile - 1) // tile * tile
    start = jnp.cumsum(counts) - counts
    pend = jnp.cumsum(padded)
    pstart = pend - padded
    n_tiles = (n_assign + n_exp * (tile - 1) + tile - 1) // tile
    tile_first = jnp.arange(n_tiles, dtype=jnp.int32) * tile
    tile_e = jnp.minimum(jnp.sum((pend[None, :] <= tile_first[:, None]).astype(jnp.int32), axis=1), n_exp - 1)
    n_used = (pend[-1] // tile).astype(jnp.int32).reshape(1)
    within = (tile_first - pstart[tile_e])[:, None] + jnp.arange(tile, dtype=jnp.int32)[None, :]
    valid = within < counts[tile_e][:, None]
    sorted_idx = jnp.clip(start[tile_e][:, None] + within, 0, n_assign - 1)
    spread = (tile_first[:, None] + jnp.arange(tile, dtype=jnp.int32)[None, :]) % n_tok
    row_tok = jnp.where(valid, order[sorted_idx] % n_tok, spread).reshape(-1)
    shift = jnp.sum(jnp.where(e_sorted[:, None] == experts[None, :], (pstart - start)[None, :], 0), axis=1)
    dest = jnp.arange(n_assign, dtype=jnp.int32) + shift
    _, pos = lax.sort((order, dest), num_keys=1)
    pos = pos.reshape(TOP_K, n_tok)
    n_chunks = MOE_GATHER_CHUNKS if n_tiles % MOE_GATHER_CHUNKS == 0 else 1
    tpc = n_tiles // n_chunks
    row_tok = row_tok.reshape(n_chunks, tpc * tile)
    tile_e = (tile_e + expert_base).reshape(n_chunks, tpc)
    yb = row_buf
    for c in range(n_chunks):
        yb = moe_experts(x1b[row_tok[c]], tile_e[c], jnp.clip(n_used - c * tpc, 0, tpc), *expert_params, tile, c * tpc, yb)
    return moe_combine_ln(x1, [yb[pos[s]] for s in range(TOP_K)], gate, ln_gb, alpha), yb


def moe_row_buffer(n_tok, D, n_exp, tile=MOE_TILE):
    n_tiles = (n_tok * TOP_K + n_exp * (tile - 1) + tile - 1) // tile
    return jnp.zeros((n_tiles * tile, D), BF16)


def prep_expert_params(exp_w_gu, exp_b_gu, exp_w_dn, exp_b_dn):
    L, E, D, F2 = exp_w_gu.shape
    w_glu, w_lin = deinterleave_to_bf16(exp_w_gu.reshape(L * E, D, F2))
    b_gu = exp_b_gu.reshape(L * E, F2)
    return (w_glu, w_lin, b_gu[:, 0::2], b_gu[:, 1::2], exp_w_dn.astype(BF16).reshape(L * E, F2 // 2, D),
            exp_b_dn.reshape(L * E, D))


def kernel(x, w_in, w_out, ret_norm_g, dil_norm_g, rwkv_mu, rwkv_w0, rwkv_w_up, rwkv_a0, rwkv_a_up, rwkv_g_up, rwkv_k_k, rwkv_k_a, rwkv_r_k, rwkv_ln_g, rwkv_ln_b, rwkv_vres_down, rwkv_vres_mu, rwkv_v0, rwkv_v_up, mla_q_norm_g, mla_w_q_up, mla_kv_norm_g, mla_w_kv_up, mla_out_norm_g, ln1_g, ln1_b, router_w, router_b, exp_w_gu, exp_b_gu, exp_w_dn, exp_b_dn, ln2_g, ln2_b):
    depth = w_in.shape[0]
    B_, S_, D = x.shape
    alpha = (2 * depth) ** 0.25
    tables = rope_tables(S_)
    ret_tabs = retention_tables()
    expert_params = prep_expert_params(exp_w_gu, exp_b_gu, exp_w_dn, exp_b_dn)
    n_exp = exp_w_gu.shape[1]
    v_first = None
    n_tok = B_ * S_
    row_buf = moe_row_buffer(n_tok, D, n_exp)
    flat = lambda t: t.reshape(n_tok, t.shape[-1])
    for l in range(depth):
        w_cat = prep_in_weights(w_in[l], rwkv_vres_down[l - 1] if l > 0 else None)
        rw_params = prep_rwkv_params(rwkv_mu[l], rwkv_w0[l], rwkv_w_up[l], rwkv_a0[l], rwkv_a_up[l], rwkv_g_up[l],
                                     rwkv_k_k[l], rwkv_k_a[l], rwkv_vres_mu[l - 1] if l > 0 else None,
                                     rwkv_v0[l - 1] if l > 0 else None, rwkv_v_up[l - 1] if l > 0 else None)
        mla_params = prep_mla_params(mla_q_norm_g[l], mla_w_q_up[l], mla_kv_norm_g[l], mla_w_kv_up[l])
        gn = jnp.stack([rwkv_ln_g[l], rwkv_ln_b[l], rwkv_r_k[l].reshape(-1)])
        w_abc, w_d, g_d = prep_out_weights(w_out[l], mla_out_norm_g[l])
        (ret_q, ret_k, ret_v, ret_g, dil_q, dil_k, dil_v, rw_r, rw_lw, rw_k, rw_v, rw_a, rw_b, rw_g,
         mla_q, mla_k, mla_v) = fused_in_proj(x, w_cat, tables, rw_params, mla_params, v_first)
        if l == 0:
            v_first = rw_v
        o_a = retention_fused(ret_q, ret_k, ret_v, ret_g, ret_norm_g[l], ret_tabs)
        o_b = dilated_fused(dil_q, dil_k, dil_v)
        o_c = rwkv7_mixer_fused(rw_r, rw_lw, rw_k, rw_v, rw_a, rw_b, rw_g, gn)
        o_d = mla_attention(mla_q, mla_k, mla_v)
        x1, x1b, gate, top_idx = fused_out_proj(flat(x), flat(o_a), flat(o_b), flat(o_c), flat(o_d), w_abc, w_d,
                                                dil_norm_g[l].reshape(1, DIL_W), g_d, jnp.stack([ln1_g[l], ln1_b[l]]),
                                                router_w[l], router_b[l], alpha)
        x, row_buf = moe_layer(x1, x1b, gate, top_idx, n_exp, expert_params, l * n_exp,
                               jnp.stack([ln2_g[l], ln2_b[l]]), alpha, row_buf)
        x = x.reshape(B_, S_, D)
    return x
```

```python
import functools

import numpy as np
import jax
import jax.numpy as jnp
from jax import lax
from jax.experimental import pallas as pl
from jax.experimental.pallas import tpu as pltpu

F32 = jnp.float32
BF16 = jnp.bfloat16

LANES = 128
BLK = 128
LN_EPS = 1e-5
NORM_EPS = 1e-6

RET_HEADS, RET_DK, RET_DV = 4, 32, 64
RET_QK = RET_HEADS * RET_DK
RET_W = RET_HEADS * RET_DV
RET_THETA = 10000.0
DIL_HEADS, DIL_DH = 4, 64
DIL_W = DIL_HEADS * DIL_DH
DIL_PATTERNS = ((128, 1), (512, 4), (2048, 16))
ROPE_THETA = 500000.0
ROPE_ROT_DIM = DIL_DH // 4
RWKV_HEADS, RWKV_DH = 4, 64
RWKV_W = RWKV_HEADS * RWKV_DH
DECAY_LORA, AAA_LORA, MV_LORA, GATE_LORA = 64, 64, 32, 128
RWKV_GN_EPS = 64e-5
MLA_HEADS, MLA_NOPE, MLA_ROPE, MLA_DV = 4, 64, 32, 64
MLA_W = MLA_HEADS * MLA_DV
Q_LORA, KV_LORA = 256, 128
MLA_THETA = 10000.0
TOP_K = 4
SWIGLU_LIMIT, SWIGLU_ALPHA = 7.0, 1.702

RET_SPLITS = (RET_QK, RET_QK, RET_W, RET_W)
DIL_SPLITS = (DIL_W, DIL_W, DIL_W)
RWKV_SPLITS = (RWKV_W, RWKV_W, RWKV_W, DECAY_LORA, AAA_LORA, GATE_LORA)
MLA_SPLITS = (Q_LORA, KV_LORA, MLA_ROPE)
A_END = sum(RET_SPLITS)
B_END = A_END + sum(DIL_SPLITS)
C_END = B_END + sum(RWKV_SPLITS)
N_IN = C_END + sum(MLA_SPLITS)

RWKV_CHUNK = 64
RWKV_BLOCK = 512

MLA_HEAD_PAD = 128
MLA_QK_W = MLA_HEADS * MLA_HEAD_PAD
ATTN_TILE = 512

IN_RET_W = 4 * RET_QK + 2 * RET_W
IN_DIL_W = 5 * DIL_W
IN_RWKV_W = sum(RWKV_SPLITS) + LANES
IN_MLA_W = Q_LORA + KV_LORA + LANES
IN_COLS = IN_RET_W + IN_DIL_W + IN_RWKV_W + IN_MLA_W
IN_TILE = 256
OUT_TILE = 512
ROUTER_PAD = 128
DIL_PAIR_W = 2 * DIL_DH
DEINT_ROWS = 1024
MOE_TILE = 512
DIL_UNROLL = 4
MOE_GATHER_CHUNKS = 4
VMEM_LIMIT = 48 * 1024 * 1024
DIL_VMEM_LIMIT = 56 * 1024 * 1024


def split_cols(p, sizes):
    idx = np.cumsum(sizes)[:-1].tolist()
    return jnp.split(p, idx, axis=-1)


def rope_table(n_pos, rot_dim, theta):
    inv_freq = 1.0 / (theta ** (jnp.arange(0, rot_dim, 2, dtype=F32) / rot_dim))
    ang = jnp.arange(n_pos, dtype=F32)[:, None] * inv_freq[None, :]
    return jnp.cos(ang), jnp.sin(ang)


def _bdot(a, b, dims):
    return lax.dot_general(a.astype(BF16), b.astype(BF16), ((dims[0], dims[1]), ((), ())), preferred_element_type=F32)


_NN = ((1,), (0,))
_NT = ((1,), (1,))
_TN = ((0,), (0,))


def _split3(x):
    h1 = x.astype(BF16)
    r1 = x - h1.astype(F32)
    h2 = r1.astype(BF16)
    h3 = (r1 - h2.astype(F32)).astype(BF16)
    return h1, h2, h3


def _head_sum(x, dh):
    w = x.shape[1]
    i = lax.broadcasted_iota(jnp.int32, (w, w), 0) // dh
    j = lax.broadcasted_iota(jnp.int32, (w, w), 1) // dh
    ones = (i == j).astype(BF16)
    return sum(jnp.dot(p, ones, preferred_element_type=F32) for p in _split3(x))


def _layer_norm_rows(h, ln_ref):
    mu = jnp.mean(h, -1, keepdims=True)
    hc = h - mu
    var = jnp.mean(hc * hc, -1, keepdims=True)
    return hc * lax.rsqrt(var + LN_EPS) * ln_ref[0:1, :] + ln_ref[1:2, :]


def _rot_half_cols(w, heads, dh, rot):
    d_in = w.shape[0]
    w = w.reshape(d_in, heads, dh)
    half = rot // 2
    sw = jnp.concatenate([-w[..., half:rot], w[..., :half], jnp.zeros((d_in, heads, dh - rot), w.dtype)], axis=-1)
    return sw.reshape(d_in, heads * dh)


def _rope_lanes(cs, heads, dh, rot, lead=0):
    cos, sin = cs
    S_ = cos.shape[0]
    c = jnp.concatenate([jnp.ones((S_, lead), F32), cos, cos, jnp.ones((S_, dh - lead - rot), F32)], axis=1)
    s = jnp.concatenate([jnp.zeros((S_, lead), F32), sin, sin, jnp.zeros((S_, dh - lead - rot), F32)], axis=1)
    return jnp.tile(c, (1, heads)), jnp.tile(s, (1, heads))


def rope_tables(S_):
    ret = _rope_lanes(rope_table(S_, RET_DK, RET_THETA), RET_HEADS, RET_DK, RET_DK)
    dil = _rope_lanes(rope_table(S_, ROPE_ROT_DIM, ROPE_THETA), DIL_HEADS, DIL_DH, ROPE_ROT_DIM)
    cos, sin = rope_table(S_, MLA_ROPE, MLA_THETA)
    mq = _rope_lanes((cos, sin), MLA_HEADS, MLA_HEAD_PAD, MLA_ROPE, lead=MLA_NOPE)
    mk = jnp.concatenate([cos, cos, sin, sin, jnp.zeros((S_, LANES - 2 * MLA_ROPE), F32)], axis=1)
    return ret + dil + mq + (mk,)


def prep_in_weights(w_in_l, vres_down_l):
    D = w_in_l.shape[0]
    a_q, a_k, a_v, a_g = split_cols(w_in_l[:, :A_END], RET_SPLITS)
    b_q, b_k, b_v = split_cols(w_in_l[:, A_END:B_END], DIL_SPLITS)
    d_cq, d_ckv, d_kr = split_cols(w_in_l[:, C_END:N_IN], MLA_SPLITS)
    vres = jnp.zeros((D, LANES), F32)
    if vres_down_l is not None:
        vres = vres.at[:, :MV_LORA].set(vres_down_l)
    cols = [a_q, _rot_half_cols(a_q, RET_HEADS, RET_DK, RET_DK), a_k, _rot_half_cols(a_k, RET_HEADS, RET_DK, RET_DK), a_v, a_g,
            b_q, _rot_half_cols(b_q, DIL_HEADS, DIL_DH, ROPE_ROT_DIM), b_k, _rot_half_cols(b_k, DIL_HEADS, DIL_DH, ROPE_ROT_DIM), b_v,
            w_in_l[:, B_END:C_END], vres,
            d_cq, d_ckv, d_kr, _rot_half_cols(d_kr, 1, MLA_ROPE, MLA_ROPE), jnp.zeros((D, LANES - 2 * MLA_ROPE), F32)]
    w = jnp.concatenate(cols, axis=1)
    assert w.shape[1] == IN_COLS
    return w.astype(BF16)


def prep_rwkv_params(mu, w0, w_up, a0, a_up, g_up, k_k, k_a, vres_mu, v0, v_up):
    W = RWKV_W
    mu_ext = jnp.zeros((1, IN_RWKV_W), F32).at[0, :mu.shape[0]].set(mu)
    v_up_ext = jnp.zeros((LANES, W), F32)
    v0_ext = jnp.zeros((1, W), F32)
    if vres_mu is not None:
        mu_ext = mu_ext.at[0, mu.shape[0]:mu.shape[0] + MV_LORA].set(vres_mu)
        v_up_ext = v_up_ext.at[:MV_LORA].set(v_up)
        v0_ext = v0.reshape(1, W)
    wa_up = jnp.zeros((LANES, 2 * W), F32).at[:DECAY_LORA, :W].set(w_up).at[DECAY_LORA:, W:].set(a_up)
    return [mu_ext, wa_up.astype(BF16), jnp.concatenate([w0, a0]).reshape(1, 2 * W), g_up.astype(BF16),
            jnp.stack([k_k, k_a]), v_up_ext.astype(BF16), v0_ext]


def prep_mla_params(q_norm_g, w_q_up, kv_norm_g, w_kv_up):
    ql = w_q_up.shape[0]
    wq = w_q_up.reshape(ql, MLA_HEADS, MLA_NOPE + MLA_ROPE)
    pad = jnp.zeros((ql, MLA_HEADS, MLA_HEAD_PAD - MLA_NOPE - MLA_ROPE), F32)
    q_main = jnp.concatenate([wq, pad], axis=-1).reshape(ql, MLA_QK_W)
    rope = wq[..., MLA_NOPE:]
    half = MLA_ROPE // 2
    q_rot = jnp.concatenate([jnp.zeros((ql, MLA_HEADS, MLA_NOPE), F32), -rope[..., half:], rope[..., :half], pad], axis=-1)
    w_q = jnp.concatenate([q_main, q_rot.reshape(ql, MLA_QK_W)], axis=1).astype(BF16)
    kl = w_kv_up.shape[0]
    wkv = w_kv_up.reshape(kl, MLA_HEADS, MLA_NOPE + MLA_DV)
    k_main = jnp.concatenate([wkv[..., :MLA_NOPE], jnp.zeros((kl, MLA_HEADS, MLA_HEAD_PAD - MLA_NOPE), F32)], axis=-1)
    w_kv = jnp.concatenate([k_main.reshape(kl, MLA_QK_W), wkv[..., MLA_NOPE:].reshape(kl, MLA_W)], axis=1).astype(BF16)
    return [q_norm_g.reshape(1, Q_LORA), w_q, kv_norm_g.reshape(1, KV_LORA), w_kv]


def prep_out_weights(w_out_l, mla_out_norm_g):
    W = RET_W
    w_abc = w_out_l[:3 * W].astype(BF16)
    w_d = w_out_l[3 * W:].reshape(MLA_HEADS, MLA_DV, -1)
    g_d = mla_out_norm_g.reshape(MLA_HEADS, MLA_DV)
    zw = jnp.zeros_like(w_d[0])
    zg = jnp.zeros_like(g_d[0])
    rows, gains = [], []
    for h in range(MLA_HEADS):
        rows += [w_d[h], zw] if h % 2 == 0 else [zw, w_d[h]]
        gains += [g_d[h], zg] if h % 2 == 0 else [zg, g_d[h]]
    return w_abc, jnp.concatenate(rows, axis=0).astype(BF16), jnp.concatenate(gains).reshape(1, MLA_QK_W)


def retention_tables():
    log_gamma = jnp.log(1.0 - 2.0 ** (-5.0 - jnp.arange(RET_HEADS, dtype=F32)))
    idx = jnp.arange(BLK, dtype=F32)
    dist = idx[:, None] - idx[None, :]
    inner = jnp.where(dist >= 0, jnp.exp(jnp.maximum(dist, 0.0)[None] * log_gamma[:, None, None]), 0.0)
    zeta = jnp.exp((BLK - 1 - idx)[None, :] * log_gamma[:, None])
    xi = jnp.exp((idx + 1.0)[None, :] * log_gamma[:, None])
    zeta_k = jnp.repeat(zeta.T, RET_DK, axis=1)
    xi_v = jnp.repeat(xi.T, RET_DV, axis=1)
    row_h = jnp.arange(RET_QK) // RET_DK
    col_h = jnp.arange(RET_W) // RET_DV
    same = row_h[:, None] == col_h[None, :]
    state_decay = jnp.where(same, jnp.exp(BLK * log_gamma)[row_h][:, None], 0.0)
    return inner, zeta_k, xi_v, state_decay, same.astype(F32)


def _in_kernel(x_ref, w_ref, cr_ref, sr_ref, cd_ref, sd_ref, cq_ref, sq_ref, ck_ref,
               mu_ref, wa_up_ref, w0a0_ref, g_up_ref, kk_ka_ref, v_up_ref, v0_ref, vfirst_ref,
               qn_ref, wq_ref, kvn_ref, wkv_ref,
               ret_q, ret_k, ret_v, ret_g, dil_q, dil_k, dil_v,
               rw_r, rw_lw, rw_k, rw_v, rw_a, rw_b, rw_g, mla_q, mla_k, mla_v,
               carry_ref, *, has_vres):
    tm = x_ref.shape[1]

    @pl.when(pl.program_id(1) == 0)
    def _():
        carry_ref[...] = jnp.zeros_like(carry_ref)

    xb = x_ref[0].astype(BF16)
    o1, o2, o3 = IN_RET_W, IN_RET_W + IN_DIL_W, IN_RET_W + IN_DIL_W + IN_RWKV_W
    pa = jnp.dot(xb, w_ref[:, :o1], preferred_element_type=F32)
    pb = jnp.dot(xb, w_ref[:, o1:o2], preferred_element_type=F32)
    pc = jnp.dot(xb, w_ref[:, o2:o3], preferred_element_type=F32)
    pd = jnp.dot(xb, w_ref[:, o3:], preferred_element_type=F32)

    qk = RET_QK
    cr, sr = cr_ref[...], sr_ref[...]
    ret_q[0] = (pa[:, 0:qk] * cr + pa[:, qk:2 * qk] * sr).astype(ret_q.dtype)
    ret_k[0] = ((pa[:, 2 * qk:3 * qk] * cr + pa[:, 3 * qk:4 * qk] * sr) * (RET_DK ** -0.5)).astype(ret_k.dtype)
    ret_v[0] = pa[:, 4 * qk:4 * qk + RET_W].astype(ret_v.dtype)
    ret_g[0] = pa[:, 4 * qk + RET_W:]

    cd, sd = cd_ref[...], sd_ref[...]
    dil_q[0] = pb[:, 0:DIL_W] * cd + pb[:, DIL_W:2 * DIL_W] * sd
    dil_k[0] = pb[:, 2 * DIL_W:3 * DIL_W] * cd + pb[:, 3 * DIL_W:4 * DIL_W] * sd
    dil_v[0] = pb[:, 4 * DIL_W:]

    row = lax.broadcasted_iota(jnp.int32, pc.shape, 0)
    prev = jnp.where(row == 0, carry_ref[...], pltpu.roll(pc, 1, 0))
    carry_ref[...] = pc[tm - 1:tm, :]
    ps = pc + (prev - pc) * mu_ref[...]
    W = RWKV_W
    r, k, v = ps[:, 0:W], ps[:, W:2 * W], ps[:, 2 * W:3 * W]
    wd_ad = ps[:, 3 * W:3 * W + LANES]
    lane = lax.broadcasted_iota(jnp.int32, wd_ad.shape, 1)
    lora_in = jnp.where(lane < DECAY_LORA, jnp.tanh(wd_ad), wd_ad)
    lora = jnp.dot(lora_in.astype(BF16), wa_up_ref[...], preferred_element_type=F32) + w0a0_ref[...]
    w_raw = -jax.nn.softplus(-lora[:, :W]) - 0.5
    a_sig = jax.nn.sigmoid(lora[:, W:])
    gd = ps[:, 3 * W + LANES:3 * W + 2 * LANES]
    rw_g[0] = jnp.dot(jax.nn.sigmoid(gd).astype(BF16), g_up_ref[...], preferred_element_type=F32)
    kk = k * kk_ka_ref[0:1, :]
    kk = kk / jnp.maximum(jnp.sqrt(_head_sum(kk * kk, RWKV_DH)), 1e-12)
    if has_vres:
        vd = ps[:, 3 * W + 2 * LANES:]
        mix = jax.nn.sigmoid(jnp.dot(vd.astype(BF16), v_up_ref[...], preferred_element_type=F32) + v0_ref[...])
        v = v + (vfirst_ref[0] - v) * mix
    rw_r[0] = r
    rw_lw[0] = -jnp.exp(w_raw)
    rw_k[0] = k * (1.0 + (a_sig - 1.0) * kk_ka_ref[1:2, :])
    rw_v[0] = v
    rw_a[0] = -kk
    rw_b[0] = kk * a_sig

    c_q = pd[:, :Q_LORA]
    c_q = c_q * lax.rsqrt(jnp.mean(c_q * c_q, -1, keepdims=True) + NORM_EPS) * qn_ref[...]
    q2 = jnp.dot(c_q.astype(BF16), wq_ref[...], preferred_element_type=F32)
    scale = (MLA_NOPE + MLA_ROPE) ** -0.5
    mla_q[0] = ((q2[:, :MLA_QK_W] * cq_ref[...] + q2[:, MLA_QK_W:] * sq_ref[...]) * scale).astype(mla_q.dtype)
    c_kv = pd[:, Q_LORA:Q_LORA + KV_LORA]
    c_kv = c_kv * lax.rsqrt(jnp.mean(c_kv * c_kv, -1, keepdims=True) + NORM_EPS) * kvn_ref[...]
    kv = jnp.dot(c_kv.astype(BF16), wkv_ref[...], preferred_element_type=F32)
    kr = pd[:, Q_LORA + KV_LORA:] * ck_ref[...]
    i = lax.broadcasted_iota(jnp.int32, (LANES, MLA_QK_W), 0)
    j = lax.broadcasted_iota(jnp.int32, (LANES, MLA_QK_W), 1) % MLA_HEAD_PAD
    place = ((i < 2 * MLA_ROPE) & (j == MLA_NOPE + i % MLA_ROPE)).astype(BF16)
    k_pe = sum(jnp.dot(p, place, preferred_element_type=F32) for p in _split3(kr)[:2])
    mla_k[0] = (kv[:, :MLA_QK_W] + k_pe).astype(mla_k.dtype)
    mla_v[0] = kv[:, MLA_QK_W:].astype(mla_v.dtype)


def fused_in_proj(x, w_cat, tables, rw_params, mla_params, v_first, tile=IN_TILE):
    B_, S_, D = x.shape
    tile = min(tile, S_)
    assert S_ % tile == 0
    has_vres = v_first is not None
    tok = lambda w: pl.BlockSpec((1, tile, w), lambda b, i: (b, i, 0))
    pos = lambda a: pl.BlockSpec((tile, a.shape[1]), lambda b, i: (i, 0))
    full = lambda a: pl.BlockSpec(a.shape, lambda b, i: (0,) * a.ndim)
    if has_vres:
        vf_spec = tok(RWKV_W)
    else:
        v_first = jnp.zeros((1, tile, RWKV_W), F32)
        vf_spec = pl.BlockSpec((1, tile, RWKV_W), lambda b, i: (0, 0, 0))
    f32o = lambda w: jax.ShapeDtypeStruct((B_, S_, w), F32)
    bfo = lambda w: jax.ShapeDtypeStruct((B_, S_, w), BF16)
    out_shape = ([bfo(RET_QK), bfo(RET_QK), bfo(RET_W), f32o(RET_W)] + [f32o(DIL_W)] * 3 + [f32o(RWKV_W)] * 7
                 + [bfo(MLA_QK_W), bfo(MLA_QK_W), bfo(MLA_W)])
    return pl.pallas_call(
        functools.partial(_in_kernel, has_vres=has_vres),
        grid=(B_, S_ // tile),
        in_specs=[tok(D), full(w_cat)] + [pos(t) for t in tables] + [full(p) for p in rw_params] + [vf_spec]
                 + [full(p) for p in mla_params],
        out_specs=[tok(s.shape[-1]) for s in out_shape],
        out_shape=out_shape,
        scratch_shapes=[pltpu.VMEM((1, IN_RWKV_W), F32)],
        compiler_params=pltpu.CompilerParams(dimension_semantics=("arbitrary", "arbitrary"), vmem_limit_bytes=VMEM_LIMIT),
        name="fused_in_proj",
    )(x, w_cat, *tables, *rw_params, v_first, *mla_params)


def _retention_kernel(q_ref, k_ref, v_ref, g_ref, ng_ref, inner_ref, zeta_ref, xi_ref, sdec_ref, smask_ref, o_ref, state_ref):
    @pl.when(pl.program_id(1) == 0)
    def _():
        state_ref[...] = jnp.zeros_like(state_ref)

    q, k, v = q_ref[0], k_ref[0], v_ref[0]
    q_head = lax.broadcasted_iota(jnp.int32, q.shape, 1) // RET_DK
    v_head = lax.broadcasted_iota(jnp.int32, v.shape, 1) // RET_DV
    zero_q = jnp.zeros_like(q)
    scores = [_bdot(jnp.where(q_head == h, q, zero_q), k, _NT) * inner_ref[h] for h in range(RET_HEADS)]
    o_heads = [_bdot(s, v, _NN) for s in scores]
    o = _bdot(q, state_ref[...], _NN) * xi_ref[...]
    for h in range(RET_HEADS):
        o = o + jnp.where(v_head == h, o_heads[h], 0.0)
    state_ref[...] = state_ref[...] * sdec_ref[...] + _bdot(k.astype(F32) * zeta_ref[...], v, _TN) * smask_ref[...]
    o = o * lax.rsqrt(_head_sum(o * o, RET_DV) * (1.0 / RET_DV) + NORM_EPS) * ng_ref[...]
    g = g_ref[0]
    o_ref[0] = g * jax.nn.sigmoid(g) * o


def retention_fused(q, k, v, g, norm_g, tables):
    B_, S_, _ = q.shape
    assert S_ % BLK == 0
    tok = lambda w: pl.BlockSpec((1, BLK, w), lambda b, c: (b, c, 0))
    full = lambda a: pl.BlockSpec(a.shape, lambda b, c: (0,) * a.ndim)
    ng = norm_g.reshape(1, RET_W)
    return pl.pallas_call(
        _retention_kernel,
        grid=(B_, S_ // BLK),
        in_specs=[tok(RET_QK), tok(RET_QK), tok(RET_W), tok(RET_W), full(ng)] + [full(t) for t in tables],
        out_specs=tok(RET_W),
        out_shape=jax.ShapeDtypeStruct((B_, S_, RET_W), F32),
        scratch_shapes=[pltpu.VMEM((RET_QK, RET_W), F32)],
        compiler_params=pltpu.CompilerParams(dimension_semantics=("arbitrary", "arbitrary")),
        name="retention_chunk",
    )(q, k, v, g, ng, *tables)


def _dilated_kernel(q_ref, k_ref, v_ref, o_ref, m_ref, l_ref, *, dils):
    S_ = q_ref.shape[1]
    lane_head = lax.broadcasted_iota(jnp.int32, (BLK, DIL_PAIR_W), 1) // DIL_DH
    i_idx = lax.broadcasted_iota(jnp.int32, (BLK, 2 * BLK), 0)
    j_idx = lax.broadcasted_iota(jnp.int32, (BLK, 2 * BLK), 1)
    scale = DIL_DH ** -0.5

    def rows(start, n, d):
        return pl.ds(start, n) if d == 1 else pl.ds(start, n, stride=d)

    n_un = DIL_UNROLL
    for pi, d in enumerate(dils):
        def tiles(tt, carry, d=d, first=(pi == 0)):
            nb = [(tt * n_un + u) // d for u in range(n_un)]
            r = [(tt * n_un + u) % d for u in range(n_un)]
            kb = [jnp.maximum(n - 1, 0) for n in nb]
            q_rows = [rows(nb[u] * (BLK * d) + r[u], BLK, d) for u in range(n_un)]
            k_rows = [rows(kb[u] * (BLK * d) + r[u], 2 * BLK, d) for u in range(n_un)]
            q = [q_ref[0, q_rows[u], :] for u in range(n_un)]
            kk = [k_ref[0, k_rows[u], :].astype(BF16) for u in range(n_un)]
            vv = [v_ref[0, k_rows[u], :].astype(BF16) for u in range(n_un)]
            if not first:
                m_old = [m_ref[q_rows[u], :] for u in range(n_un)]
                l_old = [l_ref[q_rows[u], :] for u in range(n_un)]
                o_old = [o_ref[0, q_rows[u], :] for u in range(n_un)]
            delta = [(nb[u] - kb[u]) * BLK + i_idx - j_idx for u in range(n_un)]
            valid = [(dl >= 0) & (dl <= BLK) for dl in delta]
            pairs = [(u, h) for u in range(n_un) for h in range(2)]
            s = {(u, h): jnp.where(valid[u], _bdot(jnp.where(lane_head == h, q[u], jnp.zeros_like(q[u])), kk[u], _NT) * scale,
                                   -jnp.inf) for u, h in pairs}
            m = {p: jnp.max(s[p], axis=-1, keepdims=True) for p in pairs}
            pr = {p: jnp.exp(s[p] - m[p]) for p in pairs}
            l = {p: jnp.sum(pr[p], axis=-1, keepdims=True) for p in pairs}
            pv = {(u, h): jnp.dot(pr[u, h].astype(BF16), vv[u], preferred_element_type=F32) for u, h in pairs}
            for u in range(n_un):
                m_t = jnp.where(lane_head == 0, m[u, 0], m[u, 1])
                l_t = jnp.where(lane_head == 0, l[u, 0], l[u, 1])
                pv_t = jnp.where(lane_head == 0, pv[u, 0], pv[u, 1])
                if first:
                    m_ref[q_rows[u], :] = m_t
                    l_ref[q_rows[u], :] = l_t
                    o_ref[0, q_rows[u], :] = pv_t
                else:
                    m_new = jnp.maximum(m_old[u], m_t)
                    c_old = jnp.exp(m_old[u] - m_new)
                    c_t = jnp.exp(m_t - m_new)
                    m_ref[q_rows[u], :] = m_new
                    l_ref[q_rows[u], :] = l_old[u] * c_old + l_t * c_t
                    o_ref[0, q_rows[u], :] = o_old[u] * c_old + pv_t * c_t
            return carry

        lax.fori_loop(0, S_ // (BLK * n_un), tiles, 0)

    def finish(t, carry):
        r0 = pl.multiple_of(t * BLK, BLK)
        o_ref[0, pl.ds(r0, BLK), :] = o_ref[0, pl.ds(r0, BLK), :] / l_ref[pl.ds(r0, BLK), :]
        return carry

    lax.fori_loop(0, S_ // BLK, finish, 0)


def dilated_fused(q, k, v):
    B_, S_, W_ = q.shape
    dils = tuple(d for _, d in DIL_PATTERNS)
    assert all(w // d == BLK for w, d in DIL_PATTERNS) and S_ % (2 * BLK * max(dils)) == 0 and S_ % (BLK * DIL_UNROLL) == 0
    spec = pl.BlockSpec((1, S_, DIL_PAIR_W), lambda b, hp: (b, 0, hp))
    return pl.pallas_call(
        functools.partial(_dilated_kernel, dils=dils),
        grid=(B_, W_ // DIL_PAIR_W),
        in_specs=[spec] * 3,
        out_specs=spec,
        out_shape=jax.ShapeDtypeStruct((B_, S_, W_), F32),
        scratch_shapes=[pltpu.VMEM((S_, DIL_PAIR_W), F32)] * 2,
        compiler_params=pltpu.CompilerParams(dimension_semantics=("arbitrary", "arbitrary"), vmem_limit_bytes=DIL_VMEM_LIMIT),
        name="dilated_attention",
    )(q, k, v)


def _rwkv7_mixer_kernel(r_ref, lw_ref, k_ref, v_ref, a_ref, b_ref, g_ref, gn_ref, o_ref, state_ref, *, heads, dh, chunk):
    rows = r_ref.shape[1]
    n_sub = rows // chunk
    n_double = int(np.log2(chunk))
    assert 2 ** n_double == chunk and n_sub * chunk == rows

    @pl.when(pl.program_id(1) == 0)
    def _():
        state_ref[...] = jnp.zeros_like(state_ref)

    ri = lax.broadcasted_iota(jnp.int32, (rows, rows), 0)
    ci = lax.broadcasted_iota(jnp.int32, (rows, rows), 1)
    tri = ((ci <= ri) & (ci >= (ri // chunk) * chunk)).astype(BF16)
    logw = lw_ref[0]
    cum = sum(jnp.dot(tri, piece, preferred_element_type=F32) for piece in _split3(logw))

    row = lax.broadcasted_iota(jnp.int32, (chunk, 2 * chunk), 0)
    col = lax.broadcasted_iota(jnp.int32, (chunk, 2 * chunk), 1) % chunk
    strict = col < row
    incl = col <= row
    zeros_cv = jnp.zeros((chunk, dh), F32)

    pairs = [(s, h) for s in range(n_sub) for h in range(heads)]
    ah, rh, vh, bk_rem, w_total = {}, {}, {}, {}, {}
    l_abk, m_rbk = {}, {}
    for s in range(n_sub):
        rs = slice(s * chunk, (s + 1) * chunk)
        cum_s, logw_s = cum[rs], logw[rs]
        total = cum_s[chunk - 1:chunk, :]
        e_in = jnp.exp(cum_s)
        e_neg = jnp.exp(-cum_s)
        e_rem = jnp.exp(total - cum_s)
        w_tot = jnp.exp(total)
        r_hat = r_ref[0, rs, :] * e_in
        a_hat = a_ref[0, rs, :] * jnp.exp(cum_s - logw_s)
        b_all, k_all, v_all = b_ref[0, rs, :], k_ref[0, rs, :], v_ref[0, rs, :]
        b_til, k_til = b_all * e_neg, k_all * e_neg
        b_rem, k_rem = b_all * e_rem, k_all * e_rem
        for h in range(heads):
            sl = slice(h * dh, (h + 1) * dh)
            ah[s, h], rh[s, h], vh[s, h] = a_hat[:, sl], r_hat[:, sl], v_all[:, sl]
            w_total[s, h] = w_tot[:, sl]
            bk_rem[s, h] = jnp.concatenate([b_rem[:, sl], k_rem[:, sl]], axis=0)
            bk_til = jnp.concatenate([b_til[:, sl], k_til[:, sl]], axis=0)
            l_abk[s, h] = jnp.where(strict, _bdot(ah[s, h], bk_til, _NT), 0.0)
            m_rbk[s, h] = jnp.where(incl, _bdot(rh[s, h], bk_til, _NT), 0.0)
    z = {p: jnp.concatenate([ah[p], _bdot(l_abk[p], jnp.concatenate([zeros_cv, vh[p]], axis=0), _NN)], axis=1)
         for p in pairs}
    lp = {p: l_abk[p][:, :chunk] for p in pairs}
    for m in range(n_double):
        z = {p: z[p] + _bdot(lp[p], z[p], _NN) for p in pairs}
        if m + 1 < n_double:
            lp = {p: _bdot(lp[p], lp[p], _NN) for p in pairs}
    state = [state_ref[h] for h in range(heads)]
    y_parts = []
    for s in range(n_sub):
        pr_s = [_bdot(jnp.concatenate([z[s, h][:, :dh], rh[s, h]], axis=0), state[h], _NT) for h in range(heads)]
        uv = [jnp.concatenate([pr_s[h][:chunk] + z[s, h][:, dh:], vh[s, h]], axis=0) for h in range(heads)]
        ys = [pr_s[h][chunk:] + _bdot(m_rbk[s, h], uv[h], _NN) for h in range(heads)]
        state = [state[h] * w_total[s, h] + _bdot(uv[h], bk_rem[s, h], _TN) for h in range(heads)]
        y_parts.append(jnp.concatenate(ys, axis=1))
    for h in range(heads):
        state_ref[h] = state[h]
    y = jnp.concatenate(y_parts, axis=0)
    mean = _head_sum(y, dh) * (1.0 / dh)
    yc = y - mean
    var = _head_sum(yc * yc, dh) * (1.0 / dh)
    yn = yc * lax.rsqrt(var + RWKV_GN_EPS) * gn_ref[0:1, :] + gn_ref[1:2, :]
    bonus = _head_sum(r_ref[0] * k_ref[0] * gn_ref[2:3, :], dh) * v_ref[0]
    o_ref[0] = (yn + bonus) * g_ref[0]


def rwkv7_mixer_fused(r, logw, k, v, a, b, g, gn, heads=RWKV_HEADS, dh=RWKV_DH, chunk=RWKV_CHUNK, block=RWKV_BLOCK):
    B_, S_, W_ = r.shape
    block = min(block, S_)
    assert W_ == heads * dh and S_ % block == 0 and block % chunk == 0
    spec = pl.BlockSpec((1, block, W_), lambda bi, ci: (bi, ci, 0))
    return pl.pallas_call(
        functools.partial(_rwkv7_mixer_kernel, heads=heads, dh=dh, chunk=chunk),
        grid=(B_, S_ // block),
        in_specs=[spec] * 7 + [pl.BlockSpec(gn.shape, lambda bi, ci: (0, 0))],
        out_specs=spec,
        out_shape=jax.ShapeDtypeStruct((B_, S_, W_), F32),
        scratch_shapes=[pltpu.VMEM((heads, dh, dh), F32)],
        compiler_params=pltpu.CompilerParams(dimension_semantics=("arbitrary", "arbitrary")),
        name="rwkv7_mixer",
    )(r, logw, k, v, a, b, g, gn)


def _mla_flash_kernel(q_ref, k_ref, v_ref, o_ref, *, tile):
    h = pl.program_id(1)
    qi = pl.program_id(2)
    q = q_ref[0]
    dv = v_ref.shape[2]

    def kv_tile(j):
        start = pl.multiple_of(j * tile, tile)
        return k_ref[0, pl.ds(start, tile), :], v_ref[0, pl.ds(start, tile), :]

    def update(carry, s, v_t):
        m, l, acc = carry
        m_new = jnp.maximum(m, jnp.max(s, axis=-1, keepdims=True))
        corr = jnp.exp(m - m_new)
        p = jnp.exp(s - m_new)
        l_new = corr * l + jnp.sum(p, axis=-1, keepdims=True)
        acc_new = corr * acc + jnp.dot(p.astype(v_t.dtype), v_t, preferred_element_type=F32)
        return m_new, l_new, acc_new

    def body(j, carry):
        k_t, v_t = kv_tile(j)
        s = lax.dot_general(q, k_t, (_NT, ((), ())), preferred_element_type=F32)
        return update(carry, s, v_t)

    init = (jnp.full((tile, 1), -jnp.inf, F32), jnp.zeros((tile, 1), F32), jnp.zeros((tile, dv), F32))
    carry = lax.fori_loop(0, qi, body, init)
    k_t, v_t = kv_tile(qi)
    s = lax.dot_general(q, k_t, (_NT, ((), ())), preferred_element_type=F32)
    row = lax.broadcasted_iota(jnp.int32, (tile, tile), 0)
    col = lax.broadcasted_iota(jnp.int32, (tile, tile), 1)
    s = jnp.where(col <= row, s, -jnp.inf)
    m, l, acc = update(carry, s, v_t)
    lane_head = lax.broadcasted_iota(jnp.int32, (tile, dv), 1) // MLA_DV
    o_ref[0] = jnp.where(lane_head == h % 2, acc / l, 0.0)


def mla_attention(q, k, v, tile=ATTN_TILE):
    B_, S_, _ = q.shape
    tile = min(tile, S_)
    assert S_ % tile == 0 and MLA_HEAD_PAD == 2 * MLA_DV
    return pl.pallas_call(
        functools.partial(_mla_flash_kernel, tile=tile),
        grid=(B_, MLA_HEADS, S_ // tile),
        in_specs=[pl.BlockSpec((1, tile, MLA_HEAD_PAD), lambda b, h, i: (b, i, h)),
                  pl.BlockSpec((1, S_, MLA_HEAD_PAD), lambda b, h, i: (b, 0, h)),
                  pl.BlockSpec((1, S_, 2 * MLA_DV), lambda b, h, i: (b, 0, h // 2))],
        out_specs=pl.BlockSpec((1, tile, MLA_HEAD_PAD), lambda b, h, i: (b, i, h)),
        out_shape=jax.ShapeDtypeStruct((B_, S_, MLA_QK_W), F32),
        compiler_params=pltpu.CompilerParams(dimension_semantics=("arbitrary", "arbitrary", "arbitrary")),
        name="mla_causal_flash",
    )(q, k, v)


def _out_kernel(x_ref, oa_ref, ob_ref, oc_ref, od_ref, wabc_ref, wd_ref, gb_ref, gd_ref, ln_ref, wr_ref, br_ref,
                x1_ref, x1b_ref, route_ref, *, alpha):
    ob = ob_ref[...]
    ob = ob * lax.rsqrt(jnp.mean(ob * ob, -1, keepdims=True) + NORM_EPS) * gb_ref[...]
    od = od_ref[...]
    od = od * lax.rsqrt(jnp.sum(od * od, -1, keepdims=True) * (1.0 / MLA_W) + NORM_EPS) * gd_ref[...]
    W = RET_W
    mix = jnp.dot(oa_ref[...].astype(BF16), wabc_ref[0:W, :], preferred_element_type=F32)
    mix += jnp.dot(ob.astype(BF16), wabc_ref[W:2 * W, :], preferred_element_type=F32)
    mix += jnp.dot(oc_ref[...].astype(BF16), wabc_ref[2 * W:3 * W, :], preferred_element_type=F32)
    mix += jnp.dot(od.astype(BF16), wd_ref[...], preferred_element_type=F32)
    x1 = _layer_norm_rows(alpha * x_ref[...] + mix, ln_ref)
    x1_ref[...] = x1
    x1b = x1.astype(BF16)
    x1b_ref[...] = x1b
    logits = jnp.dot(x1b, wr_ref[...], preferred_element_type=F32) + br_ref[...]
    lane = lax.broadcasted_iota(jnp.int32, logits.shape, 1).astype(F32)
    vals, idxs = [], []
    for _ in range(TOP_K):
        m = jnp.max(logits, -1, keepdims=True)
        idx = jnp.min(jnp.where(logits == m, lane, float(ROUTER_PAD)), -1, keepdims=True)
        vals.append(m)
        idxs.append(idx)
        logits = jnp.where(lane == idx, -jnp.inf, logits)
    es = [jnp.exp(v - vals[0]) for v in vals]
    inv = 1.0 / sum(es)
    route = jnp.zeros_like(logits)
    for r in range(TOP_K):
        route = jnp.where(lane == float(r), es[r] * inv, route)
        route = jnp.where(lane == float(TOP_K + r), idxs[r], route)
    route_ref[...] = route


def fused_out_proj(x, o_a, o_b, o_c, o_d, w_abc, w_d, g_b, g_d, ln_gb, w_router, b_router, alpha, tile=OUT_TILE):
    N, D = x.shape
    tile = min(tile, N)
    n_exp = w_router.shape[1]
    assert N % tile == 0 and TOP_K <= n_exp <= ROUTER_PAD and 2 * TOP_K <= ROUTER_PAD
    wr = jnp.zeros((D, ROUTER_PAD), F32).at[:, :n_exp].set(w_router).astype(BF16)
    br = jnp.full((1, ROUTER_PAD), -jnp.inf, F32).at[0, :n_exp].set(b_router)
    consts = [w_abc, w_d, g_b, g_d, ln_gb, wr, br]
    tok = lambda w: pl.BlockSpec((tile, w), lambda i: (i, 0))
    full = lambda a: pl.BlockSpec(a.shape, lambda i: (0,) * a.ndim)
    x1, x1b, route = pl.pallas_call(
        functools.partial(_out_kernel, alpha=alpha),
        grid=(N // tile,),
        in_specs=[tok(D), tok(RET_W), tok(DIL_W), tok(RWKV_W), tok(MLA_QK_W)] + [full(c) for c in consts],
        out_specs=[tok(D), tok(D), tok(ROUTER_PAD)],
        out_shape=[jax.ShapeDtypeStruct((N, D), F32), jax.ShapeDtypeStruct((N, D), BF16),
                   jax.ShapeDtypeStruct((N, ROUTER_PAD), F32)],
        compiler_params=pltpu.CompilerParams(dimension_semantics=("arbitrary",), vmem_limit_bytes=VMEM_LIMIT),
        name="fused_out_proj",
    )(x, o_a, o_b, o_c, o_d, *consts)
    return x1, x1b, route[:, :TOP_K], route[:, TOP_K:2 * TOP_K].astype(jnp.int32)


def _moe_expert_kernel(tile_e_ref, n_used_ref, x_ref, wg_ref, wl_ref, bg_ref, bl_ref, wd_ref, bd_ref, buf_ref, o_ref):
    del tile_e_ref, buf_ref
    i = pl.program_id(0)

    @pl.when(i < n_used_ref[0])
    def _():
        x = x_ref[...]
        glu = jnp.dot(x, wg_ref[0], preferred_element_type=F32) + bg_ref[0]
        lin = jnp.dot(x, wl_ref[0], preferred_element_type=F32) + bl_ref[0]
        glu = jnp.minimum(glu, SWIGLU_LIMIT)
        lin = jnp.clip(lin, -SWIGLU_LIMIT, SWIGLU_LIMIT)
        act = glu * jax.nn.sigmoid(SWIGLU_ALPHA * glu) * (lin + 1.0)
        y = jnp.dot(act.astype(wd_ref.dtype), wd_ref[0], preferred_element_type=F32) + bd_ref[0]
        o_ref[...] = y.astype(o_ref.dtype)

    @pl.when(i >= n_used_ref[0])
    def _():
        o_ref[...] = jnp.zeros_like(o_ref)


def _deinterleave_kernel(w_ref, even_ref, odd_ref):
    g = 2 * LANES
    i = lax.broadcasted_iota(jnp.int32, (g, g), 0)
    j = lax.broadcasted_iota(jnp.int32, (g, g), 1)
    perm = (i == jnp.where(j < LANES, 2 * j, 2 * (j - LANES) + 1)).astype(BF16)
    for c in range(w_ref.shape[2] // g):
        blk = w_ref[0, :, c * g:(c + 1) * g].astype(BF16)
        sorted_cols = jnp.dot(blk, perm, preferred_element_type=F32)
        even_ref[0, :, c * LANES:(c + 1) * LANES] = sorted_cols[:, :LANES].astype(even_ref.dtype)
        odd_ref[0, :, c * LANES:(c + 1) * LANES] = sorted_cols[:, LANES:].astype(odd_ref.dtype)


def deinterleave_to_bf16(w):
    E, D, F2 = w.shape
    rows = min(DEINT_ROWS, D)
    assert D % rows == 0 and F2 % (2 * LANES) == 0
    out = jax.ShapeDtypeStruct((E, D, F2 // 2), BF16)
    return pl.pallas_call(
        _deinterleave_kernel,
        grid=(E, D // rows),
        in_specs=[pl.BlockSpec((1, rows, F2), lambda e, r: (e, r, 0))],
        out_specs=[pl.BlockSpec((1, rows, F2 // 2), lambda e, r: (e, r, 0))] * 2,
        out_shape=[out, out],
        compiler_params=pltpu.CompilerParams(dimension_semantics=("arbitrary", "arbitrary"), vmem_limit_bytes=VMEM_LIMIT),
        name="deinterleave_cast",
    )(w)


def moe_experts(xs, tile_e, n_used, w_glu, w_lin, b_glu, b_lin, w_dn, b_dn, tile, first_tile, out_buf):
    n_rows, D = xs.shape
    E, _, F = w_glu.shape
    w_spec = lambda shape: pl.BlockSpec((1,) + shape, lambda i, te, nu: (te[i], 0, 0))
    operands = [tile_e, n_used, xs, w_glu, w_lin, b_glu.reshape(E, 1, F), b_lin.reshape(E, 1, F), w_dn, b_dn.reshape(E, 1, D),
                out_buf]
    return pl.pallas_call(
        _moe_expert_kernel,
        grid_spec=pltpu.PrefetchScalarGridSpec(
            num_scalar_prefetch=2,
            grid=(n_rows // tile,),
            in_specs=[pl.BlockSpec((tile, D), lambda i, te, nu: (i, 0)),
                      w_spec((D, F)), w_spec((D, F)), w_spec((1, F)), w_spec((1, F)), w_spec((F, D)), w_spec((1, D)),
                      pl.BlockSpec(memory_space=pl.ANY)],
            out_specs=pl.BlockSpec((tile, D), lambda i, te, nu: (i + first_tile, 0)),
        ),
        out_shape=jax.ShapeDtypeStruct(out_buf.shape, out_buf.dtype),
        input_output_aliases={len(operands) - 1: 0},
        compiler_params=pltpu.CompilerParams(dimension_semantics=("arbitrary",), vmem_limit_bytes=VMEM_LIMIT),
        name="moe_expert_ffn",
    )(*operands)


def _combine_kernel(x_ref, y0_ref, y1_ref, y2_ref, y3_ref, gate_ref, ln_ref, o_ref, *, alpha):
    gate = gate_ref[...]
    y = sum(y_ref[...].astype(F32) * gate[:, s:s + 1] for s, y_ref in enumerate((y0_ref, y1_ref, y2_ref, y3_ref)))
    o_ref[...] = _layer_norm_rows(alpha * x_ref[...] + y, ln_ref)


def moe_combine_ln(x1, ys, gate, ln_gb, alpha, tile=OUT_TILE):
    N, D = x1.shape
    tile = min(tile, N)
    assert N % tile == 0 and len(ys) == TOP_K
    tok = lambda w: pl.BlockSpec((tile, w), lambda i: (i, 0))
    return pl.pallas_call(
        functools.partial(_combine_kernel, alpha=alpha),
        grid=(N // tile,),
        in_specs=[tok(D)] * (1 + TOP_K) + [tok(TOP_K), pl.BlockSpec(ln_gb.shape, lambda i: (0, 0))],
        out_specs=tok(D),
        out_shape=jax.ShapeDtypeStruct((N, D), F32),
        compiler_params=pltpu.CompilerParams(dimension_semantics=("arbitrary",), vmem_limit_bytes=VMEM_LIMIT),
        name="moe_combine_ln",
    )(x1, *ys, gate, ln_gb)


def moe_layer(x1, x1b, gate, top_idx, n_exp, expert_params, expert_base, ln_gb, alpha, row_buf, tile=MOE_TILE):
    n_tok, D = x1.shape
    n_assign = n_tok * TOP_K
    e_flat = top_idx.T.reshape(-1)
    experts = jnp.arange(n_exp, dtype=jnp.int32)
    assert n_exp * n_assign < 2 ** 31
    packed = lax.sort(e_flat * n_assign + jnp.arange(n_assign, dtype=jnp.int32))
    e_sorted, order = packed // n_assign, packed % n_assign
    counts = jnp.sum((e_flat[:, None] == experts[None, :]).astype(jnp.int32), axis=0)
    padded = (counts + tile - 1) // tile * tile
    start = jnp.cumsum(counts) - counts
    pend = jnp.cumsum(padded)
    pstart = pend - padded
    n_tiles = (n_assign + n_exp * (tile - 1) + tile - 1) // tile
    tile_first = jnp.arange(n_tiles, dtype=jnp.int32) * tile
    tile_e = jnp.minimum(jnp.sum((pend[None, :] <= tile_first[:, None]).astype(jnp.int32), axis=1), n_exp - 1)
    n_used = (pend[-1] // tile).astype(jnp.int32).reshape(1)
    within = (tile_first - pstart[tile_e])[:, None] + jnp.arange(tile, dtype=jnp.int32)[None, :]
    valid = within < counts[tile_e][:, None]
    sorted_idx = jnp.clip(start[tile_e][:, None] + within, 0, n_assign - 1)
    spread = (tile_first[:, None] + jnp.arange(tile, dtype=jnp.int32)[None, :]) % n_tok
    row_tok = jnp.where(valid, order[sorted_idx] % n_tok, spread).reshape(-1)
    shift = jnp.sum(jnp.where(e_sorted[:, None] == experts[None, :], (pstart - start)[None, :], 0), axis=1)
    dest = jnp.arange(n_assign, dtype=jnp.int32) + shift
    _, pos = lax.sort((order, dest), num_keys=1)
    pos = pos.reshape(TOP_K, n_tok)
    n_chunks = MOE_GATHER_CHUNKS if n_tiles % MOE_GATHER_CHUNKS == 0 else 1
    tpc = n_tiles // n_chunks
    row_tok = row_tok.reshape(n_chunks, tpc * tile)
    tile_e = (tile_e + expert_base).reshape(n_chunks, tpc)
    yb = row_buf
    for c in range(n_chunks):
        yb = moe_experts(x1b[row_tok[c]], tile_e[c], jnp.clip(n_used - c * tpc, 0, tpc), *expert_params, tile, c * tpc, yb)
    return moe_combine_ln(x1, [yb[pos[s]] for s in range(TOP_K)], gate, ln_gb, alpha), yb


def moe_row_buffer(n_tok, D, n_exp, tile=MOE_TILE):
    n_tiles = (n_tok * TOP_K + n_exp * (tile - 1) + tile - 1) // tile
    return jnp.zeros((n_tiles * tile, D), BF16)


def prep_expert_params(exp_w_gu, exp_b_gu, exp_w_dn, exp_b_dn):
    L, E, D, F2 = exp_w_gu.shape
    w_glu, w_lin = deinterleave_to_bf16(exp_w_gu.reshape(L * E, D, F2))
    b_gu = exp_b_gu.reshape(L * E, F2)
    return (w_glu, w_lin, b_gu[:, 0::2], b_gu[:, 1::2], exp_w_dn.astype(BF16).reshape(L * E, F2 // 2, D),
            exp_b_dn.reshape(L * E, D))


def kernel(x, w_in, w_out, ret_norm_g, dil_norm_g, rwkv_mu, rwkv_w0, rwkv_w_up, rwkv_a0, rwkv_a_up, rwkv_g_up, rwkv_k_k, rwkv_k_a, rwkv_r_k, rwkv_ln_g, rwkv_ln_b, rwkv_vres_down, rwkv_vres_mu, rwkv_v0, rwkv_v_up, mla_q_norm_g, mla_w_q_up, mla_kv_norm_g, mla_w_kv_up, mla_out_norm_g, ln1_g, ln1_b, router_w, router_b, exp_w_gu, exp_b_gu, exp_w_dn, exp_b_dn, ln2_g, ln2_b):
    depth = w_in.shape[0]
    B_, S_, D = x.shape
    alpha = (2 * depth) ** 0.25
    tables = rope_tables(S_)
    ret_tabs = retention_tables()
    expert_params = prep_expert_params(exp_w_gu, exp_b_gu, exp_w_dn, exp_b_dn)
    n_exp = exp_w_gu.shape[1]
    v_first = None
    n_tok = B_ * S_
    row_buf = moe_row_buffer(n_tok, D, n_exp)
    flat = lambda t: t.reshape(n_tok, t.shape[-1])
    for l in range(depth):
        w_cat = prep_in_weights(w_in[l], rwkv_vres_down[l - 1] if l > 0 else None)
        rw_params = prep_rwkv_params(rwkv_mu[l], rwkv_w0[l], rwkv_w_up[l], rwkv_a0[l], rwkv_a_up[l], rwkv_g_up[l],
                                     rwkv_k_k[l], rwkv_k_a[l], rwkv_vres_mu[l - 1] if l > 0 else None,
                                     rwkv_v0[l - 1] if l > 0 else None, rwkv_v_up[l - 1] if l > 0 else None)
        mla_params = prep_mla_params(mla_q_norm_g[l], mla_w_q_up[l], mla_kv_norm_g[l], mla_w_kv_up[l])
        gn = jnp.stack([rwkv_ln_g[l], rwkv_ln_b[l], rwkv_r_k[l].reshape(-1)])
        w_abc, w_d, g_d = prep_out_weights(w_out[l], mla_out_norm_g[l])
        (ret_q, ret_k, ret_v, ret_g, dil_q, dil_k, dil_v, rw_r, rw_lw, rw_k, rw_v, rw_a, rw_b, rw_g,
         mla_q, mla_k, mla_v) = fused_in_proj(x, w_cat, tables, rw_params, mla_params, v_first)
        if l == 0:
            v_first = rw_v
        o_a = retention_fused(ret_q, ret_k, ret_v, ret_g, ret_norm_g[l], ret_tabs)
        o_b = dilated_fused(dil_q, dil_k, dil_v)
        o_c = rwkv7_mixer_fused(rw_r, rw_lw, rw_k, rw_v, rw_a, rw_b, rw_g, gn)
        o_d = mla_attention(mla_q, mla_k, mla_v)
        x1, x1b, gate, top_idx = fused_out_proj(flat(x), flat(o_a), flat(o_b), flat(o_c), flat(o_d), w_abc, w_d,
                                                dil_norm_g[l].reshape(1, DIL_W), g_d, jnp.stack([ln1_g[l], ln1_b[l]]),
                                                router_w[l], router_b[l], alpha)
        x, row_buf = moe_layer(x1, x1b, gate, top_idx, n_exp, expert_params, l * n_exp,
                               jnp.stack([ln2_g[l], ln2_b[l]]), alpha, row_buf)
        x = x.reshape(B_, S_, D)
    return x
```

```python
import functools

import numpy as np
import jax
import jax.numpy as jnp
from jax import lax
from jax.experimental import pallas as pl
from jax.experimental.pallas import tpu as pltpu

F32 = jnp.float32
BF16 = jnp.bfloat16

LANES = 128
BLK = 128
LN_EPS = 1e-5
NORM_EPS = 1e-6

RET_HEADS, RET_DK, RET_DV = 4, 32, 64
RET_QK = RET_HEADS * RET_DK
RET_W = RET_HEADS * RET_DV
RET_THETA = 10000.0
DIL_HEADS, DIL_DH = 4, 64
DIL_W = DIL_HEADS * DIL_DH
DIL_PATTERNS = ((128, 1), (512, 4), (2048, 16))
ROPE_THETA = 500000.0
ROPE_ROT_DIM = DIL_DH // 4
RWKV_HEADS, RWKV_DH = 4, 64
RWKV_W = RWKV_HEADS * RWKV_DH
DECAY_LORA, AAA_LORA, MV_LORA, GATE_LORA = 64, 64, 32, 128
RWKV_GN_EPS = 64e-5
MLA_HEADS, MLA_NOPE, MLA_ROPE, MLA_DV = 4, 64, 32, 64
MLA_W = MLA_HEADS * MLA_DV
Q_LORA, KV_LORA = 256, 128
MLA_THETA = 10000.0
TOP_K = 4
SWIGLU_LIMIT, SWIGLU_ALPHA = 7.0, 1.702

RET_SPLITS = (RET_QK, RET_QK, RET_W, RET_W)
DIL_SPLITS = (DIL_W, DIL_W, DIL_W)
RWKV_SPLITS = (RWKV_W, RWKV_W, RWKV_W, DECAY_LORA, AAA_LORA, GATE_LORA)
MLA_SPLITS = (Q_LORA, KV_LORA, MLA_ROPE)
A_END = sum(RET_SPLITS)
B_END = A_END + sum(DIL_SPLITS)
C_END = B_END + sum(RWKV_SPLITS)
N_IN = C_END + sum(MLA_SPLITS)

RWKV_CHUNK = 64
RWKV_BLOCK = 512

MLA_HEAD_PAD = 128
MLA_QK_W = MLA_HEADS * MLA_HEAD_PAD
ATTN_TILE = 512

IN_RET_W = 4 * RET_QK + 2 * RET_W
IN_DIL_W = 5 * DIL_W
IN_RWKV_W = sum(RWKV_SPLITS) + LANES
IN_MLA_W = Q_LORA + KV_LORA + LANES
IN_COLS = IN_RET_W + IN_DIL_W + IN_RWKV_W + IN_MLA_W
IN_TILE = 256
OUT_TILE = 512
ROUTER_PAD = 128
DIL_PAIR_W = 2 * DIL_DH
DEINT_ROWS = 512
MOE_TILE = 512
DIL_UNROLL = 4
MOE_GATHER_CHUNKS = 4
VMEM_LIMIT = 48 * 1024 * 1024
DIL_VMEM_LIMIT = 56 * 1024 * 1024


def split_cols(p, sizes):
    idx = np.cumsum(sizes)[:-1].tolist()
    return jnp.split(p, idx, axis=-1)


def rope_table(n_pos, rot_dim, theta):
    inv_freq = 1.0 / (theta ** (jnp.arange(0, rot_dim, 2, dtype=F32) / rot_dim))
    ang = jnp.arange(n_pos, dtype=F32)[:, None] * inv_freq[None, :]
    return jnp.cos(ang), jnp.sin(ang)


def _bdot(a, b, dims):
    return lax.dot_general(a.astype(BF16), b.astype(BF16), ((dims[0], dims[1]), ((), ())), preferred_element_type=F32)


_NN = ((1,), (0,))
_NT = ((1,), (1,))
_TN = ((0,), (0,))


def _split3(x):
    h1 = x.astype(BF16)
    r1 = x - h1.astype(F32)
    h2 = r1.astype(BF16)
    h3 = (r1 - h2.astype(F32)).astype(BF16)
    return h1, h2, h3


def _head_sum(x, dh):
    w = x.shape[1]
    i = lax.broadcasted_iota(jnp.int32, (w, w), 0) // dh
    j = lax.broadcasted_iota(jnp.int32, (w, w), 1) // dh
    ones = (i == j).astype(BF16)
    return sum(jnp.dot(p, ones, preferred_element_type=F32) for p in _split3(x))


def _layer_norm_rows(h, ln_ref):
    mu = jnp.mean(h, -1, keepdims=True)
    hc = h - mu
    var = jnp.mean(hc * hc, -1, keepdims=True)
    return hc * lax.rsqrt(var + LN_EPS) * ln_ref[0:1, :] + ln_ref[1:2, :]


def _rot_half_cols(w, heads, dh, rot):
    d_in = w.shape[0]
    w = w.reshape(d_in, heads, dh)
    half = rot // 2
    sw = jnp.concatenate([-w[..., half:rot], w[..., :half], jnp.zeros((d_in, heads, dh - rot), w.dtype)], axis=-1)
    return sw.reshape(d_in, heads * dh)


def _rope_lanes(cs, heads, dh, rot, lead=0):
    cos, sin = cs
    S_ = cos.shape[0]
    c = jnp.concatenate([jnp.ones((S_, lead), F32), cos, cos, jnp.ones((S_, dh - lead - rot), F32)], axis=1)
    s = jnp.concatenate([jnp.zeros((S_, lead), F32), sin, sin, jnp.zeros((S_, dh - lead - rot), F32)], axis=1)
    return jnp.tile(c, (1, heads)), jnp.tile(s, (1, heads))


def rope_tables(S_):
    ret = _rope_lanes(rope_table(S_, RET_DK, RET_THETA), RET_HEADS, RET_DK, RET_DK)
    dil = _rope_lanes(rope_table(S_, ROPE_ROT_DIM, ROPE_THETA), DIL_HEADS, DIL_DH, ROPE_ROT_DIM)
    cos, sin = rope_table(S_, MLA_ROPE, MLA_THETA)
    mq = _rope_lanes((cos, sin), MLA_HEADS, MLA_HEAD_PAD, MLA_ROPE, lead=MLA_NOPE)
    mk = jnp.concatenate([cos, cos, sin, sin, jnp.zeros((S_, LANES - 2 * MLA_ROPE), F32)], axis=1)
    return ret + dil + mq + (mk,)


def prep_in_weights(w_in_l, vres_down_l):
    D = w_in_l.shape[0]
    a_q, a_k, a_v, a_g = split_cols(w_in_l[:, :A_END], RET_SPLITS)
    b_q, b_k, b_v = split_cols(w_in_l[:, A_END:B_END], DIL_SPLITS)
    d_cq, d_ckv, d_kr = split_cols(w_in_l[:, C_END:N_IN], MLA_SPLITS)
    vres = jnp.zeros((D, LANES), F32)
    if vres_down_l is not None:
        vres = vres.at[:, :MV_LORA].set(vres_down_l)
    cols = [a_q, _rot_half_cols(a_q, RET_HEADS, RET_DK, RET_DK), a_k, _rot_half_cols(a_k, RET_HEADS, RET_DK, RET_DK), a_v, a_g,
            b_q, _rot_half_cols(b_q, DIL_HEADS, DIL_DH, ROPE_ROT_DIM), b_k, _rot_half_cols(b_k, DIL_HEADS, DIL_DH, ROPE_ROT_DIM), b_v,
            w_in_l[:, B_END:C_END], vres,
            d_cq, d_ckv, d_kr, _rot_half_cols(d_kr, 1, MLA_ROPE, MLA_ROPE), jnp.zeros((D, LANES - 2 * MLA_ROPE), F32)]
    w = jnp.concatenate(cols, axis=1)
    assert w.shape[1] == IN_COLS
    return w.astype(BF16)


def prep_rwkv_params(mu, w0, w_up, a0, a_up, g_up, k_k, k_a, vres_mu, v0, v_up):
    W = RWKV_W
    mu_ext = jnp.zeros((1, IN_RWKV_W), F32).at[0, :mu.shape[0]].set(mu)
    v_up_ext = jnp.zeros((LANES, W), F32)
    v0_ext = jnp.zeros((1, W), F32)
    if vres_mu is not None:
        mu_ext = mu_ext.at[0, mu.shape[0]:mu.shape[0] + MV_LORA].set(vres_mu)
        v_up_ext = v_up_ext.at[:MV_LORA].set(v_up)
        v0_ext = v0.reshape(1, W)
    wa_up = jnp.zeros((LANES, 2 * W), F32).at[:DECAY_LORA, :W].set(w_up).at[DECAY_LORA:, W:].set(a_up)
    return [mu_ext, wa_up.astype(BF16), jnp.concatenate([w0, a0]).reshape(1, 2 * W), g_up.astype(BF16),
            jnp.stack([k_k, k_a]), v_up_ext.astype(BF16), v0_ext]


def prep_mla_params(q_norm_g, w_q_up, kv_norm_g, w_kv_up):
    ql = w_q_up.shape[0]
    wq = w_q_up.reshape(ql, MLA_HEADS, MLA_NOPE + MLA_ROPE)
    pad = jnp.zeros((ql, MLA_HEADS, MLA_HEAD_PAD - MLA_NOPE - MLA_ROPE), F32)
    q_main = jnp.concatenate([wq, pad], axis=-1).reshape(ql, MLA_QK_W)
    rope = wq[..., MLA_NOPE:]
    half = MLA_ROPE // 2
    q_rot = jnp.concatenate([jnp.zeros((ql, MLA_HEADS, MLA_NOPE), F32), -rope[..., half:], rope[..., :half], pad], axis=-1)
    w_q = jnp.concatenate([q_main, q_rot.reshape(ql, MLA_QK_W)], axis=1).astype(BF16)
    kl = w_kv_up.shape[0]
    wkv = w_kv_up.reshape(kl, MLA_HEADS, MLA_NOPE + MLA_DV)
    k_main = jnp.concatenate([wkv[..., :MLA_NOPE], jnp.zeros((kl, MLA_HEADS, MLA_HEAD_PAD - MLA_NOPE), F32)], axis=-1)
    w_kv = jnp.concatenate([k_main.reshape(kl, MLA_QK_W), wkv[..., MLA_NOPE:].reshape(kl, MLA_W)], axis=1).astype(BF16)
    return [q_norm_g.reshape(1, Q_LORA), w_q, kv_norm_g.reshape(1, KV_LORA), w_kv]


def prep_out_weights(w_out_l, mla_out_norm_g):
    W = RET_W
    w_abc = w_out_l[:3 * W].astype(BF16)
    w_d = w_out_l[3 * W:].reshape(MLA_HEADS, MLA_DV, -1)
    g_d = mla_out_norm_g.reshape(MLA_HEADS, MLA_DV)
    zw = jnp.zeros_like(w_d[0])
    zg = jnp.zeros_like(g_d[0])
    rows, gains = [], []
    for h in range(MLA_HEADS):
        rows += [w_d[h], zw] if h % 2 == 0 else [zw, w_d[h]]
        gains += [g_d[h], zg] if h % 2 == 0 else [zg, g_d[h]]
    return w_abc, jnp.concatenate(rows, axis=0).astype(BF16), jnp.concatenate(gains).reshape(1, MLA_QK_W)


def retention_tables():
    log_gamma = jnp.log(1.0 - 2.0 ** (-5.0 - jnp.arange(RET_HEADS, dtype=F32)))
    idx = jnp.arange(BLK, dtype=F32)
    dist = idx[:, None] - idx[None, :]
    inner = jnp.where(dist >= 0, jnp.exp(jnp.maximum(dist, 0.0)[None] * log_gamma[:, None, None]), 0.0)
    zeta = jnp.exp((BLK - 1 - idx)[None, :] * log_gamma[:, None])
    xi = jnp.exp((idx + 1.0)[None, :] * log_gamma[:, None])
    zeta_k = jnp.repeat(zeta.T, RET_DK, axis=1)
    xi_v = jnp.repeat(xi.T, RET_DV, axis=1)
    row_h = jnp.arange(RET_QK) // RET_DK
    col_h = jnp.arange(RET_W) // RET_DV
    same = row_h[:, None] == col_h[None, :]
    state_decay = jnp.where(same, jnp.exp(BLK * log_gamma)[row_h][:, None], 0.0)
    return inner, zeta_k, xi_v, state_decay, same.astype(F32)


def _in_kernel(x_ref, w_ref, cr_ref, sr_ref, cd_ref, sd_ref, cq_ref, sq_ref, ck_ref,
               mu_ref, wa_up_ref, w0a0_ref, g_up_ref, kk_ka_ref, v_up_ref, v0_ref, vfirst_ref,
               qn_ref, wq_ref, kvn_ref, wkv_ref,
               ret_q, ret_k, ret_v, ret_g, dil_q, dil_k, dil_v,
               rw_r, rw_lw, rw_k, rw_v, rw_a, rw_b, rw_g, mla_q, mla_k, mla_v,
               carry_ref, *, has_vres):
    tm = x_ref.shape[1]

    @pl.when(pl.program_id(1) == 0)
    def _():
        carry_ref[...] = jnp.zeros_like(carry_ref)

    xb = x_ref[0].astype(BF16)
    o1, o2, o3 = IN_RET_W, IN_RET_W + IN_DIL_W, IN_RET_W + IN_DIL_W + IN_RWKV_W
    pa = jnp.dot(xb, w_ref[:, :o1], preferred_element_type=F32)
    pb = jnp.dot(xb, w_ref[:, o1:o2], preferred_element_type=F32)
    pc = jnp.dot(xb, w_ref[:, o2:o3], preferred_element_type=F32)
    pd = jnp.dot(xb, w_ref[:, o3:], preferred_element_type=F32)

    qk = RET_QK
    cr, sr = cr_ref[...], sr_ref[...]
    ret_q[0] = (pa[:, 0:qk] * cr + pa[:, qk:2 * qk] * sr).astype(ret_q.dtype)
    ret_k[0] = ((pa[:, 2 * qk:3 * qk] * cr + pa[:, 3 * qk:4 * qk] * sr) * (RET_DK ** -0.5)).astype(ret_k.dtype)
    ret_v[0] = pa[:, 4 * qk:4 * qk + RET_W].astype(ret_v.dtype)
    ret_g[0] = pa[:, 4 * qk + RET_W:]

    cd, sd = cd_ref[...], sd_ref[...]
    dil_q[0] = pb[:, 0:DIL_W] * cd + pb[:, DIL_W:2 * DIL_W] * sd
    dil_k[0] = pb[:, 2 * DIL_W:3 * DIL_W] * cd + pb[:, 3 * DIL_W:4 * DIL_W] * sd
    dil_v[0] = pb[:, 4 * DIL_W:]

    row = lax.broadcasted_iota(jnp.int32, pc.shape, 0)
    prev = jnp.where(row == 0, carry_ref[...], pltpu.roll(pc, 1, 0))
    carry_ref[...] = pc[tm - 1:tm, :]
    ps = pc + (prev - pc) * mu_ref[...]
    W = RWKV_W
    r, k, v = ps[:, 0:W], ps[:, W:2 * W], ps[:, 2 * W:3 * W]
    wd_ad = ps[:, 3 * W:3 * W + LANES]
    lane = lax.broadcasted_iota(jnp.int32, wd_ad.shape, 1)
    lora_in = jnp.where(lane < DECAY_LORA, jnp.tanh(wd_ad), wd_ad)
    lora = jnp.dot(lora_in.astype(BF16), wa_up_ref[...], preferred_element_type=F32) + w0a0_ref[...]
    w_raw = -jax.nn.softplus(-lora[:, :W]) - 0.5
    a_sig = jax.nn.sigmoid(lora[:, W:])
    gd = ps[:, 3 * W + LANES:3 * W + 2 * LANES]
    rw_g[0] = jnp.dot(jax.nn.sigmoid(gd).astype(BF16), g_up_ref[...], preferred_element_type=F32)
    kk = k * kk_ka_ref[0:1, :]
    kk = kk / jnp.maximum(jnp.sqrt(_head_sum(kk * kk, RWKV_DH)), 1e-12)
    if has_vres:
        vd = ps[:, 3 * W + 2 * LANES:]
        mix = jax.nn.sigmoid(jnp.dot(vd.astype(BF16), v_up_ref[...], preferred_element_type=F32) + v0_ref[...])
        v = v + (vfirst_ref[0] - v) * mix
    rw_r[0] = r
    rw_lw[0] = -jnp.exp(w_raw)
    rw_k[0] = k * (1.0 + (a_sig - 1.0) * kk_ka_ref[1:2, :])
    rw_v[0] = v
    rw_a[0] = -kk
    rw_b[0] = kk * a_sig

    c_q = pd[:, :Q_LORA]
    c_q = c_q * lax.rsqrt(jnp.mean(c_q * c_q, -1, keepdims=True) + NORM_EPS) * qn_ref[...]
    q2 = jnp.dot(c_q.astype(BF16), wq_ref[...], preferred_element_type=F32)
    scale = (MLA_NOPE + MLA_ROPE) ** -0.5
    mla_q[0] = ((q2[:, :MLA_QK_W] * cq_ref[...] + q2[:, MLA_QK_W:] * sq_ref[...]) * scale).astype(mla_q.dtype)
    c_kv = pd[:, Q_LORA:Q_LORA + KV_LORA]
    c_kv = c_kv * lax.rsqrt(jnp.mean(c_kv * c_kv, -1, keepdims=True) + NORM_EPS) * kvn_ref[...]
    kv = jnp.dot(c_kv.astype(BF16), wkv_ref[...], preferred_element_type=F32)
    kr = pd[:, Q_LORA + KV_LORA:] * ck_ref[...]
    i = lax.broadcasted_iota(jnp.int32, (LANES, MLA_QK_W), 0)
    j = lax.broadcasted_iota(jnp.int32, (LANES, MLA_QK_W), 1) % MLA_HEAD_PAD
    place = ((i < 2 * MLA_ROPE) & (j == MLA_NOPE + i % MLA_ROPE)).astype(BF16)
    k_pe = sum(jnp.dot(p, place, preferred_element_type=F32) for p in _split3(kr)[:2])
    mla_k[0] = (kv[:, :MLA_QK_W] + k_pe).astype(mla_k.dtype)
    mla_v[0] = kv[:, MLA_QK_W:].astype(mla_v.dtype)


def fused_in_proj(x, w_cat, tables, rw_params, mla_params, v_first, tile=IN_TILE):
    B_, S_, D = x.shape
    tile = min(tile, S_)
    assert S_ % tile == 0
    has_vres = v_first is not None
    tok = lambda w: pl.BlockSpec((1, tile, w), lambda b, i: (b, i, 0))
    pos = lambda a: pl.BlockSpec((tile, a.shape[1]), lambda b, i: (i, 0))
    full = lambda a: pl.BlockSpec(a.shape, lambda b, i: (0,) * a.ndim)
    if has_vres:
        vf_spec = tok(RWKV_W)
    else:
        v_first = jnp.zeros((1, tile, RWKV_W), F32)
        vf_spec = pl.BlockSpec((1, tile, RWKV_W), lambda b, i: (0, 0, 0))
    f32o = lambda w: jax.ShapeDtypeStruct((B_, S_, w), F32)
    bfo = lambda w: jax.ShapeDtypeStruct((B_, S_, w), BF16)
    out_shape = ([bfo(RET_QK), bfo(RET_QK), bfo(RET_W), f32o(RET_W)] + [f32o(DIL_W)] * 3 + [f32o(RWKV_W)] * 7
                 + [bfo(MLA_QK_W), bfo(MLA_QK_W), bfo(MLA_W)])
    return pl.pallas_call(
        functools.partial(_in_kernel, has_vres=has_vres),
        grid=(B_, S_ // tile),
        in_specs=[tok(D), full(w_cat)] + [pos(t) for t in tables] + [full(p) for p in rw_params] + [vf_spec]
                 + [full(p) for p in mla_params],
        out_specs=[tok(s.shape[-1]) for s in out_shape],
        out_shape=out_shape,
        scratch_shapes=[pltpu.VMEM((1, IN_RWKV_W), F32)],
        compiler_params=pltpu.CompilerParams(dimension_semantics=("arbitrary", "arbitrary"), vmem_limit_bytes=VMEM_LIMIT),
        name="fused_in_proj",
    )(x, w_cat, *tables, *rw_params, v_first, *mla_params)


def _retention_kernel(q_ref, k_ref, v_ref, g_ref, ng_ref, inner_ref, zeta_ref, xi_ref, sdec_ref, smask_ref, o_ref, state_ref):
    @pl.when(pl.program_id(1) == 0)
    def _():
        state_ref[...] = jnp.zeros_like(state_ref)

    q, k, v = q_ref[0], k_ref[0], v_ref[0]
    q_head = lax.broadcasted_iota(jnp.int32, q.shape, 1) // RET_DK
    v_head = lax.broadcasted_iota(jnp.int32, v.shape, 1) // RET_DV
    zero_q = jnp.zeros_like(q)
    scores = [_bdot(jnp.where(q_head == h, q, zero_q), k, _NT) * inner_ref[h] for h in range(RET_HEADS)]
    o_heads = [_bdot(s, v, _NN) for s in scores]
    o = _bdot(q, state_ref[...], _NN) * xi_ref[...]
    for h in range(RET_HEADS):
        o = o + jnp.where(v_head == h, o_heads[h], 0.0)
    state_ref[...] = state_ref[...] * sdec_ref[...] + _bdot(k.astype(F32) * zeta_ref[...], v, _TN) * smask_ref[...]
    o = o * lax.rsqrt(_head_sum(o * o, RET_DV) * (1.0 / RET_DV) + NORM_EPS) * ng_ref[...]
    g = g_ref[0]
    o_ref[0] = g * jax.nn.sigmoid(g) * o


def retention_fused(q, k, v, g, norm_g, tables):
    B_, S_, _ = q.shape
    assert S_ % BLK == 0
    tok = lambda w: pl.BlockSpec((1, BLK, w), lambda b, c: (b, c, 0))
    full = lambda a: pl.BlockSpec(a.shape, lambda b, c: (0,) * a.ndim)
    ng = norm_g.reshape(1, RET_W)
    return pl.pallas_call(
        _retention_kernel,
        grid=(B_, S_ // BLK),
        in_specs=[tok(RET_QK), tok(RET_QK), tok(RET_W), tok(RET_W), full(ng)] + [full(t) for t in tables],
        out_specs=tok(RET_W),
        out_shape=jax.ShapeDtypeStruct((B_, S_, RET_W), F32),
        scratch_shapes=[pltpu.VMEM((RET_QK, RET_W), F32)],
        compiler_params=pltpu.CompilerParams(dimension_semantics=("arbitrary", "arbitrary")),
        name="retention_chunk",
    )(q, k, v, g, ng, *tables)


def _dilated_kernel(q_ref, k_ref, v_ref, o_ref, m_ref, l_ref, *, dils):
    S_ = q_ref.shape[1]
    lane_head = lax.broadcasted_iota(jnp.int32, (BLK, DIL_PAIR_W), 1) // DIL_DH
    i_idx = lax.broadcasted_iota(jnp.int32, (BLK, 2 * BLK), 0)
    j_idx = lax.broadcasted_iota(jnp.int32, (BLK, 2 * BLK), 1)
    scale = DIL_DH ** -0.5

    def rows(start, n, d):
        return pl.ds(start, n) if d == 1 else pl.ds(start, n, stride=d)

    n_un = DIL_UNROLL
    for pi, d in enumerate(dils):
        def tiles(tt, carry, d=d, first=(pi == 0)):
            nb = [(tt * n_un + u) // d for u in range(n_un)]
            r = [(tt * n_un + u) % d for u in range(n_un)]
            kb = [jnp.maximum(n - 1, 0) for n in nb]
            q_rows = [rows(nb[u] * (BLK * d) + r[u], BLK, d) for u in range(n_un)]
            k_rows = [rows(kb[u] * (BLK * d) + r[u], 2 * BLK, d) for u in range(n_un)]
            q = [q_ref[0, q_rows[u], :] for u in range(n_un)]
            kk = [k_ref[0, k_rows[u], :].astype(BF16) for u in range(n_un)]
            vv = [v_ref[0, k_rows[u], :].astype(BF16) for u in range(n_un)]
            if not first:
                m_old = [m_ref[q_rows[u], :] for u in range(n_un)]
                l_old = [l_ref[q_rows[u], :] for u in range(n_un)]
                o_old = [o_ref[0, q_rows[u], :] for u in range(n_un)]
            delta = [(nb[u] - kb[u]) * BLK + i_idx - j_idx for u in range(n_un)]
            valid = [(dl >= 0) & (dl <= BLK) for dl in delta]
            pairs = [(u, h) for u in range(n_un) for h in range(2)]
            s = {(u, h): jnp.where(valid[u], _bdot(jnp.where(lane_head == h, q[u], jnp.zeros_like(q[u])), kk[u], _NT) * scale,
                                   -jnp.inf) for u, h in pairs}
            m = {p: jnp.max(s[p], axis=-1, keepdims=True) for p in pairs}
            pr = {p: jnp.exp(s[p] - m[p]) for p in pairs}
            l = {p: jnp.sum(pr[p], axis=-1, keepdims=True) for p in pairs}
            pv = {(u, h): jnp.dot(pr[u, h].astype(BF16), vv[u], preferred_element_type=F32) for u, h in pairs}
            for u in range(n_un):
                m_t = jnp.where(lane_head == 0, m[u, 0], m[u, 1])
                l_t = jnp.where(lane_head == 0, l[u, 0], l[u, 1])
                pv_t = jnp.where(lane_head == 0, pv[u, 0], pv[u, 1])
                if first:
                    m_ref[q_rows[u], :] = m_t
                    l_ref[q_rows[u], :] = l_t
                    o_ref[0, q_rows[u], :] = pv_t
                else:
                    m_new = jnp.maximum(m_old[u], m_t)
                    c_old = jnp.exp(m_old[u] - m_new)
                    c_t = jnp.exp(m_t - m_new)
                    m_ref[q_rows[u], :] = m_new
                    l_ref[q_rows[u], :] = l_old[u] * c_old + l_t * c_t
                    o_ref[0, q_rows[u], :] = o_old[u] * c_old + pv_t * c_t
            return carry

        lax.fori_loop(0, S_ // (BLK * n_un), tiles, 0)

    def finish(t, carry):
        r0 = pl.multiple_of(t * BLK, BLK)
        o_ref[0, pl.ds(r0, BLK), :] = o_ref[0, pl.ds(r0, BLK), :] / l_ref[pl.ds(r0, BLK), :]
        return carry

    lax.fori_loop(0, S_ // BLK, finish, 0)


def dilated_fused(q, k, v):
    B_, S_, W_ = q.shape
    dils = tuple(d for _, d in DIL_PATTERNS)
    assert all(w // d == BLK for w, d in DIL_PATTERNS) and S_ % (2 * BLK * max(dils)) == 0 and S_ % (BLK * DIL_UNROLL) == 0
    spec = pl.BlockSpec((1, S_, DIL_PAIR_W), lambda b, hp: (b, 0, hp))
    return pl.pallas_call(
        functools.partial(_dilated_kernel, dils=dils),
        grid=(B_, W_ // DIL_PAIR_W),
        in_specs=[spec] * 3,
        out_specs=spec,
        out_shape=jax.ShapeDtypeStruct((B_, S_, W_), F32),
        scratch_shapes=[pltpu.VMEM((S_, DIL_PAIR_W), F32)] * 2,
        compiler_params=pltpu.CompilerParams(dimension_semantics=("arbitrary", "arbitrary"), vmem_limit_bytes=DIL_VMEM_LIMIT),
        name="dilated_attention",
    )(q, k, v)


def _rwkv7_mixer_kernel(r_ref, lw_ref, k_ref, v_ref, a_ref, b_ref, g_ref, gn_ref, o_ref, state_ref, *, heads, dh, chunk):
    rows = r_ref.shape[1]
    n_sub = rows // chunk
    n_double = int(np.log2(chunk))
    assert 2 ** n_double == chunk and n_sub * chunk == rows

    @pl.when(pl.program_id(1) == 0)
    def _():
        state_ref[...] = jnp.zeros_like(state_ref)

    ri = lax.broadcasted_iota(jnp.int32, (rows, rows), 0)
    ci = lax.broadcasted_iota(jnp.int32, (rows, rows), 1)
    tri = ((ci <= ri) & (ci >= (ri // chunk) * chunk)).astype(BF16)
    logw = lw_ref[0]
    cum = sum(jnp.dot(tri, piece, preferred_element_type=F32) for piece in _split3(logw))

    row = lax.broadcasted_iota(jnp.int32, (chunk, 2 * chunk), 0)
    col = lax.broadcasted_iota(jnp.int32, (chunk, 2 * chunk), 1) % chunk
    strict = col < row
    incl = col <= row
    zeros_cv = jnp.zeros((chunk, dh), F32)

    pairs = [(s, h) for s in range(n_sub) for h in range(heads)]
    ah, rh, vh, bk_rem, w_total = {}, {}, {}, {}, {}
    l_abk, m_rbk = {}, {}
    for s in range(n_sub):
        rs = slice(s * chunk, (s + 1) * chunk)
        cum_s, logw_s = cum[rs], logw[rs]
        total = cum_s[chunk - 1:chunk, :]
        e_in = jnp.exp(cum_s)
        e_neg = jnp.exp(-cum_s)
        e_rem = jnp.exp(total - cum_s)
        w_tot = jnp.exp(total)
        r_hat = r_ref[0, rs, :] * e_in
        a_hat = a_ref[0, rs, :] * jnp.exp(cum_s - logw_s)
        b_all, k_all, v_all = b_ref[0, rs, :], k_ref[0, rs, :], v_ref[0, rs, :]
        b_til, k_til = b_all * e_neg, k_all * e_neg
        b_rem, k_rem = b_all * e_rem, k_all * e_rem
        for h in range(heads):
            sl = slice(h * dh, (h + 1) * dh)
            ah[s, h], rh[s, h], vh[s, h] = a_hat[:, sl], r_hat[:, sl], v_all[:, sl]
            w_total[s, h] = w_tot[:, sl]
            bk_rem[s, h] = jnp.concatenate([b_rem[:, sl], k_rem[:, sl]], axis=0)
            bk_til = jnp.concatenate([b_til[:, sl], k_til[:, sl]], axis=0)
            l_abk[s, h] = jnp.where(strict, _bdot(ah[s, h], bk_til, _NT), 0.0)
            m_rbk[s, h] = jnp.where(incl, _bdot(rh[s, h], bk_til, _NT), 0.0)
    z = {p: jnp.concatenate([ah[p], _bdot(l_abk[p], jnp.concatenate([zeros_cv, vh[p]], axis=0), _NN)], axis=1)
         for p in pairs}
    lp = {p: l_abk[p][:, :chunk] for p in pairs}
    for m in range(n_double):
        z = {p: z[p] + _bdot(lp[p], z[p], _NN) for p in pairs}
        if m + 1 < n_double:
            lp = {p: _bdot(lp[p], lp[p], _NN) for p in pairs}
    state = [state_ref[h] for h in range(heads)]
    y_parts = []
    for s in range(n_sub):
        pr_s = [_bdot(jnp.concatenate([z[s, h][:, :dh], rh[s, h]], axis=0), state[h], _NT) for h in range(heads)]
        uv = [jnp.concatenate([pr_s[h][:chunk] + z[s, h][:, dh:], vh[s, h]], axis=0) for h in range(heads)]
        ys = [pr_s[h][chunk:] + _bdot(m_rbk[s, h], uv[h], _NN) for h in range(heads)]
        state = [state[h] * w_total[s, h] + _bdot(uv[h], bk_rem[s, h], _TN) for h in range(heads)]
        y_parts.append(jnp.concatenate(ys, axis=1))
    for h in range(heads):
        state_ref[h] = state[h]
    y = jnp.concatenate(y_parts, axis=0)
    mean = _head_sum(y, dh) * (1.0 / dh)
    yc = y - mean
    var = _head_sum(yc * yc, dh) * (1.0 / dh)
    yn = yc * lax.rsqrt(var + RWKV_GN_EPS) * gn_ref[0:1, :] + gn_ref[1:2, :]
    bonus = _head_sum(r_ref[0] * k_ref[0] * gn_ref[2:3, :], dh) * v_ref[0]
    o_ref[0] = (yn + bonus) * g_ref[0]


def rwkv7_mixer_fused(r, logw, k, v, a, b, g, gn, heads=RWKV_HEADS, dh=RWKV_DH, chunk=RWKV_CHUNK, block=RWKV_BLOCK):
    B_, S_, W_ = r.shape
    block = min(block, S_)
    assert W_ == heads * dh and S_ % block == 0 and block % chunk == 0
    spec = pl.BlockSpec((1, block, W_), lambda bi, ci: (bi, ci, 0))
    return pl.pallas_call(
        functools.partial(_rwkv7_mixer_kernel, heads=heads, dh=dh, chunk=chunk),
        grid=(B_, S_ // block),
        in_specs=[spec] * 7 + [pl.BlockSpec(gn.shape, lambda bi, ci: (0, 0))],
        out_specs=spec,
        out_shape=jax.ShapeDtypeStruct((B_, S_, W_), F32),
        scratch_shapes=[pltpu.VMEM((heads, dh, dh), F32)],
        compiler_params=pltpu.CompilerParams(dimension_semantics=("arbitrary", "arbitrary")),
        name="rwkv7_mixer",
    )(r, logw, k, v, a, b, g, gn)


def _mla_flash_kernel(q_ref, k_ref, v_ref, o_ref, *, tile):
    h = pl.program_id(1)
    qi = pl.program_id(2)
    q = q_ref[0]
    dv = v_ref.shape[2]

    def kv_tile(j):
        start = pl.multiple_of(j * tile, tile)
        return k_ref[0, pl.ds(start, tile), :], v_ref[0, pl.ds(start, tile), :]

    def update(carry, s, v_t):
        m, l, acc = carry
        m_new = jnp.maximum(m, jnp.max(s, axis=-1, keepdims=True))
        corr = jnp.exp(m - m_new)
        p = jnp.exp(s - m_new)
        l_new = corr * l + jnp.sum(p, axis=-1, keepdims=True)
        acc_new = corr * acc + jnp.dot(p.astype(v_t.dtype), v_t, preferred_element_type=F32)
        return m_new, l_new, acc_new

    def body(j, carry):
        k_t, v_t = kv_tile(j)
        s = lax.dot_general(q, k_t, (_NT, ((), ())), preferred_element_type=F32)
        return update(carry, s, v_t)

    init = (jnp.full((tile, 1), -jnp.inf, F32), jnp.zeros((tile, 1), F32), jnp.zeros((tile, dv), F32))
    carry = lax.fori_loop(0, qi, body, init)
    k_t, v_t = kv_tile(qi)
    s = lax.dot_general(q, k_t, (_NT, ((), ())), preferred_element_type=F32)
    row = lax.broadcasted_iota(jnp.int32, (tile, tile), 0)
    col = lax.broadcasted_iota(jnp.int32, (tile, tile), 1)
    s = jnp.where(col <= row, s, -jnp.inf)
    m, l, acc = update(carry, s, v_t)
    lane_head = lax.broadcasted_iota(jnp.int32, (tile, dv), 1) // MLA_DV
    o_ref[0] = jnp.where(lane_head == h % 2, acc / l, 0.0)


def mla_attention(q, k, v, tile=ATTN_TILE):
    B_, S_, _ = q.shape
    tile = min(tile, S_)
    assert S_ % tile == 0 and MLA_HEAD_PAD == 2 * MLA_DV
    return pl.pallas_call(
        functools.partial(_mla_flash_kernel, tile=tile),
        grid=(B_, MLA_HEADS, S_ // tile),
        in_specs=[pl.BlockSpec((1, tile, MLA_HEAD_PAD), lambda b, h, i: (b, i, h)),
                  pl.BlockSpec((1, S_, MLA_HEAD_PAD), lambda b, h, i: (b, 0, h)),
                  pl.BlockSpec((1, S_, 2 * MLA_DV), lambda b, h, i: (b, 0, h // 2))],
        out_specs=pl.BlockSpec((1, tile, MLA_HEAD_PAD), lambda b, h, i: (b, i, h)),
        out_shape=jax.ShapeDtypeStruct((B_, S_, MLA_QK_W), F32),
        compiler_params=pltpu.CompilerParams(dimension_semantics=("arbitrary", "arbitrary", "arbitrary")),
        name="mla_causal_flash",
    )(q, k, v)


def _out_kernel(x_ref, oa_ref, ob_ref, oc_ref, od_ref, wabc_ref, wd_ref, gb_ref, gd_ref, ln_ref, wr_ref, br_ref,
                x1_ref, x1b_ref, route_ref, *, alpha):
    ob = ob_ref[...]
    ob = ob * lax.rsqrt(jnp.mean(ob * ob, -1, keepdims=True) + NORM_EPS) * gb_ref[...]
    od = od_ref[...]
    od = od * lax.rsqrt(jnp.sum(od * od, -1, keepdims=True) * (1.0 / MLA_W) + NORM_EPS) * gd_ref[...]
    W = RET_W
    mix = jnp.dot(oa_ref[...].astype(BF16), wabc_ref[0:W, :], preferred_element_type=F32)
    mix += jnp.dot(ob.astype(BF16), wabc_ref[W:2 * W, :], preferred_element_type=F32)
    mix += jnp.dot(oc_ref[...].astype(BF16), wabc_ref[2 * W:3 * W, :], preferred_element_type=F32)
    mix += jnp.dot(od.astype(BF16), wd_ref[...], preferred_element_type=F32)
    x1 = _layer_norm_rows(alpha * x_ref[...] + mix, ln_ref)
    x1_ref[...] = x1
    x1b = x1.astype(BF16)
    x1b_ref[...] = x1b
    logits = jnp.dot(x1b, wr_ref[...], preferred_element_type=F32) + br_ref[...]
    lane = lax.broadcasted_iota(jnp.int32, logits.shape, 1).astype(F32)
    vals, idxs = [], []
    for _ in range(TOP_K):
        m = jnp.max(logits, -1, keepdims=True)
        idx = jnp.min(jnp.where(logits == m, lane, float(ROUTER_PAD)), -1, keepdims=True)
        vals.append(m)
        idxs.append(idx)
        logits = jnp.where(lane == idx, -jnp.inf, logits)
    es = [jnp.exp(v - vals[0]) for v in vals]
    inv = 1.0 / sum(es)
    route = jnp.zeros_like(logits)
    for r in range(TOP_K):
        route = jnp.where(lane == float(r), es[r] * inv, route)
        route = jnp.where(lane == float(TOP_K + r), idxs[r], route)
    route_ref[...] = route


def fused_out_proj(x, o_a, o_b, o_c, o_d, w_abc, w_d, g_b, g_d, ln_gb, w_router, b_router, alpha, tile=OUT_TILE):
    N, D = x.shape
    tile = min(tile, N)
    n_exp = w_router.shape[1]
    assert N % tile == 0 and TOP_K <= n_exp <= ROUTER_PAD and 2 * TOP_K <= ROUTER_PAD
    wr = jnp.zeros((D, ROUTER_PAD), F32).at[:, :n_exp].set(w_router).astype(BF16)
    br = jnp.full((1, ROUTER_PAD), -jnp.inf, F32).at[0, :n_exp].set(b_router)
    consts = [w_abc, w_d, g_b, g_d, ln_gb, wr, br]
    tok = lambda w: pl.BlockSpec((tile, w), lambda i: (i, 0))
    full = lambda a: pl.BlockSpec(a.shape, lambda i: (0,) * a.ndim)
    x1, x1b, route = pl.pallas_call(
        functools.partial(_out_kernel, alpha=alpha),
        grid=(N // tile,),
        in_specs=[tok(D), tok(RET_W), tok(DIL_W), tok(RWKV_W), tok(MLA_QK_W)] + [full(c) for c in consts],
        out_specs=[tok(D), tok(D), tok(ROUTER_PAD)],
        out_shape=[jax.ShapeDtypeStruct((N, D), F32), jax.ShapeDtypeStruct((N, D), BF16),
                   jax.ShapeDtypeStruct((N, ROUTER_PAD), F32)],
        compiler_params=pltpu.CompilerParams(dimension_semantics=("arbitrary",), vmem_limit_bytes=VMEM_LIMIT),
        name="fused_out_proj",
    )(x, o_a, o_b, o_c, o_d, *consts)
    return x1, x1b, route[:, :TOP_K], route[:, TOP_K:2 * TOP_K].astype(jnp.int32)


def _moe_expert_kernel(tile_e_ref, n_used_ref, x_ref, wg_ref, wl_ref, bg_ref, bl_ref, wd_ref, bd_ref, buf_ref, o_ref):
    del tile_e_ref, buf_ref
    i = pl.program_id(0)

    @pl.when(i < n_used_ref[0])
    def _():
        x = x_ref[...]
        glu = jnp.dot(x, wg_ref[0], preferred_element_type=F32) + bg_ref[0]
        lin = jnp.dot(x, wl_ref[0], preferred_element_type=F32) + bl_ref[0]
        glu = jnp.minimum(glu, SWIGLU_LIMIT)
        lin = jnp.clip(lin, -SWIGLU_LIMIT, SWIGLU_LIMIT)
        act = glu * jax.nn.sigmoid(SWIGLU_ALPHA * glu) * (lin + 1.0)
        y = jnp.dot(act.astype(wd_ref.dtype), wd_ref[0], preferred_element_type=F32) + bd_ref[0]
        o_ref[...] = y.astype(o_ref.dtype)

    @pl.when(i >= n_used_ref[0])
    def _():
        o_ref[...] = jnp.zeros_like(o_ref)


def _deinterleave_kernel(w_ref, even_ref, odd_ref):
    g = 2 * LANES
    i = lax.broadcasted_iota(jnp.int32, (g, g), 0)
    j = lax.broadcasted_iota(jnp.int32, (g, g), 1)
    perm = (i == jnp.where(j < LANES, 2 * j, 2 * (j - LANES) + 1)).astype(BF16)
    for c in range(w_ref.shape[2] // g):
        blk = w_ref[0, :, c * g:(c + 1) * g].astype(BF16)
        sorted_cols = jnp.dot(blk, perm, preferred_element_type=F32)
        even_ref[0, :, c * LANES:(c + 1) * LANES] = sorted_cols[:, :LANES].astype(even_ref.dtype)
        odd_ref[0, :, c * LANES:(c + 1) * LANES] = sorted_cols[:, LANES:].astype(odd_ref.dtype)


def deinterleave_to_bf16(w, first, count):
    _, D, F2 = w.shape
    rows = min(DEINT_ROWS, D)
    assert D % rows == 0 and F2 % (2 * LANES) == 0
    out = jax.ShapeDtypeStruct((count, D, F2 // 2), BF16)
    return pl.pallas_call(
        _deinterleave_kernel,
        grid=(count, D // rows),
        in_specs=[pl.BlockSpec((1, rows, F2), lambda e, r: (e + first, r, 0))],
        out_specs=[pl.BlockSpec((1, rows, F2 // 2), lambda e, r: (e, r, 0))] * 2,
        out_shape=[out, out],
        compiler_params=pltpu.CompilerParams(dimension_semantics=("arbitrary", "arbitrary"), vmem_limit_bytes=VMEM_LIMIT),
        name="deinterleave_cast",
    )(w)


def moe_experts(xs, tile_e, n_used, w_glu, w_lin, b_glu, b_lin, w_dn, b_dn, tile, first_tile, out_buf):
    n_rows, D = xs.shape
    E, _, F = w_glu.shape
    w_spec = lambda shape: pl.BlockSpec((1,) + shape, lambda i, te, nu: (te[i], 0, 0))
    operands = [tile_e, n_used, xs, w_glu, w_lin, b_glu.reshape(E, 1, F), b_lin.reshape(E, 1, F), w_dn, b_dn.reshape(E, 1, D),
                out_buf]
    return pl.pallas_call(
        _moe_expert_kernel,
        grid_spec=pltpu.PrefetchScalarGridSpec(
            num_scalar_prefetch=2,
            grid=(n_rows // tile,),
            in_specs=[pl.BlockSpec((tile, D), lambda i, te, nu: (i, 0)),
                      w_spec((D, F)), w_spec((D, F)), w_spec((1, F)), w_spec((1, F)), w_spec((F, D)), w_spec((1, D)),
                      pl.BlockSpec(memory_space=pl.ANY)],
            out_specs=pl.BlockSpec((tile, D), lambda i, te, nu: (i + first_tile, 0)),
        ),
        out_shape=jax.ShapeDtypeStruct(out_buf.shape, out_buf.dtype),
        input_output_aliases={len(operands) - 1: 0},
        compiler_params=pltpu.CompilerParams(dimension_semantics=("arbitrary",), vmem_limit_bytes=VMEM_LIMIT),
        name="moe_expert_ffn",
    )(*operands)


def _combine_kernel(x_ref, y0_ref, y1_ref, y2_ref, y3_ref, gate_ref, ln_ref, o_ref, *, alpha):
    gate = gate_ref[...]
    y = sum(y_ref[...].astype(F32) * gate[:, s:s + 1] for s, y_ref in enumerate((y0_ref, y1_ref, y2_ref, y3_ref)))
    o_ref[...] = _layer_norm_rows(alpha * x_ref[...] + y, ln_ref)


def moe_combine_ln(x1, ys, gate, ln_gb, alpha, tile=OUT_TILE):
    N, D = x1.shape
    tile = min(tile, N)
    assert N % tile == 0 and len(ys) == TOP_K
    tok = lambda w: pl.BlockSpec((tile, w), lambda i: (i, 0))
    return pl.pallas_call(
        functools.partial(_combine_kernel, alpha=alpha),
        grid=(N // tile,),
        in_specs=[tok(D)] * (1 + TOP_K) + [tok(TOP_K), pl.BlockSpec(ln_gb.shape, lambda i: (0, 0))],
        out_specs=tok(D),
        out_shape=jax.ShapeDtypeStruct((N, D), F32),
        compiler_params=pltpu.CompilerParams(dimension_semantics=("arbitrary",), vmem_limit_bytes=VMEM_LIMIT),
        name="moe_combine_ln",
    )(x1, *ys, gate, ln_gb)


def moe_layer(x1, x1b, gate, top_idx, n_exp, expert_params, expert_base, ln_gb, alpha, row_buf, tile=MOE_TILE):
    n_tok, D = x1.shape
    n_assign = n_tok * TOP_K
    e_flat = top_idx.T.reshape(-1)
    experts = jnp.arange(n_exp, dtype=jnp.int32)
    e_sorted, order = lax.sort((e_flat, jnp.arange(n_assign, dtype=jnp.int32)), num_keys=1)
    counts = jnp.sum((e_flat[:, None] == experts[None, :]).astype(jnp.int32), axis=0)
    padded = (counts + tile - 1) // tile * tile
    start = jnp.cumsum(counts) - counts
    pend = jnp.cumsum(padded)
    pstart = pend - padded
    n_tiles = (n_assign + n_exp * (tile - 1) + tile - 1) // tile
    tile_first = jnp.arange(n_tiles, dtype=jnp.int32) * tile
    tile_e = jnp.minimum(jnp.sum((pend[None, :] <= tile_first[:, None]).astype(jnp.int32), axis=1), n_exp - 1)
    n_used = (pend[-1] // tile).astype(jnp.int32).reshape(1)
    within = (tile_first - pstart[tile_e])[:, None] + jnp.arange(tile, dtype=jnp.int32)[None, :]
    valid = within < counts[tile_e][:, None]
    sorted_idx = jnp.clip(start[tile_e][:, None] + within, 0, n_assign - 1)
    spread = (tile_first[:, None] + jnp.arange(tile, dtype=jnp.int32)[None, :]) % n_tok
    row_tok = jnp.where(valid, order[sorted_idx] % n_tok, spread).reshape(-1)
    shift = jnp.sum(jnp.where(e_sorted[:, None] == experts[None, :], (pstart - start)[None, :], 0), axis=1)
    dest = jnp.arange(n_assign, dtype=jnp.int32) + shift
    _, pos = lax.sort((order, dest), num_keys=1)
    pos = pos.reshape(TOP_K, n_tok)
    n_chunks = MOE_GATHER_CHUNKS if n_tiles % MOE_GATHER_CHUNKS == 0 else 1
    tpc = n_tiles // n_chunks
    row_tok = row_tok.reshape(n_chunks, tpc * tile)
    tile_e = (tile_e + expert_base).reshape(n_chunks, tpc)
    yb = row_buf
    for c in range(n_chunks):
        yb = moe_experts(x1b[row_tok[c]], tile_e[c], jnp.clip(n_used - c * tpc, 0, tpc), *expert_params, tile, c * tpc, yb)
    return moe_combine_ln(x1, [yb[pos[s]] for s in range(TOP_K)], gate, ln_gb, alpha), yb


def moe_row_buffer(n_tok, D, n_exp, tile=MOE_TILE):
    n_tiles = (n_tok * TOP_K + n_exp * (tile - 1) + tile - 1) // tile
    return jnp.zeros((n_tiles * tile, D), BF16)


def prep_expert_params(exp_w_gu, exp_b_gu, exp_w_dn, exp_b_dn, l):
    L, E, D, F2 = exp_w_gu.shape
    w_glu, w_lin = deinterleave_to_bf16(exp_w_gu.reshape(L * E, D, F2), l * E, E)
    return (w_glu, w_lin, exp_b_gu[l][:, 0::2], exp_b_gu[l][:, 1::2], exp_w_dn[l].astype(BF16), exp_b_dn[l])


def kernel(x, w_in, w_out, ret_norm_g, dil_norm_g, rwkv_mu, rwkv_w0, rwkv_w_up, rwkv_a0, rwkv_a_up, rwkv_g_up, rwkv_k_k, rwkv_k_a, rwkv_r_k, rwkv_ln_g, rwkv_ln_b, rwkv_vres_down, rwkv_vres_mu, rwkv_v0, rwkv_v_up, mla_q_norm_g, mla_w_q_up, mla_kv_norm_g, mla_w_kv_up, mla_out_norm_g, ln1_g, ln1_b, router_w, router_b, exp_w_gu, exp_b_gu, exp_w_dn, exp_b_dn, ln2_g, ln2_b):
    depth = w_in.shape[0]
    B_, S_, D = x.shape
    alpha = (2 * depth) ** 0.25
    tables = rope_tables(S_)
    ret_tabs = retention_tables()
    expert_params = [prep_expert_params(exp_w_gu, exp_b_gu, exp_w_dn, exp_b_dn, l) for l in range(depth)]
    n_exp = exp_w_gu.shape[1]
    v_first = None
    n_tok = B_ * S_
    row_buf = moe_row_buffer(n_tok, D, n_exp)
    flat = lambda t: t.reshape(n_tok, t.shape[-1])
    for l in range(depth):
        w_cat = prep_in_weights(w_in[l], rwkv_vres_down[l - 1] if l > 0 else None)
        rw_params = prep_rwkv_params(rwkv_mu[l], rwkv_w0[l], rwkv_w_up[l], rwkv_a0[l], rwkv_a_up[l], rwkv_g_up[l],
                                     rwkv_k_k[l], rwkv_k_a[l], rwkv_vres_mu[l - 1] if l > 0 else None,
                                     rwkv_v0[l - 1] if l > 0 else None, rwkv_v_up[l - 1] if l > 0 else None)
        mla_params = prep_mla_params(mla_q_norm_g[l], mla_w_q_up[l], mla_kv_norm_g[l], mla_w_kv_up[l])
        gn = jnp.stack([rwkv_ln_g[l], rwkv_ln_b[l], rwkv_r_k[l].reshape(-1)])
        w_abc, w_d, g_d = prep_out_weights(w_out[l], mla_out_norm_g[l])
        (ret_q, ret_k, ret_v, ret_g, dil_q, dil_k, dil_v, rw_r, rw_lw, rw_k, rw_v, rw_a, rw_b, rw_g,
         mla_q, mla_k, mla_v) = fused_in_proj(x, w_cat, tables, rw_params, mla_params, v_first)
        if l == 0:
            v_first = rw_v
        o_a = retention_fused(ret_q, ret_k, ret_v, ret_g, ret_norm_g[l], ret_tabs)
        o_b = dilated_fused(dil_q, dil_k, dil_v)
        o_c = rwkv7_mixer_fused(rw_r, rw_lw, rw_k, rw_v, rw_a, rw_b, rw_g, gn)
        o_d = mla_attention(mla_q, mla_k, mla_v)
        x1, x1b, gate, top_idx = fused_out_proj(flat(x), flat(o_a), flat(o_b), flat(o_c), flat(o_d), w_abc, w_d,
                                                dil_norm_g[l].reshape(1, DIL_W), g_d, jnp.stack([ln1_g[l], ln1_b[l]]),
                                                router_w[l], router_b[l], alpha)
        x, row_buf = moe_layer(x1, x1b, gate, top_idx, n_exp, expert_params[l], 0,
                               jnp.stack([ln2_g[l], ln2_b[l]]), alpha, row_buf)
        x = x.reshape(B_, S_, D)
    return x
```

```python
import functools

import numpy as np
import jax
import jax.numpy as jnp
from jax import lax
from jax.experimental import pallas as pl
from jax.experimental.pallas import tpu as pltpu

F32 = jnp.float32
BF16 = jnp.bfloat16

LANES = 128
BLK = 128
LN_EPS = 1e-5
NORM_EPS = 1e-6

RET_HEADS, RET_DK, RET_DV = 4, 32, 64
RET_QK = RET_HEADS * RET_DK
RET_W = RET_HEADS * RET_DV
RET_THETA = 10000.0
DIL_HEADS, DIL_DH = 4, 64
DIL_W = DIL_HEADS * DIL_DH
DIL_PATTERNS = ((128, 1), (512, 4), (2048, 16))
ROPE_THETA = 500000.0
ROPE_ROT_DIM = DIL_DH // 4
RWKV_HEADS, RWKV_DH = 4, 64
RWKV_W = RWKV_HEADS * RWKV_DH
DECAY_LORA, AAA_LORA, MV_LORA, GATE_LORA = 64, 64, 32, 128
RWKV_GN_EPS = 64e-5
MLA_HEADS, MLA_NOPE, MLA_ROPE, MLA_DV = 4, 64, 32, 64
MLA_W = MLA_HEADS * MLA_DV
Q_LORA, KV_LORA = 256, 128
MLA_THETA = 10000.0
TOP_K = 4
SWIGLU_LIMIT, SWIGLU_ALPHA = 7.0, 1.702

RET_SPLITS = (RET_QK, RET_QK, RET_W, RET_W)
DIL_SPLITS = (DIL_W, DIL_W, DIL_W)
RWKV_SPLITS = (RWKV_W, RWKV_W, RWKV_W, DECAY_LORA, AAA_LORA, GATE_LORA)
MLA_SPLITS = (Q_LORA, KV_LORA, MLA_ROPE)
A_END = sum(RET_SPLITS)
B_END = A_END + sum(DIL_SPLITS)
C_END = B_END + sum(RWKV_SPLITS)
N_IN = C_END + sum(MLA_SPLITS)

RWKV_CHUNK = 64
RWKV_BLOCK = 512

MLA_HEAD_PAD = 128
MLA_QK_W = MLA_HEADS * MLA_HEAD_PAD
ATTN_TILE = 512

IN_RET_W = 4 * RET_QK + 2 * RET_W
IN_DIL_W = 5 * DIL_W
IN_RWKV_W = sum(RWKV_SPLITS) + LANES
IN_MLA_W = Q_LORA + KV_LORA + LANES
IN_COLS = IN_RET_W + IN_DIL_W + IN_RWKV_W + IN_MLA_W
IN_TILE = 256
OUT_TILE = 512
ROUTER_PAD = 128
DIL_PAIR_W = 2 * DIL_DH
DEINT_ROWS = 512
MOE_TILE = 512
DIL_UNROLL = 4
MOE_GATHER_CHUNKS = 4
VMEM_LIMIT = 48 * 1024 * 1024
DIL_VMEM_LIMIT = 56 * 1024 * 1024


def split_cols(p, sizes):
    idx = np.cumsum(sizes)[:-1].tolist()
    return jnp.split(p, idx, axis=-1)


def rope_table(n_pos, rot_dim, theta):
    inv_freq = 1.0 / (theta ** (jnp.arange(0, rot_dim, 2, dtype=F32) / rot_dim))
    ang = jnp.arange(n_pos, dtype=F32)[:, None] * inv_freq[None, :]
    return jnp.cos(ang), jnp.sin(ang)


def _bdot(a, b, dims):
    return lax.dot_general(a.astype(BF16), b.astype(BF16), ((dims[0], dims[1]), ((), ())), preferred_element_type=F32)


_NN = ((1,), (0,))
_NT = ((1,), (1,))
_TN = ((0,), (0,))


def _split3(x):
    h1 = x.astype(BF16)
    r1 = x - h1.astype(F32)
    h2 = r1.astype(BF16)
    h3 = (r1 - h2.astype(F32)).astype(BF16)
    return h1, h2, h3


def _head_sum(x, dh):
    w = x.shape[1]
    i = lax.broadcasted_iota(jnp.int32, (w, w), 0) // dh
    j = lax.broadcasted_iota(jnp.int32, (w, w), 1) // dh
    ones = (i == j).astype(BF16)
    return sum(jnp.dot(p, ones, preferred_element_type=F32) for p in _split3(x))


def _layer_norm_rows(h, ln_ref):
    mu = jnp.mean(h, -1, keepdims=True)
    hc = h - mu
    var = jnp.mean(hc * hc, -1, keepdims=True)
    return hc * lax.rsqrt(var + LN_EPS) * ln_ref[0:1, :] + ln_ref[1:2, :]


def _rot_half_cols(w, heads, dh, rot):
    d_in = w.shape[0]
    w = w.reshape(d_in, heads, dh)
    half = rot // 2
    sw = jnp.concatenate([-w[..., half:rot], w[..., :half], jnp.zeros((d_in, heads, dh - rot), w.dtype)], axis=-1)
    return sw.reshape(d_in, heads * dh)


def _rope_lanes(cs, heads, dh, rot, lead=0):
    cos, sin = cs
    S_ = cos.shape[0]
    c = jnp.concatenate([jnp.ones((S_, lead), F32), cos, cos, jnp.ones((S_, dh - lead - rot), F32)], axis=1)
    s = jnp.concatenate([jnp.zeros((S_, lead), F32), sin, sin, jnp.zeros((S_, dh - lead - rot), F32)], axis=1)
    return jnp.tile(c, (1, heads)), jnp.tile(s, (1, heads))


def rope_tables(S_):
    ret = _rope_lanes(rope_table(S_, RET_DK, RET_THETA), RET_HEADS, RET_DK, RET_DK)
    dil = _rope_lanes(rope_table(S_, ROPE_ROT_DIM, ROPE_THETA), DIL_HEADS, DIL_DH, ROPE_ROT_DIM)
    cos, sin = rope_table(S_, MLA_ROPE, MLA_THETA)
    mq = _rope_lanes((cos, sin), MLA_HEADS, MLA_HEAD_PAD, MLA_ROPE, lead=MLA_NOPE)
    mk = jnp.concatenate([cos, cos, sin, sin, jnp.zeros((S_, LANES - 2 * MLA_ROPE), F32)], axis=1)
    return ret + dil + mq + (mk,)


def prep_in_weights(w_in_l, vres_down_l):
    D = w_in_l.shape[0]
    a_q, a_k, a_v, a_g = split_cols(w_in_l[:, :A_END], RET_SPLITS)
    b_q, b_k, b_v = split_cols(w_in_l[:, A_END:B_END], DIL_SPLITS)
    d_cq, d_ckv, d_kr = split_cols(w_in_l[:, C_END:N_IN], MLA_SPLITS)
    vres = jnp.zeros((D, LANES), F32)
    if vres_down_l is not None:
        vres = vres.at[:, :MV_LORA].set(vres_down_l)
    cols = [a_q, _rot_half_cols(a_q, RET_HEADS, RET_DK, RET_DK), a_k, _rot_half_cols(a_k, RET_HEADS, RET_DK, RET_DK), a_v, a_g,
            b_q, _rot_half_cols(b_q, DIL_HEADS, DIL_DH, ROPE_ROT_DIM), b_k, _rot_half_cols(b_k, DIL_HEADS, DIL_DH, ROPE_ROT_DIM), b_v,
            w_in_l[:, B_END:C_END], vres,
            d_cq, d_ckv, d_kr, _rot_half_cols(d_kr, 1, MLA_ROPE, MLA_ROPE), jnp.zeros((D, LANES - 2 * MLA_ROPE), F32)]
    w = jnp.concatenate(cols, axis=1)
    assert w.shape[1] == IN_COLS
    return w.astype(BF16)


def prep_rwkv_params(mu, w0, w_up, a0, a_up, g_up, k_k, k_a, vres_mu, v0, v_up):
    W = RWKV_W
    mu_ext = jnp.zeros((1, IN_RWKV_W), F32).at[0, :mu.shape[0]].set(mu)
    v_up_ext = jnp.zeros((LANES, W), F32)
    v0_ext = jnp.zeros((1, W), F32)
    if vres_mu is not None:
        mu_ext = mu_ext.at[0, mu.shape[0]:mu.shape[0] + MV_LORA].set(vres_mu)
        v_up_ext = v_up_ext.at[:MV_LORA].set(v_up)
        v0_ext = v0.reshape(1, W)
    wa_up = jnp.zeros((LANES, 2 * W), F32).at[:DECAY_LORA, :W].set(w_up).at[DECAY_LORA:, W:].set(a_up)
    return [mu_ext, wa_up.astype(BF16), jnp.concatenate([w0, a0]).reshape(1, 2 * W), g_up.astype(BF16),
            jnp.stack([k_k, k_a]), v_up_ext.astype(BF16), v0_ext]


def prep_mla_params(q_norm_g, w_q_up, kv_norm_g, w_kv_up):
    ql = w_q_up.shape[0]
    wq = w_q_up.reshape(ql, MLA_HEADS, MLA_NOPE + MLA_ROPE)
    pad = jnp.zeros((ql, MLA_HEADS, MLA_HEAD_PAD - MLA_NOPE - MLA_ROPE), F32)
    q_main = jnp.concatenate([wq, pad], axis=-1).reshape(ql, MLA_QK_W)
    rope = wq[..., MLA_NOPE:]
    half = MLA_ROPE // 2
    q_rot = jnp.concatenate([jnp.zeros((ql, MLA_HEADS, MLA_NOPE), F32), -rope[..., half:], rope[..., :half], pad], axis=-1)
    w_q = jnp.concatenate([q_main, q_rot.reshape(ql, MLA_QK_W)], axis=1).astype(BF16)
    kl = w_kv_up.shape[0]
    wkv = w_kv_up.reshape(kl, MLA_HEADS, MLA_NOPE + MLA_DV)
    k_main = jnp.concatenate([wkv[..., :MLA_NOPE], jnp.zeros((kl, MLA_HEADS, MLA_HEAD_PAD - MLA_NOPE), F32)], axis=-1)
    w_kv = jnp.concatenate([k_main.reshape(kl, MLA_QK_W), wkv[..., MLA_NOPE:].reshape(kl, MLA_W)], axis=1).astype(BF16)
    return [q_norm_g.reshape(1, Q_LORA), w_q, kv_norm_g.reshape(1, KV_LORA), w_kv]


def prep_out_weights(w_out_l, mla_out_norm_g):
    W = RET_W
    w_abc = w_out_l[:3 * W].astype(BF16)
    w_d = w_out_l[3 * W:].reshape(MLA_HEADS, MLA_DV, -1)
    g_d = mla_out_norm_g.reshape(MLA_HEADS, MLA_DV)
    zw = jnp.zeros_like(w_d[0])
    zg = jnp.zeros_like(g_d[0])
    rows, gains = [], []
    for h in range(MLA_HEADS):
        rows += [w_d[h], zw] if h % 2 == 0 else [zw, w_d[h]]
        gains += [g_d[h], zg] if h % 2 == 0 else [zg, g_d[h]]
    return w_abc, jnp.concatenate(rows, axis=0).astype(BF16), jnp.concatenate(gains).reshape(1, MLA_QK_W)


def retention_tables():
    log_gamma = jnp.log(1.0 - 2.0 ** (-5.0 - jnp.arange(RET_HEADS, dtype=F32)))
    idx = jnp.arange(BLK, dtype=F32)
    dist = idx[:, None] - idx[None, :]
    inner = jnp.where(dist >= 0, jnp.exp(jnp.maximum(dist, 0.0)[None] * log_gamma[:, None, None]), 0.0)
    zeta = jnp.exp((BLK - 1 - idx)[None, :] * log_gamma[:, None])
    xi = jnp.exp((idx + 1.0)[None, :] * log_gamma[:, None])
    zeta_k = jnp.repeat(zeta.T, RET_DK, axis=1)
    xi_v = jnp.repeat(xi.T, RET_DV, axis=1)
    row_h = jnp.arange(RET_QK) // RET_DK
    col_h = jnp.arange(RET_W) // RET_DV
    same = row_h[:, None] == col_h[None, :]
    state_decay = jnp.where(same, jnp.exp(BLK * log_gamma)[row_h][:, None], 0.0)
    return inner, zeta_k, xi_v, state_decay, same.astype(F32)


def _in_kernel(x_ref, w_ref, cr_ref, sr_ref, cd_ref, sd_ref, cq_ref, sq_ref, ck_ref,
               mu_ref, wa_up_ref, w0a0_ref, g_up_ref, kk_ka_ref, v_up_ref, v0_ref, vfirst_ref,
               qn_ref, wq_ref, kvn_ref, wkv_ref,
               ret_q, ret_k, ret_v, ret_g, dil_q, dil_k, dil_v,
               rw_r, rw_lw, rw_k, rw_v, rw_a, rw_b, rw_g, mla_q, mla_k, mla_v,
               carry_ref, *, has_vres):
    tm = x_ref.shape[1]

    @pl.when(pl.program_id(1) == 0)
    def _():
        carry_ref[...] = jnp.zeros_like(carry_ref)

    xb = x_ref[0].astype(BF16)
    o1, o2, o3 = IN_RET_W, IN_RET_W + IN_DIL_W, IN_RET_W + IN_DIL_W + IN_RWKV_W
    pa = jnp.dot(xb, w_ref[:, :o1], preferred_element_type=F32)
    pb = jnp.dot(xb, w_ref[:, o1:o2], preferred_element_type=F32)
    pc = jnp.dot(xb, w_ref[:, o2:o3], preferred_element_type=F32)
    pd = jnp.dot(xb, w_ref[:, o3:], preferred_element_type=F32)

    qk = RET_QK
    cr, sr = cr_ref[...], sr_ref[...]
    ret_q[0] = (pa[:, 0:qk] * cr + pa[:, qk:2 * qk] * sr).astype(ret_q.dtype)
    ret_k[0] = ((pa[:, 2 * qk:3 * qk] * cr + pa[:, 3 * qk:4 * qk] * sr) * (RET_DK ** -0.5)).astype(ret_k.dtype)
    ret_v[0] = pa[:, 4 * qk:4 * qk + RET_W].astype(ret_v.dtype)
    ret_g[0] = pa[:, 4 * qk + RET_W:]

    cd, sd = cd_ref[...], sd_ref[...]
    dil_q[0] = pb[:, 0:DIL_W] * cd + pb[:, DIL_W:2 * DIL_W] * sd
    dil_k[0] = pb[:, 2 * DIL_W:3 * DIL_W] * cd + pb[:, 3 * DIL_W:4 * DIL_W] * sd
    dil_v[0] = pb[:, 4 * DIL_W:]

    row = lax.broadcasted_iota(jnp.int32, pc.shape, 0)
    prev = jnp.where(row == 0, carry_ref[...], pltpu.roll(pc, 1, 0))
    carry_ref[...] = pc[tm - 1:tm, :]
    ps = pc + (prev - pc) * mu_ref[...]
    W = RWKV_W
    r, k, v = ps[:, 0:W], ps[:, W:2 * W], ps[:, 2 * W:3 * W]
    wd_ad = ps[:, 3 * W:3 * W + LANES]
    lane = lax.broadcasted_iota(jnp.int32, wd_ad.shape, 1)
    lora_in = jnp.where(lane < DECAY_LORA, jnp.tanh(wd_ad), wd_ad)
    lora = jnp.dot(lora_in.astype(BF16), wa_up_ref[...], preferred_element_type=F32) + w0a0_ref[...]
    w_raw = -jax.nn.softplus(-lora[:, :W]) - 0.5
    a_sig = jax.nn.sigmoid(lora[:, W:])
    gd = ps[:, 3 * W + LANES:3 * W + 2 * LANES]
    rw_g[0] = jnp.dot(jax.nn.sigmoid(gd).astype(BF16), g_up_ref[...], preferred_element_type=F32)
    kk = k * kk_ka_ref[0:1, :]
    kk = kk / jnp.maximum(jnp.sqrt(_head_sum(kk * kk, RWKV_DH)), 1e-12)
    if has_vres:
        vd = ps[:, 3 * W + 2 * LANES:]
        mix = jax.nn.sigmoid(jnp.dot(vd.astype(BF16), v_up_ref[...], preferred_element_type=F32) + v0_ref[...])
        v = v + (vfirst_ref[0] - v) * mix
    rw_r[0] = r
    rw_lw[0] = -jnp.exp(w_raw)
    rw_k[0] = k * (1.0 + (a_sig - 1.0) * kk_ka_ref[1:2, :])
    rw_v[0] = v
    rw_a[0] = -kk
    rw_b[0] = kk * a_sig

    c_q = pd[:, :Q_LORA]
    c_q = c_q * lax.rsqrt(jnp.mean(c_q * c_q, -1, keepdims=True) + NORM_EPS) * qn_ref[...]
    q2 = jnp.dot(c_q.astype(BF16), wq_ref[...], preferred_element_type=F32)
    scale = (MLA_NOPE + MLA_ROPE) ** -0.5
    mla_q[0] = ((q2[:, :MLA_QK_W] * cq_ref[...] + q2[:, MLA_QK_W:] * sq_ref[...]) * scale).astype(mla_q.dtype)
    c_kv = pd[:, Q_LORA:Q_LORA + KV_LORA]
    c_kv = c_kv * lax.rsqrt(jnp.mean(c_kv * c_kv, -1, keepdims=True) + NORM_EPS) * kvn_ref[...]
    kv = jnp.dot(c_kv.astype(BF16), wkv_ref[...], preferred_element_type=F32)
    kr = pd[:, Q_LORA + KV_LORA:] * ck_ref[...]
    i = lax.broadcasted_iota(jnp.int32, (LANES, MLA_QK_W), 0)
    j = lax.broadcasted_iota(jnp.int32, (LANES, MLA_QK_W), 1) % MLA_HEAD_PAD
    place = ((i < 2 * MLA_ROPE) & (j == MLA_NOPE + i % MLA_ROPE)).astype(BF16)
    k_pe = sum(jnp.dot(p, place, preferred_element_type=F32) for p in _split3(kr)[:2])
    mla_k[0] = (kv[:, :MLA_QK_W] + k_pe).astype(mla_k.dtype)
    mla_v[0] = kv[:, MLA_QK_W:].astype(mla_v.dtype)


def fused_in_proj(x, w_cat, tables, rw_params, mla_params, v_first, tile=IN_TILE):
    B_, S_, D = x.shape
    tile = min(tile, S_)
    assert S_ % tile == 0
    has_vres = v_first is not None
    tok = lambda w: pl.BlockSpec((1, tile, w), lambda b, i: (b, i, 0))
    pos = lambda a: pl.BlockSpec((tile, a.shape[1]), lambda b, i: (i, 0))
    full = lambda a: pl.BlockSpec(a.shape, lambda b, i: (0,) * a.ndim)
    if has_vres:
        vf_spec = tok(RWKV_W)
    else:
        v_first = jnp.zeros((1, tile, RWKV_W), F32)
        vf_spec = pl.BlockSpec((1, tile, RWKV_W), lambda b, i: (0, 0, 0))
    f32o = lambda w: jax.ShapeDtypeStruct((B_, S_, w), F32)
    bfo = lambda w: jax.ShapeDtypeStruct((B_, S_, w), BF16)
    out_shape = ([bfo(RET_QK), bfo(RET_QK), bfo(RET_W), f32o(RET_W)] + [f32o(DIL_W)] * 3 + [f32o(RWKV_W)] * 7
                 + [bfo(MLA_QK_W), bfo(MLA_QK_W), bfo(MLA_W)])
    return pl.pallas_call(
        functools.partial(_in_kernel, has_vres=has_vres),
        grid=(B_, S_ // tile),
        in_specs=[tok(D), full(w_cat)] + [pos(t) for t in tables] + [full(p) for p in rw_params] + [vf_spec]
                 + [full(p) for p in mla_params],
        out_specs=[tok(s.shape[-1]) for s in out_shape],
        out_shape=out_shape,
        scratch_shapes=[pltpu.VMEM((1, IN_RWKV_W), F32)],
        compiler_params=pltpu.CompilerParams(dimension_semantics=("arbitrary", "arbitrary"), vmem_limit_bytes=VMEM_LIMIT),
        name="fused_in_proj",
    )(x, w_cat, *tables, *rw_params, v_first, *mla_params)


def _retention_kernel(q_ref, k_ref, v_ref, g_ref, ng_ref, inner_ref, zeta_ref, xi_ref, sdec_ref, smask_ref, o_ref, state_ref):
    @pl.when(pl.program_id(1) == 0)
    def _():
        state_ref[...] = jnp.zeros_like(state_ref)

    q, k, v = q_ref[0], k_ref[0], v_ref[0]
    q_head = lax.broadcasted_iota(jnp.int32, q.shape, 1) // RET_DK
    v_head = lax.broadcasted_iota(jnp.int32, v.shape, 1) // RET_DV
    zero_q = jnp.zeros_like(q)
    scores = [_bdot(jnp.where(q_head == h, q, zero_q), k, _NT) * inner_ref[h] for h in range(RET_HEADS)]
    o_heads = [_bdot(s, v, _NN) for s in scores]
    o = _bdot(q, state_ref[...], _NN) * xi_ref[...]
    for h in range(RET_HEADS):
        o = o + jnp.where(v_head == h, o_heads[h], 0.0)
    state_ref[...] = state_ref[...] * sdec_ref[...] + _bdot(k.astype(F32) * zeta_ref[...], v, _TN) * smask_ref[...]
    o = o * lax.rsqrt(_head_sum(o * o, RET_DV) * (1.0 / RET_DV) + NORM_EPS) * ng_ref[...]
    g = g_ref[0]
    o_ref[0] = g * jax.nn.sigmoid(g) * o


def retention_fused(q, k, v, g, norm_g, tables):
    B_, S_, _ = q.shape
    assert S_ % BLK == 0
    tok = lambda w: pl.BlockSpec((1, BLK, w), lambda b, c: (b, c, 0))
    full = lambda a: pl.BlockSpec(a.shape, lambda b, c: (0,) * a.ndim)
    ng = norm_g.reshape(1, RET_W)
    return pl.pallas_call(
        _retention_kernel,
        grid=(B_, S_ // BLK),
        in_specs=[tok(RET_QK), tok(RET_QK), tok(RET_W), tok(RET_W), full(ng)] + [full(t) for t in tables],
        out_specs=tok(RET_W),
        out_shape=jax.ShapeDtypeStruct((B_, S_, RET_W), F32),
        scratch_shapes=[pltpu.VMEM((RET_QK, RET_W), F32)],
        compiler_params=pltpu.CompilerParams(dimension_semantics=("arbitrary", "arbitrary")),
        name="retention_chunk",
    )(q, k, v, g, ng, *tables)


def _dilated_kernel(q_ref, k_ref, v_ref, o_ref, m_ref, l_ref, *, dils):
    S_ = q_ref.shape[1]
    lane_head = lax.broadcasted_iota(jnp.int32, (BLK, DIL_PAIR_W), 1) // DIL_DH
    i_idx = lax.broadcasted_iota(jnp.int32, (BLK, 2 * BLK), 0)
    j_idx = lax.broadcasted_iota(jnp.int32, (BLK, 2 * BLK), 1)
    scale = DIL_DH ** -0.5

    def rows(start, n, d):
        return pl.ds(start, n) if d == 1 else pl.ds(start, n, stride=d)

    n_un = DIL_UNROLL
    for pi, d in enumerate(dils):
        def tiles(tt, carry, d=d, first=(pi == 0)):
            nb = [(tt * n_un + u) // d for u in range(n_un)]
            r = [(tt * n_un + u) % d for u in range(n_un)]
            kb = [jnp.maximum(n - 1, 0) for n in nb]
            q_rows = [rows(nb[u] * (BLK * d) + r[u], BLK, d) for u in range(n_un)]
            k_rows = [rows(kb[u] * (BLK * d) + r[u], 2 * BLK, d) for u in range(n_un)]
            q = [q_ref[0, q_rows[u], :] for u in range(n_un)]
            kk = [k_ref[0, k_rows[u], :].astype(BF16) for u in range(n_un)]
            vv = [v_ref[0, k_rows[u], :].astype(BF16) for u in range(n_un)]
            if not first:
                m_old = [m_ref[q_rows[u], :] for u in range(n_un)]
                l_old = [l_ref[q_rows[u], :] for u in range(n_un)]
                o_old = [o_ref[0, q_rows[u], :] for u in range(n_un)]
            delta = [(nb[u] - kb[u]) * BLK + i_idx - j_idx for u in range(n_un)]
            valid = [(dl >= 0) & (dl <= BLK) for dl in delta]
            pairs = [(u, h) for u in range(n_un) for h in range(2)]
            s = {(u, h): jnp.where(valid[u], _bdot(jnp.where(lane_head == h, q[u], jnp.zeros_like(q[u])), kk[u], _NT) * scale,
                                   -jnp.inf) for u, h in pairs}
            m = {p: jnp.max(s[p], axis=-1, keepdims=True) for p in pairs}
            pr = {p: jnp.exp(s[p] - m[p]) for p in pairs}
            l = {p: jnp.sum(pr[p], axis=-1, keepdims=True) for p in pairs}
            pv = {(u, h): jnp.dot(pr[u, h].astype(BF16), vv[u], preferred_element_type=F32) for u, h in pairs}
            for u in range(n_un):
                m_t = jnp.where(lane_head == 0, m[u, 0], m[u, 1])
                l_t = jnp.where(lane_head == 0, l[u, 0], l[u, 1])
                pv_t = jnp.where(lane_head == 0, pv[u, 0], pv[u, 1])
                if first:
                    m_ref[q_rows[u], :] = m_t
                    l_ref[q_rows[u], :] = l_t
                    o_ref[0, q_rows[u], :] = pv_t
                else:
                    m_new = jnp.maximum(m_old[u], m_t)
                    c_old = jnp.exp(m_old[u] - m_new)
                    c_t = jnp.exp(m_t - m_new)
                    m_ref[q_rows[u], :] = m_new
                    l_ref[q_rows[u], :] = l_old[u] * c_old + l_t * c_t
                    o_ref[0, q_rows[u], :] = o_old[u] * c_old + pv_t * c_t
            return carry

        lax.fori_loop(0, S_ // (BLK * n_un), tiles, 0)

    def finish(t, carry):
        r0 = pl.multiple_of(t * BLK, BLK)
        o_ref[0, pl.ds(r0, BLK), :] = o_ref[0, pl.ds(r0, BLK), :] / l_ref[pl.ds(r0, BLK), :]
        return carry

    lax.fori_loop(0, S_ // BLK, finish, 0)


def dilated_fused(q, k, v):
    B_, S_, W_ = q.shape
    dils = tuple(d for _, d in DIL_PATTERNS)
    assert all(w // d == BLK for w, d in DIL_PATTERNS) and S_ % (2 * BLK * max(dils)) == 0 and S_ % (BLK * DIL_UNROLL) == 0
    spec = pl.BlockSpec((1, S_, DIL_PAIR_W), lambda b, hp: (b, 0, hp))
    return pl.pallas_call(
        functools.partial(_dilated_kernel, dils=dils),
        grid=(B_, W_ // DIL_PAIR_W),
        in_specs=[spec] * 3,
        out_specs=spec,
        out_shape=jax.ShapeDtypeStruct((B_, S_, W_), F32),
        scratch_shapes=[pltpu.VMEM((S_, DIL_PAIR_W), F32)] * 2,
        compiler_params=pltpu.CompilerParams(dimension_semantics=("arbitrary", "arbitrary"), vmem_limit_bytes=DIL_VMEM_LIMIT),
        name="dilated_attention",
    )(q, k, v)


def _rwkv7_mixer_kernel(r_ref, lw_ref, k_ref, v_ref, a_ref, b_ref, g_ref, gn_ref, o_ref, state_ref, *, heads, dh, chunk):
    rows = r_ref.shape[1]
    n_sub = rows // chunk
    n_double = int(np.log2(chunk))
    assert 2 ** n_double == chunk and n_sub * chunk == rows

    @pl.when(pl.program_id(1) == 0)
    def _():
        state_ref[...] = jnp.zeros_like(state_ref)

    ri = lax.broadcasted_iota(jnp.int32, (rows, rows), 0)
    ci = lax.broadcasted_iota(jnp.int32, (rows, rows), 1)
    tri = ((ci <= ri) & (ci >= (ri // chunk) * chunk)).astype(BF16)
    logw = lw_ref[0]
    cum = sum(jnp.dot(tri, piece, preferred_element_type=F32) for piece in _split3(logw))

    row = lax.broadcasted_iota(jnp.int32, (chunk, 2 * chunk), 0)
    col = lax.broadcasted_iota(jnp.int32, (chunk, 2 * chunk), 1) % chunk
    strict = col < row
    incl = col <= row
    zeros_cv = jnp.zeros((chunk, dh), F32)

    pairs = [(s, h) for s in range(n_sub) for h in range(heads)]
    ah, rh, vh, bk_rem, w_total = {}, {}, {}, {}, {}
    l_abk, m_rbk = {}, {}
    for s in range(n_sub):
        rs = slice(s * chunk, (s + 1) * chunk)
        cum_s, logw_s = cum[rs], logw[rs]
        total = cum_s[chunk - 1:chunk, :]
        e_in = jnp.exp(cum_s)
        e_neg = jnp.exp(-cum_s)
        e_rem = jnp.exp(total - cum_s)
        w_tot = jnp.exp(total)
        r_hat = r_ref[0, rs, :] * e_in
        a_hat = a_ref[0, rs, :] * jnp.exp(cum_s - logw_s)
        b_all, k_all, v_all = b_ref[0, rs, :], k_ref[0, rs, :], v_ref[0, rs, :]
        b_til, k_til = b_all * e_neg, k_all * e_neg
        b_rem, k_rem = b_all * e_rem, k_all * e_rem
        for h in range(heads):
            sl = slice(h * dh, (h + 1) * dh)
            ah[s, h], rh[s, h], vh[s, h] = a_hat[:, sl], r_hat[:, sl], v_all[:, sl]
            w_total[s, h] = w_tot[:, sl]
            bk_rem[s, h] = jnp.concatenate([b_rem[:, sl], k_rem[:, sl]], axis=0)
            bk_til = jnp.concatenate([b_til[:, sl], k_til[:, sl]], axis=0)
            l_abk[s, h] = jnp.where(strict, _bdot(ah[s, h], bk_til, _NT), 0.0)
            m_rbk[s, h] = jnp.where(incl, _bdot(rh[s, h], bk_til, _NT), 0.0)
    z = {p: jnp.concatenate([ah[p], _bdot(l_abk[p], jnp.concatenate([zeros_cv, vh[p]], axis=0), _NN)], axis=1)
         for p in pairs}
    lp = {p: l_abk[p][:, :chunk] for p in pairs}
    for m in range(n_double):
        z = {p: z[p] + _bdot(lp[p], z[p], _NN) for p in pairs}
        if m + 1 < n_double:
            lp = {p: _bdot(lp[p], lp[p], _NN) for p in pairs}
    state = [state_ref[h] for h in range(heads)]
    y_parts = []
    for s in range(n_sub):
        pr_s = [_bdot(jnp.concatenate([z[s, h][:, :dh], rh[s, h]], axis=0), state[h], _NT) for h in range(heads)]
        uv = [jnp.concatenate([pr_s[h][:chunk] + z[s, h][:, dh:], vh[s, h]], axis=0) for h in range(heads)]
        ys = [pr_s[h][chunk:] + _bdot(m_rbk[s, h], uv[h], _NN) for h in range(heads)]
        state = [state[h] * w_total[s, h] + _bdot(uv[h], bk_rem[s, h], _TN) for h in range(heads)]
        y_parts.append(jnp.concatenate(ys, axis=1))
    for h in range(heads):
        state_ref[h] = state[h]
    y = jnp.concatenate(y_parts, axis=0)
    mean = _head_sum(y, dh) * (1.0 / dh)
    yc = y - mean
    var = _head_sum(yc * yc, dh) * (1.0 / dh)
    yn = yc * lax.rsqrt(var + RWKV_GN_EPS) * gn_ref[0:1, :] + gn_ref[1:2, :]
    bonus = _head_sum(r_ref[0] * k_ref[0] * gn_ref[2:3, :], dh) * v_ref[0]
    o_ref[0] = (yn + bonus) * g_ref[0]


def rwkv7_mixer_fused(r, logw, k, v, a, b, g, gn, heads=RWKV_HEADS, dh=RWKV_DH, chunk=RWKV_CHUNK, block=RWKV_BLOCK):
    B_, S_, W_ = r.shape
    block = min(block, S_)
    assert W_ == heads * dh and S_ % block == 0 and block % chunk == 0
    spec = pl.BlockSpec((1, block, W_), lambda bi, ci: (bi, ci, 0))
    return pl.pallas_call(
        functools.partial(_rwkv7_mixer_kernel, heads=heads, dh=dh, chunk=chunk),
        grid=(B_, S_ // block),
        in_specs=[spec] * 7 + [pl.BlockSpec(gn.shape, lambda bi, ci: (0, 0))],
        out_specs=spec,
        out_shape=jax.ShapeDtypeStruct((B_, S_, W_), F32),
        scratch_shapes=[pltpu.VMEM((heads, dh, dh), F32)],
        compiler_params=pltpu.CompilerParams(dimension_semantics=("arbitrary", "arbitrary")),
        name="rwkv7_mixer",
    )(r, logw, k, v, a, b, g, gn)


def _mla_flash_kernel(q_ref, k_ref, v_ref, o_ref, s_ref, m_ref, l_ref, acc_ref, *, tile):
    h = pl.program_id(1)
    qi = pl.program_id(2)
    q = q_ref[0]
    dv = v_ref.shape[2]
    row = lax.broadcasted_iota(jnp.int32, (tile, tile), 0)
    col = lax.broadcasted_iota(jnp.int32, (tile, tile), 1)

    def rows(j):
        return pl.ds(pl.multiple_of(j * tile, tile), tile)

    def fill(slot, j):
        s_ref[slot] = lax.dot_general(q, k_ref[0, rows(j), :], (_NT, ((), ())), preferred_element_type=F32)

    def consume(slot, j):
        s = jnp.where((col <= row) | (j < qi), s_ref[slot], -jnp.inf)
        m = m_ref[...]
        m_new = jnp.maximum(m, jnp.max(s, axis=-1, keepdims=True))
        corr = jnp.exp(m - m_new)
        p = jnp.exp(s - m_new)
        m_ref[...] = m_new
        l_ref[...] = corr * l_ref[...] + jnp.sum(p, axis=-1, keepdims=True)
        acc_ref[...] = corr * acc_ref[...] + jnp.dot(p.astype(BF16), v_ref[0, rows(j), :], preferred_element_type=F32)

    m_ref[...] = jnp.full(m_ref.shape, -jnp.inf, F32)
    l_ref[...] = jnp.zeros(l_ref.shape, F32)
    acc_ref[...] = jnp.zeros(acc_ref.shape, F32)
    fill(0, 0)
    n_kv = qi + 1

    def pair(i, carry):
        j = 2 * i
        fill(1, jnp.minimum(j + 1, qi))
        consume(0, j)
        fill(0, jnp.minimum(j + 2, qi))

        @pl.when(j + 1 < n_kv)
        def _():
            consume(1, j + 1)

        return carry

    lax.fori_loop(0, (n_kv + 1) // 2, pair, 0)
    lane_head = lax.broadcasted_iota(jnp.int32, (tile, dv), 1) // MLA_DV
    o_ref[0] = jnp.where(lane_head == h % 2, acc_ref[...] / l_ref[...], 0.0)


def mla_attention(q, k, v, tile=ATTN_TILE):
    B_, S_, _ = q.shape
    tile = min(tile, S_)
    assert S_ % tile == 0 and MLA_HEAD_PAD == 2 * MLA_DV
    return pl.pallas_call(
        functools.partial(_mla_flash_kernel, tile=tile),
        grid=(B_, MLA_HEADS, S_ // tile),
        in_specs=[pl.BlockSpec((1, tile, MLA_HEAD_PAD), lambda b, h, i: (b, i, h)),
                  pl.BlockSpec((1, S_, MLA_HEAD_PAD), lambda b, h, i: (b, 0, h)),
                  pl.BlockSpec((1, S_, 2 * MLA_DV), lambda b, h, i: (b, 0, h // 2))],
        out_specs=pl.BlockSpec((1, tile, MLA_HEAD_PAD), lambda b, h, i: (b, i, h)),
        out_shape=jax.ShapeDtypeStruct((B_, S_, MLA_QK_W), F32),
        scratch_shapes=[pltpu.VMEM((2, tile, tile), F32), pltpu.VMEM((tile, 1), F32), pltpu.VMEM((tile, 1), F32),
                        pltpu.VMEM((tile, 2 * MLA_DV), F32)],
        compiler_params=pltpu.CompilerParams(dimension_semantics=("arbitrary", "arbitrary", "arbitrary")),
        name="mla_causal_flash",
    )(q, k, v)


def _out_kernel(x_ref, oa_ref, ob_ref, oc_ref, od_ref, wabc_ref, wd_ref, gb_ref, gd_ref, ln_ref, wr_ref, br_ref,
                x1_ref, x1b_ref, route_ref, *, alpha):
    ob = ob_ref[...]
    ob = ob * lax.rsqrt(jnp.mean(ob * ob, -1, keepdims=True) + NORM_EPS) * gb_ref[...]
    od = od_ref[...]
    od = od * lax.rsqrt(jnp.sum(od * od, -1, keepdims=True) * (1.0 / MLA_W) + NORM_EPS) * gd_ref[...]
    W = RET_W
    mix = jnp.dot(oa_ref[...].astype(BF16), wabc_ref[0:W, :], preferred_element_type=F32)
    mix += jnp.dot(ob.astype(BF16), wabc_ref[W:2 * W, :], preferred_element_type=F32)
    mix += jnp.dot(oc_ref[...].astype(BF16), wabc_ref[2 * W:3 * W, :], preferred_element_type=F32)
    mix += jnp.dot(od.astype(BF16), wd_ref[...], preferred_element_type=F32)
    x1 = _layer_norm_rows(alpha * x_ref[...] + mix, ln_ref)
    x1_ref[...] = x1
    x1b = x1.astype(BF16)
    x1b_ref[...] = x1b
    logits = jnp.dot(x1b, wr_ref[...], preferred_element_type=F32) + br_ref[...]
    lane = lax.broadcasted_iota(jnp.int32, logits.shape, 1).astype(F32)
    vals, idxs = [], []
    for _ in range(TOP_K):
        m = jnp.max(logits, -1, keepdims=True)
        idx = jnp.min(jnp.where(logits == m, lane, float(ROUTER_PAD)), -1, keepdims=True)
        vals.append(m)
        idxs.append(idx)
        logits = jnp.where(lane == idx, -jnp.inf, logits)
    es = [jnp.exp(v - vals[0]) for v in vals]
    inv = 1.0 / sum(es)
    route = jnp.zeros_like(logits)
    for r in range(TOP_K):
        route = jnp.where(lane == float(r), es[r] * inv, route)
        route = jnp.where(lane == float(TOP_K + r), idxs[r], route)
    route_ref[...] = route


def fused_out_proj(x, o_a, o_b, o_c, o_d, w_abc, w_d, g_b, g_d, ln_gb, w_router, b_router, alpha, tile=OUT_TILE):
    N, D = x.shape
    tile = min(tile, N)
    n_exp = w_router.shape[1]
    assert N % tile == 0 and TOP_K <= n_exp <= ROUTER_PAD and 2 * TOP_K <= ROUTER_PAD
    wr = jnp.zeros((D, ROUTER_PAD), F32).at[:, :n_exp].set(w_router).astype(BF16)
    br = jnp.full((1, ROUTER_PAD), -jnp.inf, F32).at[0, :n_exp].set(b_router)
    consts = [w_abc, w_d, g_b, g_d, ln_gb, wr, br]
    tok = lambda w: pl.BlockSpec((tile, w), lambda i: (i, 0))
    full = lambda a: pl.BlockSpec(a.shape, lambda i: (0,) * a.ndim)
    x1, x1b, route = pl.pallas_call(
        functools.partial(_out_kernel, alpha=alpha),
        grid=(N // tile,),
        in_specs=[tok(D), tok(RET_W), tok(DIL_W), tok(RWKV_W), tok(MLA_QK_W)] + [full(c) for c in consts],
        out_specs=[tok(D), tok(D), tok(ROUTER_PAD)],
        out_shape=[jax.ShapeDtypeStruct((N, D), F32), jax.ShapeDtypeStruct((N, D), BF16),
                   jax.ShapeDtypeStruct((N, ROUTER_PAD), F32)],
        compiler_params=pltpu.CompilerParams(dimension_semantics=("arbitrary",), vmem_limit_bytes=VMEM_LIMIT),
        name="fused_out_proj",
    )(x, o_a, o_b, o_c, o_d, *consts)
    return x1, x1b, route[:, :TOP_K], route[:, TOP_K:2 * TOP_K].astype(jnp.int32)


def _moe_expert_kernel(tile_e_ref, n_used_ref, x_ref, wg_ref, wl_ref, bg_ref, bl_ref, wd_ref, bd_ref, buf_ref, o_ref):
    del tile_e_ref, buf_ref
    i = pl.program_id(0)

    @pl.when(i < n_used_ref[0])
    def _():
        x = x_ref[...]
        glu = jnp.dot(x, wg_ref[0], preferred_element_type=F32) + bg_ref[0]
        lin = jnp.dot(x, wl_ref[0], preferred_element_type=F32) + bl_ref[0]
        glu = jnp.minimum(glu, SWIGLU_LIMIT)
        lin = jnp.clip(lin, -SWIGLU_LIMIT, SWIGLU_LIMIT)
        act = glu * jax.nn.sigmoid(SWIGLU_ALPHA * glu) * (lin + 1.0)
        y = jnp.dot(act.astype(wd_ref.dtype), wd_ref[0], preferred_element_type=F32) + bd_ref[0]
        o_ref[...] = y.astype(o_ref.dtype)

    @pl.when(i >= n_used_ref[0])
    def _():
        o_ref[...] = jnp.zeros_like(o_ref)


def _deinterleave_kernel(w_ref, even_ref, odd_ref):
    g = 2 * LANES
    i = lax.broadcasted_iota(jnp.int32, (g, g), 0)
    j = lax.broadcasted_iota(jnp.int32, (g, g), 1)
    perm = (i == jnp.where(j < LANES, 2 * j, 2 * (j - LANES) + 1)).astype(BF16)
    for c in range(w_ref.shape[2] // g):
        blk = w_ref[0, :, c * g:(c + 1) * g].astype(BF16)
        sorted_cols = jnp.dot(blk, perm, preferred_element_type=F32)
        even_ref[0, :, c * LANES:(c + 1) * LANES] = sorted_cols[:, :LANES].astype(even_ref.dtype)
        odd_ref[0, :, c * LANES:(c + 1) * LANES] = sorted_cols[:, LANES:].astype(odd_ref.dtype)


def deinterleave_to_bf16(w):
    E, D, F2 = w.shape
    rows = min(DEINT_ROWS, D)
    assert D % rows == 0 and F2 % (2 * LANES) == 0
    out = jax.ShapeDtypeStruct((E, D, F2 // 2), BF16)
    return pl.pallas_call(
        _deinterleave_kernel,
        grid=(E, D // rows),
        in_specs=[pl.BlockSpec((1, rows, F2), lambda e, r: (e, r, 0))],
        out_specs=[pl.BlockSpec((1, rows, F2 // 2), lambda e, r: (e, r, 0))] * 2,
        out_shape=[out, out],
        compiler_params=pltpu.CompilerParams(dimension_semantics=("arbitrary", "arbitrary"), vmem_limit_bytes=VMEM_LIMIT),
        name="deinterleave_cast",
    )(w)


def moe_experts(xs, tile_e, n_used, w_glu, w_lin, b_glu, b_lin, w_dn, b_dn, tile, first_tile, out_buf):
    n_rows, D = xs.shape
    E, _, F = w_glu.shape
    w_spec = lambda shape: pl.BlockSpec((1,) + shape, lambda i, te, nu: (te[i], 0, 0))
    operands = [tile_e, n_used, xs, w_glu, w_lin, b_glu.reshape(E, 1, F), b_lin.reshape(E, 1, F), w_dn, b_dn.reshape(E, 1, D),
                out_buf]
    return pl.pallas_call(
        _moe_expert_kernel,
        grid_spec=pltpu.PrefetchScalarGridSpec(
            num_scalar_prefetch=2,
            grid=(n_rows // tile,),
            in_specs=[pl.BlockSpec((tile, D), lambda i, te, nu: (i, 0)),
                      w_spec((D, F)), w_spec((D, F)), w_spec((1, F)), w_spec((1, F)), w_spec((F, D)), w_spec((1, D)),
                      pl.BlockSpec(memory_space=pl.ANY)],
            out_specs=pl.BlockSpec((tile, D), lambda i, te, nu: (i + first_tile, 0)),
        ),
        out_shape=jax.ShapeDtypeStruct(out_buf.shape, out_buf.dtype),
        input_output_aliases={len(operands) - 1: 0},
        compiler_params=pltpu.CompilerParams(dimension_semantics=("arbitrary",), vmem_limit_bytes=VMEM_LIMIT),
        name="moe_expert_ffn",
    )(*operands)


def _combine_kernel(x_ref, y0_ref, y1_ref, y2_ref, y3_ref, gate_ref, ln_ref, o_ref, *, alpha):
    gate = gate_ref[...]
    y = sum(y_ref[...].astype(F32) * gate[:, s:s + 1] for s, y_ref in enumerate((y0_ref, y1_ref, y2_ref, y3_ref)))
    o_ref[...] = _layer_norm_rows(alpha * x_ref[...] + y, ln_ref)


def moe_combine_ln(x1, ys, gate, ln_gb, alpha, tile=OUT_TILE):
    N, D = x1.shape
    tile = min(tile, N)
    assert N % tile == 0 and len(ys) == TOP_K
    tok = lambda w: pl.BlockSpec((tile, w), lambda i: (i, 0))
    return pl.pallas_call(
        functools.partial(_combine_kernel, alpha=alpha),
        grid=(N // tile,),
        in_specs=[tok(D)] * (1 + TOP_K) + [tok(TOP_K), pl.BlockSpec(ln_gb.shape, lambda i: (0, 0))],
        out_specs=tok(D),
        out_shape=jax.ShapeDtypeStruct((N, D), F32),
        compiler_params=pltpu.CompilerParams(dimension_semantics=("arbitrary",), vmem_limit_bytes=VMEM_LIMIT),
        name="moe_combine_ln",
    )(x1, *ys, gate, ln_gb)


def moe_layer(x1, x1b, gate, top_idx, n_exp, expert_params, expert_base, ln_gb, alpha, row_buf, tile=MOE_TILE):
    n_tok, D = x1.shape
    n_assign = n_tok * TOP_K
    e_flat = top_idx.T.reshape(-1)
    experts = jnp.arange(n_exp, dtype=jnp.int32)
    e_sorted, order = lax.sort((e_flat, jnp.arange(n_assign, dtype=jnp.int32)), num_keys=1)
    counts = jnp.sum((e_flat[:, None] == experts[None, :]).astype(jnp.int32), axis=0)
    padded = (counts + tile - 1) // tile * tile
    start = jnp.cumsum(counts) - counts
    pend = jnp.cumsum(padded)
    pstart = pend - padded
    n_tiles = (n_assign + n_exp * (tile - 1) + tile - 1) // tile
    tile_first = jnp.arange(n_tiles, dtype=jnp.int32) * tile
    tile_e = jnp.minimum(jnp.sum((pend[None, :] <= tile_first[:, None]).astype(jnp.int32), axis=1), n_exp - 1)
    n_used = (pend[-1] // tile).astype(jnp.int32).reshape(1)
    within = (tile_first - pstart[tile_e])[:, None] + jnp.arange(tile, dtype=jnp.int32)[None, :]
    valid = within < counts[tile_e][:, None]
    sorted_idx = jnp.clip(start[tile_e][:, None] + within, 0, n_assign - 1)
    spread = (tile_first[:, None] + jnp.arange(tile, dtype=jnp.int32)[None, :]) % n_tok
    row_tok = jnp.where(valid, order[sorted_idx] % n_tok, spread).reshape(-1)
    shift = jnp.sum(jnp.where(e_sorted[:, None] == experts[None, :], (pstart - start)[None, :], 0), axis=1)
    dest = jnp.arange(n_assign, dtype=jnp.int32) + shift
    _, pos = lax.sort((order, dest), num_keys=1)
    pos = pos.reshape(TOP_K, n_tok)
    n_chunks = MOE_GATHER_CHUNKS if n_tiles % MOE_GATHER_CHUNKS == 0 else 1
    tpc = n_tiles // n_chunks
    row_tok = row_tok.reshape(n_chunks, tpc * tile)
    tile_e = (tile_e + expert_base).reshape(n_chunks, tpc)
    yb = row_buf
    for c in range(n_chunks):
        yb = moe_experts(x1b[row_tok[c]], tile_e[c], jnp.clip(n_used - c * tpc, 0, tpc), *expert_params, tile, c * tpc, yb)
    return moe_combine_ln(x1, [yb[pos[s]] for s in range(TOP_K)], gate, ln_gb, alpha), yb


def moe_row_buffer(n_tok, D, n_exp, tile=MOE_TILE):
    n_tiles = (n_tok * TOP_K + n_exp * (tile - 1) + tile - 1) // tile
    return jnp.zeros((n_tiles * tile, D), BF16)


def prep_expert_params(exp_w_gu, exp_b_gu, exp_w_dn, exp_b_dn):
    L, E, D, F2 = exp_w_gu.shape
    w_glu, w_lin = deinterleave_to_bf16(exp_w_gu.reshape(L * E, D, F2))
    b_gu = exp_b_gu.reshape(L * E, F2)
    return (w_glu, w_lin, b_gu[:, 0::2], b_gu[:, 1::2], exp_w_dn.astype(BF16).reshape(L * E, F2 // 2, D),
            exp_b_dn.reshape(L * E, D))


def kernel(x, w_in, w_out, ret_norm_g, dil_norm_g, rwkv_mu, rwkv_w0, rwkv_w_up, rwkv_a0, rwkv_a_up, rwkv_g_up, rwkv_k_k, rwkv_k_a, rwkv_r_k, rwkv_ln_g, rwkv_ln_b, rwkv_vres_down, rwkv_vres_mu, rwkv_v0, rwkv_v_up, mla_q_norm_g, mla_w_q_up, mla_kv_norm_g, mla_w_kv_up, mla_out_norm_g, ln1_g, ln1_b, router_w, router_b, exp_w_gu, exp_b_gu, exp_w_dn, exp_b_dn, ln2_g, ln2_b):
    depth = w_in.shape[0]
    B_, S_, D = x.shape
    alpha = (2 * depth) ** 0.25
    tables = rope_tables(S_)
    ret_tabs = retention_tables()
    expert_params = prep_expert_params(exp_w_gu, exp_b_gu, exp_w_dn, exp_b_dn)
    n_exp = exp_w_gu.shape[1]
    v_first = None
    n_tok = B_ * S_
    row_buf = moe_row_buffer(n_tok, D, n_exp)
    flat = lambda t: t.reshape(n_tok, t.shape[-1])
    for l in range(depth):
        w_cat = prep_in_weights(w_in[l], rwkv_vres_down[l - 1] if l > 0 else None)
        rw_params = prep_rwkv_params(rwkv_mu[l], rwkv_w0[l], rwkv_w_up[l], rwkv_a0[l], rwkv_a_up[l], rwkv_g_up[l],
                                     rwkv_k_k[l], rwkv_k_a[l], rwkv_vres_mu[l - 1] if l > 0 else None,
                                     rwkv_v0[l - 1] if l > 0 else None, rwkv_v_up[l - 1] if l > 0 else None)
        mla_params = prep_mla_params(mla_q_norm_g[l], mla_w_q_up[l], mla_kv_norm_g[l], mla_w_kv_up[l])
        gn = jnp.stack([rwkv_ln_g[l], rwkv_ln_b[l], rwkv_r_k[l].reshape(-1)])
        w_abc, w_d, g_d = prep_out_weights(w_out[l], mla_out_norm_g[l])
        (ret_q, ret_k, ret_v, ret_g, dil_q, dil_k, dil_v, rw_r, rw_lw, rw_k, rw_v, rw_a, rw_b, rw_g,
         mla_q, mla_k, mla_v) = fused_in_proj(x, w_cat, tables, rw_params, mla_params, v_first)
        if l == 0:
            v_first = rw_v
        o_a = retention_fused(ret_q, ret_k, ret_v, ret_g, ret_norm_g[l], ret_tabs)
        o_b = dilated_fused(dil_q, dil_k, dil_v)
        o_c = rwkv7_mixer_fused(rw_r, rw_lw, rw_k, rw_v, rw_a, rw_b, rw_g, gn)
        o_d = mla_attention(mla_q, mla_k, mla_v)
        x1, x1b, gate, top_idx = fused_out_proj(flat(x), flat(o_a), flat(o_b), flat(o_c), flat(o_d), w_abc, w_d,
                                                dil_norm_g[l].reshape(1, DIL_W), g_d, jnp.stack([ln1_g[l], ln1_b[l]]),
                                                router_w[l], router_b[l], alpha)
        x, row_buf = moe_layer(x1, x1b, gate, top_idx, n_exp, expert_params, l * n_exp,
                               jnp.stack([ln2_g[l], ln2_b[l]]), alpha, row_buf)
        x = x.reshape(B_, S_, D)
    return x
```
